```python
import math
import jax
import jax.numpy as jnp
from jax import lax
import numpy as np

D_MODEL = 1024
BATCH = 2
SEQ = 16384
DEPTH = 2

GRID_W = 64
CTX_LEN = 256
NORM_EPS = 1e-6
MASK_VALUE = -1e30
Q_BLOCK = 128
N_BRANCH = 4
BRANCH_WIDTH = D_MODEL // 4

MLA_HEADS = 4
MLA_Q_RANK = D_MODEL // 4
MLA_KV_RANK = D_MODEL // 8
MLA_NOPE = 64
MLA_ROPE = 32
MLA_V = BRANCH_WIDTH // MLA_HEADS
MLA_SCALE = 1.0 / math.sqrt(MLA_NOPE + MLA_ROPE)
ROPE_BASE = 10000.0

HY_WIDTH = BRANCH_WIDTH
HY_ORDER = 2
HY_BANDS = 16
HY_EMB = 1 + 2 * HY_BANDS
HY_HIDDEN = 64
HY_DECAY_TARGET = 1e-2
HY_DECAY_SHORT_PCT = 0.3
HY_DECAY_LONG_PCT = 1.5

FN_GROUPS = 4
FN_GROUP_DIM = BRANCH_WIDTH // FN_GROUPS

NA_HEADS = 4
NA_HEAD_DIM = BRANCH_WIDTH // NA_HEADS
NA_KH = 8
NA_KW = 16
NA_SCALE = 1.0 / math.sqrt(NA_HEAD_DIM)

MOE_GROUPS = 4
MOE_PER_GROUP = 4
MOE_EXPERTS = MOE_GROUPS * MOE_PER_GROUP
MOE_TOPK = 2
MOE_HIDDEN = 256

MLA_IN = MLA_Q_RANK + MLA_KV_RANK + MLA_ROPE
HY_IN = (HY_ORDER + 1) * HY_WIDTH
FN_IN = BRANCH_WIDTH
NA_IN = 3 * BRANCH_WIDTH
GATE_IN = N_BRANCH * D_MODEL
IN_WIDTH = MLA_IN + HY_IN + FN_IN + NA_IN + GATE_IN

kernel_name = 'hybrid_flow_backbone'


def rmsnorm(x, g):
    xf = x.astype(jnp.float32)
    y = xf * lax.rsqrt(jnp.mean(xf * xf, axis=-1, keepdims=True) + NORM_EPS)
    return (y * g.astype(jnp.float32)).astype(x.dtype)


def modulate(h, shift, scale):
    return h * (1.0 + scale) + shift


def split_columns(p):
    bounds = (MLA_Q_RANK, MLA_Q_RANK + MLA_KV_RANK, MLA_IN, MLA_IN + HY_IN,
              MLA_IN + HY_IN + FN_IN, MLA_IN + HY_IN + FN_IN + NA_IN)
    return jnp.split(p, bounds, axis=-1)


def axial_rope(n_tokens):
    t = jnp.arange(n_tokens, dtype=jnp.int32)
    row = (t // GRID_W).astype(jnp.float32)
    col = (t % GRID_W).astype(jnp.float32)
    n_freq = MLA_ROPE // 4
    inv = ROPE_BASE ** (-jnp.arange(n_freq, dtype=jnp.float32) / n_freq)
    ang = jnp.concatenate([row[:, None] * inv, col[:, None] * inv], axis=-1)
    return jnp.cos(ang), jnp.sin(ang)


def apply_rope(x, cos, sin):
    xf = x.astype(jnp.float32)
    xe, xo = xf[..., 0::2], xf[..., 1::2]
    c = cos[None, :, None, :]
    s = sin[None, :, None, :]
    out = jnp.stack([xe * c - xo * s, xe * s + xo * c], axis=-1).reshape(x.shape)
    return out.astype(x.dtype)


def blocked_attention(q, k, v, scale):
    b, n_q, h, dk = q.shape
    n_blocks = n_q // Q_BLOCK
    q_blocks = jnp.moveaxis(q.reshape(b, n_blocks, Q_BLOCK, h, dk), 1, 0)

    def one_block(qb):
        s = jnp.einsum('bqhd,bkhd->bhqk', qb, k).astype(jnp.float32) * scale
        p = jax.nn.softmax(s, axis=-1).astype(v.dtype)
        return jnp.einsum('bhqk,bkhd->bqhd', p, v)

    o = lax.map(one_block, q_blocks)
    return jnp.moveaxis(o, 0, 1).reshape(b, n_q, h * v.shape[-1])


def mla_qkv(cq, ckv, kr, q_norm, w_uq, kv_norm, w_ukv, rope):
    b, n, _ = cq.shape
    q = (rmsnorm(cq, q_norm) @ w_uq).reshape(b, n, MLA_HEADS, MLA_NOPE + MLA_ROPE)
    kv = (rmsnorm(ckv, kv_norm) @ w_ukv).reshape(b, n, MLA_HEADS, MLA_NOPE + MLA_V)
    q_nope, q_rot = q[..., :MLA_NOPE], q[..., MLA_NOPE:]
    k_nope, v = kv[..., :MLA_NOPE], kv[..., MLA_NOPE:]
    k_rot = kr[:, :, None, :]
    if rope is not None:
        cos, sin = rope
        q_rot = apply_rope(q_rot, cos, sin)
        k_rot = apply_rope(k_rot, cos, sin)
    k_rot = jnp.broadcast_to(k_rot, (b, n, MLA_HEADS, MLA_ROPE))
    return (jnp.concatenate([q_nope, q_rot], axis=-1),
            jnp.concatenate([k_nope, k_rot], axis=-1), v)


def split_heads_qkv(p):
    b, n, _ = p.shape
    q, k, v = jnp.split(p, 3, axis=-1)
    shape = (b, n, NA_HEADS, NA_HEAD_DIM)
    return q.reshape(shape), k.reshape(shape), v.reshape(shape)


def neighbourhood_attention(q, k, v, k_ctx, v_ctx, rpb):
    b, n, h, dh = q.shape
    rows = n // GRID_W
    kh = min(NA_KH, rows)
    r = jnp.arange(rows)
    row_idx = jnp.clip(r - kh // 2, 0, rows - kh)[:, None] + jnp.arange(kh)[None, :]
    col = jnp.arange(GRID_W)
    col_start = jnp.clip(col - NA_KW // 2, 0, GRID_W - NA_KW)
    col_ok = (col[None, :] >= col_start[:, None]) & (col[None, :] < col_start[:, None] + NA_KW)
    dr = row_idx - r[:, None]
    dc = jnp.clip(col[None, :] - col[:, None], -(NA_KW - 1), NA_KW - 1)
    bias = rpb[:, :, dc + NA_KW - 1][:, dr + NA_KH - 1]
    bias = jnp.transpose(bias, (0, 1, 3, 2, 4)).astype(jnp.float32)
    qg = q.reshape(b, rows, GRID_W, h, dh)
    kg = k.reshape(b, rows, GRID_W, h, dh)[:, row_idx]
    vg = v.reshape(b, rows, GRID_W, h, dh)[:, row_idx]
    s_loc = jnp.einsum('brqhd,brjkhd->bhrqjk', qg, kg).astype(jnp.float32) * NA_SCALE + bias[None]
    s_loc = jnp.where(col_ok[:, None, :], s_loc, MASK_VALUE)
    s_ctx = jnp.einsum('brqhd,bchd->bhrqc', qg, k_ctx).astype(jnp.float32) * NA_SCALE
    n_loc = kh * GRID_W
    s = jnp.concatenate([s_loc.reshape(b, h, rows, GRID_W, n_loc), s_ctx], axis=-1)
    p = jax.nn.softmax(s, axis=-1).astype(v.dtype)
    p_loc = p[..., :n_loc].reshape(b, h, rows, GRID_W, kh, GRID_W)
    p_ctx = p[..., n_loc:]
    o = (jnp.einsum('bhrqjk,brjkhd->brqhd', p_loc, vg)
         + jnp.einsum('bhrqc,bchd->brqhd', p_ctx, v_ctx))
    return o.reshape(b, n, h * dh)


def short_conv(u, w):
    up = jnp.pad(u, ((0, 0), (1, 1), (0, 0)))
    return up[:, :-2] * w[0] + up[:, 1:-1] * w[1] + up[:, 2:] * w[2]


def hyena_filters(n_tokens, w1, b1, freq1, w2, b2, freq2, w3):
    f32 = jnp.float32
    t = jnp.linspace(0.0, 1.0, n_tokens, dtype=f32)[:, None]
    bands = jnp.linspace(1e-4, HY_BANDS - 1, HY_BANDS, dtype=f32)
    ang = (2.0 * math.pi / n_tokens) * jnp.arange(n_tokens, dtype=f32)[:, None] * bands
    z = jnp.concatenate([t, jnp.cos(ang), -jnp.sin(ang)], axis=-1)
    h = jnp.sin(freq1.astype(f32) * (z @ w1.astype(f32) + b1.astype(f32)))
    h = jnp.sin(freq2.astype(f32) * (h @ w2.astype(f32) + b2.astype(f32)))
    h = h @ w3.astype(f32)
    deltas = jnp.linspace(math.log(HY_DECAY_TARGET) / HY_DECAY_LONG_PCT,
                          math.log(HY_DECAY_TARGET) / HY_DECAY_SHORT_PCT, HY_WIDTH, dtype=f32)
    window = jnp.exp(-t * jnp.abs(deltas))
    return h.reshape(n_tokens, 2, HY_ORDER, HY_WIDTH) * window[:, None, None, :]


def bidir_fftconv(u, h_fwd, h_bwd):
    n = u.shape[1]
    g = jnp.concatenate([h_fwd, jnp.zeros_like(h_fwd[:1]), h_bwd[:0:-1]], axis=0)
    g = g * lax.rsqrt(jnp.sum(g * g, axis=0, keepdims=True) + NORM_EPS)
    u_f = jnp.fft.rfft(u.astype(jnp.float32), n=2 * n, axis=1)
    g_f = jnp.fft.rfft(g, n=2 * n, axis=0)
    return jnp.fft.irfft(u_f * g_f[None], n=2 * n, axis=1)[:, :n]


def hyena(u, short_w, filt, bias):
    u = short_conv(u, short_w)
    v, x1, x2 = jnp.split(u, 3, axis=-1)
    z = v
    for n, gate in enumerate((x1, x2)):
        y = bidir_fftconv(z, filt[:, 0, n], filt[:, 1, n]).astype(z.dtype) + z * bias[n]
        z = gate * y
    return z


def fnet_mix(u):
    b, n, _ = u.shape
    ug = u.astype(jnp.float32).reshape(b, n, FN_GROUPS, FN_GROUP_DIM)
    y = jnp.fft.fftn(ug, axes=(1, 3), norm='ortho').real
    return y.reshape(b, n, FN_GROUPS * FN_GROUP_DIM).astype(u.dtype)


def merge_branches(outs, gate_logits, w_branch, w_out):
    o = jnp.stack(outs, axis=2)
    proj = jnp.einsum('blnc,ncd->blnd', o, w_branch)
    g = jax.nn.sigmoid(gate_logits.reshape(proj.shape))
    return jnp.sum(g * proj, axis=2) @ w_out


def token_mixers(h_lat, h_ctx, w_in, mla_q_norm, mla_w_uq, mla_kv_norm, mla_w_ukv,
                 hy_short_w, hy_w1, hy_b1, hy_freq1, hy_w2, hy_b2, hy_freq2, hy_w3, hy_bias,
                 na_rpb, w_branch, w_out, update_ctx):
    n_lat = h_lat.shape[1]
    n_ctx = h_ctx.shape[1]
    cq_l, ckv_l, kr_l, hy_l, fn_l, na_l, gate_l = split_columns(h_lat @ w_in)
    cq_c, ckv_c, kr_c, hy_c, fn_c, na_c, gate_c = split_columns(h_ctx @ w_in)

    qa_l, ka_l, va_l = mla_qkv(cq_l, ckv_l, kr_l, mla_q_norm, mla_w_uq, mla_kv_norm, mla_w_ukv,
                               axial_rope(n_lat))
    qa_c, ka_c, va_c = mla_qkv(cq_c, ckv_c, kr_c, mla_q_norm, mla_w_uq, mla_kv_norm, mla_w_ukv, None)
    oa_l = blocked_attention(qa_l, jnp.concatenate([ka_l, ka_c], axis=1),
                             jnp.concatenate([va_l, va_c], axis=1), MLA_SCALE)
    qd_l, kd_l, vd_l = split_heads_qkv(na_l)
    qd_c, kd_c, vd_c = split_heads_qkv(na_c)
    od_l = neighbourhood_attention(qd_l, kd_l, vd_l, kd_c, vd_c, na_rpb)
    filt_l = hyena_filters(n_lat, hy_w1, hy_b1, hy_freq1, hy_w2, hy_b2, hy_freq2, hy_w3)
    ob_l = hyena(hy_l, hy_short_w, filt_l, hy_bias)
    oc_l = fnet_mix(fn_l)
    m_lat = merge_branches((oa_l, ob_l, oc_l, od_l), gate_l, w_branch, w_out)

    m_ctx = None
    if update_ctx:
        oa_c = blocked_attention(qa_c, ka_c, va_c, MLA_SCALE)
        filt_c = hyena_filters(n_ctx, hy_w1, hy_b1, hy_freq1, hy_w2, hy_b2, hy_freq2, hy_w3)
        ob_c = hyena(hy_c, hy_short_w, filt_c, hy_bias)
        oc_c = fnet_mix(fn_c)
        od_c = blocked_attention(qd_c, kd_c, vd_c, NA_SCALE)
        m_ctx = merge_branches((oa_c, ob_c, oc_c, od_c), gate_c, w_branch, w_out)
    return m_lat, m_ctx


def hier_moe(h, w_group, b_group, w_expert, b_expert, w_gate, w_up, w_down):
    t_count = h.shape[0]
    f32 = jnp.float32
    g_prob = jax.nn.softmax((h @ w_group).astype(f32) + b_group.astype(f32), axis=-1)
    g_p, g_idx = lax.top_k(g_prob, 1)
    g_onehot = jax.nn.one_hot(g_idx[:, 0], MOE_GROUPS, dtype=f32)
    e_logits = ((h @ w_expert).astype(f32) + b_expert.astype(f32)).reshape(
        t_count, MOE_GROUPS, MOE_PER_GROUP)
    e_sel = jnp.sum(e_logits * g_onehot[:, :, None], axis=1)
    e_p, e_idx = lax.top_k(jax.nn.softmax(e_sel, axis=-1), MOE_TOPK)
    e_p = e_p / jnp.sum(e_p, axis=-1, keepdims=True)
    w_local = jnp.sum(jax.nn.one_hot(e_idx, MOE_PER_GROUP, dtype=f32) * e_p[..., None], axis=1)
    weights = (g_onehot[:, :, None] * (g_p * w_local)[:, None, :]).astype(h.dtype)
    out = jnp.zeros_like(h)
    for gi in range(MOE_GROUPS):
        sl = slice(gi * MOE_PER_GROUP, (gi + 1) * MOE_PER_GROUP)
        a = jnp.einsum('td,edf->tef', h, w_gate[sl])
        u = jnp.einsum('td,edf->tef', h, w_up[sl])
        act = jax.nn.silu(a) * u * weights[:, gi, :, None]
        out = out + jnp.einsum('tef,efd->td', act, w_down[sl])
    return out


def setup_inputs(seed: int = 0) -> dict:
    key = jax.random.key(seed)
    ks = jax.random.split(key, 33)

    def nrm(k, shape, scale):
        return scale * jax.random.normal(k, shape, jnp.float32)

    D, L = D_MODEL, DEPTH
    E, F = MOE_EXPERTS, MOE_HIDDEN
    return {
        'x': nrm(ks[0], (BATCH, SEQ, D), 1.0),
        'c': nrm(ks[1], (BATCH, D), 1.0),
        'ctx': nrm(ks[2], (BATCH, CTX_LEN, D), 1.0),
        'c_ctx': nrm(ks[3], (D,), 1.0),
        'ada_w': nrm(ks[4], (L, D, 6 * D), 0.5 * D ** -0.5),
        'ada_b': nrm(ks[5], (L, 6 * D), 0.02),
        'norm1_g': 1.0 + nrm(ks[6], (L, D), 0.1),
        'norm2_g': 1.0 + nrm(ks[7], (L, D), 0.1),
        'w_in': nrm(ks[8], (L, D, IN_WIDTH), D ** -0.5),
        'mla_q_norm': 1.0 + nrm(ks[9], (L, MLA_Q_RANK), 0.1),
        'mla_w_uq': nrm(ks[10], (L, MLA_Q_RANK, MLA_HEADS * (MLA_NOPE + MLA_ROPE)), MLA_Q_RANK ** -0.5),
        'mla_kv_norm': 1.0 + nrm(ks[11], (L, MLA_KV_RANK), 0.1),
        'mla_w_ukv': nrm(ks[12], (L, MLA_KV_RANK, MLA_HEADS * (MLA_NOPE + MLA_V)), MLA_KV_RANK ** -0.5),
        'hy_short_w': nrm(ks[13], (L, 3, HY_IN), 3 ** -0.5),
        'hy_w1': nrm(ks[14], (L, HY_EMB, HY_HIDDEN), HY_EMB ** -0.5),
        'hy_b1': nrm(ks[15], (L, HY_HIDDEN), 0.1),
        'hy_freq1': 1.0 + nrm(ks[16], (L, HY_HIDDEN), 0.1),
        'hy_w2': nrm(ks[17], (L, HY_HIDDEN, HY_HIDDEN), HY_HIDDEN ** -0.5),
        'hy_b2': nrm(ks[18], (L, HY_HIDDEN), 0.1),
        'hy_freq2': 1.0 + nrm(ks[19], (L, HY_HIDDEN), 0.1),
        'hy_w3': nrm(ks[20], (L, HY_HIDDEN, 2 * HY_ORDER * HY_WIDTH), HY_HIDDEN ** -0.5),
        'hy_bias': nrm(ks[21], (L, HY_ORDER, HY_WIDTH), 0.5),
        'na_rpb': nrm(ks[22], (L, NA_HEADS, 2 * NA_KH - 1, 2 * NA_KW - 1), 0.1),
        'w_branch': nrm(ks[23], (L, N_BRANCH, BRANCH_WIDTH, D), BRANCH_WIDTH ** -0.5),
        'w_out': nrm(ks[24], (L, D, D), D ** -0.5),
        'moe_w_group': nrm(ks[25], (L, D, MOE_GROUPS), D ** -0.5),
        'moe_b_group': nrm(ks[26], (L, MOE_GROUPS), 0.01),
        'moe_w_expert': nrm(ks[27], (L, D, E), D ** -0.5),
        'moe_b_expert': nrm(ks[28], (L, E), 0.01),
        'moe_w_gate': nrm(ks[29], (L, E, D, F), D ** -0.5),
        'moe_w_up': nrm(ks[30], (L, E, D, F), D ** -0.5),
        'moe_w_down': nrm(ks[31], (L, E, F, D), F ** -0.5),
        'final_norm_g': 1.0 + nrm(ks[32], (D,), 0.1),
    }


def reference(x, c, ctx, c_ctx, ada_w, ada_b, norm1_g, norm2_g, w_in,
              mla_q_norm, mla_w_uq, mla_kv_norm, mla_w_ukv,
              hy_short_w, hy_w1, hy_b1, hy_freq1, hy_w2, hy_b2, hy_freq2, hy_w3, hy_bias,
              na_rpb, w_branch, w_out,
              moe_w_group, moe_b_group, moe_w_expert, moe_b_expert, moe_w_gate, moe_w_up, moe_w_down,
              final_norm_g):
    n_lat_tokens = x.shape[0] * x.shape[1]
    for l in range(DEPTH):
        last = l == DEPTH - 1
        mod_lat = jax.nn.silu(c) @ ada_w[l] + ada_b[l]
        mod_ctx = jax.nn.silu(c_ctx) @ ada_w[l] + ada_b[l]
        sh1, sc1, gt1, sh2, sc2, gt2 = jnp.split(mod_lat[:, None, :], 6, axis=-1)
        sh1c, sc1c, gt1c, sh2c, sc2c, gt2c = jnp.split(mod_ctx, 6, axis=-1)

        h_lat = modulate(rmsnorm(x, norm1_g[l]), sh1, sc1)
        h_ctx = modulate(rmsnorm(ctx, norm1_g[l]), sh1c, sc1c)
        m_lat, m_ctx = token_mixers(h_lat, h_ctx, w_in[l], mla_q_norm[l], mla_w_uq[l], mla_kv_norm[l],
                                    mla_w_ukv[l], hy_short_w[l], hy_w1[l], hy_b1[l], hy_freq1[l],
                                    hy_w2[l], hy_b2[l], hy_freq2[l], hy_w3[l], hy_bias[l],
                                    na_rpb[l], w_branch[l], w_out[l], not last)
        x = x + gt1 * m_lat

        h_lat = modulate(rmsnorm(x, norm2_g[l]), sh2, sc2).reshape(-1, D_MODEL)
        if last:
            f = hier_moe(h_lat, moe_w_group[l], moe_b_group[l], moe_w_expert[l], moe_b_expert[l],
                         moe_w_gate[l], moe_w_up[l], moe_w_down[l])
            x = x + gt2 * f.reshape(x.shape)
        else:
            ctx = ctx + gt1c * m_ctx
            h_ctx = modulate(rmsnorm(ctx, norm2_g[l]), sh2c, sc2c).reshape(-1, D_MODEL)
            f = hier_moe(jnp.concatenate([h_lat, h_ctx], axis=0), moe_w_group[l], moe_b_group[l],
                         moe_w_expert[l], moe_b_expert[l], moe_w_gate[l], moe_w_up[l], moe_w_down[l])
            x = x + gt2 * f[:n_lat_tokens].reshape(x.shape)
            ctx = ctx + gt2c * f[n_lat_tokens:].reshape(ctx.shape)
    return rmsnorm(x, final_norm_g)
```

```python
import functools
import math

import numpy as np
import jax
import jax.numpy as jnp
from jax import lax
from jax.experimental import pallas as pl
from jax.experimental.pallas import tpu as pltpu

F32 = jnp.float32
BF16 = jnp.bfloat16

D_MODEL = 1024
DEPTH = 2
GRID_W = 64
NORM_EPS = 1e-6
MASK_VALUE = -1e30
BRANCH_WIDTH = D_MODEL // 4
N_BRANCH = 4

MLA_HEADS = 4
MLA_Q_RANK = D_MODEL // 4
MLA_KV_RANK = D_MODEL // 8
MLA_NOPE = 64
MLA_ROPE = 32
MLA_V = BRANCH_WIDTH // MLA_HEADS
MLA_SCALE = 1.0 / math.sqrt(MLA_NOPE + MLA_ROPE)
MLA_HEAD_PAD = 128
ROPE_BASE = 10000.0
LOG2E = 1.4426950408889634

HY_WIDTH = BRANCH_WIDTH
HY_ORDER = 2
HY_BANDS = 16
HY_EMB = 1 + 2 * HY_BANDS
HY_EMB_PAD = 128
HY_HIDDEN = 64
HY_DECAY_TARGET = 1e-2
HY_DECAY_SHORT_PCT = 0.3
HY_DECAY_LONG_PCT = 1.5
HY_IN = (HY_ORDER + 1) * HY_WIDTH

FN_GROUPS = 4
FN_GROUP_DIM = BRANCH_WIDTH // FN_GROUPS

NA_HEADS = 4
NA_HEAD_DIM = BRANCH_WIDTH // NA_HEADS
NA_KH = 8
NA_KW = 16
NA_SCALE = 1.0 / math.sqrt(NA_HEAD_DIM)
NA_QROWS = 8
NA_KROWS = 16

MOE_GROUPS = 4
MOE_PER_GROUP = 4
MOE_EXPERTS = MOE_GROUPS * MOE_PER_GROUP
MOE_HIDDEN = 256
ROUTER_LANES = 128

MLA_IN = MLA_Q_RANK + MLA_KV_RANK + MLA_ROPE
MIX_IN = MLA_IN + HY_IN + BRANCH_WIDTH + 3 * BRANCH_WIDTH
MIX_PAD = MLA_Q_RANK + MLA_KV_RANK + MLA_HEAD_PAD + HY_IN + BRANCH_WIDTH + 3 * BRANCH_WIDTH

DFT_N2 = 256
FN_N2 = 128


def _params(sem, vmem_mb=None):
    kw = dict(dimension_semantics=sem)
    if vmem_mb is not None:
        kw["vmem_limit_bytes"] = vmem_mb << 20
    return pltpu.CompilerParams(**kw)


def _pick_tile(n, cap, mult):
    best = None
    for t in range(mult, min(n, cap) + 1, mult):
        if n % t == 0:
            best = t
    return best if best is not None else n


def _rms(x):
    return x * lax.rsqrt(jnp.mean(x * x, axis=-1, keepdims=True) + NORM_EPS)


def _normmod(x, g, sc, sh):
    return (_rms(x) * g) * (1.0 + sc) + sh


def _dot(a, b):
    return jnp.dot(a, b, preferred_element_type=F32)


def _dot_nt(a, b):
    return lax.dot_general(a, b, (((1,), (1,)), ((), ())), preferred_element_type=F32)


def _mod_kernel(c_ref, w_ref, b_ref, o_ref):
    c = c_ref[...]
    s = c * jax.nn.sigmoid(c)
    o_ref[0] = _dot(s.astype(BF16), w_ref[0].astype(BF16)) + b_ref[0]


def _mod_vectors(cvec, ada_w, ada_b):
    n6 = ada_w.shape[-1]
    tn = 1024
    return pl.pallas_call(
        _mod_kernel,
        grid=(DEPTH, n6 // tn),
        in_specs=[pl.BlockSpec((8, D_MODEL), lambda l, j: (0, 0)),
                  pl.BlockSpec((1, D_MODEL, tn), lambda l, j: (l, 0, j)),
                  pl.BlockSpec((1, 1, tn), lambda l, j: (l, 0, j))],
        out_specs=pl.BlockSpec((1, 8, tn), lambda l, j: (l, 0, j)),
        out_shape=jax.ShapeDtypeStruct((DEPTH, 8, n6), F32),
        compiler_params=_params(("parallel", "parallel")),
        name="adaln_vectors",
    )(cvec, ada_w, ada_b.reshape(DEPTH, 1, n6))


def _inproj_kernel(x_ref, g_ref, sc_ref, sh_ref, wa_ref, qn_ref, kvn_ref, wuq_ref, wuk_ref, wuv_ref,
                   cos_ref, s1_ref, s2_ref,
                   q_ref, k_ref, v_ref, hy_ref, fn_ref, naq_ref, nak_ref, nav_ref):
    h = _normmod(x_ref[0], g_ref[...], sc_ref[0], sh_ref[0]).astype(BF16)
    p = _dot(h, wa_ref[...])
    c0 = MLA_Q_RANK
    c1 = c0 + MLA_KV_RANK
    c2 = c1 + MLA_HEAD_PAD
    c3 = c2 + HY_IN
    c4 = c3 + BRANCH_WIDTH
    c5 = c4 + BRANCH_WIDTH
    c6 = c5 + BRANCH_WIDTH
    cq, ckv, krp = p[:, :c0], p[:, c0:c1], p[:, c1:c2]
    hy_ref[0] = p[:, c2:c3]
    fn_ref[0] = p[:, c3:c4]
    naq_ref[0] = p[:, c4:c5].astype(BF16)
    nak_ref[0] = p[:, c5:c6].astype(BF16)
    nav_ref[0] = p[:, c6:].astype(BF16)
    cqn = (_rms(cq) * qn_ref[...]).astype(BF16)
    ckn = (_rms(ckv) * kvn_ref[...]).astype(BF16)
    q = _dot(cqn, wuq_ref[...])
    kn = _dot(ckn, wuk_ref[...])
    v = _dot(ckn, wuv_ref[...])
    cos, s1, s2 = cos_ref[...], s1_ref[...], s2_ref[...]

    def rope(t):
        return (t * cos + pltpu.roll(t, MLA_HEAD_PAD - MLA_ROPE // 2, 1) * s1
                + pltpu.roll(t, MLA_ROPE // 2, 1) * s2)

    kr = rope(krp)
    for hh in range(MLA_HEADS):
        sl = slice(hh * MLA_HEAD_PAD, (hh + 1) * MLA_HEAD_PAD)
        q_ref[0, hh] = rope(q[:, sl]).astype(BF16)
        k_ref[0, hh] = (kn[:, sl] + kr).astype(BF16)
        v_ref[0, hh] = v[:, hh * MLA_V:(hh + 1) * MLA_V].astype(BF16)


def _inproj(x, g, sc, sh, w, rope_tabs):
    b, n, _ = x.shape
    tm = _pick_tile(n, 512, 8)
    cos, s1, s2 = rope_tabs
    tok = lambda bb, i: (bb, i, 0)
    row = lambda bb, i: (bb, 0, 0)
    const2 = lambda bb, i: (0, 0)
    hd = lambda bb, i: (bb, 0, i, 0)
    tab = lambda bb, i: (i, 0)
    hp = MLA_HEADS * MLA_HEAD_PAD
    outs = pl.pallas_call(
        _inproj_kernel,
        grid=(b, n // tm),
        in_specs=[pl.BlockSpec((1, tm, D_MODEL), tok),
                  pl.BlockSpec((1, D_MODEL), const2),
                  pl.BlockSpec((1, 1, D_MODEL), row),
                  pl.BlockSpec((1, 1, D_MODEL), row),
                  pl.BlockSpec((D_MODEL, MIX_PAD), const2),
                  pl.BlockSpec((1, MLA_Q_RANK), const2),
                  pl.BlockSpec((1, MLA_KV_RANK), const2),
                  pl.BlockSpec((MLA_Q_RANK, hp), const2),
                  pl.BlockSpec((MLA_KV_RANK, hp), const2),
                  pl.BlockSpec((MLA_KV_RANK, BRANCH_WIDTH), const2),
                  pl.BlockSpec((tm, MLA_HEAD_PAD), tab),
                  pl.BlockSpec((tm, MLA_HEAD_PAD), tab),
                  pl.BlockSpec((tm, MLA_HEAD_PAD), tab)],
        out_specs=[pl.BlockSpec((1, MLA_HEADS, tm, MLA_HEAD_PAD), hd),
                   pl.BlockSpec((1, MLA_HEADS, tm, MLA_HEAD_PAD), hd),
                   pl.BlockSpec((1, MLA_HEADS, tm, MLA_V), hd),
                   pl.BlockSpec((1, tm, HY_IN), tok),
                   pl.BlockSpec((1, tm, BRANCH_WIDTH), tok),
                   pl.BlockSpec((1, tm, BRANCH_WIDTH), tok),
                   pl.BlockSpec((1, tm, BRANCH_WIDTH), tok),
                   pl.BlockSpec((1, tm, BRANCH_WIDTH), tok)],
        out_shape=[jax.ShapeDtypeStruct((b, MLA_HEADS, n, MLA_HEAD_PAD), BF16),
                   jax.ShapeDtypeStruct((b, MLA_HEADS, n, MLA_HEAD_PAD), BF16),
                   jax.ShapeDtypeStruct((b, MLA_HEADS, n, MLA_V), BF16),
                   jax.ShapeDtypeStruct((b, n, HY_IN), F32),
                   jax.ShapeDtypeStruct((b, n, BRANCH_WIDTH), F32),
                   jax.ShapeDtypeStruct((b, n, BRANCH_WIDTH), BF16),
                   jax.ShapeDtypeStruct((b, n, BRANCH_WIDTH), BF16),
                   jax.ShapeDtypeStruct((b, n, BRANCH_WIDTH), BF16)],
        compiler_params=_params(("parallel", "parallel"), 48),
        name="input_projection",
    )(x, g, sc, sh, w["wa"], w["qn"], w["kvn"], w["wuq"], w["wuk"], w["wuv"], cos, s1, s2)
    return outs


def _flash_kernel(q_ref, k_ref, v_ref, o_ref, m_ref, l_ref, acc_ref, *, nh, use_exp2):
    j = pl.program_id(2)

    @pl.when(j == 0)
    def _():
        m_ref[...] = jnp.full(m_ref.shape, MASK_VALUE, F32)
        l_ref[...] = jnp.zeros(l_ref.shape, F32)
        acc_ref[...] = jnp.zeros(acc_ref.shape, F32)

    ex = jnp.exp2 if use_exp2 else jnp.exp
    for h in range(nh):
        s = _dot_nt(q_ref[0, h], k_ref[0, h])
        m_prev = m_ref[h]
        m_cur = jnp.maximum(m_prev, jnp.max(s, axis=1, keepdims=True))
        alpha = ex(m_prev - m_cur)
        p = ex(s - m_cur)
        l_ref[h] = alpha * l_ref[h] + jnp.sum(p, axis=1, keepdims=True)
        acc_ref[h] = alpha * acc_ref[h] + _dot(p.astype(BF16), v_ref[0, h])
        m_ref[h] = m_cur

    @pl.when(j == pl.num_programs(2) - 1)
    def _():
        outs = [acc_ref[h] / l_ref[h] for h in range(nh)]
        o_ref[0] = jnp.concatenate(outs, axis=1).astype(o_ref.dtype)


def _flash(q, k, v, use_exp2):
    b, nh, lq, dk = q.shape
    lk, dv = k.shape[2], v.shape[3]
    tq = _pick_tile(lq, 512, 8)
    tk = _pick_tile(lk, 1280, 128)
    return pl.pallas_call(
        functools.partial(_flash_kernel, nh=nh, use_exp2=use_exp2),
        grid=(b, lq // tq, lk // tk),
        in_specs=[pl.BlockSpec((1, nh, tq, dk), lambda bb, i, j: (bb, 0, i, 0)),
                  pl.BlockSpec((1, nh, tk, dk), lambda bb, i, j: (bb, 0, j, 0)),
                  pl.BlockSpec((1, nh, tk, dv), lambda bb, i, j: (bb, 0, j, 0))],
        out_specs=pl.BlockSpec((1, tq, nh * dv), lambda bb, i, j: (bb, i, 0)),
        out_shape=jax.ShapeDtypeStruct((b, lq, nh * dv), BF16),
        scratch_shapes=[pltpu.VMEM((nh, tq, 1), F32), pltpu.VMEM((nh, tq, 1), F32),
                        pltpu.VMEM((nh, tq, dv), F32)],
        compiler_params=_params(("parallel", "parallel", "arbitrary"), 48),
        name="softmax_attention",
    )(q, k, v)


def _na_kernel(q_ref, k0, k1, k2, k3, v0, v1, v2, v3, kc_ref, vc_ref, b_ref, o_ref):
    q = q_ref[0]
    kcat = jnp.concatenate([k0[0], k1[0], k2[0], k3[0]], axis=0)
    vcat = jnp.concatenate([v0[0], v1[0], v2[0], v3[0]], axis=0)
    kc, vc = kc_ref[0], vc_ref[0]
    outs = []
    for h in range(NA_HEADS):
        sl = slice(h * NA_HEAD_DIM, (h + 1) * NA_HEAD_DIM)
        qh = q[:, sl]
        s = _dot_nt(qh, kcat[:, sl]) + b_ref[0, h]
        sc = _dot_nt(qh, kc[:, sl])
        m = jnp.maximum(jnp.max(s, axis=1, keepdims=True), jnp.max(sc, axis=1, keepdims=True))
        p = jnp.exp(s - m)
        pc = jnp.exp(sc - m)
        l = jnp.sum(p, axis=1, keepdims=True) + jnp.sum(pc, axis=1, keepdims=True)
        o = _dot(p.astype(BF16), vcat[:, sl]) + _dot(pc.astype(BF16), vc[:, sl])
        outs.append(o / l)
    o_ref[0] = jnp.concatenate(outs, axis=1).astype(o_ref.dtype)


def _na_bias_tables(rpb):
    j = np.arange(NA_QROWS)[:, None]
    m = np.arange(NA_KROWS)[None, :]
    valid = np.zeros((3, NA_QROWS, NA_KROWS), bool)
    st0 = np.maximum(j - NA_KH // 2, 0)
    valid[0] = (m - 4 >= st0) & (m - 4 < st0 + NA_KH)
    valid[1] = (m - j >= 0) & (m - j < NA_KH)
    st2 = 4 + np.minimum(j - NA_KH // 2, 0)
    valid[2] = (m < 12) & (m >= st2) & (m < st2 + NA_KH)
    dr_idx = np.clip(m - 4 - j + NA_KH - 1, 0, 2 * NA_KH - 2)
    qc = np.arange(GRID_W)[:, None]
    kc = np.arange(GRID_W)[None, :]
    dc_idx = np.clip(kc - qc, -(NA_KW - 1), NA_KW - 1) + NA_KW - 1
    cs = np.clip(qc - NA_KW // 2, 0, GRID_W - NA_KW)
    col_ok = (kc >= cs) & (kc < cs + NA_KW)
    t = rpb.astype(F32)[:, dr_idx]
    t = t[..., dc_idx]
    t = jnp.transpose(t, (0, 1, 3, 2, 4))
    ok = valid[:, None, :, None, :, None] & col_ok[None, None, None, :, None, :]
    out = jnp.where(jnp.asarray(ok), t[None], MASK_VALUE)
    return out.reshape(3, NA_HEADS, NA_QROWS * GRID_W, NA_KROWS * GRID_W)


def _na(q, k, v, kc, vc, bias):
    b, n, w = q.shape
    c = kc.shape[1]
    tq = NA_QROWS * GRID_W
    tkb = 4 * GRID_W
    nb = n // tq
    nkb = n // tkb
    kmaps = [lambda bb, i: (bb, jnp.maximum(2 * i - 1, 0), 0),
             lambda bb, i: (bb, 2 * i, 0),
             lambda bb, i: (bb, 2 * i + 1, 0),
             lambda bb, i: (bb, jnp.minimum(2 * i + 2, nkb - 1), 0)]
    kspecs = [pl.BlockSpec((1, tkb, w), mp) for mp in kmaps]
    return pl.pallas_call(
        _na_kernel,
        grid=(b, nb),
        in_specs=[pl.BlockSpec((1, tq, w), lambda bb, i: (bb, i, 0))] + kspecs + kspecs
                 + [pl.BlockSpec((1, c, w), lambda bb, i: (bb, 0, 0)),
                    pl.BlockSpec((1, c, w), lambda bb, i: (bb, 0, 0)),
                    pl.BlockSpec((1, NA_HEADS, tq, NA_KROWS * GRID_W),
                                 lambda bb, i: (jnp.where(i == 0, 0, jnp.where(i == nb - 1, 2, 1)), 0, 0, 0))],
        out_specs=pl.BlockSpec((1, tq, w), lambda bb, i: (bb, i, 0)),
        out_shape=jax.ShapeDtypeStruct((b, n, w), BF16),
        compiler_params=_params(("parallel", "arbitrary"), 56),
        name="neighbourhood_attention",
    )(q, k, k, k, k, v, v, v, v, kc, vc, bias)


def _sconv_kernel(u_ref, pv_ref, nx_ref, w_ref, v_ref, x1_ref, x2_ref, *, nt):
    i = pl.program_id(1)
    u = u_ref[0]
    tl = u.shape[0]
    prev = jnp.where(i > 0, pv_ref[0, 7:8, :], 0.0)
    nxt = jnp.where(i < nt - 1, nx_ref[0, 0:1, :], 0.0)
    rows = lax.broadcasted_iota(jnp.int32, u.shape, 0)
    um = jnp.where(rows == 0, prev, pltpu.roll(u, 1, 0))
    up = jnp.where(rows == tl - 1, nxt, pltpu.roll(u, tl - 1, 0))
    w = w_ref[...]
    y = um * w[0:1] + u * w[1:2] + up * w[2:3]
    v_ref[...] = y[:, :HY_WIDTH]
    x1_ref[...] = y[:, HY_WIDTH:2 * HY_WIDTH]
    x2_ref[...] = y[:, 2 * HY_WIDTH:]


def _sconv(u, w):
    b, n, _ = u.shape
    tl = _pick_tile(n, 512, 8)
    nt = n // tl
    r8 = tl // 8
    osd = jax.ShapeDtypeStruct((n, b * HY_WIDTH), F32)
    ospec = pl.BlockSpec((tl, HY_WIDTH), lambda bb, i: (i, bb))
    return pl.pallas_call(
        functools.partial(_sconv_kernel, nt=nt),
        grid=(b, nt),
        in_specs=[pl.BlockSpec((1, tl, HY_IN), lambda bb, i: (bb, i, 0)),
                  pl.BlockSpec((1, 8, HY_IN), lambda bb, i: (bb, jnp.maximum(i * r8 - 1, 0), 0)),
                  pl.BlockSpec((1, 8, HY_IN), lambda bb, i: (bb, jnp.minimum((i + 1) * r8, n // 8 - 1), 0)),
                  pl.BlockSpec((3, HY_IN), lambda bb, i: (0, 0))],
        out_specs=[ospec, ospec, ospec],
        out_shape=[osd, osd, osd],
        compiler_params=_params(("parallel", "parallel")),
        name="hyena_short_conv",
    )(u, u, u, w)


def _filt_kernel(z_ref, w1_ref, b1_ref, f1_ref, w2_ref, b2_ref, f2_ref, w3_ref, win_ref, h_ref, ss_ref):
    i = pl.program_id(0)
    h = jnp.sin(f1_ref[...] * (_dot(z_ref[...].astype(BF16), w1_ref[...]) + b1_ref[...]))
    h = jnp.sin(f2_ref[...] * (_dot(h.astype(BF16), w2_ref[...]) + b2_ref[...]))
    h = _dot(h.astype(BF16), w3_ref[...])
    win = win_ref[...]
    h = h * jnp.concatenate([win] * (2 * HY_ORDER), axis=1)
    h_ref[...] = h
    rows = lax.broadcasted_iota(jnp.int32, h.shape, 0)
    cols = lax.broadcasted_iota(jnp.int32, h.shape, 1)
    drop = (rows == 0) & (cols >= HY_ORDER * HY_WIDTH) & (i == 0)
    ss = jnp.sum(jnp.where(drop, 0.0, h * h), axis=0, keepdims=True)

    @pl.when(i == 0)
    def _():
        ss_ref[...] = ss

    @pl.when(i > 0)
    def _():
        ss_ref[...] += ss


def _hyena_filters(n, fw):
    t = jnp.linspace(0.0, 1.0, n, dtype=F32)[:, None]
    bands = jnp.linspace(1e-4, HY_BANDS - 1, HY_BANDS, dtype=F32)
    ang = (2.0 * math.pi / n) * jnp.arange(n, dtype=F32)[:, None] * bands
    z = jnp.concatenate([t, jnp.cos(ang), -jnp.sin(ang), jnp.zeros((n, HY_EMB_PAD - HY_EMB), F32)], axis=-1)
    deltas = jnp.linspace(math.log(HY_DECAY_TARGET) / HY_DECAY_LONG_PCT,
                          math.log(HY_DECAY_TARGET) / HY_DECAY_SHORT_PCT, HY_WIDTH, dtype=F32)
    window = jnp.exp(-t * jnp.abs(deltas))
    tl = _pick_tile(n, 512, 8)
    nf = 2 * HY_ORDER * HY_WIDTH
    c2 = lambda i: (0, 0)
    return pl.pallas_call(
        _filt_kernel,
        grid=(n // tl,),
        in_specs=[pl.BlockSpec((tl, HY_EMB_PAD), lambda i: (i, 0)),
                  pl.BlockSpec((HY_EMB_PAD, HY_HIDDEN), c2), pl.BlockSpec((1, HY_HIDDEN), c2),
                  pl.BlockSpec((1, HY_HIDDEN), c2),
                  pl.BlockSpec((HY_HIDDEN, HY_HIDDEN), c2), pl.BlockSpec((1, HY_HIDDEN), c2),
                  pl.BlockSpec((1, HY_HIDDEN), c2),
                  pl.BlockSpec((HY_HIDDEN, nf), c2),
                  pl.BlockSpec((tl, HY_WIDTH), lambda i: (i, 0))],
        out_specs=[pl.BlockSpec((tl, nf), lambda i: (i, 0)), pl.BlockSpec((1, nf), c2)],
        out_shape=[jax.ShapeDtypeStruct((n, nf), F32), jax.ShapeDtypeStruct((1, nf), F32)],
        compiler_params=_params(("arbitrary",)),
        name="hyena_filter_mlp",
    )(z, fw["w1"], fw["b1"], fw["f1"], fw["w2"], fw["b2"], fw["f2"], fw["w3"], window)


def _padded_filter(hf, ss):
    half = HY_ORDER * HY_WIDTH
    g = jnp.concatenate([hf[:, :half], jnp.zeros((1, half), F32), jnp.flip(hf[1:, half:], axis=0)], axis=0)
    scale = lax.rsqrt(ss[:, :half] + ss[:, half:] + NORM_EPS)
    return g, scale


def _colmm_kernel(*refs, epi):
    w_ref, x_ref = refs[0], refs[1]
    o_ref = refs[-1]
    acc = _dot(w_ref[...], x_ref[...].astype(BF16))
    if epi == "gate":
        xg_ref, z_ref, b_ref = refs[2:5]
        acc = xg_ref[...] * (acc + z_ref[...] * b_ref[...])
    elif epi == "scale":
        acc = acc * refs[2][...]
    elif epi == "cmul":
        g = refs[2][...]
        mh = acc.shape[0] // 2
        ar, ai, gr, gi = acc[:mh], acc[mh:], g[:mh], g[mh:]
        acc = jnp.concatenate([ar * gr - ai * gi, ar * gi + ai * gr], axis=0)
    o_ref[...] = acc.astype(o_ref.dtype)


def _colmm(w, x, out_dtype, epi="none", extra=(), tn_cap=4096, g_col=None):
    m, k = w.shape
    nc = x.shape[1]
    tn = _pick_tile(nc, tn_cap, 128)
    col = lambda j: (0, j)
    specs = [pl.BlockSpec((m, k), lambda j: (0, 0)), pl.BlockSpec((k, tn), col)]
    if epi == "gate":
        specs += [pl.BlockSpec((m, tn), col), pl.BlockSpec((m, tn), col), pl.BlockSpec((1, tn), col)]
    elif epi == "scale":
        specs += [pl.BlockSpec((1, tn), col)]
    elif epi == "cmul":
        specs += [pl.BlockSpec((m, tn), lambda j: (0, g_col))]
    return pl.pallas_call(
        functools.partial(_colmm_kernel, epi=epi),
        grid=(nc // tn,),
        in_specs=specs,
        out_specs=pl.BlockSpec((m, tn), col),
        out_shape=jax.ShapeDtypeStruct((m, nc), out_dtype),
        compiler_params=_params(("parallel",), 48),
        name="dft_stage_" + epi,
    )(w, x, *extra)


def _bmm_kernel(*refs, epi, cplx_out):
    m_ref, x_ref = refs[0], refs[1]
    o_ref = refs[-1]
    x = x_ref[:, 0]
    x = x.reshape(x.shape[0] * x.shape[1], x.shape[2])
    acc = _dot(m_ref[0], x)
    if epi == "cmul":
        g_ref = refs[2]
        ph = acc.shape[0] // 2
        gr, gi = g_ref[0, 0], g_ref[1, 0]
        cg = gr.shape[1]
        res_r, res_i = [], []
        for t in range(acc.shape[1] // cg):
            ar, ai = acc[:ph, t * cg:(t + 1) * cg], acc[ph:, t * cg:(t + 1) * cg]
            res_r.append(ar * gr - ai * gi)
            res_i.append(ar * gi + ai * gr)
        o_ref[0, 0] = jnp.concatenate(res_r, axis=1).astype(o_ref.dtype)
        o_ref[1, 0] = jnp.concatenate(res_i, axis=1).astype(o_ref.dtype)
    elif epi == "scale":
        ph = acc.shape[0] // 2
        sc = refs[2][...]
        o_ref[0, 0] = (acc[:ph] * sc).astype(o_ref.dtype)
        o_ref[1, 0] = (acc[ph:] * sc).astype(o_ref.dtype)
    elif cplx_out:
        ph = acc.shape[0] // 2
        o_ref[0, 0] = acc[:ph].astype(o_ref.dtype)
        o_ref[1, 0] = acc[ph:].astype(o_ref.dtype)
    else:
        o_ref[0] = acc.astype(o_ref.dtype)


def _bmm(mats, x, out_dtype, epi="none", extra=None, g_col=0, cplx_out=True):
    ng, p, qq = mats.shape
    c = x.shape[3]
    specs = [pl.BlockSpec((1, p, qq), lambda g: (g, 0, 0)),
             pl.BlockSpec((2, 1, qq // 2, c), lambda g: (0, g, 0, 0))]
    args = [mats, x]
    if epi == "cmul":
        cg = HY_WIDTH
        specs.append(pl.BlockSpec((2, 1, p // 2, cg), lambda g: (0, g, 0, g_col)))
        args.append(extra)
    elif epi == "scale":
        specs.append(pl.BlockSpec((1, c), lambda g: (0, 0)))
        args.append(extra)
    if cplx_out:
        ospec = pl.BlockSpec((2, 1, p // 2, c), lambda g: (0, g, 0, 0))
        oshape = jax.ShapeDtypeStruct((2, ng, p // 2, c), out_dtype)
    else:
        ospec = pl.BlockSpec((1, p, c), lambda g: (g, 0, 0))
        oshape = jax.ShapeDtypeStruct((ng, p, c), out_dtype)
    return pl.pallas_call(
        functools.partial(_bmm_kernel, epi=epi, cplx_out=cplx_out),
        grid=(ng,),
        in_specs=specs,
        out_specs=ospec,
        out_shape=oshape,
        compiler_params=_params(("parallel",), 48),
        name="dft_batched_" + epi,
    )(*args)


def _angle(num, den):
    return (2.0 * math.pi / den) * jnp.mod(num, den).astype(F32)


def _dft_tables(n_lat):
    nn = 2 * n_lat
    n2 = DFT_N2
    n1 = nn // n2
    i1 = jnp.arange(n1, dtype=jnp.int32)
    a1 = _angle(i1[:, None] * i1[None, :], n1)
    c1, s1 = jnp.cos(a1), jnp.sin(a1)
    t = {}
    t["hy_f1"] = jnp.concatenate([c1, -s1], axis=0).astype(BF16)
    t["hy_i2"] = jnp.concatenate([c1[:n1 // 2], -s1[:n1 // 2]], axis=1).astype(BF16)
    i2 = jnp.arange(n2, dtype=jnp.int32)
    kk = i1[:, None, None] + n1 * i2[None, :, None]
    a2 = _angle(kk * i2[None, None, :], nn)
    c2, s2 = jnp.cos(a2), jnp.sin(a2)
    fwd = jnp.concatenate([jnp.concatenate([c2, s2], axis=2), jnp.concatenate([-s2, c2], axis=2)], axis=1)
    t["hy_fwd"] = fwd.astype(BF16)
    t["hy_inv"] = jnp.swapaxes(fwd, 1, 2).astype(BF16)
    m2 = FN_N2
    m1 = n_lat // m2
    j1 = jnp.arange(m1, dtype=jnp.int32)
    b1 = _angle(j1[:, None] * j1[None, :], m1)
    cb, sb = jnp.cos(b1), jnp.sin(b1)
    t["fn_s1"] = jnp.concatenate([jnp.concatenate([cb, sb], axis=1),
                                  jnp.concatenate([-sb, cb], axis=1)], axis=0).astype(BF16)
    j2 = jnp.arange(m2, dtype=jnp.int32)
    kf = j1[:, None, None] + m1 * j2[None, :, None]
    b2 = _angle(kf * j2[None, None, :], n_lat)
    norm = 1.0 / math.sqrt(n_lat * FN_GROUP_DIM)
    t["fn_s3"] = (norm * jnp.concatenate([jnp.cos(b2), jnp.sin(b2)], axis=2)).astype(BF16)
    return t


def _small_dft_tables(n):
    nn = 2 * n
    k = jnp.arange(nn, dtype=jnp.int32)
    a = _angle(k[:, None] * k[None, :], nn)
    c, s = jnp.cos(a), jnp.sin(a)
    t = {}
    t["f_full"] = jnp.concatenate([c, -s], axis=0).astype(BF16)
    t["f_half"] = jnp.concatenate([c[:, :n], -s[:, :n]], axis=0).astype(BF16)
    t["i_half"] = jnp.concatenate([c[:n], -s[:n]], axis=1).astype(BF16)
    j = jnp.arange(n, dtype=jnp.int32)
    b = _angle(j[:, None] * j[None, :], n)
    norm = 1.0 / math.sqrt(n * FN_GROUP_DIM)
    t["fn"] = (norm * jnp.concatenate([jnp.cos(b), jnp.sin(b)], axis=1)).astype(BF16)
    return t


def _hyena_long(hy, short_w, fw, bias, tabs):
    b, n, _ = hy.shape
    cw = b * HY_WIDTH
    n2 = DFT_N2
    n1 = 2 * n // n2
    v, x1, x2 = _sconv(hy, short_w)
    hf, ss = _hyena_filters(n, fw)
    g, gscale = _padded_filter(hf, ss)
    half = HY_ORDER * HY_WIDTH
    ga = _colmm(tabs["hy_f1"], g.reshape(n1, n2 * half), BF16)
    gspec = _bmm(tabs["hy_fwd"], ga.reshape(2, n1, n2, half), F32, epi="scale",
                 extra=gscale * (1.0 / (2 * n)))
    z = v
    for o, xg in enumerate((x1, x2)):
        a = _colmm(tabs["hy_f1"][:, :n1 // 2], z.reshape(n1 // 2, n2 * cw), BF16)
        y = _bmm(tabs["hy_fwd"], a.reshape(2, n1, n2, cw), BF16, epi="cmul", extra=gspec, g_col=o)
        bm = _bmm(tabs["hy_inv"], y, BF16)
        brow = jnp.tile(bias[o], n2 * b)[None, :]
        z = _colmm(tabs["hy_i2"], bm.reshape(2 * n1, n2 * cw), F32, epi="gate",
                   extra=(xg.reshape(n1 // 2, n2 * cw), z.reshape(n1 // 2, n2 * cw), brow))
        z = z.reshape(n, cw)
    return z


def _hyena_short(hy, short_w, fw, bias, tabs):
    b, n, _ = hy.shape
    cw = b * HY_WIDTH
    v, x1, x2 = _sconv(hy, short_w)
    hf, ss = _hyena_filters(n, fw)
    g, gscale = _padded_filter(hf, ss)
    gspec = _colmm(tabs["f_full"], g, F32, epi="scale", extra=(gscale * (1.0 / (2 * n)),), tn_cap=256)
    z = v
    for o, xg in enumerate((x1, x2)):
        y = _colmm(tabs["f_half"], z, BF16, epi="cmul", extra=(gspec,), tn_cap=HY_WIDTH, g_col=o)
        brow = jnp.tile(bias[o], b)[None, :]
        z = _colmm(tabs["i_half"], y, F32, epi="gate", extra=(xg, z, brow), tn_cap=256)
    return z


def _fn0_kernel(x_ref, w_ref, o_ref):
    acc = _dot(x_ref[0].astype(BF16), w_ref[...])
    o_ref[0] = acc[:, :BRANCH_WIDTH].astype(o_ref.dtype)
    o_ref[1] = acc[:, BRANCH_WIDTH:].astype(o_ref.dtype)


def _fnet_channel_dft(u):
    b, n, w = u.shape
    j = np.arange(FN_GROUP_DIM)
    ang = 2.0 * np.pi * ((j[:, None] * j[None, :]) % FN_GROUP_DIM) / FN_GROUP_DIM
    eye = np.eye(FN_GROUPS)
    wc = np.concatenate([np.kron(eye, np.cos(ang)), -np.kron(eye, np.sin(ang))], axis=1)
    wc = jnp.asarray(wc, F32).astype(BF16)
    tm = _pick_tile(n, 1024, 8)
    return pl.pallas_call(
        _fn0_kernel,
        grid=(b, n // tm),
        in_specs=[pl.BlockSpec((1, tm, w), lambda bb, i: (bb, i, 0)),
                  pl.BlockSpec((w, 2 * w), lambda bb, i: (0, 0))],
        out_specs=pl.BlockSpec((2, tm, w), lambda bb, i: (0, i, bb)),
        out_shape=jax.ShapeDtypeStruct((2, n, b * w), BF16),
        compiler_params=_params(("parallel", "parallel")),
        name="fnet_channel_dft",
    )(u, wc)


def _fnet_long(u, tabs):
    b, n, w = u.shape
    cw = b * w
    m2 = FN_N2
    m1 = n // m2
    wri = _fnet_channel_dft(u)
    a = _colmm(tabs["fn_s1"], wri.reshape(2 * m1, m2 * cw), BF16)
    y = _bmm(tabs["fn_s3"], a.reshape(2, m1, m2, cw), BF16, cplx_out=False)
    return jnp.transpose(y, (1, 0, 2)).reshape(n, cw)


def _fnet_short(u, tabs):
    b, n, w = u.shape
    wri = _fnet_channel_dft(u)
    return _colmm(tabs["fn"], wri.reshape(2 * n, b * w), BF16, tn_cap=256)


def _merge_kernel(x_ref, g_ref, sc_ref, sh_ref, gt_ref, oa_ref, ob_ref, oc_ref, od_ref,
                  wg_ref, wb_ref, wo_ref, o_ref):
    x = x_ref[0]
    hb = _normmod(x, g_ref[...], sc_ref[0], sh_ref[0]).astype(BF16)
    branches = (oa_ref[0], ob_ref[...], oc_ref[...], od_ref[0])
    acc = jnp.zeros(x.shape, F32)
    for nbr, o in enumerate(branches):
        gate = _dot(hb, wg_ref[:, nbr * D_MODEL:(nbr + 1) * D_MODEL])
        proj = _dot(o.astype(BF16), wb_ref[nbr])
        acc = acc + jax.nn.sigmoid(gate) * proj
    m = _dot(acc.astype(BF16), wo_ref[...])
    o_ref[0] = x + gt_ref[0] * m


def _merge(x, g, sc, sh, gt, oa, ob, oc, od, w):
    b, n, _ = x.shape
    tm = _pick_tile(n, 512, 8)
    tok = lambda bb, i: (bb, i, 0)
    row = lambda bb, i: (bb, 0, 0)
    seq = lambda bb, i: (i, bb)
    c2 = lambda bb, i: (0, 0)
    return pl.pallas_call(
        _merge_kernel,
        grid=(b, n // tm),
        in_specs=[pl.BlockSpec((1, tm, D_MODEL), tok),
                  pl.BlockSpec((1, D_MODEL), c2),
                  pl.BlockSpec((1, 1, D_MODEL), row), pl.BlockSpec((1, 1, D_MODEL), row),
                  pl.BlockSpec((1, 1, D_MODEL), row),
                  pl.BlockSpec((1, tm, BRANCH_WIDTH), tok),
                  pl.BlockSpec((tm, BRANCH_WIDTH), seq),
                  pl.BlockSpec((tm, BRANCH_WIDTH), seq),
                  pl.BlockSpec((1, tm, BRANCH_WIDTH), tok),
                  pl.BlockSpec((D_MODEL, N_BRANCH * D_MODEL), c2),
                  pl.BlockSpec((N_BRANCH, BRANCH_WIDTH, D_MODEL), lambda bb, i: (0, 0, 0)),
                  pl.BlockSpec((D_MODEL, D_MODEL), c2)],
        out_specs=pl.BlockSpec((1, tm, D_MODEL), tok),
        out_shape=jax.ShapeDtypeStruct(x.shape, F32),
        compiler_params=_params(("parallel", "parallel"), 56),
        name="branch_merge",
    )(x, g, sc, sh, gt, oa, ob, oc, od, w["wgate"], w["wbranch"], w["wout"])


def _moe_kernel(x_ref, g_ref, sc_ref, sh_ref, gt_ref, wr_ref, br_ref, wg_ref, wu_ref, wd_ref, fg_ref,
                o_ref, h_s, wt_s, acc_s, *, final):
    e = pl.program_id(2)

    @pl.when(e == 0)
    def _():
        hb = _normmod(x_ref[0], g_ref[...], sc_ref[0], sh_ref[0]).astype(BF16)
        h_s[...] = hb
        r = _dot(hb, wr_ref[...]) + br_ref[...]
        lane = lax.broadcasted_iota(jnp.int32, r.shape, 1)
        neg = jnp.float32(-jnp.inf)
        isg = lane < MOE_GROUPS
        gmax = jnp.max(jnp.where(isg, r, neg), axis=1, keepdims=True)
        gsum = jnp.sum(jnp.where(isg, jnp.exp(r - gmax), 0.0), axis=1, keepdims=True)
        g_p = 1.0 / gsum
        gidx = jnp.min(jnp.where(isg & (r == gmax), lane, ROUTER_LANES), axis=1, keepdims=True)
        ise = ((lane >= MOE_GROUPS) & (lane < MOE_GROUPS + MOE_EXPERTS)
               & (jnp.right_shift(lane - MOE_GROUPS, 2) == gidx))
        el = jnp.where(ise, r, neg)
        e1 = jnp.max(el, axis=1, keepdims=True)
        i1 = jnp.min(jnp.where(ise & (r == e1), lane, ROUTER_LANES), axis=1, keepdims=True)
        el2 = jnp.where(lane == i1, neg, el)
        e2 = jnp.max(el2, axis=1, keepdims=True)
        i2 = jnp.min(jnp.where(el2 == e2, lane, ROUTER_LANES), axis=1, keepdims=True)
        d = jnp.exp(e2 - e1)
        w1 = 1.0 / (1.0 + d)
        w2 = d / (1.0 + d)
        wt_s[...] = jnp.where(lane == i1, g_p * w1, jnp.where(lane == i2, g_p * w2, 0.0))
        acc_s[...] = jnp.zeros(acc_s.shape, F32)

    hb = h_s[...]
    a = _dot(hb, wg_ref[0])
    u = _dot(hb, wu_ref[0])
    wt = wt_s[...]
    lane = lax.broadcasted_iota(jnp.int32, wt.shape, 1)
    wcol = jnp.sum(jnp.where(lane == e + MOE_GROUPS, wt, 0.0), axis=1, keepdims=True)
    act = (a * jax.nn.sigmoid(a)) * u * wcol
    acc_s[...] += _dot(act.astype(BF16), wd_ref[0])

    @pl.when(e == MOE_EXPERTS - 1)
    def _():
        y = x_ref[0] + gt_ref[0] * acc_s[...]
        if final:
            y = _rms(y) * fg_ref[...]
        o_ref[0] = y


def _moe(x, g, sc, sh, gt, w, fg, final):
    b, n, _ = x.shape
    tm = _pick_tile(n, 1024, 8)
    tok = lambda bb, i, e: (bb, i, 0)
    row = lambda bb, i, e: (bb, 0, 0)
    c2 = lambda bb, i, e: (0, 0)
    ex = lambda bb, i, e: (e, 0, 0)
    return pl.pallas_call(
        functools.partial(_moe_kernel, final=final),
        grid=(b, n // tm, MOE_EXPERTS),
        in_specs=[pl.BlockSpec((1, tm, D_MODEL), tok),
                  pl.BlockSpec((1, D_MODEL), c2),
                  pl.BlockSpec((1, 1, D_MODEL), row), pl.BlockSpec((1, 1, D_MODEL), row),
                  pl.BlockSpec((1, 1, D_MODEL), row),
                  pl.BlockSpec((D_MODEL, ROUTER_LANES), c2), pl.BlockSpec((1, ROUTER_LANES), c2),
                  pl.BlockSpec((1, D_MODEL, MOE_HIDDEN), ex),
                  pl.BlockSpec((1, D_MODEL, MOE_HIDDEN), ex),
                  pl.BlockSpec((1, MOE_HIDDEN, D_MODEL), ex),
                  pl.BlockSpec((1, D_MODEL), c2)],
        out_specs=pl.BlockSpec((1, tm, D_MODEL), tok),
        out_shape=jax.ShapeDtypeStruct(x.shape, F32),
        scratch_shapes=[pltpu.VMEM((tm, D_MODEL), BF16), pltpu.VMEM((tm, ROUTER_LANES), F32),
                        pltpu.VMEM((tm, D_MODEL), F32)],
        compiler_params=_params(("parallel", "parallel", "arbitrary"), 48),
        name="hier_moe",
    )(x, g, sc, sh, gt, w["wr"], w["br"], w["wg"], w["wu"], w["wd"], fg)


def _layer_weights(l, w_in, mla_q_norm, mla_w_uq, mla_kv_norm, mla_w_ukv, w_branch, w_out,
                   moe_w_group, moe_b_group, moe_w_expert, moe_b_expert, moe_w_gate, moe_w_up, moe_w_down):
    wi = w_in[l]
    b0 = MLA_Q_RANK
    b1 = b0 + MLA_KV_RANK
    b2 = b1 + MLA_ROPE
    b3 = b2 + HY_IN
    b4 = b3 + BRANCH_WIDTH
    b5 = b4 + 3 * BRANCH_WIDTH
    zpad = lambda r, c: jnp.zeros((r, c), F32)
    w_kr = wi[:, b1:b2]
    krp = jnp.concatenate([zpad(D_MODEL, MLA_NOPE), w_kr[:, 0::2], w_kr[:, 1::2],
                           zpad(D_MODEL, MLA_HEAD_PAD - MLA_NOPE - MLA_ROPE)], axis=1)
    w_na = wi[:, b4:b5]
    w_na = jnp.concatenate([w_na[:, :BRANCH_WIDTH] * NA_SCALE, w_na[:, BRANCH_WIDTH:]], axis=1)
    wa = jnp.concatenate([wi[:, :b1], krp, wi[:, b2:b4], w_na], axis=1).astype(BF16)
    uq = mla_w_uq[l].reshape(MLA_Q_RANK, MLA_HEADS, MLA_NOPE + MLA_ROPE) * (MLA_SCALE * LOG2E)
    uq = jnp.concatenate([uq[..., :MLA_NOPE], uq[..., MLA_NOPE::2], uq[..., MLA_NOPE + 1::2],
                          jnp.zeros((MLA_Q_RANK, MLA_HEADS, MLA_HEAD_PAD - MLA_NOPE - MLA_ROPE), F32)], axis=-1)
    ukv = mla_w_ukv[l].reshape(MLA_KV_RANK, MLA_HEADS, MLA_NOPE + MLA_V)
    uk = jnp.concatenate([ukv[..., :MLA_NOPE],
                          jnp.zeros((MLA_KV_RANK, MLA_HEADS, MLA_HEAD_PAD - MLA_NOPE), F32)], axis=-1)
    uv = ukv[..., MLA_NOPE:]
    wr = jnp.concatenate([moe_w_group[l], moe_w_expert[l],
                          zpad(D_MODEL, ROUTER_LANES - MOE_GROUPS - MOE_EXPERTS)], axis=1)
    br = jnp.concatenate([moe_b_group[l], moe_b_expert[l],
                          jnp.zeros((ROUTER_LANES - MOE_GROUPS - MOE_EXPERTS,), F32)])[None, :]
    return dict(
        wa=wa, qn=mla_q_norm[l][None, :], kvn=mla_kv_norm[l][None, :],
        wuq=uq.reshape(MLA_Q_RANK, -1).astype(BF16), wuk=uk.reshape(MLA_KV_RANK, -1).astype(BF16),
        wuv=uv.reshape(MLA_KV_RANK, -1).astype(BF16),
        wgate=wi[:, b5:].astype(BF16), wbranch=w_branch[l].astype(BF16), wout=w_out[l].astype(BF16),
        wr=wr.astype(BF16), br=br, wg=moe_w_gate[l].astype(BF16), wu=moe_w_up[l].astype(BF16),
        wd=moe_w_down[l].astype(BF16))


def _rope_tables(n):
    t = jnp.arange(n, dtype=jnp.int32)
    row = (t // GRID_W).astype(F32)
    col = (t % GRID_W).astype(F32)
    n_freq = MLA_ROPE // 4
    inv = ROPE_BASE ** (-jnp.arange(n_freq, dtype=F32) / n_freq)
    ang = jnp.concatenate([row[:, None] * inv, col[:, None] * inv], axis=-1)
    c, s = jnp.cos(ang), jnp.sin(ang)
    one = jnp.ones((n, MLA_NOPE), F32)
    zero = jnp.zeros((n, MLA_NOPE), F32)
    hr = MLA_ROPE // 2
    tail1 = jnp.ones((n, MLA_HEAD_PAD - MLA_NOPE - MLA_ROPE), F32)
    tail0 = jnp.zeros((n, MLA_HEAD_PAD - MLA_NOPE - MLA_ROPE), F32)
    cos = jnp.concatenate([one, c, c, tail1], axis=1)
    s1 = jnp.concatenate([zero, -s, jnp.zeros((n, hr), F32), tail0], axis=1)
    s2 = jnp.concatenate([zero, jnp.zeros((n, hr), F32), s, tail0], axis=1)
    return cos, s1, s2


def _identity_rope(n):
    return (jnp.ones((n, MLA_HEAD_PAD), F32), jnp.zeros((n, MLA_HEAD_PAD), F32),
            jnp.zeros((n, MLA_HEAD_PAD), F32))


def kernel(x, c, ctx, c_ctx, ada_w, ada_b, norm1_g, norm2_g, w_in, mla_q_norm, mla_w_uq, mla_kv_norm, mla_w_ukv, hy_short_w, hy_w1, hy_b1, hy_freq1, hy_w2, hy_b2, hy_freq2, hy_w3, hy_bias, na_rpb, w_branch, w_out, moe_w_group, moe_b_group, moe_w_expert, moe_b_expert, moe_w_gate, moe_w_up, moe_w_down, final_norm_g):
    b, n_lat, _ = x.shape
    n_ctx = ctx.shape[1]
    assert b <= 4 and n_lat % (NA_QROWS * GRID_W) == 0 and n_lat // GRID_W >= 2 * NA_QROWS

    cvec = jnp.concatenate([c, c_ctx[None, :], jnp.zeros((8 - b - 1, D_MODEL), F32)], axis=0)
    mod = _mod_vectors(cvec, ada_w, ada_b)

    rope_lat = _rope_tables(n_lat)
    rope_ctx = _identity_rope(n_ctx)
    tabs_lat = _dft_tables(n_lat)
    tabs_ctx = _small_dft_tables(n_ctx)
    fg = final_norm_g[None, :]

    for l in range(DEPTH):
        last = l == DEPTH - 1
        w = _layer_weights(l, w_in, mla_q_norm, mla_w_uq, mla_kv_norm, mla_w_ukv, w_branch, w_out,
                           moe_w_group, moe_b_group, moe_w_expert, moe_b_expert,
                           moe_w_gate, moe_w_up, moe_w_down)
        ml = mod[l, :b].reshape(b, 1, 6, D_MODEL)
        mc = jnp.broadcast_to(mod[l, b].reshape(1, 1, 6, D_MODEL), (b, 1, 6, D_MODEL))
        sh1, sc1, gt1, sh2, sc2, gt2 = [ml[:, :, i] for i in range(6)]
        sh1c, sc1c, gt1c, sh2c, sc2c, gt2c = [mc[:, :, i] for i in range(6)]
        g1 = norm1_g[l][None, :]
        g2 = norm2_g[l][None, :]
        fw = dict(
            w1=jnp.concatenate([hy_w1[l], jnp.zeros((HY_EMB_PAD - HY_EMB, HY_HIDDEN), F32)], axis=0).astype(BF16),
            b1=hy_b1[l][None, :], f1=hy_freq1[l][None, :], w2=hy_w2[l].astype(BF16), b2=hy_b2[l][None, :],
            f2=hy_freq2[l][None, :], w3=hy_w3[l].astype(BF16))

        q, k, v, hy, fn, naq, nak, nav = _inproj(x, g1, sc1, sh1, w, rope_lat)
        qc, kc, vc, hyc, fnc, naqc, nakc, navc = _inproj(ctx, g1, sc1c, sh1c, w, rope_ctx)

        oa = _flash(q, jnp.concatenate([k, kc], axis=2), jnp.concatenate([v, vc], axis=2), True)
        od = _na(naq, nak, nav, nakc, navc, _na_bias_tables(na_rpb[l]))
        ob = _hyena_long(hy, hy_short_w[l], fw, hy_bias[l], tabs_lat)
        oc = _fnet_long(fn, tabs_lat)
        x_new = _merge(x, g1, sc1, sh1, gt1, oa, ob, oc, od, w)

        if not last:
            oa_c = _flash(qc, kc, vc, True)
            heads = lambda t: jnp.transpose(t.reshape(b, n_ctx, NA_HEADS, NA_HEAD_DIM), (0, 2, 1, 3))
            od_c = _flash(heads(naqc), heads(nakc), heads(navc), False)
            ob_c = _hyena_short(hyc, hy_short_w[l], fw, hy_bias[l], tabs_ctx)
            oc_c = _fnet_short(fnc, tabs_ctx)
            ctx = _merge(ctx, g1, sc1c, sh1c, gt1c, oa_c, ob_c, oc_c, od_c, w)
            ctx = _moe(ctx, g2, sc2c, sh2c, gt2c, w, fg, False)
        x = _moe(x_new, g2, sc2, sh2, gt2, w, fg, last)
    return x
```

```python
import functools
import math

import numpy as np
import jax
import jax.numpy as jnp
from jax import lax
from jax.experimental import pallas as pl
from jax.experimental.pallas import tpu as pltpu

F32 = jnp.float32
BF16 = jnp.bfloat16

D_MODEL = 1024
DEPTH = 2
GRID_W = 64
NORM_EPS = 1e-6
MASK_VALUE = -1e30
BRANCH_WIDTH = D_MODEL // 4
N_BRANCH = 4

MLA_HEADS = 4
MLA_Q_RANK = D_MODEL // 4
MLA_KV_RANK = D_MODEL // 8
MLA_NOPE = 64
MLA_ROPE = 32
MLA_V = BRANCH_WIDTH // MLA_HEADS
MLA_SCALE = 1.0 / math.sqrt(MLA_NOPE + MLA_ROPE)
MLA_HEAD_PAD = 128
ROPE_BASE = 10000.0
LOG2E = 1.4426950408889634

HY_WIDTH = BRANCH_WIDTH
HY_ORDER = 2
HY_BANDS = 16
HY_EMB = 1 + 2 * HY_BANDS
HY_EMB_PAD = 128
HY_HIDDEN = 64
HY_DECAY_TARGET = 1e-2
HY_DECAY_SHORT_PCT = 0.3
HY_DECAY_LONG_PCT = 1.5
HY_IN = (HY_ORDER + 1) * HY_WIDTH

FN_GROUPS = 4
FN_GROUP_DIM = BRANCH_WIDTH // FN_GROUPS

NA_HEADS = 4
NA_HEAD_DIM = BRANCH_WIDTH // NA_HEADS
NA_KH = 8
NA_KW = 16
NA_SCALE = 1.0 / math.sqrt(NA_HEAD_DIM)
NA_QROWS = 8
NA_KROWS = 16

MOE_GROUPS = 4
MOE_PER_GROUP = 4
MOE_EXPERTS = MOE_GROUPS * MOE_PER_GROUP
MOE_HIDDEN = 256
ROUTER_LANES = 128

MLA_IN = MLA_Q_RANK + MLA_KV_RANK + MLA_ROPE
MIX_IN = MLA_IN + HY_IN + BRANCH_WIDTH + 3 * BRANCH_WIDTH
MIX_PAD = MLA_Q_RANK + MLA_KV_RANK + MLA_HEAD_PAD + HY_IN + BRANCH_WIDTH + 3 * BRANCH_WIDTH

DFT_N2 = 256
FN_N2 = 128
FLASH_Q_SUB = 512


def _params(sem, vmem_mb=None):
    kw = dict(dimension_semantics=sem)
    if vmem_mb is not None:
        kw["vmem_limit_bytes"] = vmem_mb << 20
    return pltpu.CompilerParams(**kw)


def _pick_tile(n, cap, mult):
    best = None
    for t in range(mult, min(n, cap) + 1, mult):
        if n % t == 0:
            best = t
    return best if best is not None else n


def _rms(x):
    return x * lax.rsqrt(jnp.mean(x * x, axis=-1, keepdims=True) + NORM_EPS)


def _normmod(x, g, sc, sh):
    return (_rms(x) * g) * (1.0 + sc) + sh


def _dot(a, b):
    return jnp.dot(a, b, preferred_element_type=F32)


def _dot_nt(a, b):
    return lax.dot_general(a, b, (((1,), (1,)), ((), ())), preferred_element_type=F32)


def _mod_kernel(c_ref, w_ref, b_ref, o_ref):
    c = c_ref[...]
    s = c * jax.nn.sigmoid(c)
    o_ref[0] = _dot(s.astype(BF16), w_ref[0].astype(BF16)) + b_ref[0]


def _mod_vectors(cvec, ada_w, ada_b):
    n6 = ada_w.shape[-1]
    tn = 1024
    return pl.pallas_call(
        _mod_kernel,
        grid=(DEPTH, n6 // tn),
        in_specs=[pl.BlockSpec((8, D_MODEL), lambda l, j: (0, 0)),
                  pl.BlockSpec((1, D_MODEL, tn), lambda l, j: (l, 0, j)),
                  pl.BlockSpec((1, 1, tn), lambda l, j: (l, 0, j))],
        out_specs=pl.BlockSpec((1, 8, tn), lambda l, j: (l, 0, j)),
        out_shape=jax.ShapeDtypeStruct((DEPTH, 8, n6), F32),
        compiler_params=_params(("parallel", "parallel")),
        name="adaln_vectors",
    )(cvec, ada_w, ada_b.reshape(DEPTH, 1, n6))


def _inproj_kernel(x_ref, g_ref, sc_ref, sh_ref, wa_ref, qn_ref, kvn_ref, wuq_ref, wuk_ref, wuv_ref,
                   cos_ref, s1_ref, s2_ref,
                   q_ref, k_ref, v_ref, hy_ref, fn_ref, naq_ref, nak_ref, nav_ref):
    h = _normmod(x_ref[0], g_ref[...], sc_ref[0], sh_ref[0]).astype(BF16)
    p = _dot(h, wa_ref[...])
    c0 = MLA_Q_RANK
    c1 = c0 + MLA_KV_RANK
    c2 = c1 + MLA_HEAD_PAD
    c3 = c2 + HY_IN
    c4 = c3 + BRANCH_WIDTH
    c5 = c4 + BRANCH_WIDTH
    c6 = c5 + BRANCH_WIDTH
    cq, ckv, krp = p[:, :c0], p[:, c0:c1], p[:, c1:c2]
    hy_ref[0] = p[:, c2:c3]
    fn_ref[0] = p[:, c3:c4]
    naq_ref[0] = p[:, c4:c5].astype(BF16)
    nak_ref[0] = p[:, c5:c6].astype(BF16)
    nav_ref[0] = p[:, c6:].astype(BF16)
    cqn = (_rms(cq) * qn_ref[...]).astype(BF16)
    ckn = (_rms(ckv) * kvn_ref[...]).astype(BF16)
    q = _dot(cqn, wuq_ref[...])
    kn = _dot(ckn, wuk_ref[...])
    v = _dot(ckn, wuv_ref[...])
    cos, s1, s2 = cos_ref[...], s1_ref[...], s2_ref[...]

    def rope(t):
        return (t * cos + pltpu.roll(t, MLA_HEAD_PAD - MLA_ROPE // 2, 1) * s1
                + pltpu.roll(t, MLA_ROPE // 2, 1) * s2)

    kr = rope(krp)
    for hh in range(MLA_HEADS):
        sl = slice(hh * MLA_HEAD_PAD, (hh + 1) * MLA_HEAD_PAD)
        q_ref[0, hh] = rope(q[:, sl]).astype(BF16)
        k_ref[0, hh] = (kn[:, sl] + kr).astype(BF16)
        v_ref[0, hh] = v[:, hh * MLA_V:(hh + 1) * MLA_V].astype(BF16)


def _inproj(x, g, sc, sh, w, rope_tabs):
    b, n, _ = x.shape
    tm = _pick_tile(n, 512, 8)
    cos, s1, s2 = rope_tabs
    tok = lambda bb, i: (bb, i, 0)
    row = lambda bb, i: (bb, 0, 0)
    const2 = lambda bb, i: (0, 0)
    hd = lambda bb, i: (bb, 0, i, 0)
    tab = lambda bb, i: (i, 0)
    hp = MLA_HEADS * MLA_HEAD_PAD
    outs = pl.pallas_call(
        _inproj_kernel,
        grid=(b, n // tm),
        in_specs=[pl.BlockSpec((1, tm, D_MODEL), tok),
                  pl.BlockSpec((1, D_MODEL), const2),
                  pl.BlockSpec((1, 1, D_MODEL), row),
                  pl.BlockSpec((1, 1, D_MODEL), row),
                  pl.BlockSpec((D_MODEL, MIX_PAD), const2),
                  pl.BlockSpec((1, MLA_Q_RANK), const2),
                  pl.BlockSpec((1, MLA_KV_RANK), const2),
                  pl.BlockSpec((MLA_Q_RANK, hp), const2),
                  pl.BlockSpec((MLA_KV_RANK, hp), const2),
                  pl.BlockSpec((MLA_KV_RANK, BRANCH_WIDTH), const2),
                  pl.BlockSpec((tm, MLA_HEAD_PAD), tab),
                  pl.BlockSpec((tm, MLA_HEAD_PAD), tab),
                  pl.BlockSpec((tm, MLA_HEAD_PAD), tab)],
        out_specs=[pl.BlockSpec((1, MLA_HEADS, tm, MLA_HEAD_PAD), hd),
                   pl.BlockSpec((1, MLA_HEADS, tm, MLA_HEAD_PAD), hd),
                   pl.BlockSpec((1, MLA_HEADS, tm, MLA_V), hd),
                   pl.BlockSpec((1, tm, HY_IN), tok),
                   pl.BlockSpec((1, tm, BRANCH_WIDTH), tok),
                   pl.BlockSpec((1, tm, BRANCH_WIDTH), tok),
                   pl.BlockSpec((1, tm, BRANCH_WIDTH), tok),
                   pl.BlockSpec((1, tm, BRANCH_WIDTH), tok)],
        out_shape=[jax.ShapeDtypeStruct((b, MLA_HEADS, n, MLA_HEAD_PAD), BF16),
                   jax.ShapeDtypeStruct((b, MLA_HEADS, n, MLA_HEAD_PAD), BF16),
                   jax.ShapeDtypeStruct((b, MLA_HEADS, n, MLA_V), BF16),
                   jax.ShapeDtypeStruct((b, n, HY_IN), F32),
                   jax.ShapeDtypeStruct((b, n, BRANCH_WIDTH), F32),
                   jax.ShapeDtypeStruct((b, n, BRANCH_WIDTH), BF16),
                   jax.ShapeDtypeStruct((b, n, BRANCH_WIDTH), BF16),
                   jax.ShapeDtypeStruct((b, n, BRANCH_WIDTH), BF16)],
        compiler_params=_params(("parallel", "parallel"), 48),
        name="input_projection",
    )(x, g, sc, sh, w["wa"], w["qn"], w["kvn"], w["wuq"], w["wuk"], w["wuv"], cos, s1, s2)
    return outs


def _flash_kernel(q_ref, k_ref, v_ref, o_ref, m_ref, l_ref, acc_ref, *, nh, use_exp2):
    j = pl.program_id(2)

    @pl.when(j == 0)
    def _():
        m_ref[...] = jnp.full(m_ref.shape, MASK_VALUE, F32)
        l_ref[...] = jnp.zeros(l_ref.shape, F32)
        acc_ref[...] = jnp.zeros(acc_ref.shape, F32)

    ex = jnp.exp2 if use_exp2 else jnp.exp
    for h in range(nh):
        s = _dot_nt(q_ref[0, h], k_ref[0, h])
        m_prev = m_ref[h]
        m_cur = jnp.maximum(m_prev, jnp.max(s, axis=1, keepdims=True))
        alpha = ex(m_prev - m_cur)
        p = ex(s - m_cur)
        l_ref[h] = alpha * l_ref[h] + jnp.sum(p, axis=1, keepdims=True)
        acc_ref[h] = alpha * acc_ref[h] + _dot(p.astype(BF16), v_ref[0, h])
        m_ref[h] = m_cur

    @pl.when(j == pl.num_programs(2) - 1)
    def _():
        outs = [acc_ref[h] / l_ref[h] for h in range(nh)]
        o_ref[0] = jnp.concatenate(outs, axis=1).astype(o_ref.dtype)


def _flash(q, k, v, use_exp2):
    b, nh, lq, dk = q.shape
    lk, dv = k.shape[2], v.shape[3]
    tq = _pick_tile(lq, 512, 8)
    tk = _pick_tile(lk, 1280, 128)
    return pl.pallas_call(
        functools.partial(_flash_kernel, nh=nh, use_exp2=use_exp2),
        grid=(b, lq // tq, lk // tk),
        in_specs=[pl.BlockSpec((1, nh, tq, dk), lambda bb, i, j: (bb, 0, i, 0)),
                  pl.BlockSpec((1, nh, tk, dk), lambda bb, i, j: (bb, 0, j, 0)),
                  pl.BlockSpec((1, nh, tk, dv), lambda bb, i, j: (bb, 0, j, 0))],
        out_specs=pl.BlockSpec((1, tq, nh * dv), lambda bb, i, j: (bb, i, 0)),
        out_shape=jax.ShapeDtypeStruct((b, lq, nh * dv), BF16),
        scratch_shapes=[pltpu.VMEM((nh, tq, 1), F32), pltpu.VMEM((nh, tq, 1), F32),
                        pltpu.VMEM((nh, tq, dv), F32)],
        compiler_params=_params(("parallel", "parallel", "arbitrary"), 48),
        name="softmax_attention",
    )(q, k, v)


def _flash_t_kernel(q_ref, k_ref, vt_ref, o_ref, m_ref, acc_ref, *, nh, dv):
    j = pl.program_id(2)

    @pl.when(j == 0)
    def _():
        m_ref[...] = jnp.full(m_ref.shape, MASK_VALUE, F32)
        acc_ref[...] = jnp.zeros(acc_ref.shape, F32)

    rows = vt_ref.shape[2]

    tq = q_ref.shape[2]
    qs = min(tq, FLASH_Q_SUB)
    units = [(h, c) for h in range(nh) for c in range(tq // qs)]

    def scores(h, c):
        cols = slice(c * qs, (c + 1) * qs)
        st = _dot_nt(k_ref[0, h], q_ref[0, h, cols, :])
        m_prev = m_ref[h, :, cols]
        m_cur = jnp.maximum(m_prev, jnp.max(st, axis=0, keepdims=True))
        m_ref[h, :, cols] = m_cur
        return st, m_prev, m_cur

    def values(h, c, st, m_prev, m_cur):
        cols = slice(c * qs, (c + 1) * qs)
        alpha = jnp.exp2(m_prev - m_cur)
        p = jnp.exp2(st - m_cur).astype(BF16)
        acc_ref[h, :rows, cols] = alpha * acc_ref[h, :rows, cols] + _dot(vt_ref[0, h], p)

    pending = scores(*units[0])
    for i, u in enumerate(units):
        nxt = scores(*units[i + 1]) if i + 1 < len(units) else None
        values(*u, *pending)
        pending = nxt

    @pl.when(j == pl.num_programs(2) - 1)
    def _():
        outs = []
        for h in range(nh):
            a = acc_ref[h]
            o = (a / a[dv:dv + 1]).T
            outs.append(o[:, :dv])
        o_ref[0] = jnp.concatenate(outs, axis=1).astype(o_ref.dtype)


def _flash_t(q, k, v):
    b, nh, lq, dk = q.shape
    lk, dv = k.shape[2], v.shape[3]
    rows = dv + 16
    vt = jnp.concatenate([jnp.swapaxes(v, 2, 3), jnp.ones((b, nh, 1, lk), BF16),
                          jnp.zeros((b, nh, rows - dv - 1, lk), BF16)], axis=2)
    tq = _pick_tile(lq, 1024, 128)
    tk = _pick_tile(lk, 1280, 128)
    return pl.pallas_call(
        functools.partial(_flash_t_kernel, nh=nh, dv=dv),
        grid=(b, lq // tq, lk // tk),
        in_specs=[pl.BlockSpec((1, nh, tq, dk), lambda bb, i, j: (bb, 0, i, 0)),
                  pl.BlockSpec((1, nh, tk, dk), lambda bb, i, j: (bb, 0, j, 0)),
                  pl.BlockSpec((1, nh, rows, tk), lambda bb, i, j: (bb, 0, 0, j))],
        out_specs=pl.BlockSpec((1, tq, nh * dv), lambda bb, i, j: (bb, i, 0)),
        out_shape=jax.ShapeDtypeStruct((b, lq, nh * dv), BF16),
        scratch_shapes=[pltpu.VMEM((nh, 1, tq), F32), pltpu.VMEM((nh, 128, tq), F32)],
        compiler_params=_params(("parallel", "parallel", "arbitrary"), 48),
        name="softmax_attention_t",
    )(q, k, vt)


def _na_kernel(q_ref, k0, k1, k2, k3, v0, v1, v2, v3, kc_ref, vc_ref, b_ref, o_ref):
    q = q_ref[0]
    kcat = jnp.concatenate([k0[0], k1[0], k2[0], k3[0]], axis=0)
    vcat = jnp.concatenate([v0[0], v1[0], v2[0], v3[0]], axis=0)
    kc, vc = kc_ref[0], vc_ref[0]
    outs = []
    for h in range(NA_HEADS):
        sl = slice(h * NA_HEAD_DIM, (h + 1) * NA_HEAD_DIM)
        qh = q[:, sl]
        s = _dot_nt(qh, kcat[:, sl]) + b_ref[0, h]
        sc = _dot_nt(qh, kc[:, sl])
        m = jnp.maximum(jnp.max(s, axis=1, keepdims=True), jnp.max(sc, axis=1, keepdims=True))
        p = jnp.exp(s - m)
        pc = jnp.exp(sc - m)
        l = jnp.sum(p, axis=1, keepdims=True) + jnp.sum(pc, axis=1, keepdims=True)
        o = _dot(p.astype(BF16), vcat[:, sl]) + _dot(pc.astype(BF16), vc[:, sl])
        outs.append(o / l)
    o_ref[0] = jnp.concatenate(outs, axis=1).astype(o_ref.dtype)


def _na_bias_tables(rpb):
    j = np.arange(NA_QROWS)[:, None]
    m = np.arange(NA_KROWS)[None, :]
    valid = np.zeros((3, NA_QROWS, NA_KROWS), bool)
    st0 = np.maximum(j - NA_KH // 2, 0)
    valid[0] = (m - 4 >= st0) & (m - 4 < st0 + NA_KH)
    valid[1] = (m - j >= 0) & (m - j < NA_KH)
    st2 = 4 + np.minimum(j - NA_KH // 2, 0)
    valid[2] = (m < 12) & (m >= st2) & (m < st2 + NA_KH)
    dr_idx = np.clip(m - 4 - j + NA_KH - 1, 0, 2 * NA_KH - 2)
    qc = np.arange(GRID_W)[:, None]
    kc = np.arange(GRID_W)[None, :]
    dc_idx = np.clip(kc - qc, -(NA_KW - 1), NA_KW - 1) + NA_KW - 1
    cs = np.clip(qc - NA_KW // 2, 0, GRID_W - NA_KW)
    col_ok = (kc >= cs) & (kc < cs + NA_KW)
    t = rpb.astype(F32)[:, dr_idx]
    t = t[..., dc_idx]
    t = jnp.transpose(t, (0, 1, 3, 2, 4))
    ok = valid[:, None, :, None, :, None] & col_ok[None, None, None, :, None, :]
    out = jnp.where(jnp.asarray(ok), t[None], MASK_VALUE)
    return out.reshape(3, NA_HEADS, NA_QROWS * GRID_W, NA_KROWS * GRID_W)


def _na(q, k, v, kc, vc, bias):
    b, n, w = q.shape
    c = kc.shape[1]
    tq = NA_QROWS * GRID_W
    tkb = 4 * GRID_W
    nb = n // tq
    nkb = n // tkb
    kmaps = [lambda bb, i: (bb, jnp.maximum(2 * i - 1, 0), 0),
             lambda bb, i: (bb, 2 * i, 0),
             lambda bb, i: (bb, 2 * i + 1, 0),
             lambda bb, i: (bb, jnp.minimum(2 * i + 2, nkb - 1), 0)]
    kspecs = [pl.BlockSpec((1, tkb, w), mp) for mp in kmaps]
    return pl.pallas_call(
        _na_kernel,
        grid=(b, nb),
        in_specs=[pl.BlockSpec((1, tq, w), lambda bb, i: (bb, i, 0))] + kspecs + kspecs
                 + [pl.BlockSpec((1, c, w), lambda bb, i: (bb, 0, 0)),
                    pl.BlockSpec((1, c, w), lambda bb, i: (bb, 0, 0)),
                    pl.BlockSpec((1, NA_HEADS, tq, NA_KROWS * GRID_W),
                                 lambda bb, i: (jnp.where(i == 0, 0, jnp.where(i == nb - 1, 2, 1)), 0, 0, 0))],
        out_specs=pl.BlockSpec((1, tq, w), lambda bb, i: (bb, i, 0)),
        out_shape=jax.ShapeDtypeStruct((b, n, w), BF16),
        compiler_params=_params(("parallel", "arbitrary"), 56),
        name="neighbourhood_attention",
    )(q, k, k, k, k, v, v, v, v, kc, vc, bias)


def _sconv_kernel(u_ref, pv_ref, nx_ref, w_ref, v_ref, x1_ref, x2_ref, *, nt):
    i = pl.program_id(1)
    u = u_ref[0]
    tl = u.shape[0]
    prev = jnp.where(i > 0, pv_ref[0, 7:8, :], 0.0)
    nxt = jnp.where(i < nt - 1, nx_ref[0, 0:1, :], 0.0)
    rows = lax.broadcasted_iota(jnp.int32, u.shape, 0)
    um = jnp.where(rows == 0, prev, pltpu.roll(u, 1, 0))
    up = jnp.where(rows == tl - 1, nxt, pltpu.roll(u, tl - 1, 0))
    w = w_ref[...]
    y = um * w[0:1] + u * w[1:2] + up * w[2:3]
    v_ref[...] = y[:, :HY_WIDTH]
    x1_ref[...] = y[:, HY_WIDTH:2 * HY_WIDTH]
    x2_ref[...] = y[:, 2 * HY_WIDTH:]


def _sconv(u, w):
    b, n, _ = u.shape
    tl = _pick_tile(n, 512, 8)
    nt = n // tl
    r8 = tl // 8
    osd = jax.ShapeDtypeStruct((n, b * HY_WIDTH), F32)
    ospec = pl.BlockSpec((tl, HY_WIDTH), lambda bb, i: (i, bb))
    return pl.pallas_call(
        functools.partial(_sconv_kernel, nt=nt),
        grid=(b, nt),
        in_specs=[pl.BlockSpec((1, tl, HY_IN), lambda bb, i: (bb, i, 0)),
                  pl.BlockSpec((1, 8, HY_IN), lambda bb, i: (bb, jnp.maximum(i * r8 - 1, 0), 0)),
                  pl.BlockSpec((1, 8, HY_IN), lambda bb, i: (bb, jnp.minimum((i + 1) * r8, n // 8 - 1), 0)),
                  pl.BlockSpec((3, HY_IN), lambda bb, i: (0, 0))],
        out_specs=[ospec, ospec, ospec],
        out_shape=[osd, osd, osd],
        compiler_params=_params(("parallel", "parallel")),
        name="hyena_short_conv",
    )(u, u, u, w)


def _filt_kernel(z_ref, w1_ref, b1_ref, f1_ref, w2_ref, b2_ref, f2_ref, w3_ref, win_ref, h_ref, ss_ref):
    i = pl.program_id(0)
    h = jnp.sin(f1_ref[...] * (_dot(z_ref[...].astype(BF16), w1_ref[...]) + b1_ref[...]))
    h = jnp.sin(f2_ref[...] * (_dot(h.astype(BF16), w2_ref[...]) + b2_ref[...]))
    h = _dot(h.astype(BF16), w3_ref[...])
    win = win_ref[...]
    h = h * jnp.concatenate([win] * (2 * HY_ORDER), axis=1)
    h_ref[...] = h
    rows = lax.broadcasted_iota(jnp.int32, h.shape, 0)
    cols = lax.broadcasted_iota(jnp.int32, h.shape, 1)
    drop = (rows == 0) & (cols >= HY_ORDER * HY_WIDTH) & (i == 0)
    ss = jnp.sum(jnp.where(drop, 0.0, h * h), axis=0, keepdims=True)

    @pl.when(i == 0)
    def _():
        ss_ref[...] = ss

    @pl.when(i > 0)
    def _():
        ss_ref[...] += ss


def _hyena_filters(n, fw):
    t = jnp.linspace(0.0, 1.0, n, dtype=F32)[:, None]
    bands = jnp.linspace(1e-4, HY_BANDS - 1, HY_BANDS, dtype=F32)
    ang = (2.0 * math.pi / n) * jnp.arange(n, dtype=F32)[:, None] * bands
    z = jnp.concatenate([t, jnp.cos(ang), -jnp.sin(ang), jnp.zeros((n, HY_EMB_PAD - HY_EMB), F32)], axis=-1)
    deltas = jnp.linspace(math.log(HY_DECAY_TARGET) / HY_DECAY_LONG_PCT,
                          math.log(HY_DECAY_TARGET) / HY_DECAY_SHORT_PCT, HY_WIDTH, dtype=F32)
    window = jnp.exp(-t * jnp.abs(deltas))
    tl = _pick_tile(n, 512, 8)
    nf = 2 * HY_ORDER * HY_WIDTH
    c2 = lambda i: (0, 0)
    return pl.pallas_call(
        _filt_kernel,
        grid=(n // tl,),
        in_specs=[pl.BlockSpec((tl, HY_EMB_PAD), lambda i: (i, 0)),
                  pl.BlockSpec((HY_EMB_PAD, HY_HIDDEN), c2), pl.BlockSpec((1, HY_HIDDEN), c2),
                  pl.BlockSpec((1, HY_HIDDEN), c2),
                  pl.BlockSpec((HY_HIDDEN, HY_HIDDEN), c2), pl.BlockSpec((1, HY_HIDDEN), c2),
                  pl.BlockSpec((1, HY_HIDDEN), c2),
                  pl.BlockSpec((HY_HIDDEN, nf), c2),
                  pl.BlockSpec((tl, HY_WIDTH), lambda i: (i, 0))],
        out_specs=[pl.BlockSpec((tl, nf), lambda i: (i, 0)), pl.BlockSpec((1, nf), c2)],
        out_shape=[jax.ShapeDtypeStruct((n, nf), F32), jax.ShapeDtypeStruct((1, nf), F32)],
        compiler_params=_params(("arbitrary",)),
        name="hyena_filter_mlp",
    )(z, fw["w1"], fw["b1"], fw["f1"], fw["w2"], fw["b2"], fw["f2"], fw["w3"], window)


def _padded_filter(hf, ss):
    half = HY_ORDER * HY_WIDTH
    g = jnp.concatenate([hf[:, :half], jnp.zeros((1, half), F32), jnp.flip(hf[1:, half:], axis=0)], axis=0)
    scale = lax.rsqrt(ss[:, :half] + ss[:, half:] + NORM_EPS)
    return g, scale


def _colmm_kernel(*refs, epi):
    w_ref, x_ref = refs[0], refs[1]
    o_ref = refs[-1]
    acc = _dot(w_ref[...], x_ref[...].astype(BF16))
    if epi == "gate":
        xg_ref, z_ref, b_ref = refs[2:5]
        acc = xg_ref[...] * (acc + z_ref[...] * b_ref[...])
    elif epi == "scale":
        acc = acc * refs[2][...]
    elif epi == "cmul":
        g = refs[2][...]
        mh = acc.shape[0] // 2
        ar, ai, gr, gi = acc[:mh], acc[mh:], g[:mh], g[mh:]
        acc = jnp.concatenate([ar * gr - ai * gi, ar * gi + ai * gr], axis=0)
    o_ref[...] = acc.astype(o_ref.dtype)


def _colmm(w, x, out_dtype, epi="none", extra=(), tn_cap=4096, g_col=None):
    m, k = w.shape
    nc = x.shape[1]
    tn = _pick_tile(nc, tn_cap, 128)
    col = lambda j: (0, j)
    specs = [pl.BlockSpec((m, k), lambda j: (0, 0)), pl.BlockSpec((k, tn), col)]
    if epi == "gate":
        specs += [pl.BlockSpec((m, tn), col), pl.BlockSpec((m, tn), col), pl.BlockSpec((1, tn), col)]
    elif epi == "scale":
        specs += [pl.BlockSpec((1, tn), col)]
    elif epi == "cmul":
        specs += [pl.BlockSpec((m, tn), lambda j: (0, g_col))]
    return pl.pallas_call(
        functools.partial(_colmm_kernel, epi=epi),
        grid=(nc // tn,),
        in_specs=specs,
        out_specs=pl.BlockSpec((m, tn), col),
        out_shape=jax.ShapeDtypeStruct((m, nc), out_dtype),
        compiler_params=_params(("parallel",), 48),
        name="dft_stage_" + epi,
    )(w, x, *extra)


def _bmm_kernel(*refs, epi, cplx_out):
    m_ref, x_ref = refs[0], refs[1]
    o_ref = refs[-1]
    x = x_ref[:, 0]
    x = x.reshape(x.shape[0] * x.shape[1], x.shape[2])
    acc = _dot(m_ref[0], x)
    if epi == "cmul":
        g_ref = refs[2]
        ph = acc.shape[0] // 2
        gr, gi = g_ref[0, 0], g_ref[1, 0]
        cg = gr.shape[1]
        res_r, res_i = [], []
        for t in range(acc.shape[1] // cg):
            ar, ai = acc[:ph, t * cg:(t + 1) * cg], acc[ph:, t * cg:(t + 1) * cg]
            res_r.append(ar * gr - ai * gi)
            res_i.append(ar * gi + ai * gr)
        o_ref[0, 0] = jnp.concatenate(res_r, axis=1).astype(o_ref.dtype)
        o_ref[1, 0] = jnp.concatenate(res_i, axis=1).astype(o_ref.dtype)
    elif epi == "scale":
        ph = acc.shape[0] // 2
        sc = refs[2][...]
        o_ref[0, 0] = (acc[:ph] * sc).astype(o_ref.dtype)
        o_ref[1, 0] = (acc[ph:] * sc).astype(o_ref.dtype)
    elif cplx_out:
        ph = acc.shape[0] // 2
        o_ref[0, 0] = acc[:ph].astype(o_ref.dtype)
        o_ref[1, 0] = acc[ph:].astype(o_ref.dtype)
    else:
        o_ref[0] = acc.astype(o_ref.dtype)


def _bmm(mats, x, out_dtype, epi="none", extra=None, g_col=0, cplx_out=True):
    ng, p, qq = mats.shape
    c = x.shape[3]
    specs = [pl.BlockSpec((1, p, qq), lambda g: (g, 0, 0)),
             pl.BlockSpec((2, 1, qq // 2, c), lambda g: (0, g, 0, 0))]
    args = [mats, x]
    if epi == "cmul":
        cg = HY_WIDTH
        specs.append(pl.BlockSpec((2, 1, p // 2, cg), lambda g: (0, g, 0, g_col)))
        args.append(extra)
    elif epi == "scale":
        specs.append(pl.BlockSpec((1, c), lambda g: (0, 0)))
        args.append(extra)
    if cplx_out:
        ospec = pl.BlockSpec((2, 1, p // 2, c), lambda g: (0, g, 0, 0))
        oshape = jax.ShapeDtypeStruct((2, ng, p // 2, c), out_dtype)
    else:
        ospec = pl.BlockSpec((1, p, c), lambda g: (g, 0, 0))
        oshape = jax.ShapeDtypeStruct((ng, p, c), out_dtype)
    return pl.pallas_call(
        functools.partial(_bmm_kernel, epi=epi, cplx_out=cplx_out),
        grid=(ng,),
        in_specs=specs,
        out_specs=ospec,
        out_shape=oshape,
        compiler_params=_params(("parallel",), 48),
        name="dft_batched_" + epi,
    )(*args)


def _angle(num, den):
    return (2.0 * math.pi / den) * jnp.mod(num, den).astype(F32)


def _dft_tables(n_lat):
    nn = 2 * n_lat
    n2 = DFT_N2
    n1 = nn // n2
    i1 = jnp.arange(n1, dtype=jnp.int32)
    a1 = _angle(i1[:, None] * i1[None, :], n1)
    c1, s1 = jnp.cos(a1), jnp.sin(a1)
    t = {}
    t["hy_f1"] = jnp.concatenate([c1, -s1], axis=0).astype(BF16)
    t["hy_i2"] = jnp.concatenate([c1[:n1 // 2], -s1[:n1 // 2]], axis=1).astype(BF16)
    i2 = jnp.arange(n2, dtype=jnp.int32)
    kk = i1[:, None, None] + n1 * i2[None, :, None]
    a2 = _angle(kk * i2[None, None, :], nn)
    c2, s2 = jnp.cos(a2), jnp.sin(a2)
    fwd = jnp.concatenate([jnp.concatenate([c2, s2], axis=2), jnp.concatenate([-s2, c2], axis=2)], axis=1)
    t["hy_fwd"] = fwd.astype(BF16)
    t["hy_inv"] = jnp.swapaxes(fwd, 1, 2).astype(BF16)
    m2 = FN_N2
    m1 = n_lat // m2
    j1 = jnp.arange(m1, dtype=jnp.int32)
    b1 = _angle(j1[:, None] * j1[None, :], m1)
    cb, sb = jnp.cos(b1), jnp.sin(b1)
    t["fn_s1"] = jnp.concatenate([jnp.concatenate([cb, sb], axis=1),
                                  jnp.concatenate([-sb, cb], axis=1)], axis=0).astype(BF16)
    j2 = jnp.arange(m2, dtype=jnp.int32)
    kf = j1[:, None, None] + m1 * j2[None, :, None]
    b2 = _angle(kf * j2[None, None, :], n_lat)
    norm = 1.0 / math.sqrt(n_lat * FN_GROUP_DIM)
    t["fn_s3"] = (norm * jnp.concatenate([jnp.cos(b2), jnp.sin(b2)], axis=2)).astype(BF16)
    return t


def _small_dft_tables(n):
    nn = 2 * n
    k = jnp.arange(nn, dtype=jnp.int32)
    a = _angle(k[:, None] * k[None, :], nn)
    c, s = jnp.cos(a), jnp.sin(a)
    t = {}
    t["f_full"] = jnp.concatenate([c, -s], axis=0).astype(BF16)
    t["f_half"] = jnp.concatenate([c[:, :n], -s[:, :n]], axis=0).astype(BF16)
    t["i_half"] = jnp.concatenate([c[:n], -s[:n]], axis=1).astype(BF16)
    j = jnp.arange(n, dtype=jnp.int32)
    b = _angle(j[:, None] * j[None, :], n)
    norm = 1.0 / math.sqrt(n * FN_GROUP_DIM)
    t["fn"] = (norm * jnp.concatenate([jnp.cos(b), jnp.sin(b)], axis=1)).astype(BF16)
    return t


def _hyena_long(hy, short_w, fw, bias, tabs):
    b, n, _ = hy.shape
    cw = b * HY_WIDTH
    n2 = DFT_N2
    n1 = 2 * n // n2
    v, x1, x2 = _sconv(hy, short_w)
    hf, ss = _hyena_filters(n, fw)
    g, gscale = _padded_filter(hf, ss)
    half = HY_ORDER * HY_WIDTH
    ga = _colmm(tabs["hy_f1"], g.reshape(n1, n2 * half), BF16)
    gspec = _bmm(tabs["hy_fwd"], ga.reshape(2, n1, n2, half), F32, epi="scale",
                 extra=gscale * (1.0 / (2 * n)))
    z = v
    for o, xg in enumerate((x1, x2)):
        a = _colmm(tabs["hy_f1"][:, :n1 // 2], z.reshape(n1 // 2, n2 * cw), BF16)
        y = _bmm(tabs["hy_fwd"], a.reshape(2, n1, n2, cw), BF16, epi="cmul", extra=gspec, g_col=o)
        bm = _bmm(tabs["hy_inv"], y, BF16)
        brow = jnp.tile(bias[o], n2 * b)[None, :]
        z = _colmm(tabs["hy_i2"], bm.reshape(2 * n1, n2 * cw), F32, epi="gate",
                   extra=(xg.reshape(n1 // 2, n2 * cw), z.reshape(n1 // 2, n2 * cw), brow))
        z = z.reshape(n, cw)
    return z


def _hyena_short(hy, short_w, fw, bias, tabs):
    b, n, _ = hy.shape
    cw = b * HY_WIDTH
    v, x1, x2 = _sconv(hy, short_w)
    hf, ss = _hyena_filters(n, fw)
    g, gscale = _padded_filter(hf, ss)
    gspec = _colmm(tabs["f_full"], g, F32, epi="scale", extra=(gscale * (1.0 / (2 * n)),), tn_cap=256)
    z = v
    for o, xg in enumerate((x1, x2)):
        y = _colmm(tabs["f_half"], z, BF16, epi="cmul", extra=(gspec,), tn_cap=HY_WIDTH, g_col=o)
        brow = jnp.tile(bias[o], b)[None, :]
        z = _colmm(tabs["i_half"], y, F32, epi="gate", extra=(xg, z, brow), tn_cap=256)
    return z


def _fn0_kernel(x_ref, w_ref, o_ref):
    acc = _dot(x_ref[0].astype(BF16), w_ref[...])
    o_ref[0] = acc[:, :BRANCH_WIDTH].astype(o_ref.dtype)
    o_ref[1] = acc[:, BRANCH_WIDTH:].astype(o_ref.dtype)


def _fnet_channel_dft(u):
    b, n, w = u.shape
    j = np.arange(FN_GROUP_DIM)
    ang = 2.0 * np.pi * ((j[:, None] * j[None, :]) % FN_GROUP_DIM) / FN_GROUP_DIM
    eye = np.eye(FN_GROUPS)
    wc = np.concatenate([np.kron(eye, np.cos(ang)), -np.kron(eye, np.sin(ang))], axis=1)
    wc = jnp.asarray(wc, F32).astype(BF16)
    tm = _pick_tile(n, 1024, 8)
    return pl.pallas_call(
        _fn0_kernel,
        grid=(b, n // tm),
        in_specs=[pl.BlockSpec((1, tm, w), lambda bb, i: (bb, i, 0)),
                  pl.BlockSpec((w, 2 * w), lambda bb, i: (0, 0))],
        out_specs=pl.BlockSpec((2, tm, w), lambda bb, i: (0, i, bb)),
        out_shape=jax.ShapeDtypeStruct((2, n, b * w), BF16),
        compiler_params=_params(("parallel", "parallel")),
        name="fnet_channel_dft",
    )(u, wc)


def _fnet_long(u, tabs):
    b, n, w = u.shape
    cw = b * w
    m2 = FN_N2
    m1 = n // m2
    wri = _fnet_channel_dft(u)
    a = _colmm(tabs["fn_s1"], wri.reshape(2 * m1, m2 * cw), BF16)
    y = _bmm(tabs["fn_s3"], a.reshape(2, m1, m2, cw), BF16, cplx_out=False)
    return jnp.transpose(y, (1, 0, 2)).reshape(n, cw)


def _fnet_short(u, tabs):
    b, n, w = u.shape
    wri = _fnet_channel_dft(u)
    return _colmm(tabs["fn"], wri.reshape(2 * n, b * w), BF16, tn_cap=256)


def _merge_kernel(x_ref, g_ref, sc_ref, sh_ref, gt_ref, oa_ref, ob_ref, oc_ref, od_ref,
                  wg_ref, wb_ref, wo_ref, o_ref):
    x = x_ref[0]
    hb = _normmod(x, g_ref[...], sc_ref[0], sh_ref[0]).astype(BF16)
    branches = (oa_ref[0], ob_ref[...], oc_ref[...], od_ref[0])
    acc = jnp.zeros(x.shape, F32)
    for nbr, o in enumerate(branches):
        gate = _dot(hb, wg_ref[:, nbr * D_MODEL:(nbr + 1) * D_MODEL])
        proj = _dot(o.astype(BF16), wb_ref[nbr])
        acc = acc + jax.nn.sigmoid(gate) * proj
    m = _dot(acc.astype(BF16), wo_ref[...])
    o_ref[0] = x + gt_ref[0] * m


def _merge(x, g, sc, sh, gt, oa, ob, oc, od, w):
    b, n, _ = x.shape
    tm = _pick_tile(n, 512, 8)
    tok = lambda bb, i: (bb, i, 0)
    row = lambda bb, i: (bb, 0, 0)
    seq = lambda bb, i: (i, bb)
    c2 = lambda bb, i: (0, 0)
    return pl.pallas_call(
        _merge_kernel,
        grid=(b, n // tm),
        in_specs=[pl.BlockSpec((1, tm, D_MODEL), tok),
                  pl.BlockSpec((1, D_MODEL), c2),
                  pl.BlockSpec((1, 1, D_MODEL), row), pl.BlockSpec((1, 1, D_MODEL), row),
                  pl.BlockSpec((1, 1, D_MODEL), row),
                  pl.BlockSpec((1, tm, BRANCH_WIDTH), tok),
                  pl.BlockSpec((tm, BRANCH_WIDTH), seq),
                  pl.BlockSpec((tm, BRANCH_WIDTH), seq),
                  pl.BlockSpec((1, tm, BRANCH_WIDTH), tok),
                  pl.BlockSpec((D_MODEL, N_BRANCH * D_MODEL), c2),
                  pl.BlockSpec((N_BRANCH, BRANCH_WIDTH, D_MODEL), lambda bb, i: (0, 0, 0)),
                  pl.BlockSpec((D_MODEL, D_MODEL), c2)],
        out_specs=pl.BlockSpec((1, tm, D_MODEL), tok),
        out_shape=jax.ShapeDtypeStruct(x.shape, F32),
        compiler_params=_params(("parallel", "parallel"), 56),
        name="branch_merge",
    )(x, g, sc, sh, gt, oa, ob, oc, od, w["wgate"], w["wbranch"], w["wout"])


def _moe_kernel(x_ref, g_ref, sc_ref, sh_ref, gt_ref, wr_ref, br_ref, wg_ref, wu_ref, wd_ref, fg_ref,
                o_ref, h_s, wt_s, acc_s, *, final):
    e = pl.program_id(2)

    @pl.when(e == 0)
    def _():
        hb = _normmod(x_ref[0], g_ref[...], sc_ref[0], sh_ref[0]).astype(BF16)
        h_s[...] = hb
        r = _dot(hb, wr_ref[...]) + br_ref[...]
        lane = lax.broadcasted_iota(jnp.int32, r.shape, 1)
        neg = jnp.float32(-jnp.inf)
        isg = lane < MOE_GROUPS
        gmax = jnp.max(jnp.where(isg, r, neg), axis=1, keepdims=True)
        gsum = jnp.sum(jnp.where(isg, jnp.exp(r - gmax), 0.0), axis=1, keepdims=True)
        g_p = 1.0 / gsum
        gidx = jnp.min(jnp.where(isg & (r == gmax), lane, ROUTER_LANES), axis=1, keepdims=True)
        ise = ((lane >= MOE_GROUPS) & (lane < MOE_GROUPS + MOE_EXPERTS)
               & (jnp.right_shift(lane - MOE_GROUPS, 2) == gidx))
        el = jnp.where(ise, r, neg)
        e1 = jnp.max(el, axis=1, keepdims=True)
        i1 = jnp.min(jnp.where(ise & (r == e1), lane, ROUTER_LANES), axis=1, keepdims=True)
        el2 = jnp.where(lane == i1, neg, el)
        e2 = jnp.max(el2, axis=1, keepdims=True)
        i2 = jnp.min(jnp.where(el2 == e2, lane, ROUTER_LANES), axis=1, keepdims=True)
        d = jnp.exp(e2 - e1)
        w1 = 1.0 / (1.0 + d)
        w2 = d / (1.0 + d)
        wt_s[...] = jnp.where(lane == i1, g_p * w1, jnp.where(lane == i2, g_p * w2, 0.0))
        acc_s[...] = jnp.zeros(acc_s.shape, F32)

    hb = h_s[...]
    a = _dot(hb, wg_ref[0])
    u = _dot(hb, wu_ref[0])
    wt = wt_s[...]
    lane = lax.broadcasted_iota(jnp.int32, wt.shape, 1)
    wcol = jnp.sum(jnp.where(lane == e + MOE_GROUPS, wt, 0.0), axis=1, keepdims=True)
    act = (a * jax.nn.sigmoid(a)) * u * wcol
    acc_s[...] += _dot(act.astype(BF16), wd_ref[0])

    @pl.when(e == MOE_EXPERTS - 1)
    def _():
        y = x_ref[0] + gt_ref[0] * acc_s[...]
        if final:
            y = _rms(y) * fg_ref[...]
        o_ref[0] = y


def _moe(x, g, sc, sh, gt, w, fg, final):
    b, n, _ = x.shape
    tm = _pick_tile(n, 1024, 8)
    tok = lambda bb, i, e: (bb, i, 0)
    row = lambda bb, i, e: (bb, 0, 0)
    c2 = lambda bb, i, e: (0, 0)
    ex = lambda bb, i, e: (e, 0, 0)
    return pl.pallas_call(
        functools.partial(_moe_kernel, final=final),
        grid=(b, n // tm, MOE_EXPERTS),
        in_specs=[pl.BlockSpec((1, tm, D_MODEL), tok),
                  pl.BlockSpec((1, D_MODEL), c2),
                  pl.BlockSpec((1, 1, D_MODEL), row), pl.BlockSpec((1, 1, D_MODEL), row),
                  pl.BlockSpec((1, 1, D_MODEL), row),
                  pl.BlockSpec((D_MODEL, ROUTER_LANES), c2), pl.BlockSpec((1, ROUTER_LANES), c2),
                  pl.BlockSpec((1, D_MODEL, MOE_HIDDEN), ex),
                  pl.BlockSpec((1, D_MODEL, MOE_HIDDEN), ex),
                  pl.BlockSpec((1, MOE_HIDDEN, D_MODEL), ex),
                  pl.BlockSpec((1, D_MODEL), c2)],
        out_specs=pl.BlockSpec((1, tm, D_MODEL), tok),
        out_shape=jax.ShapeDtypeStruct(x.shape, F32),
        scratch_shapes=[pltpu.VMEM((tm, D_MODEL), BF16), pltpu.VMEM((tm, ROUTER_LANES), F32),
                        pltpu.VMEM((tm, D_MODEL), F32)],
        compiler_params=_params(("parallel", "parallel", "arbitrary"), 48),
        name="hier_moe",
    )(x, g, sc, sh, gt, w["wr"], w["br"], w["wg"], w["wu"], w["wd"], fg)


def _layer_weights(l, w_in, mla_q_norm, mla_w_uq, mla_kv_norm, mla_w_ukv, w_branch, w_out,
                   moe_w_group, moe_b_group, moe_w_expert, moe_b_expert, moe_w_gate, moe_w_up, moe_w_down):
    wi = w_in[l]
    b0 = MLA_Q_RANK
    b1 = b0 + MLA_KV_RANK
    b2 = b1 + MLA_ROPE
    b3 = b2 + HY_IN
    b4 = b3 + BRANCH_WIDTH
    b5 = b4 + 3 * BRANCH_WIDTH
    zpad = lambda r, c: jnp.zeros((r, c), F32)
    w_kr = wi[:, b1:b2]
    krp = jnp.concatenate([zpad(D_MODEL, MLA_NOPE), w_kr[:, 0::2], w_kr[:, 1::2],
                           zpad(D_MODEL, MLA_HEAD_PAD - MLA_NOPE - MLA_ROPE)], axis=1)
    w_na = wi[:, b4:b5]
    w_na = jnp.concatenate([w_na[:, :BRANCH_WIDTH] * NA_SCALE, w_na[:, BRANCH_WIDTH:]], axis=1)
    wa = jnp.concatenate([wi[:, :b1], krp, wi[:, b2:b4], w_na], axis=1).astype(BF16)
    uq = mla_w_uq[l].reshape(MLA_Q_RANK, MLA_HEADS, MLA_NOPE + MLA_ROPE) * (MLA_SCALE * LOG2E)
    uq = jnp.concatenate([uq[..., :MLA_NOPE], uq[..., MLA_NOPE::2], uq[..., MLA_NOPE + 1::2],
                          jnp.zeros((MLA_Q_RANK, MLA_HEADS, MLA_HEAD_PAD - MLA_NOPE - MLA_ROPE), F32)], axis=-1)
    ukv = mla_w_ukv[l].reshape(MLA_KV_RANK, MLA_HEADS, MLA_NOPE + MLA_V)
    uk = jnp.concatenate([ukv[..., :MLA_NOPE],
                          jnp.zeros((MLA_KV_RANK, MLA_HEADS, MLA_HEAD_PAD - MLA_NOPE), F32)], axis=-1)
    uv = ukv[..., MLA_NOPE:]
    wr = jnp.concatenate([moe_w_group[l], moe_w_expert[l],
                          zpad(D_MODEL, ROUTER_LANES - MOE_GROUPS - MOE_EXPERTS)], axis=1)
    br = jnp.concatenate([moe_b_group[l], moe_b_expert[l],
                          jnp.zeros((ROUTER_LANES - MOE_GROUPS - MOE_EXPERTS,), F32)])[None, :]
    return dict(
        wa=wa, qn=mla_q_norm[l][None, :], kvn=mla_kv_norm[l][None, :],
        wuq=uq.reshape(MLA_Q_RANK, -1).astype(BF16), wuk=uk.reshape(MLA_KV_RANK, -1).astype(BF16),
        wuv=uv.reshape(MLA_KV_RANK, -1).astype(BF16),
        wgate=wi[:, b5:].astype(BF16), wbranch=w_branch[l].astype(BF16), wout=w_out[l].astype(BF16),
        wr=wr.astype(BF16), br=br, wg=moe_w_gate[l].astype(BF16), wu=moe_w_up[l].astype(BF16),
        wd=moe_w_down[l].astype(BF16))


def _rope_tables(n):
    t = jnp.arange(n, dtype=jnp.int32)
    row = (t // GRID_W).astype(F32)
    col = (t % GRID_W).astype(F32)
    n_freq = MLA_ROPE // 4
    inv = ROPE_BASE ** (-jnp.arange(n_freq, dtype=F32) / n_freq)
    ang = jnp.concatenate([row[:, None] * inv, col[:, None] * inv], axis=-1)
    c, s = jnp.cos(ang), jnp.sin(ang)
    one = jnp.ones((n, MLA_NOPE), F32)
    zero = jnp.zeros((n, MLA_NOPE), F32)
    hr = MLA_ROPE // 2
    tail1 = jnp.ones((n, MLA_HEAD_PAD - MLA_NOPE - MLA_ROPE), F32)
    tail0 = jnp.zeros((n, MLA_HEAD_PAD - MLA_NOPE - MLA_ROPE), F32)
    cos = jnp.concatenate([one, c, c, tail1], axis=1)
    s1 = jnp.concatenate([zero, -s, jnp.zeros((n, hr), F32), tail0], axis=1)
    s2 = jnp.concatenate([zero, jnp.zeros((n, hr), F32), s, tail0], axis=1)
    return cos, s1, s2


def _identity_rope(n):
    return (jnp.ones((n, MLA_HEAD_PAD), F32), jnp.zeros((n, MLA_HEAD_PAD), F32),
            jnp.zeros((n, MLA_HEAD_PAD), F32))


def kernel(x, c, ctx, c_ctx, ada_w, ada_b, norm1_g, norm2_g, w_in, mla_q_norm, mla_w_uq, mla_kv_norm, mla_w_ukv, hy_short_w, hy_w1, hy_b1, hy_freq1, hy_w2, hy_b2, hy_freq2, hy_w3, hy_bias, na_rpb, w_branch, w_out, moe_w_group, moe_b_group, moe_w_expert, moe_b_expert, moe_w_gate, moe_w_up, moe_w_down, final_norm_g):
    b, n_lat, _ = x.shape
    n_ctx = ctx.shape[1]
    assert b <= 4 and n_lat % (NA_QROWS * GRID_W) == 0 and n_lat // GRID_W >= 2 * NA_QROWS

    cvec = jnp.concatenate([c, c_ctx[None, :], jnp.zeros((8 - b - 1, D_MODEL), F32)], axis=0)
    mod = _mod_vectors(cvec, ada_w, ada_b)

    rope_lat = _rope_tables(n_lat)
    rope_ctx = _identity_rope(n_ctx)
    tabs_lat = _dft_tables(n_lat)
    tabs_ctx = _small_dft_tables(n_ctx)
    fg = final_norm_g[None, :]

    for l in range(DEPTH):
        last = l == DEPTH - 1
        w = _layer_weights(l, w_in, mla_q_norm, mla_w_uq, mla_kv_norm, mla_w_ukv, w_branch, w_out,
                           moe_w_group, moe_b_group, moe_w_expert, moe_b_expert,
                           moe_w_gate, moe_w_up, moe_w_down)
        ml = mod[l, :b].reshape(b, 1, 6, D_MODEL)
        mc = jnp.broadcast_to(mod[l, b].reshape(1, 1, 6, D_MODEL), (b, 1, 6, D_MODEL))
        sh1, sc1, gt1, sh2, sc2, gt2 = [ml[:, :, i] for i in range(6)]
        sh1c, sc1c, gt1c, sh2c, sc2c, gt2c = [mc[:, :, i] for i in range(6)]
        g1 = norm1_g[l][None, :]
        g2 = norm2_g[l][None, :]
        fw = dict(
            w1=jnp.concatenate([hy_w1[l], jnp.zeros((HY_EMB_PAD - HY_EMB, HY_HIDDEN), F32)], axis=0).astype(BF16),
            b1=hy_b1[l][None, :], f1=hy_freq1[l][None, :], w2=hy_w2[l].astype(BF16), b2=hy_b2[l][None, :],
            f2=hy_freq2[l][None, :], w3=hy_w3[l].astype(BF16))

        q, k, v, hy, fn, naq, nak, nav = _inproj(x, g1, sc1, sh1, w, rope_lat)
        qc, kc, vc, hyc, fnc, naqc, nakc, navc = _inproj(ctx, g1, sc1c, sh1c, w, rope_ctx)

        oa = _flash_t(q, jnp.concatenate([k, kc], axis=2), jnp.concatenate([v, vc], axis=2))
        od = _na(naq, nak, nav, nakc, navc, _na_bias_tables(na_rpb[l]))
        ob = _hyena_long(hy, hy_short_w[l], fw, hy_bias[l], tabs_lat)
        oc = _fnet_long(fn, tabs_lat)
        x_new = _merge(x, g1, sc1, sh1, gt1, oa, ob, oc, od, w)

        if not last:
            oa_c = _flash(qc, kc, vc, True)
            heads = lambda t: jnp.transpose(t.reshape(b, n_ctx, NA_HEADS, NA_HEAD_DIM), (0, 2, 1, 3))
            od_c = _flash(heads(naqc), heads(nakc), heads(navc), False)
            ob_c = _hyena_short(hyc, hy_short_w[l], fw, hy_bias[l], tabs_ctx)
            oc_c = _fnet_short(fnc, tabs_ctx)
            ctx = _merge(ctx, g1, sc1c, sh1c, gt1c, oa_c, ob_c, oc_c, od_c, w)
            ctx = _moe(ctx, g2, sc2c, sh2c, gt2c, w, fg, False)
        x = _moe(x_new, g2, sc2, sh2, gt2, w, fg, last)
    return x
```

```python
import functools
import math

import numpy as np
import jax
import jax.numpy as jnp
from jax import lax
from jax.experimental import pallas as pl
from jax.experimental.pallas import tpu as pltpu

F32 = jnp.float32
BF16 = jnp.bfloat16

D_MODEL = 1024
DEPTH = 2
GRID_W = 64
NORM_EPS = 1e-6
MASK_VALUE = -1e30
BRANCH_WIDTH = D_MODEL // 4
N_BRANCH = 4

MLA_HEADS = 4
MLA_Q_RANK = D_MODEL // 4
MLA_KV_RANK = D_MODEL // 8
MLA_NOPE = 64
MLA_ROPE = 32
MLA_V = BRANCH_WIDTH // MLA_HEADS
MLA_SCALE = 1.0 / math.sqrt(MLA_NOPE + MLA_ROPE)
MLA_HEAD_PAD = 128
MLA_VT_ROWS = MLA_V + 16
ROPE_BASE = 10000.0
LOG2E = 1.4426950408889634

HY_WIDTH = BRANCH_WIDTH
HY_ORDER = 2
HY_BANDS = 16
HY_EMB = 1 + 2 * HY_BANDS
HY_EMB_PAD = 128
HY_HIDDEN = 64
HY_DECAY_TARGET = 1e-2
HY_DECAY_SHORT_PCT = 0.3
HY_DECAY_LONG_PCT = 1.5
HY_IN = (HY_ORDER + 1) * HY_WIDTH

FN_GROUPS = 4
FN_GROUP_DIM = BRANCH_WIDTH // FN_GROUPS

NA_HEADS = 4
NA_HEAD_DIM = BRANCH_WIDTH // NA_HEADS
NA_KH = 8
NA_KW = 16
NA_SCALE = 1.0 / math.sqrt(NA_HEAD_DIM)
NA_QROWS = 8
NA_KROWS = 16

MOE_GROUPS = 4
MOE_PER_GROUP = 4
MOE_EXPERTS = MOE_GROUPS * MOE_PER_GROUP
MOE_HIDDEN = 256
ROUTER_LANES = 128

MLA_IN = MLA_Q_RANK + MLA_KV_RANK + MLA_ROPE
MIX_IN = MLA_IN + HY_IN + BRANCH_WIDTH + 3 * BRANCH_WIDTH
MIX_PAD = MLA_Q_RANK + MLA_KV_RANK + MLA_HEAD_PAD + HY_IN + BRANCH_WIDTH + 3 * BRANCH_WIDTH

DFT_N2 = 256
FN_N2 = 128
BMM_GROUPS_PER_STEP = 4
FLASH_Q_SUB = 512


def _params(sem, vmem_mb=None):
    kw = dict(dimension_semantics=sem)
    if vmem_mb is not None:
        kw["vmem_limit_bytes"] = vmem_mb << 20
    return pltpu.CompilerParams(**kw)


def _pick_tile(n, cap, mult):
    best = None
    for t in range(mult, min(n, cap) + 1, mult):
        if n % t == 0:
            best = t
    return best if best is not None else n


def _rms(x):
    return x * lax.rsqrt(jnp.mean(x * x, axis=-1, keepdims=True) + NORM_EPS)


def _normmod(x, g, sc, sh):
    return (_rms(x) * g) * (1.0 + sc) + sh


def _dot(a, b):
    return jnp.dot(a, b, preferred_element_type=F32)


def _dot_nt(a, b):
    return lax.dot_general(a, b, (((1,), (1,)), ((), ())), preferred_element_type=F32)


def _mod_kernel(c_ref, w_ref, b_ref, o_ref):
    c = c_ref[...]
    s = c * jax.nn.sigmoid(c)
    o_ref[0] = _dot(s.astype(BF16), w_ref[0].astype(BF16)) + b_ref[0]


def _mod_vectors(cvec, ada_w, ada_b):
    n6 = ada_w.shape[-1]
    tn = 1024
    return pl.pallas_call(
        _mod_kernel,
        grid=(DEPTH, n6 // tn),
        in_specs=[pl.BlockSpec((8, D_MODEL), lambda l, j: (0, 0)),
                  pl.BlockSpec((1, D_MODEL, tn), lambda l, j: (l, 0, j)),
                  pl.BlockSpec((1, 1, tn), lambda l, j: (l, 0, j))],
        out_specs=pl.BlockSpec((1, 8, tn), lambda l, j: (l, 0, j)),
        out_shape=jax.ShapeDtypeStruct((DEPTH, 8, n6), F32),
        compiler_params=_params(("parallel", "parallel")),
        name="adaln_vectors",
    )(cvec, ada_w, ada_b.reshape(DEPTH, 1, n6))


def _inproj_kernel(*refs, ctx_mode):
    (x_ref, g_ref, sc_ref, sh_ref, wa_ref, qn_ref, kvn_ref, wuq_ref, wuk_ref, wuv_ref,
     cos_ref, s1_ref, s2_ref) = refs[:13]
    if ctx_mode:
        (q_ref, k_ref, v_ref, kall_ref, vtall_ref, hy_ref, fn_ref, naq_ref, nak_ref, nav_ref) = refs[15:]
    else:
        (q_ref, kall_ref, vtall_ref, hy_ref, fn_ref, naq_ref, nak_ref, nav_ref) = refs[13:]
    h = _normmod(x_ref[0], g_ref[...], sc_ref[0], sh_ref[0]).astype(BF16)
    p = _dot(h, wa_ref[...])
    c0 = MLA_Q_RANK
    c1 = c0 + MLA_KV_RANK
    c2 = c1 + MLA_HEAD_PAD
    c3 = c2 + HY_IN
    c4 = c3 + BRANCH_WIDTH
    c5 = c4 + BRANCH_WIDTH
    c6 = c5 + BRANCH_WIDTH
    cq, ckv, krp = p[:, :c0], p[:, c0:c1], p[:, c1:c2]
    hy_ref[0] = p[:, c2:c3]
    fn_ref[0] = p[:, c3:c4]
    naq_ref[0] = p[:, c4:c5].astype(BF16)
    nak_ref[0] = p[:, c5:c6].astype(BF16)
    nav_ref[0] = p[:, c6:].astype(BF16)
    cqn = (_rms(cq) * qn_ref[...]).astype(BF16)
    ckn = (_rms(ckv) * kvn_ref[...]).astype(BF16)
    q = _dot(cqn, wuq_ref[...])
    kn = _dot(ckn, wuk_ref[...])
    v = _dot(ckn, wuv_ref[...])
    cos, s1, s2 = cos_ref[...], s1_ref[...], s2_ref[...]

    def rope(t):
        return (t * cos + pltpu.roll(t, MLA_HEAD_PAD - MLA_ROPE // 2, 1) * s1
                + pltpu.roll(t, MLA_ROPE // 2, 1) * s2)

    kr = rope(krp)
    vt = v.T
    tail_rows = lax.broadcasted_iota(jnp.int32, (MLA_VT_ROWS - MLA_V, v.shape[0]), 0)
    tail = jnp.where(tail_rows == 0, 1.0, 0.0).astype(BF16)
    for hh in range(MLA_HEADS):
        sl = slice(hh * MLA_HEAD_PAD, (hh + 1) * MLA_HEAD_PAD)
        vs = slice(hh * MLA_V, (hh + 1) * MLA_V)
        kh = (kn[:, sl] + kr).astype(BF16)
        q_ref[0, hh] = rope(q[:, sl]).astype(BF16)
        kall_ref[0, hh] = kh
        vtall_ref[0, hh, :MLA_V] = vt[vs].astype(BF16)
        vtall_ref[0, hh, MLA_V:] = tail
        if ctx_mode:
            k_ref[0, hh] = kh
            v_ref[0, hh] = v[:, vs].astype(BF16)


def _inproj(x, g, sc, sh, w, rope_tabs, n_keys, key_off, kv_bufs=None):
    b, n, _ = x.shape
    ctx_mode = kv_bufs is not None
    tm = _pick_tile(n, 512, 128)
    assert key_off % tm == 0
    ko = key_off // tm
    cos, s1, s2 = rope_tabs
    tok = lambda bb, i: (bb, i, 0)
    row = lambda bb, i: (bb, 0, 0)
    const2 = lambda bb, i: (0, 0)
    hd = lambda bb, i: (bb, 0, i, 0)
    tab = lambda bb, i: (i, 0)
    hp = MLA_HEADS * MLA_HEAD_PAD
    in_specs = [pl.BlockSpec((1, tm, D_MODEL), tok),
                pl.BlockSpec((1, D_MODEL), const2),
                pl.BlockSpec((1, 1, D_MODEL), row),
                pl.BlockSpec((1, 1, D_MODEL), row),
                pl.BlockSpec((D_MODEL, MIX_PAD), const2),
                pl.BlockSpec((1, MLA_Q_RANK), const2),
                pl.BlockSpec((1, MLA_KV_RANK), const2),
                pl.BlockSpec((MLA_Q_RANK, hp), const2),
                pl.BlockSpec((MLA_KV_RANK, hp), const2),
                pl.BlockSpec((MLA_KV_RANK, BRANCH_WIDTH), const2),
                pl.BlockSpec((tm, MLA_HEAD_PAD), tab),
                pl.BlockSpec((tm, MLA_HEAD_PAD), tab),
                pl.BlockSpec((tm, MLA_HEAD_PAD), tab)]
    args = [x, g, sc, sh, w["wa"], w["qn"], w["kvn"], w["wuq"], w["wuk"], w["wuv"], cos, s1, s2]
    q_spec = pl.BlockSpec((1, MLA_HEADS, tm, MLA_HEAD_PAD), hd)
    q_sd = jax.ShapeDtypeStruct((b, MLA_HEADS, n, MLA_HEAD_PAD), BF16)
    kall_spec = pl.BlockSpec((1, MLA_HEADS, tm, MLA_HEAD_PAD), lambda bb, i: (bb, 0, ko + i, 0))
    vtall_spec = pl.BlockSpec((1, MLA_HEADS, MLA_VT_ROWS, tm), lambda bb, i: (bb, 0, 0, ko + i))
    kall_sd = jax.ShapeDtypeStruct((b, MLA_HEADS, n_keys, MLA_HEAD_PAD), BF16)
    vtall_sd = jax.ShapeDtypeStruct((b, MLA_HEADS, MLA_VT_ROWS, n_keys), BF16)
    rest_specs = [pl.BlockSpec((1, tm, HY_IN), tok)] + [pl.BlockSpec((1, tm, BRANCH_WIDTH), tok)] * 4
    rest_sd = [jax.ShapeDtypeStruct((b, n, HY_IN), F32), jax.ShapeDtypeStruct((b, n, BRANCH_WIDTH), F32)] \
        + [jax.ShapeDtypeStruct((b, n, BRANCH_WIDTH), BF16)] * 3
    aliases = {}
    if ctx_mode:
        in_specs += [pl.BlockSpec(memory_space=pl.ANY), pl.BlockSpec(memory_space=pl.ANY)]
        args += list(kv_bufs)
        out_specs = [q_spec, q_spec, pl.BlockSpec((1, MLA_HEADS, tm, MLA_V), hd), kall_spec, vtall_spec]
        out_shape = [q_sd, q_sd, jax.ShapeDtypeStruct((b, MLA_HEADS, n, MLA_V), BF16), kall_sd, vtall_sd]
        aliases = {13: 3, 14: 4}
    else:
        out_specs = [q_spec, kall_spec, vtall_spec]
        out_shape = [q_sd, kall_sd, vtall_sd]
    return pl.pallas_call(
        functools.partial(_inproj_kernel, ctx_mode=ctx_mode),
        grid=(b, n // tm),
        in_specs=in_specs,
        out_specs=out_specs + rest_specs,
        out_shape=out_shape + rest_sd,
        input_output_aliases=aliases,
        compiler_params=_params(("parallel", "parallel"), 48),
        name="input_projection",
    )(*args)


def _flash_kernel(q_ref, k_ref, v_ref, o_ref, m_ref, l_ref, acc_ref, *, nh, use_exp2):
    j = pl.program_id(2)

    @pl.when(j == 0)
    def _():
        m_ref[...] = jnp.full(m_ref.shape, MASK_VALUE, F32)
        l_ref[...] = jnp.zeros(l_ref.shape, F32)
        acc_ref[...] = jnp.zeros(acc_ref.shape, F32)

    ex = jnp.exp2 if use_exp2 else jnp.exp
    for h in range(nh):
        s = _dot_nt(q_ref[0, h], k_ref[0, h])
        m_prev = m_ref[h]
        m_cur = jnp.maximum(m_prev, jnp.max(s, axis=1, keepdims=True))
        alpha = ex(m_prev - m_cur)
        p = ex(s - m_cur)
        l_ref[h] = alpha * l_ref[h] + jnp.sum(p, axis=1, keepdims=True)
        acc_ref[h] = alpha * acc_ref[h] + _dot(p.astype(BF16), v_ref[0, h])
        m_ref[h] = m_cur

    @pl.when(j == pl.num_programs(2) - 1)
    def _():
        outs = [acc_ref[h] / l_ref[h] for h in range(nh)]
        o_ref[0] = jnp.concatenate(outs, axis=1).astype(o_ref.dtype)


def _flash(q, k, v, use_exp2):
    b, nh, lq, dk = q.shape
    lk, dv = k.shape[2], v.shape[3]
    tq = _pick_tile(lq, 512, 8)
    tk = _pick_tile(lk, 1280, 128)
    return pl.pallas_call(
        functools.partial(_flash_kernel, nh=nh, use_exp2=use_exp2),
        grid=(b, lq // tq, lk // tk),
        in_specs=[pl.BlockSpec((1, nh, tq, dk), lambda bb, i, j: (bb, 0, i, 0)),
                  pl.BlockSpec((1, nh, tk, dk), lambda bb, i, j: (bb, 0, j, 0)),
                  pl.BlockSpec((1, nh, tk, dv), lambda bb, i, j: (bb, 0, j, 0))],
        out_specs=pl.BlockSpec((1, tq, nh * dv), lambda bb, i, j: (bb, i, 0)),
        out_shape=jax.ShapeDtypeStruct((b, lq, nh * dv), BF16),
        scratch_shapes=[pltpu.VMEM((nh, tq, 1), F32), pltpu.VMEM((nh, tq, 1), F32),
                        pltpu.VMEM((nh, tq, dv), F32)],
        compiler_params=_params(("parallel", "parallel", "arbitrary"), 48),
        name="softmax_attention",
    )(q, k, v)


def _flash_t_kernel(q_ref, k_ref, vt_ref, o_ref, m_ref, acc_ref, *, nh, dv):
    j = pl.program_id(2)

    @pl.when(j == 0)
    def _():
        m_ref[...] = jnp.full(m_ref.shape, MASK_VALUE, F32)
        acc_ref[...] = jnp.zeros(acc_ref.shape, F32)

    rows = vt_ref.shape[2]

    tq = q_ref.shape[2]
    qs = min(tq, FLASH_Q_SUB)
    units = [(h, c) for h in range(nh) for c in range(tq // qs)]

    def scores(h, c):
        cols = slice(c * qs, (c + 1) * qs)
        st = _dot_nt(k_ref[0, h], q_ref[0, h, cols, :])
        m_prev = m_ref[h, :, cols]
        m_cur = jnp.maximum(m_prev, jnp.max(st, axis=0, keepdims=True))
        m_ref[h, :, cols] = m_cur
        return st, m_prev, m_cur

    def values(h, c, st, m_prev, m_cur):
        cols = slice(c * qs, (c + 1) * qs)
        alpha = jnp.exp2(m_prev - m_cur)
        p = jnp.exp2(st - m_cur).astype(BF16)
        acc_ref[h, :rows, cols] = alpha * acc_ref[h, :rows, cols] + _dot(vt_ref[0, h], p)

    pending = scores(*units[0])
    for i, u in enumerate(units):
        nxt = scores(*units[i + 1]) if i + 1 < len(units) else None
        values(*u, *pending)
        pending = nxt

    @pl.when(j == pl.num_programs(2) - 1)
    def _():
        outs = []
        for h in range(nh):
            a = acc_ref[h]
            o = (a / a[dv:dv + 1]).T
            outs.append(o[:, :dv])
        o_ref[0] = jnp.concatenate(outs, axis=1).astype(o_ref.dtype)


def _flash_t(q, k, vt, dv):
    b, nh, lq, dk = q.shape
    lk, rows = k.shape[2], vt.shape[2]
    tq = _pick_tile(lq, 1024, 128)
    tk = _pick_tile(lk, 1280, 128)
    return pl.pallas_call(
        functools.partial(_flash_t_kernel, nh=nh, dv=dv),
        grid=(b, lq // tq, lk // tk),
        in_specs=[pl.BlockSpec((1, nh, tq, dk), lambda bb, i, j: (bb, 0, i, 0)),
                  pl.BlockSpec((1, nh, tk, dk), lambda bb, i, j: (bb, 0, j, 0)),
                  pl.BlockSpec((1, nh, rows, tk), lambda bb, i, j: (bb, 0, 0, j))],
        out_specs=pl.BlockSpec((1, tq, nh * dv), lambda bb, i, j: (bb, i, 0)),
        out_shape=jax.ShapeDtypeStruct((b, lq, nh * dv), BF16),
        scratch_shapes=[pltpu.VMEM((nh, 1, tq), F32), pltpu.VMEM((nh, 128, tq), F32)],
        compiler_params=_params(("parallel", "parallel", "arbitrary"), 48),
        name="softmax_attention_t",
    )(q, k, vt)


def _na_kernel(q_ref, k0, k1, k2, k3, v0, v1, v2, v3, kc_ref, vc_ref, b_ref, o_ref):
    q = q_ref[0]
    kcat = jnp.concatenate([k0[0], k1[0], k2[0], k3[0]], axis=0)
    vcat = jnp.concatenate([v0[0], v1[0], v2[0], v3[0]], axis=0)
    kc, vc = kc_ref[0], vc_ref[0]
    outs = []
    for h in range(NA_HEADS):
        sl = slice(h * NA_HEAD_DIM, (h + 1) * NA_HEAD_DIM)
        qh = q[:, sl]
        s = _dot_nt(qh, kcat[:, sl]) + b_ref[0, h]
        sc = _dot_nt(qh, kc[:, sl])
        m = jnp.maximum(jnp.max(s, axis=1, keepdims=True), jnp.max(sc, axis=1, keepdims=True))
        p = jnp.exp(s - m)
        pc = jnp.exp(sc - m)
        l = jnp.sum(p, axis=1, keepdims=True) + jnp.sum(pc, axis=1, keepdims=True)
        o = _dot(p.astype(BF16), vcat[:, sl]) + _dot(pc.astype(BF16), vc[:, sl])
        outs.append(o / l)
    o_ref[0] = jnp.concatenate(outs, axis=1).astype(o_ref.dtype)


def _na_bias_tables(rpb):
    j = np.arange(NA_QROWS)[:, None]
    m = np.arange(NA_KROWS)[None, :]
    valid = np.zeros((3, NA_QROWS, NA_KROWS), bool)
    st0 = np.maximum(j - NA_KH // 2, 0)
    valid[0] = (m - 4 >= st0) & (m - 4 < st0 + NA_KH)
    valid[1] = (m - j >= 0) & (m - j < NA_KH)
    st2 = 4 + np.minimum(j - NA_KH // 2, 0)
    valid[2] = (m < 12) & (m >= st2) & (m < st2 + NA_KH)
    dr_idx = np.clip(m - 4 - j + NA_KH - 1, 0, 2 * NA_KH - 2)
    qc = np.arange(GRID_W)[:, None]
    kc = np.arange(GRID_W)[None, :]
    dc_idx = np.clip(kc - qc, -(NA_KW - 1), NA_KW - 1) + NA_KW - 1
    cs = np.clip(qc - NA_KW // 2, 0, GRID_W - NA_KW)
    col_ok = (kc >= cs) & (kc < cs + NA_KW)
    t = rpb.astype(F32)[:, dr_idx]
    t = t[..., dc_idx]
    t = jnp.transpose(t, (0, 1, 3, 2, 4))
    ok = valid[:, None, :, None, :, None] & col_ok[None, None, None, :, None, :]
    out = jnp.where(jnp.asarray(ok), t[None], MASK_VALUE)
    return out.reshape(3, NA_HEADS, NA_QROWS * GRID_W, NA_KROWS * GRID_W)


def _na(q, k, v, kc, vc, bias):
    b, n, w = q.shape
    c = kc.shape[1]
    tq = NA_QROWS * GRID_W
    tkb = 4 * GRID_W
    nb = n // tq
    nkb = n // tkb
    kmaps = [lambda bb, i: (bb, jnp.maximum(2 * i - 1, 0), 0),
             lambda bb, i: (bb, 2 * i, 0),
             lambda bb, i: (bb, 2 * i + 1, 0),
             lambda bb, i: (bb, jnp.minimum(2 * i + 2, nkb - 1), 0)]
    kspecs = [pl.BlockSpec((1, tkb, w), mp) for mp in kmaps]
    return pl.pallas_call(
        _na_kernel,
        grid=(b, nb),
        in_specs=[pl.BlockSpec((1, tq, w), lambda bb, i: (bb, i, 0))] + kspecs + kspecs
                 + [pl.BlockSpec((1, c, w), lambda bb, i: (bb, 0, 0)),
                    pl.BlockSpec((1, c, w), lambda bb, i: (bb, 0, 0)),
                    pl.BlockSpec((1, NA_HEADS, tq, NA_KROWS * GRID_W),
                                 lambda bb, i: (jnp.where(i == 0, 0, jnp.where(i == nb - 1, 2, 1)), 0, 0, 0))],
        out_specs=pl.BlockSpec((1, tq, w), lambda bb, i: (bb, i, 0)),
        out_shape=jax.ShapeDtypeStruct((b, n, w), BF16),
        compiler_params=_params(("parallel", "arbitrary"), 56),
        name="neighbourhood_attention",
    )(q, k, k, k, k, v, v, v, v, kc, vc, bias)


def _sconv_kernel(u_ref, pv_ref, nx_ref, w_ref, v_ref, x1_ref, x2_ref, *, nt):
    i = pl.program_id(1)
    u = u_ref[0]
    tl = u.shape[0]
    prev = jnp.where(i > 0, pv_ref[0, 7:8, :], 0.0)
    nxt = jnp.where(i < nt - 1, nx_ref[0, 0:1, :], 0.0)
    rows = lax.broadcasted_iota(jnp.int32, u.shape, 0)
    um = jnp.where(rows == 0, prev, pltpu.roll(u, 1, 0))
    up = jnp.where(rows == tl - 1, nxt, pltpu.roll(u, tl - 1, 0))
    w = w_ref[...]
    y = um * w[0:1] + u * w[1:2] + up * w[2:3]
    v_ref[...] = y[:, :HY_WIDTH]
    x1_ref[...] = y[:, HY_WIDTH:2 * HY_WIDTH]
    x2_ref[...] = y[:, 2 * HY_WIDTH:]


def _sconv(u, w):
    b, n, _ = u.shape
    tl = _pick_tile(n, 512, 8)
    nt = n // tl
    r8 = tl // 8
    osd = jax.ShapeDtypeStruct((n, b * HY_WIDTH), F32)
    ospec = pl.BlockSpec((tl, HY_WIDTH), lambda bb, i: (i, bb))
    return pl.pallas_call(
        functools.partial(_sconv_kernel, nt=nt),
        grid=(b, nt),
        in_specs=[pl.BlockSpec((1, tl, HY_IN), lambda bb, i: (bb, i, 0)),
                  pl.BlockSpec((1, 8, HY_IN), lambda bb, i: (bb, jnp.maximum(i * r8 - 1, 0), 0)),
                  pl.BlockSpec((1, 8, HY_IN), lambda bb, i: (bb, jnp.minimum((i + 1) * r8, n // 8 - 1), 0)),
                  pl.BlockSpec((3, HY_IN), lambda bb, i: (0, 0))],
        out_specs=[ospec, ospec, ospec],
        out_shape=[osd, osd, osd],
        compiler_params=_params(("parallel", "parallel")),
        name="hyena_short_conv",
    )(u, u, u, w)


def _filt_kernel(z_ref, w1_ref, b1_ref, f1_ref, w2_ref, b2_ref, f2_ref, w3_ref, win_ref, g_ref, ss_ref):
    d = pl.program_id(0)
    i = pl.program_id(1)
    h = jnp.sin(f1_ref[...] * (_dot(z_ref[0].astype(BF16), w1_ref[...]) + b1_ref[...]))
    h = jnp.sin(f2_ref[...] * (_dot(h.astype(BF16), w2_ref[...]) + b2_ref[...]))
    h = _dot(h.astype(BF16), w3_ref[0])
    h = h * jnp.concatenate([win_ref[0]] * HY_ORDER, axis=1)
    rows = lax.broadcasted_iota(jnp.int32, h.shape, 0)
    h = jnp.where((rows == 0) & (i == 0) & (d == 1), 0.0, h)
    g_ref[0] = h
    ss = jnp.sum(h * h, axis=0, keepdims=True)
    first = (i == 0) & (d == 0)

    @pl.when(first)
    def _():
        ss_ref[...] = ss

    @pl.when(jnp.logical_not(first))
    def _():
        ss_ref[...] += ss


def _padded_filter(n, fw):
    r = jnp.arange(n, dtype=jnp.int32)
    idx = jnp.stack([r, jnp.mod(n - r, n)], axis=0).astype(F32)[:, :, None]
    t = idx / (n - 1)
    bands = jnp.linspace(1e-4, HY_BANDS - 1, HY_BANDS, dtype=F32)
    ang = (2.0 * math.pi / n) * idx * bands
    z = jnp.concatenate([t, jnp.cos(ang), -jnp.sin(ang), jnp.zeros((2, n, HY_EMB_PAD - HY_EMB), F32)], axis=-1)
    deltas = jnp.linspace(math.log(HY_DECAY_TARGET) / HY_DECAY_LONG_PCT,
                          math.log(HY_DECAY_TARGET) / HY_DECAY_SHORT_PCT, HY_WIDTH, dtype=F32)
    window = jnp.exp(-t * jnp.abs(deltas))
    tl = _pick_tile(n, 512, 8)
    half = HY_ORDER * HY_WIDTH
    c2 = lambda d, i: (0, 0)
    g, ss = pl.pallas_call(
        _filt_kernel,
        grid=(2, n // tl),
        in_specs=[pl.BlockSpec((1, tl, HY_EMB_PAD), lambda d, i: (d, i, 0)),
                  pl.BlockSpec((HY_EMB_PAD, HY_HIDDEN), c2), pl.BlockSpec((1, HY_HIDDEN), c2),
                  pl.BlockSpec((1, HY_HIDDEN), c2),
                  pl.BlockSpec((HY_HIDDEN, HY_HIDDEN), c2), pl.BlockSpec((1, HY_HIDDEN), c2),
                  pl.BlockSpec((1, HY_HIDDEN), c2),
                  pl.BlockSpec((1, HY_HIDDEN, half), lambda d, i: (d, 0, 0)),
                  pl.BlockSpec((1, tl, HY_WIDTH), lambda d, i: (d, i, 0))],
        out_specs=[pl.BlockSpec((1, tl, half), lambda d, i: (d, i, 0)), pl.BlockSpec((1, half), c2)],
        out_shape=[jax.ShapeDtypeStruct((2, n, half), F32), jax.ShapeDtypeStruct((1, half), F32)],
        compiler_params=_params(("arbitrary", "arbitrary")),
        name="hyena_filter_mlp",
    )(z, fw["w1"], fw["b1"], fw["f1"], fw["w2"], fw["b2"], fw["f2"], fw["w3"], window)
    return g.reshape(2 * n, half), lax.rsqrt(ss + NORM_EPS)


def _colmm_kernel(*refs, epi):
    w_ref, x_ref = refs[0], refs[1]
    o_ref = refs[-1]
    acc = _dot(w_ref[...], x_ref[...].astype(BF16))
    if epi == "gate":
        xg_ref, z_ref, b_ref = refs[2:5]
        acc = xg_ref[...] * (acc + z_ref[...] * b_ref[...])
    elif epi == "scale":
        acc = acc * refs[2][...]
    elif epi == "cmul":
        g = refs[2][...]
        mh = acc.shape[0] // 2
        ar, ai, gr, gi = acc[:mh], acc[mh:], g[:mh], g[mh:]
        acc = jnp.concatenate([ar * gr - ai * gi, ar * gi + ai * gr], axis=0)
    o_ref[...] = acc.astype(o_ref.dtype)


def _colmm(w, x, out_dtype, epi="none", extra=(), tn_cap=4096, g_col=None):
    m, k = w.shape
    nc = x.shape[1]
    tn = _pick_tile(nc, tn_cap, 128)
    col = lambda j: (0, j)
    specs = [pl.BlockSpec((m, k), lambda j: (0, 0)), pl.BlockSpec((k, tn), col)]
    if epi == "gate":
        specs += [pl.BlockSpec((m, tn), col), pl.BlockSpec((m, tn), col), pl.BlockSpec((1, tn), col)]
    elif epi == "scale":
        specs += [pl.BlockSpec((1, tn), col)]
    elif epi == "cmul":
        specs += [pl.BlockSpec((m, tn), lambda j: (0, g_col))]
    return pl.pallas_call(
        functools.partial(_colmm_kernel, epi=epi),
        grid=(nc // tn,),
        in_specs=specs,
        out_specs=pl.BlockSpec((m, tn), col),
        out_shape=jax.ShapeDtypeStruct((m, nc), out_dtype),
        compiler_params=_params(("parallel",), 48),
        name="dft_stage_" + epi,
    )(w, x, *extra)


def _bmm_kernel(*refs, epi, cplx_out, gs):
    m_ref, x_ref = refs[0], refs[1]
    o_ref = refs[-1]
    for s in range(gs):
        x = x_ref[:, s]
        x = x.reshape(x.shape[0] * x.shape[1], x.shape[2])
        acc = _dot(m_ref[s], x)
        ph = acc.shape[0] // 2
        if epi == "cmul":
            g_ref = refs[2]
            gr, gi = g_ref[0, s], g_ref[1, s]
            cg = gr.shape[1]
            res_r, res_i = [], []
            for t in range(acc.shape[1] // cg):
                ar, ai = acc[:ph, t * cg:(t + 1) * cg], acc[ph:, t * cg:(t + 1) * cg]
                res_r.append(ar * gr - ai * gi)
                res_i.append(ar * gi + ai * gr)
            o_ref[0, s] = jnp.concatenate(res_r, axis=1).astype(o_ref.dtype)
            o_ref[1, s] = jnp.concatenate(res_i, axis=1).astype(o_ref.dtype)
        elif epi == "scale":
            sc = refs[2][...]
            o_ref[0, s] = (acc[:ph] * sc).astype(o_ref.dtype)
            o_ref[1, s] = (acc[ph:] * sc).astype(o_ref.dtype)
        elif cplx_out:
            o_ref[0, s] = acc[:ph].astype(o_ref.dtype)
            o_ref[1, s] = acc[ph:].astype(o_ref.dtype)
        else:
            o_ref[s] = acc.astype(o_ref.dtype)


def _bmm(mats, x, out_dtype, epi="none", extra=None, g_col=0, cplx_out=True):
    ng, p, qq = mats.shape
    c = x.shape[3]
    gs = _pick_tile(ng, BMM_GROUPS_PER_STEP, 1)
    specs = [pl.BlockSpec((gs, p, qq), lambda g: (g, 0, 0)),
             pl.BlockSpec((2, gs, qq // 2, c), lambda g: (0, g, 0, 0))]
    args = [mats, x]
    if epi == "cmul":
        cg = HY_WIDTH
        specs.append(pl.BlockSpec((2, gs, p // 2, cg), lambda g: (0, g, 0, g_col)))
        args.append(extra)
    elif epi == "scale":
        specs.append(pl.BlockSpec((1, c), lambda g: (0, 0)))
        args.append(extra)
    if cplx_out:
        ospec = pl.BlockSpec((2, gs, p // 2, c), lambda g: (0, g, 0, 0))
        oshape = jax.ShapeDtypeStruct((2, ng, p // 2, c), out_dtype)
    else:
        ospec = pl.BlockSpec((gs, p, c), lambda g: (g, 0, 0))
        oshape = jax.ShapeDtypeStruct((ng, p, c), out_dtype)
    return pl.pallas_call(
        functools.partial(_bmm_kernel, epi=epi, cplx_out=cplx_out, gs=gs),
        grid=(ng // gs,),
        in_specs=specs,
        out_specs=ospec,
        out_shape=oshape,
        compiler_params=_params(("parallel",), 48),
        name="dft_batched_" + epi,
    )(*args)


def _angle(num, den):
    return (2.0 * math.pi / den) * jnp.mod(num, den).astype(F32)


def _dft_tables(n_lat):
    nn = 2 * n_lat
    n2 = DFT_N2
    n1 = nn // n2
    i1 = jnp.arange(n1, dtype=jnp.int32)
    a1 = _angle(i1[:, None] * i1[None, :], n1)
    c1, s1 = jnp.cos(a1), jnp.sin(a1)
    t = {}
    t["hy_f1"] = jnp.concatenate([c1, -s1], axis=0).astype(BF16)
    t["hy_i2"] = jnp.concatenate([c1[:n1 // 2], -s1[:n1 // 2]], axis=1).astype(BF16)
    i2 = jnp.arange(n2, dtype=jnp.int32)
    kk = i1[:, None, None] + n1 * i2[None, :, None]
    a2 = _angle(kk * i2[None, None, :], nn)
    c2, s2 = jnp.cos(a2), jnp.sin(a2)
    fwd = jnp.concatenate([jnp.concatenate([c2, s2], axis=2), jnp.concatenate([-s2, c2], axis=2)], axis=1)
    t["hy_fwd"] = fwd.astype(BF16)
    t["hy_inv"] = jnp.swapaxes(fwd, 1, 2).astype(BF16)
    m2 = FN_N2
    m1 = n_lat // m2
    j1 = jnp.arange(m1, dtype=jnp.int32)
    b1 = _angle(j1[:, None] * j1[None, :], m1)
    cb, sb = jnp.cos(b1), jnp.sin(b1)
    t["fn_s1"] = jnp.concatenate([jnp.concatenate([cb, sb], axis=1),
                                  jnp.concatenate([-sb, cb], axis=1)], axis=0).astype(BF16)
    j2 = jnp.arange(m2, dtype=jnp.int32)
    kf = j1[:, None, None] + m1 * j2[None, :, None]
    b2 = _angle(kf * j2[None, None, :], n_lat)
    norm = 1.0 / math.sqrt(n_lat * FN_GROUP_DIM)
    t["fn_s3"] = (norm * jnp.concatenate([jnp.cos(b2), jnp.sin(b2)], axis=2)).astype(BF16)
    return t


def _small_dft_tables(n):
    nn = 2 * n
    k = jnp.arange(nn, dtype=jnp.int32)
    a = _angle(k[:, None] * k[None, :], nn)
    c, s = jnp.cos(a), jnp.sin(a)
    t = {}
    t["f_full"] = jnp.concatenate([c, -s], axis=0).astype(BF16)
    t["f_half"] = jnp.concatenate([c[:, :n], -s[:, :n]], axis=0).astype(BF16)
    t["i_half"] = jnp.concatenate([c[:n], -s[:n]], axis=1).astype(BF16)
    j = jnp.arange(n, dtype=jnp.int32)
    b = _angle(j[:, None] * j[None, :], n)
    norm = 1.0 / math.sqrt(n * FN_GROUP_DIM)
    t["fn"] = (norm * jnp.concatenate([jnp.cos(b), jnp.sin(b)], axis=1)).astype(BF16)
    return t


def _hyena_long(hy, short_w, fw, bias, tabs):
    b, n, _ = hy.shape
    cw = b * HY_WIDTH
    n2 = DFT_N2
    n1 = 2 * n // n2
    v, x1, x2 = _sconv(hy, short_w)
    g, gscale = _padded_filter(n, fw)
    half = HY_ORDER * HY_WIDTH
    ga = _colmm(tabs["hy_f1"], g.reshape(n1, n2 * half), BF16)
    gspec = _bmm(tabs["hy_fwd"], ga.reshape(2, n1, n2, half), F32, epi="scale",
                 extra=gscale * (1.0 / (2 * n)))
    z = v
    for o, xg in enumerate((x1, x2)):
        a = _colmm(tabs["hy_f1"][:, :n1 // 2], z.reshape(n1 // 2, n2 * cw), BF16)
        y = _bmm(tabs["hy_fwd"], a.reshape(2, n1, n2, cw), BF16, epi="cmul", extra=gspec, g_col=o)
        bm = _bmm(tabs["hy_inv"], y, BF16)
        brow = jnp.tile(bias[o], n2 * b)[None, :]
        z = _colmm(tabs["hy_i2"], bm.reshape(2 * n1, n2 * cw), F32, epi="gate",
                   extra=(xg.reshape(n1 // 2, n2 * cw), z.reshape(n1 // 2, n2 * cw), brow))
        z = z.reshape(n, cw)
    return z


def _hyena_short(hy, short_w, fw, bias, tabs):
    b, n, _ = hy.shape
    cw = b * HY_WIDTH
    v, x1, x2 = _sconv(hy, short_w)
    g, gscale = _padded_filter(n, fw)
    gspec = _colmm(tabs["f_full"], g, F32, epi="scale", extra=(gscale * (1.0 / (2 * n)),), tn_cap=256)
    z = v
    for o, xg in enumerate((x1, x2)):
        y = _colmm(tabs["f_half"], z, BF16, epi="cmul", extra=(gspec,), tn_cap=HY_WIDTH, g_col=o)
        brow = jnp.tile(bias[o], b)[None, :]
        z = _colmm(tabs["i_half"], y, F32, epi="gate", extra=(xg, z, brow), tn_cap=256)
    return z


def _fn0_kernel(x_ref, w_ref, o_ref):
    acc = _dot(x_ref[0].astype(BF16), w_ref[...])
    o_ref[0] = acc[:, :BRANCH_WIDTH].astype(o_ref.dtype)
    o_ref[1] = acc[:, BRANCH_WIDTH:].astype(o_ref.dtype)


def _fnet_channel_dft(u):
    b, n, w = u.shape
    j = np.arange(FN_GROUP_DIM)
    ang = 2.0 * np.pi * ((j[:, None] * j[None, :]) % FN_GROUP_DIM) / FN_GROUP_DIM
    eye = np.eye(FN_GROUPS)
    wc = np.concatenate([np.kron(eye, np.cos(ang)), -np.kron(eye, np.sin(ang))], axis=1)
    wc = jnp.asarray(wc, F32).astype(BF16)
    tm = _pick_tile(n, 1024, 8)
    return pl.pallas_call(
        _fn0_kernel,
        grid=(b, n // tm),
        in_specs=[pl.BlockSpec((1, tm, w), lambda bb, i: (bb, i, 0)),
                  pl.BlockSpec((w, 2 * w), lambda bb, i: (0, 0))],
        out_specs=pl.BlockSpec((2, tm, w), lambda bb, i: (0, i, bb)),
        out_shape=jax.ShapeDtypeStruct((2, n, b * w), BF16),
        compiler_params=_params(("parallel", "parallel")),
        name="fnet_channel_dft",
    )(u, wc)


def _fnet_long(u, tabs):
    b, n, w = u.shape
    cw = b * w
    m2 = FN_N2
    m1 = n // m2
    wri = _fnet_channel_dft(u)
    a = _colmm(tabs["fn_s1"], wri.reshape(2 * m1, m2 * cw), BF16)
    y = _bmm(tabs["fn_s3"], a.reshape(2, m1, m2, cw), BF16, cplx_out=False)
    return jnp.transpose(y, (1, 0, 2)).reshape(n, cw)


def _fnet_short(u, tabs):
    b, n, w = u.shape
    wri = _fnet_channel_dft(u)
    return _colmm(tabs["fn"], wri.reshape(2 * n, b * w), BF16, tn_cap=256)


def _merge_kernel(x_ref, g_ref, sc_ref, sh_ref, gt_ref, oa_ref, ob_ref, oc_ref, od_ref,
                  wg_ref, wb_ref, wo_ref, o_ref):
    x = x_ref[0]
    hb = _normmod(x, g_ref[...], sc_ref[0], sh_ref[0]).astype(BF16)
    branches = (oa_ref[0], ob_ref[...], oc_ref[...], od_ref[0])
    acc = jnp.zeros(x.shape, F32)
    for nbr, o in enumerate(branches):
        gate = _dot(hb, wg_ref[:, nbr * D_MODEL:(nbr + 1) * D_MODEL])
        proj = _dot(o.astype(BF16), wb_ref[nbr])
        acc = acc + jax.nn.sigmoid(gate) * proj
    m = _dot(acc.astype(BF16), wo_ref[...])
    o_ref[0] = x + gt_ref[0] * m


def _merge(x, g, sc, sh, gt, oa, ob, oc, od, w):
    b, n, _ = x.shape
    tm = _pick_tile(n, 512, 8)
    tok = lambda bb, i: (bb, i, 0)
    row = lambda bb, i: (bb, 0, 0)
    seq = lambda bb, i: (i, bb)
    c2 = lambda bb, i: (0, 0)
    return pl.pallas_call(
        _merge_kernel,
        grid=(b, n // tm),
        in_specs=[pl.BlockSpec((1, tm, D_MODEL), tok),
                  pl.BlockSpec((1, D_MODEL), c2),
                  pl.BlockSpec((1, 1, D_MODEL), row), pl.BlockSpec((1, 1, D_MODEL), row),
                  pl.BlockSpec((1, 1, D_MODEL), row),
                  pl.BlockSpec((1, tm, BRANCH_WIDTH), tok),
                  pl.BlockSpec((tm, BRANCH_WIDTH), seq),
                  pl.BlockSpec((tm, BRANCH_WIDTH), seq),
                  pl.BlockSpec((1, tm, BRANCH_WIDTH), tok),
                  pl.BlockSpec((D_MODEL, N_BRANCH * D_MODEL), c2),
                  pl.BlockSpec((N_BRANCH, BRANCH_WIDTH, D_MODEL), lambda bb, i: (0, 0, 0)),
                  pl.BlockSpec((D_MODEL, D_MODEL), c2)],
        out_specs=pl.BlockSpec((1, tm, D_MODEL), tok),
        out_shape=jax.ShapeDtypeStruct(x.shape, F32),
        compiler_params=_params(("parallel", "parallel"), 56),
        name="branch_merge",
    )(x, g, sc, sh, gt, oa, ob, oc, od, w["wgate"], w["wbranch"], w["wout"])


def _moe_kernel(x_ref, g_ref, sc_ref, sh_ref, gt_ref, wr_ref, br_ref, wg_ref, wu_ref, wd_ref, fg_ref,
                o_ref, h_s, wt_s, acc_s, *, final):
    e = pl.program_id(2)

    @pl.when(e == 0)
    def _():
        hb = _normmod(x_ref[0], g_ref[...], sc_ref[0], sh_ref[0]).astype(BF16)
        h_s[...] = hb
        r = _dot(hb, wr_ref[...]) + br_ref[...]
        lane = lax.broadcasted_iota(jnp.int32, r.shape, 1)
        neg = jnp.float32(-jnp.inf)
        isg = lane < MOE_GROUPS
        gmax = jnp.max(jnp.where(isg, r, neg), axis=1, keepdims=True)
        gsum = jnp.sum(jnp.where(isg, jnp.exp(r - gmax), 0.0), axis=1, keepdims=True)
        g_p = 1.0 / gsum
        gidx = jnp.min(jnp.where(isg & (r == gmax), lane, ROUTER_LANES), axis=1, keepdims=True)
        ise = ((lane >= MOE_GROUPS) & (lane < MOE_GROUPS + MOE_EXPERTS)
               & (jnp.right_shift(lane - MOE_GROUPS, 2) == gidx))
        el = jnp.where(ise, r, neg)
        e1 = jnp.max(el, axis=1, keepdims=True)
        i1 = jnp.min(jnp.where(ise & (r == e1), lane, ROUTER_LANES), axis=1, keepdims=True)
        el2 = jnp.where(lane == i1, neg, el)
        e2 = jnp.max(el2, axis=1, keepdims=True)
        i2 = jnp.min(jnp.where(el2 == e2, lane, ROUTER_LANES), axis=1, keepdims=True)
        d = jnp.exp(e2 - e1)
        w1 = 1.0 / (1.0 + d)
        w2 = d / (1.0 + d)
        wt_s[...] = jnp.where(lane == i1, g_p * w1, jnp.where(lane == i2, g_p * w2, 0.0))
        acc_s[...] = jnp.zeros(acc_s.shape, F32)

    hb = h_s[...]
    a = _dot(hb, wg_ref[0])
    u = _dot(hb, wu_ref[0])
    wt = wt_s[...]
    lane = lax.broadcasted_iota(jnp.int32, wt.shape, 1)
    wcol = jnp.sum(jnp.where(lane == e + MOE_GROUPS, wt, 0.0), axis=1, keepdims=True)
    act = (a * jax.nn.sigmoid(a)) * u * wcol
    acc_s[...] += _dot(act.astype(BF16), wd_ref[0])

    @pl.when(e == MOE_EXPERTS - 1)
    def _():
        y = x_ref[0] + gt_ref[0] * acc_s[...]
        if final:
            y = _rms(y) * fg_ref[...]
        o_ref[0] = y


def _moe(x, g, sc, sh, gt, w, fg, final):
    b, n, _ = x.shape
    tm = _pick_tile(n, 1024, 8)
    tok = lambda bb, i, e: (bb, i, 0)
    row = lambda bb, i, e: (bb, 0, 0)
    c2 = lambda bb, i, e: (0, 0)
    ex = lambda bb, i, e: (e, 0, 0)
    return pl.pallas_call(
        functools.partial(_moe_kernel, final=final),
        grid=(b, n // tm, MOE_EXPERTS),
        in_specs=[pl.BlockSpec((1, tm, D_MODEL), tok),
                  pl.BlockSpec((1, D_MODEL), c2),
                  pl.BlockSpec((1, 1, D_MODEL), row), pl.BlockSpec((1, 1, D_MODEL), row),
                  pl.BlockSpec((1, 1, D_MODEL), row),
                  pl.BlockSpec((D_MODEL, ROUTER_LANES), c2), pl.BlockSpec((1, ROUTER_LANES), c2),
                  pl.BlockSpec((1, D_MODEL, MOE_HIDDEN), ex),
                  pl.BlockSpec((1, D_MODEL, MOE_HIDDEN), ex),
                  pl.BlockSpec((1, MOE_HIDDEN, D_MODEL), ex),
                  pl.BlockSpec((1, D_MODEL), c2)],
        out_specs=pl.BlockSpec((1, tm, D_MODEL), tok),
        out_shape=jax.ShapeDtypeStruct(x.shape, F32),
        scratch_shapes=[pltpu.VMEM((tm, D_MODEL), BF16), pltpu.VMEM((tm, ROUTER_LANES), F32),
                        pltpu.VMEM((tm, D_MODEL), F32)],
        compiler_params=_params(("parallel", "parallel", "arbitrary"), 48),
        name="hier_moe",
    )(x, g, sc, sh, gt, w["wr"], w["br"], w["wg"], w["wu"], w["wd"], fg)


def _layer_weights(l, w_in, mla_q_norm, mla_w_uq, mla_kv_norm, mla_w_ukv, w_branch, w_out,
                   moe_w_group, moe_b_group, moe_w_expert, moe_b_expert, moe_w_gate, moe_w_up, moe_w_down):
    wi = w_in[l]
    b0 = MLA_Q_RANK
    b1 = b0 + MLA_KV_RANK
    b2 = b1 + MLA_ROPE
    b3 = b2 + HY_IN
    b4 = b3 + BRANCH_WIDTH
    b5 = b4 + 3 * BRANCH_WIDTH
    zpad = lambda r, c: jnp.zeros((r, c), F32)
    w_kr = wi[:, b1:b2]
    krp = jnp.concatenate([zpad(D_MODEL, MLA_NOPE), w_kr[:, 0::2], w_kr[:, 1::2],
                           zpad(D_MODEL, MLA_HEAD_PAD - MLA_NOPE - MLA_ROPE)], axis=1)
    w_na = wi[:, b4:b5]
    w_na = jnp.concatenate([w_na[:, :BRANCH_WIDTH] * NA_SCALE, w_na[:, BRANCH_WIDTH:]], axis=1)
    wa = jnp.concatenate([wi[:, :b1], krp, wi[:, b2:b4], w_na], axis=1).astype(BF16)
    uq = mla_w_uq[l].reshape(MLA_Q_RANK, MLA_HEADS, MLA_NOPE + MLA_ROPE) * (MLA_SCALE * LOG2E)
    uq = jnp.concatenate([uq[..., :MLA_NOPE], uq[..., MLA_NOPE::2], uq[..., MLA_NOPE + 1::2],
                          jnp.zeros((MLA_Q_RANK, MLA_HEADS, MLA_HEAD_PAD - MLA_NOPE - MLA_ROPE), F32)], axis=-1)
    ukv = mla_w_ukv[l].reshape(MLA_KV_RANK, MLA_HEADS, MLA_NOPE + MLA_V)
    uk = jnp.concatenate([ukv[..., :MLA_NOPE],
                          jnp.zeros((MLA_KV_RANK, MLA_HEADS, MLA_HEAD_PAD - MLA_NOPE), F32)], axis=-1)
    uv = ukv[..., MLA_NOPE:]
    wr = jnp.concatenate([moe_w_group[l], moe_w_expert[l],
                          zpad(D_MODEL, ROUTER_LANES - MOE_GROUPS - MOE_EXPERTS)], axis=1)
    br = jnp.concatenate([moe_b_group[l], moe_b_expert[l],
                          jnp.zeros((ROUTER_LANES - MOE_GROUPS - MOE_EXPERTS,), F32)])[None, :]
    return dict(
        wa=wa, qn=mla_q_norm[l][None, :], kvn=mla_kv_norm[l][None, :],
        wuq=uq.reshape(MLA_Q_RANK, -1).astype(BF16), wuk=uk.reshape(MLA_KV_RANK, -1).astype(BF16),
        wuv=uv.reshape(MLA_KV_RANK, -1).astype(BF16),
        wgate=wi[:, b5:].astype(BF16), wbranch=w_branch[l].astype(BF16), wout=w_out[l].astype(BF16),
        wr=wr.astype(BF16), br=br, wg=moe_w_gate[l].astype(BF16), wu=moe_w_up[l].astype(BF16),
        wd=moe_w_down[l].astype(BF16))


def _rope_tables(n):
    t = jnp.arange(n, dtype=jnp.int32)
    row = (t // GRID_W).astype(F32)
    col = (t % GRID_W).astype(F32)
    n_freq = MLA_ROPE // 4
    inv = ROPE_BASE ** (-jnp.arange(n_freq, dtype=F32) / n_freq)
    ang = jnp.concatenate([row[:, None] * inv, col[:, None] * inv], axis=-1)
    c, s = jnp.cos(ang), jnp.sin(ang)
    one = jnp.ones((n, MLA_NOPE), F32)
    zero = jnp.zeros((n, MLA_NOPE), F32)
    hr = MLA_ROPE // 2
    tail1 = jnp.ones((n, MLA_HEAD_PAD - MLA_NOPE - MLA_ROPE), F32)
    tail0 = jnp.zeros((n, MLA_HEAD_PAD - MLA_NOPE - MLA_ROPE), F32)
    cos = jnp.concatenate([one, c, c, tail1], axis=1)
    s1 = jnp.concatenate([zero, -s, jnp.zeros((n, hr), F32), tail0], axis=1)
    s2 = jnp.concatenate([zero, jnp.zeros((n, hr), F32), s, tail0], axis=1)
    return cos, s1, s2


def _identity_rope(n):
    return (jnp.ones((n, MLA_HEAD_PAD), F32), jnp.zeros((n, MLA_HEAD_PAD), F32),
            jnp.zeros((n, MLA_HEAD_PAD), F32))


def kernel(x, c, ctx, c_ctx, ada_w, ada_b, norm1_g, norm2_g, w_in, mla_q_norm, mla_w_uq, mla_kv_norm, mla_w_ukv, hy_short_w, hy_w1, hy_b1, hy_freq1, hy_w2, hy_b2, hy_freq2, hy_w3, hy_bias, na_rpb, w_branch, w_out, moe_w_group, moe_b_group, moe_w_expert, moe_b_expert, moe_w_gate, moe_w_up, moe_w_down, final_norm_g):
    b, n_lat, _ = x.shape
    n_ctx = ctx.shape[1]
    assert b <= 4 and n_lat % (NA_QROWS * GRID_W) == 0 and n_lat // GRID_W >= 2 * NA_QROWS

    cvec = jnp.concatenate([c, c_ctx[None, :], jnp.zeros((8 - b - 1, D_MODEL), F32)], axis=0)
    mod = _mod_vectors(cvec, ada_w, ada_b)

    rope_lat = _rope_tables(n_lat)
    rope_ctx = _identity_rope(n_ctx)
    tabs_lat = _dft_tables(n_lat)
    tabs_ctx = _small_dft_tables(n_ctx)
    fg = final_norm_g[None, :]

    for l in range(DEPTH):
        last = l == DEPTH - 1
        w = _layer_weights(l, w_in, mla_q_norm, mla_w_uq, mla_kv_norm, mla_w_ukv, w_branch, w_out,
                           moe_w_group, moe_b_group, moe_w_expert, moe_b_expert,
                           moe_w_gate, moe_w_up, moe_w_down)
        ml = mod[l, :b].reshape(b, 1, 6, D_MODEL)
        mc = jnp.broadcast_to(mod[l, b].reshape(1, 1, 6, D_MODEL), (b, 1, 6, D_MODEL))
        sh1, sc1, gt1, sh2, sc2, gt2 = [ml[:, :, i] for i in range(6)]
        sh1c, sc1c, gt1c, sh2c, sc2c, gt2c = [mc[:, :, i] for i in range(6)]
        g1 = norm1_g[l][None, :]
        g2 = norm2_g[l][None, :]
        fw = dict(
            w1=jnp.concatenate([hy_w1[l], jnp.zeros((HY_EMB_PAD - HY_EMB, HY_HIDDEN), F32)], axis=0).astype(BF16),
            b1=hy_b1[l][None, :], f1=hy_freq1[l][None, :], w2=hy_w2[l].astype(BF16), b2=hy_b2[l][None, :],
            f2=hy_freq2[l][None, :],
            w3=jnp.transpose(hy_w3[l].reshape(HY_HIDDEN, 2, HY_ORDER * HY_WIDTH), (1, 0, 2)).astype(BF16))

        n_keys = n_lat + n_ctx
        q, kall, vtall, hy, fn, naq, nak, nav = _inproj(x, g1, sc1, sh1, w, rope_lat, n_keys, 0)
        qc, kc, vc, kall, vtall, hyc, fnc, naqc, nakc, navc = _inproj(
            ctx, g1, sc1c, sh1c, w, rope_ctx, n_keys, n_lat, kv_bufs=(kall, vtall))

        oa = _flash_t(q, kall, vtall, MLA_V)
        od = _na(naq, nak, nav, nakc, navc, _na_bias_tables(na_rpb[l]))
        ob = _hyena_long(hy, hy_short_w[l], fw, hy_bias[l], tabs_lat)
        oc = _fnet_long(fn, tabs_lat)
        x_new = _merge(x, g1, sc1, sh1, gt1, oa, ob, oc, od, w)

        if not last:
            oa_c = _flash(qc, kc, vc, True)
            heads = lambda t: jnp.transpose(t.reshape(b, n_ctx, NA_HEADS, NA_HEAD_DIM), (0, 2, 1, 3))
            od_c = _flash(heads(naqc), heads(nakc), heads(navc), False)
            ob_c = _hyena_short(hyc, hy_short_w[l], fw, hy_bias[l], tabs_ctx)
            oc_c = _fnet_short(fnc, tabs_ctx)
            ctx = _merge(ctx, g1, sc1c, sh1c, gt1c, oa_c, ob_c, oc_c, od_c, w)
            ctx = _moe(ctx, g2, sc2c, sh2c, gt2c, w, fg, False)
        x = _moe(x_new, g2, sc2, sh2, gt2, w, fg, last)
    return x
```

```python
import functools
import math

import numpy as np
import jax
import jax.numpy as jnp
from jax import lax
from jax.experimental import pallas as pl
from jax.experimental.pallas import tpu as pltpu

F32 = jnp.float32
BF16 = jnp.bfloat16

D_MODEL = 1024
DEPTH = 2
GRID_W = 64
NORM_EPS = 1e-6
MASK_VALUE = -1e30
BRANCH_WIDTH = D_MODEL // 4
N_BRANCH = 4

MLA_HEADS = 4
MLA_Q_RANK = D_MODEL // 4
MLA_KV_RANK = D_MODEL // 8
MLA_NOPE = 64
MLA_ROPE = 32
MLA_V = BRANCH_WIDTH // MLA_HEADS
MLA_SCALE = 1.0 / math.sqrt(MLA_NOPE + MLA_ROPE)
MLA_HEAD_PAD = 128
MLA_VT_ROWS = MLA_V + 16
PV_DTYPE = BF16
ROPE_BASE = 10000.0
LOG2E = 1.4426950408889634

HY_WIDTH = BRANCH_WIDTH
HY_ORDER = 2
HY_BANDS = 16
HY_EMB = 1 + 2 * HY_BANDS
HY_EMB_PAD = 128
HY_HIDDEN = 64
HY_DECAY_TARGET = 1e-2
HY_DECAY_SHORT_PCT = 0.3
HY_DECAY_LONG_PCT = 1.5
HY_IN = (HY_ORDER + 1) * HY_WIDTH

FN_GROUPS = 4
FN_GROUP_DIM = BRANCH_WIDTH // FN_GROUPS

NA_HEADS = 4
NA_HEAD_DIM = BRANCH_WIDTH // NA_HEADS
NA_KH = 8
NA_KW = 16
NA_SCALE = 1.0 / math.sqrt(NA_HEAD_DIM)
NA_QROWS = 8
NA_KROWS = 16

MOE_GROUPS = 4
MOE_PER_GROUP = 4
MOE_EXPERTS = MOE_GROUPS * MOE_PER_GROUP
MOE_HIDDEN = 256
ROUTER_LANES = 128

MLA_IN = MLA_Q_RANK + MLA_KV_RANK + MLA_ROPE
MIX_IN = MLA_IN + HY_IN + BRANCH_WIDTH + 3 * BRANCH_WIDTH
MIX_PAD = MLA_Q_RANK + MLA_KV_RANK + MLA_HEAD_PAD + HY_IN + BRANCH_WIDTH + 3 * BRANCH_WIDTH

DFT_N2 = 256
FN_N2 = 128
BMM_GROUPS_PER_STEP = 4
FLASH_K_TILE = 3328
FLASH_Q_SUB = 512


def _params(sem, vmem_mb=None):
    kw = dict(dimension_semantics=sem)
    if vmem_mb is not None:
        kw["vmem_limit_bytes"] = vmem_mb << 20
    return pltpu.CompilerParams(**kw)


def _pick_tile(n, cap, mult):
    best = None
    for t in range(mult, min(n, cap) + 1, mult):
        if n % t == 0:
            best = t
    return best if best is not None else n


def _rms(x):
    return x * lax.rsqrt(jnp.mean(x * x, axis=-1, keepdims=True) + NORM_EPS)


def _normmod(x, g, sc, sh):
    return (_rms(x) * g) * (1.0 + sc) + sh


def _dot(a, b):
    return jnp.dot(a, b, preferred_element_type=F32)


def _dot_nt(a, b):
    return lax.dot_general(a, b, (((1,), (1,)), ((), ())), preferred_element_type=F32)


def _mod_kernel(c_ref, w_ref, b_ref, o_ref):
    c = c_ref[...]
    s = c * jax.nn.sigmoid(c)
    o_ref[0] = _dot(s.astype(BF16), w_ref[0].astype(BF16)) + b_ref[0]


def _mod_vectors(cvec, ada_w, ada_b):
    n6 = ada_w.shape[-1]
    tn = 1024
    return pl.pallas_call(
        _mod_kernel,
        grid=(DEPTH, n6 // tn),
        in_specs=[pl.BlockSpec((8, D_MODEL), lambda l, j: (0, 0)),
                  pl.BlockSpec((1, D_MODEL, tn), lambda l, j: (l, 0, j)),
                  pl.BlockSpec((1, 1, tn), lambda l, j: (l, 0, j))],
        out_specs=pl.BlockSpec((1, 8, tn), lambda l, j: (l, 0, j)),
        out_shape=jax.ShapeDtypeStruct((DEPTH, 8, n6), F32),
        compiler_params=_params(("parallel", "parallel")),
        name="adaln_vectors",
    )(cvec, ada_w, ada_b.reshape(DEPTH, 1, n6))


def _inproj_kernel(*refs, ctx_mode):
    (x_ref, g_ref, sc_ref, sh_ref, wa_ref, qn_ref, kvn_ref, wuq_ref, wuk_ref, wuv_ref,
     cos_ref, s1_ref, s2_ref) = refs[:13]
    if ctx_mode:
        (q_ref, k_ref, v_ref, kall_ref, vtall_ref, hy_ref, fn_ref, naq_ref, nak_ref, nav_ref) = refs[15:]
    else:
        (q_ref, kall_ref, vtall_ref, hy_ref, fn_ref, naq_ref, nak_ref, nav_ref) = refs[13:]
    h = _normmod(x_ref[0], g_ref[...], sc_ref[0], sh_ref[0]).astype(BF16)
    p = _dot(h, wa_ref[...])
    c0 = MLA_Q_RANK
    c1 = c0 + MLA_KV_RANK
    c2 = c1 + MLA_HEAD_PAD
    c3 = c2 + HY_IN
    c4 = c3 + BRANCH_WIDTH
    c5 = c4 + BRANCH_WIDTH
    c6 = c5 + BRANCH_WIDTH
    cq, ckv, krp = p[:, :c0], p[:, c0:c1], p[:, c1:c2]
    hy_ref[0] = p[:, c2:c3]
    fn_ref[0] = p[:, c3:c4]
    naq_ref[0] = p[:, c4:c5].astype(BF16)
    nak_ref[0] = p[:, c5:c6].astype(BF16)
    nav_ref[0] = p[:, c6:].astype(BF16)
    cqn = (_rms(cq) * qn_ref[...]).astype(BF16)
    ckn = (_rms(ckv) * kvn_ref[...]).astype(BF16)
    q = _dot(cqn, wuq_ref[...])
    kn = _dot(ckn, wuk_ref[...])
    v = _dot(ckn, wuv_ref[...])
    cos, s1, s2 = cos_ref[...], s1_ref[...], s2_ref[...]

    def rope(t):
        return (t * cos + pltpu.roll(t, MLA_HEAD_PAD - MLA_ROPE // 2, 1) * s1
                + pltpu.roll(t, MLA_ROPE // 2, 1) * s2)

    kr = rope(krp)
    vt = v.T
    tail_rows = lax.broadcasted_iota(jnp.int32, (MLA_VT_ROWS - MLA_V, v.shape[0]), 0)
    tail = jnp.where(tail_rows == 0, 1.0, 0.0).astype(PV_DTYPE)
    for hh in range(MLA_HEADS):
        sl = slice(hh * MLA_HEAD_PAD, (hh + 1) * MLA_HEAD_PAD)
        vs = slice(hh * MLA_V, (hh + 1) * MLA_V)
        kh = (kn[:, sl] + kr).astype(BF16)
        q_ref[0, hh] = rope(q[:, sl]).astype(BF16)
        kall_ref[0, hh] = kh
        vtall_ref[0, hh, :MLA_V] = vt[vs].astype(PV_DTYPE)
        vtall_ref[0, hh, MLA_V:] = tail
        if ctx_mode:
            k_ref[0, hh] = kh
            v_ref[0, hh] = v[:, vs].astype(BF16)


def _inproj(x, g, sc, sh, w, rope_tabs, n_keys, key_off, kv_bufs=None):
    b, n, _ = x.shape
    ctx_mode = kv_bufs is not None
    tm = _pick_tile(n, 512, 128)
    assert key_off % tm == 0
    ko = key_off // tm
    cos, s1, s2 = rope_tabs
    tok = lambda bb, i: (bb, i, 0)
    row = lambda bb, i: (bb, 0, 0)
    const2 = lambda bb, i: (0, 0)
    hd = lambda bb, i: (bb, 0, i, 0)
    tab = lambda bb, i: (i, 0)
    hp = MLA_HEADS * MLA_HEAD_PAD
    in_specs = [pl.BlockSpec((1, tm, D_MODEL), tok),
                pl.BlockSpec((1, D_MODEL), const2),
                pl.BlockSpec((1, 1, D_MODEL), row),
                pl.BlockSpec((1, 1, D_MODEL), row),
                pl.BlockSpec((D_MODEL, MIX_PAD), const2),
                pl.BlockSpec((1, MLA_Q_RANK), const2),
                pl.BlockSpec((1, MLA_KV_RANK), const2),
                pl.BlockSpec((MLA_Q_RANK, hp), const2),
                pl.BlockSpec((MLA_KV_RANK, hp), const2),
                pl.BlockSpec((MLA_KV_RANK, BRANCH_WIDTH), const2),
                pl.BlockSpec((tm, MLA_HEAD_PAD), tab),
                pl.BlockSpec((tm, MLA_HEAD_PAD), tab),
                pl.BlockSpec((tm, MLA_HEAD_PAD), tab)]
    args = [x, g, sc, sh, w["wa"], w["qn"], w["kvn"], w["wuq"], w["wuk"], w["wuv"], cos, s1, s2]
    q_spec = pl.BlockSpec((1, MLA_HEADS, tm, MLA_HEAD_PAD), hd)
    q_sd = jax.ShapeDtypeStruct((b, MLA_HEADS, n, MLA_HEAD_PAD), BF16)
    kall_spec = pl.BlockSpec((1, MLA_HEADS, tm, MLA_HEAD_PAD), lambda bb, i: (bb, 0, ko + i, 0))
    vtall_spec = pl.BlockSpec((1, MLA_HEADS, MLA_VT_ROWS, tm), lambda bb, i: (bb, 0, 0, ko + i))
    kall_sd = jax.ShapeDtypeStruct((b, MLA_HEADS, n_keys, MLA_HEAD_PAD), BF16)
    vtall_sd = jax.ShapeDtypeStruct((b, MLA_HEADS, MLA_VT_ROWS, n_keys), PV_DTYPE)
    rest_specs = [pl.BlockSpec((1, tm, HY_IN), tok)] + [pl.BlockSpec((1, tm, BRANCH_WIDTH), tok)] * 4
    rest_sd = [jax.ShapeDtypeStruct((b, n, HY_IN), F32), jax.ShapeDtypeStruct((b, n, BRANCH_WIDTH), F32)] \
        + [jax.ShapeDtypeStruct((b, n, BRANCH_WIDTH), BF16)] * 3
    aliases = {}
    if ctx_mode:
        in_specs += [pl.BlockSpec(memory_space=pl.ANY), pl.BlockSpec(memory_space=pl.ANY)]
        args += list(kv_bufs)
        out_specs = [q_spec, q_spec, pl.BlockSpec((1, MLA_HEADS, tm, MLA_V), hd), kall_spec, vtall_spec]
        out_shape = [q_sd, q_sd, jax.ShapeDtypeStruct((b, MLA_HEADS, n, MLA_V), BF16), kall_sd, vtall_sd]
        aliases = {13: 3, 14: 4}
    else:
        out_specs = [q_spec, kall_spec, vtall_spec]
        out_shape = [q_sd, kall_sd, vtall_sd]
    return pl.pallas_call(
        functools.partial(_inproj_kernel, ctx_mode=ctx_mode),
        grid=(b, n // tm),
        in_specs=in_specs,
        out_specs=out_specs + rest_specs,
        out_shape=out_shape + rest_sd,
        input_output_aliases=aliases,
        compiler_params=_params(("parallel", "parallel"), 48),
        name="input_projection",
    )(*args)


def _flash_kernel(q_ref, k_ref, v_ref, o_ref, m_ref, l_ref, acc_ref, *, nh, use_exp2):
    j = pl.program_id(2)

    @pl.when(j == 0)
    def _():
        m_ref[...] = jnp.full(m_ref.shape, MASK_VALUE, F32)
        l_ref[...] = jnp.zeros(l_ref.shape, F32)
        acc_ref[...] = jnp.zeros(acc_ref.shape, F32)

    ex = jnp.exp2 if use_exp2 else jnp.exp
    for h in range(nh):
        s = _dot_nt(q_ref[0, h], k_ref[0, h])
        m_prev = m_ref[h]
        m_cur = jnp.maximum(m_prev, jnp.max(s, axis=1, keepdims=True))
        alpha = ex(m_prev - m_cur)
        p = ex(s - m_cur)
        l_ref[h] = alpha * l_ref[h] + jnp.sum(p, axis=1, keepdims=True)
        acc_ref[h] = alpha * acc_ref[h] + _dot(p.astype(BF16), v_ref[0, h])
        m_ref[h] = m_cur

    @pl.when(j == pl.num_programs(2) - 1)
    def _():
        outs = [acc_ref[h] / l_ref[h] for h in range(nh)]
        o_ref[0] = jnp.concatenate(outs, axis=1).astype(o_ref.dtype)


def _flash(q, k, v, use_exp2):
    b, nh, lq, dk = q.shape
    lk, dv = k.shape[2], v.shape[3]
    tq = _pick_tile(lq, 512, 8)
    tk = _pick_tile(lk, 1280, 128)
    return pl.pallas_call(
        functools.partial(_flash_kernel, nh=nh, use_exp2=use_exp2),
        grid=(b, lq // tq, lk // tk),
        in_specs=[pl.BlockSpec((1, nh, tq, dk), lambda bb, i, j: (bb, 0, i, 0)),
                  pl.BlockSpec((1, nh, tk, dk), lambda bb, i, j: (bb, 0, j, 0)),
                  pl.BlockSpec((1, nh, tk, dv), lambda bb, i, j: (bb, 0, j, 0))],
        out_specs=pl.BlockSpec((1, tq, nh * dv), lambda bb, i, j: (bb, i, 0)),
        out_shape=jax.ShapeDtypeStruct((b, lq, nh * dv), BF16),
        scratch_shapes=[pltpu.VMEM((nh, tq, 1), F32), pltpu.VMEM((nh, tq, 1), F32),
                        pltpu.VMEM((nh, tq, dv), F32)],
        compiler_params=_params(("parallel", "parallel", "arbitrary"), 48),
        name="softmax_attention",
    )(q, k, v)


def _flash_t_kernel(q_ref, k_ref, vt_ref, o_ref, m_ref, acc_ref, st_ref, p_ref, *, nh, dv):
    j = pl.program_id(2)

    @pl.when(j == 0)
    def _():
        m_ref[...] = jnp.full(m_ref.shape, MASK_VALUE, F32)
        acc_ref[...] = jnp.zeros(acc_ref.shape, F32)

    rows = vt_ref.shape[2]

    tq = q_ref.shape[2]
    qs = min(tq, FLASH_Q_SUB)
    units = [(h, c) for h in range(nh) for c in range(tq // qs)]

    def scores(i):
        h, c = units[i]
        cols = slice(c * qs, (c + 1) * qs)
        st = _dot_nt(k_ref[0, h], q_ref[0, h, cols, :])
        st_ref[i % 2] = st
        m_prev = m_ref[h, :, cols]
        m_cur = jnp.maximum(m_prev, jnp.max(st, axis=0, keepdims=True))
        m_ref[h, :, cols] = m_cur
        return m_prev, m_cur

    def probs(i, m_cur):
        p_ref[i % 2] = jnp.exp2(st_ref[i % 2] - m_cur).astype(PV_DTYPE)

    def values(i, m_prev, m_cur):
        h, c = units[i]
        cols = slice(c * qs, (c + 1) * qs)
        alpha = jnp.exp2(m_prev - m_cur)
        acc_ref[h, :rows, cols] = alpha * acc_ref[h, :rows, cols] + _dot(vt_ref[0, h], p_ref[i % 2])

    n_units = len(units)
    stats = {0: scores(0)}
    if n_units > 1:
        stats[1] = scores(1)
    probs(0, stats[0][1])
    for i in range(n_units):
        values(i, *stats.pop(i))
        if i + 2 < n_units:
            stats[i + 2] = scores(i + 2)
        if i + 1 < n_units:
            probs(i + 1, stats[i + 1][1])

    @pl.when(j == pl.num_programs(2) - 1)
    def _():
        outs = []
        for h in range(nh):
            a = acc_ref[h]
            o = (a / a[dv:dv + 1]).T
            outs.append(o[:, :dv])
        o_ref[0] = jnp.concatenate(outs, axis=1).astype(o_ref.dtype)


def _flash_t(q, k, vt, dv):
    b, nh, lq, dk = q.shape
    lk, rows = k.shape[2], vt.shape[2]
    tq = _pick_tile(lq, 1024, 128)
    tk = _pick_tile(lk, FLASH_K_TILE, 128)
    return pl.pallas_call(
        functools.partial(_flash_t_kernel, nh=nh, dv=dv),
        grid=(b, lq // tq, lk // tk),
        in_specs=[pl.BlockSpec((1, nh, tq, dk), lambda bb, i, j: (bb, 0, i, 0)),
                  pl.BlockSpec((1, nh, tk, dk), lambda bb, i, j: (bb, 0, j, 0)),
                  pl.BlockSpec((1, nh, rows, tk), lambda bb, i, j: (bb, 0, 0, j))],
        out_specs=pl.BlockSpec((1, tq, nh * dv), lambda bb, i, j: (bb, i, 0)),
        out_shape=jax.ShapeDtypeStruct((b, lq, nh * dv), BF16),
        scratch_shapes=[pltpu.VMEM((nh, 1, tq), F32), pltpu.VMEM((nh, 128, tq), F32),
                        pltpu.VMEM((2, tk, min(tq, FLASH_Q_SUB)), F32),
                        pltpu.VMEM((2, tk, min(tq, FLASH_Q_SUB)), PV_DTYPE)],
        compiler_params=_params(("parallel", "parallel", "arbitrary"), 48),
        name="softmax_attention_t",
    )(q, k, vt)


def _na_kernel(q_ref, k0, k1, k2, k3, v0, v1, v2, v3, kc_ref, vc_ref, b_ref, o_ref):
    q = q_ref[0]
    kcat = jnp.concatenate([k0[0], k1[0], k2[0], k3[0]], axis=0)
    vcat = jnp.concatenate([v0[0], v1[0], v2[0], v3[0]], axis=0)
    kc, vc = kc_ref[0], vc_ref[0]
    outs = []
    for h in range(NA_HEADS):
        sl = slice(h * NA_HEAD_DIM, (h + 1) * NA_HEAD_DIM)
        qh = q[:, sl]
        s = _dot_nt(qh, kcat[:, sl]) + b_ref[0, h]
        sc = _dot_nt(qh, kc[:, sl])
        m = jnp.maximum(jnp.max(s, axis=1, keepdims=True), jnp.max(sc, axis=1, keepdims=True))
        p = jnp.exp(s - m)
        pc = jnp.exp(sc - m)
        l = jnp.sum(p, axis=1, keepdims=True) + jnp.sum(pc, axis=1, keepdims=True)
        o = _dot(p.astype(BF16), vcat[:, sl]) + _dot(pc.astype(BF16), vc[:, sl])
        outs.append(o / l)
    o_ref[0] = jnp.concatenate(outs, axis=1).astype(o_ref.dtype)


def _na_bias_tables(rpb):
    j = np.arange(NA_QROWS)[:, None]
    m = np.arange(NA_KROWS)[None, :]
    valid = np.zeros((3, NA_QROWS, NA_KROWS), bool)
    st0 = np.maximum(j - NA_KH // 2, 0)
    valid[0] = (m - 4 >= st0) & (m - 4 < st0 + NA_KH)
    valid[1] = (m - j >= 0) & (m - j < NA_KH)
    st2 = 4 + np.minimum(j - NA_KH // 2, 0)
    valid[2] = (m < 12) & (m >= st2) & (m < st2 + NA_KH)
    dr_idx = np.clip(m - 4 - j + NA_KH - 1, 0, 2 * NA_KH - 2)
    qc = np.arange(GRID_W)[:, None]
    kc = np.arange(GRID_W)[None, :]
    dc_idx = np.clip(kc - qc, -(NA_KW - 1), NA_KW - 1) + NA_KW - 1
    cs = np.clip(qc - NA_KW // 2, 0, GRID_W - NA_KW)
    col_ok = (kc >= cs) & (kc < cs + NA_KW)
    t = rpb.astype(F32)[:, dr_idx]
    t = t[..., dc_idx]
    t = jnp.transpose(t, (0, 1, 3, 2, 4))
    ok = valid[:, None, :, None, :, None] & col_ok[None, None, None, :, None, :]
    out = jnp.where(jnp.asarray(ok), t[None], MASK_VALUE)
    return out.reshape(3, NA_HEADS, NA_QROWS * GRID_W, NA_KROWS * GRID_W)


def _na(q, k, v, kc, vc, bias):
    b, n, w = q.shape
    c = kc.shape[1]
    tq = NA_QROWS * GRID_W
    tkb = 4 * GRID_W
    nb = n // tq
    nkb = n // tkb
    kmaps = [lambda bb, i: (bb, jnp.maximum(2 * i - 1, 0), 0),
             lambda bb, i: (bb, 2 * i, 0),
             lambda bb, i: (bb, 2 * i + 1, 0),
             lambda bb, i: (bb, jnp.minimum(2 * i + 2, nkb - 1), 0)]
    kspecs = [pl.BlockSpec((1, tkb, w), mp) for mp in kmaps]
    return pl.pallas_call(
        _na_kernel,
        grid=(b, nb),
        in_specs=[pl.BlockSpec((1, tq, w), lambda bb, i: (bb, i, 0))] + kspecs + kspecs
                 + [pl.BlockSpec((1, c, w), lambda bb, i: (bb, 0, 0)),
                    pl.BlockSpec((1, c, w), lambda bb, i: (bb, 0, 0)),
                    pl.BlockSpec((1, NA_HEADS, tq, NA_KROWS * GRID_W),
                                 lambda bb, i: (jnp.where(i == 0, 0, jnp.where(i == nb - 1, 2, 1)), 0, 0, 0))],
        out_specs=pl.BlockSpec((1, tq, w), lambda bb, i: (bb, i, 0)),
        out_shape=jax.ShapeDtypeStruct((b, n, w), BF16),
        compiler_params=_params(("parallel", "arbitrary"), 56),
        name="neighbourhood_attention",
    )(q, k, k, k, k, v, v, v, v, kc, vc, bias)


def _sconv_kernel(u_ref, pv_ref, nx_ref, w_ref, v_ref, x1_ref, x2_ref, *, nt):
    i = pl.program_id(1)
    u = u_ref[0]
    tl = u.shape[0]
    prev = jnp.where(i > 0, pv_ref[0, 7:8, :], 0.0)
    nxt = jnp.where(i < nt - 1, nx_ref[0, 0:1, :], 0.0)
    rows = lax.broadcasted_iota(jnp.int32, u.shape, 0)
    um = jnp.where(rows == 0, prev, pltpu.roll(u, 1, 0))
    up = jnp.where(rows == tl - 1, nxt, pltpu.roll(u, tl - 1, 0))
    w = w_ref[...]
    y = um * w[0:1] + u * w[1:2] + up * w[2:3]
    v_ref[...] = y[:, :HY_WIDTH]
    x1_ref[...] = y[:, HY_WIDTH:2 * HY_WIDTH]
    x2_ref[...] = y[:, 2 * HY_WIDTH:]


def _sconv(u, w):
    b, n, _ = u.shape
    tl = _pick_tile(n, 512, 8)
    nt = n // tl
    r8 = tl // 8
    osd = jax.ShapeDtypeStruct((n, b * HY_WIDTH), F32)
    ospec = pl.BlockSpec((tl, HY_WIDTH), lambda bb, i: (i, bb))
    return pl.pallas_call(
        functools.partial(_sconv_kernel, nt=nt),
        grid=(b, nt),
        in_specs=[pl.BlockSpec((1, tl, HY_IN), lambda bb, i: (bb, i, 0)),
                  pl.BlockSpec((1, 8, HY_IN), lambda bb, i: (bb, jnp.maximum(i * r8 - 1, 0), 0)),
                  pl.BlockSpec((1, 8, HY_IN), lambda bb, i: (bb, jnp.minimum((i + 1) * r8, n // 8 - 1), 0)),
                  pl.BlockSpec((3, HY_IN), lambda bb, i: (0, 0))],
        out_specs=[ospec, ospec, ospec],
        out_shape=[osd, osd, osd],
        compiler_params=_params(("parallel", "parallel")),
        name="hyena_short_conv",
    )(u, u, u, w)


def _filt_kernel(z_ref, w1_ref, b1_ref, f1_ref, w2_ref, b2_ref, f2_ref, w3_ref, win_ref, g_ref, ss_ref):
    d = pl.program_id(0)
    i = pl.program_id(1)
    h = jnp.sin(f1_ref[...] * (_dot(z_ref[0].astype(BF16), w1_ref[...]) + b1_ref[...]))
    h = jnp.sin(f2_ref[...] * (_dot(h.astype(BF16), w2_ref[...]) + b2_ref[...]))
    h = _dot(h.astype(BF16), w3_ref[0])
    h = h * jnp.concatenate([win_ref[0]] * HY_ORDER, axis=1)
    rows = lax.broadcasted_iota(jnp.int32, h.shape, 0)
    h = jnp.where((rows == 0) & (i == 0) & (d == 1), 0.0, h)
    g_ref[0] = h
    ss = jnp.sum(h * h, axis=0, keepdims=True)
    first = (i == 0) & (d == 0)

    @pl.when(first)
    def _():
        ss_ref[...] = ss

    @pl.when(jnp.logical_not(first))
    def _():
        ss_ref[...] += ss


def _padded_filter(n, fw):
    r = jnp.arange(n, dtype=jnp.int32)
    idx = jnp.stack([r, jnp.mod(n - r, n)], axis=0).astype(F32)[:, :, None]
    t = idx / (n - 1)
    bands = jnp.linspace(1e-4, HY_BANDS - 1, HY_BANDS, dtype=F32)
    ang = (2.0 * math.pi / n) * idx * bands
    z = jnp.concatenate([t, jnp.cos(ang), -jnp.sin(ang), jnp.zeros((2, n, HY_EMB_PAD - HY_EMB), F32)], axis=-1)
    deltas = jnp.linspace(math.log(HY_DECAY_TARGET) / HY_DECAY_LONG_PCT,
                          math.log(HY_DECAY_TARGET) / HY_DECAY_SHORT_PCT, HY_WIDTH, dtype=F32)
    window = jnp.exp(-t * jnp.abs(deltas))
    tl = _pick_tile(n, 512, 8)
    half = HY_ORDER * HY_WIDTH
    c2 = lambda d, i: (0, 0)
    g, ss = pl.pallas_call(
        _filt_kernel,
        grid=(2, n // tl),
        in_specs=[pl.BlockSpec((1, tl, HY_EMB_PAD), lambda d, i: (d, i, 0)),
                  pl.BlockSpec((HY_EMB_PAD, HY_HIDDEN), c2), pl.BlockSpec((1, HY_HIDDEN), c2),
                  pl.BlockSpec((1, HY_HIDDEN), c2),
                  pl.BlockSpec((HY_HIDDEN, HY_HIDDEN), c2), pl.BlockSpec((1, HY_HIDDEN), c2),
                  pl.BlockSpec((1, HY_HIDDEN), c2),
                  pl.BlockSpec((1, HY_HIDDEN, half), lambda d, i: (d, 0, 0)),
                  pl.BlockSpec((1, tl, HY_WIDTH), lambda d, i: (d, i, 0))],
        out_specs=[pl.BlockSpec((1, tl, half), lambda d, i: (d, i, 0)), pl.BlockSpec((1, half), c2)],
        out_shape=[jax.ShapeDtypeStruct((2, n, half), F32), jax.ShapeDtypeStruct((1, half), F32)],
        compiler_params=_params(("arbitrary", "arbitrary")),
        name="hyena_filter_mlp",
    )(z, fw["w1"], fw["b1"], fw["f1"], fw["w2"], fw["b2"], fw["f2"], fw["w3"], window)
    return g.reshape(2 * n, half), lax.rsqrt(ss + NORM_EPS)


def _colmm_kernel(*refs, epi):
    w_ref, x_ref = refs[0], refs[1]
    o_ref = refs[-1]
    acc = _dot(w_ref[...], x_ref[...].astype(BF16))
    if epi == "gate":
        xg_ref, z_ref, b_ref = refs[2:5]
        acc = xg_ref[...] * (acc + z_ref[...] * b_ref[...])
    elif epi == "scale":
        acc = acc * refs[2][...]
    elif epi == "cmul":
        g = refs[2][...]
        mh = acc.shape[0] // 2
        ar, ai, gr, gi = acc[:mh], acc[mh:], g[:mh], g[mh:]
        acc = jnp.concatenate([ar * gr - ai * gi, ar * gi + ai * gr], axis=0)
    o_ref[...] = acc.astype(o_ref.dtype)


def _colmm(w, x, out_dtype, epi="none", extra=(), tn_cap=4096, g_col=None):
    m, k = w.shape
    nc = x.shape[1]
    tn = _pick_tile(nc, tn_cap, 128)
    col = lambda j: (0, j)
    specs = [pl.BlockSpec((m, k), lambda j: (0, 0)), pl.BlockSpec((k, tn), col)]
    if epi == "gate":
        specs += [pl.BlockSpec((m, tn), col), pl.BlockSpec((m, tn), col), pl.BlockSpec((1, tn), col)]
    elif epi == "scale":
        specs += [pl.BlockSpec((1, tn), col)]
    elif epi == "cmul":
        specs += [pl.BlockSpec((m, tn), lambda j: (0, g_col))]
    return pl.pallas_call(
        functools.partial(_colmm_kernel, epi=epi),
        grid=(nc // tn,),
        in_specs=specs,
        out_specs=pl.BlockSpec((m, tn), col),
        out_shape=jax.ShapeDtypeStruct((m, nc), out_dtype),
        compiler_params=_params(("parallel",), 48),
        name="dft_stage_" + epi,
    )(w, x, *extra)


def _bmm_kernel(*refs, epi, cplx_out, gs):
    m_ref, x_ref = refs[0], refs[1]
    o_ref = refs[-1]
    for s in range(gs):
        x = x_ref[:, s]
        x = x.reshape(x.shape[0] * x.shape[1], x.shape[2])
        acc = _dot(m_ref[s], x)
        ph = acc.shape[0] // 2
        if epi == "cmul":
            g_ref = refs[2]
            gr, gi = g_ref[0, s], g_ref[1, s]
            cg = gr.shape[1]
            res_r, res_i = [], []
            for t in range(acc.shape[1] // cg):
                ar, ai = acc[:ph, t * cg:(t + 1) * cg], acc[ph:, t * cg:(t + 1) * cg]
                res_r.append(ar * gr - ai * gi)
                res_i.append(ar * gi + ai * gr)
            o_ref[0, s] = jnp.concatenate(res_r, axis=1).astype(o_ref.dtype)
            o_ref[1, s] = jnp.concatenate(res_i, axis=1).astype(o_ref.dtype)
        elif epi == "scale":
            sc = refs[2][...]
            o_ref[0, s] = (acc[:ph] * sc).astype(o_ref.dtype)
            o_ref[1, s] = (acc[ph:] * sc).astype(o_ref.dtype)
        elif cplx_out:
            o_ref[0, s] = acc[:ph].astype(o_ref.dtype)
            o_ref[1, s] = acc[ph:].astype(o_ref.dtype)
        else:
            o_ref[s] = acc.astype(o_ref.dtype)


def _bmm(mats, x, out_dtype, epi="none", extra=None, g_col=0, cplx_out=True):
    ng, p, qq = mats.shape
    c = x.shape[3]
    gs = _pick_tile(ng, BMM_GROUPS_PER_STEP, 1)
    specs = [pl.BlockSpec((gs, p, qq), lambda g: (g, 0, 0)),
             pl.BlockSpec((2, gs, qq // 2, c), lambda g: (0, g, 0, 0))]
    args = [mats, x]
    if epi == "cmul":
        cg = HY_WIDTH
        specs.append(pl.BlockSpec((2, gs, p // 2, cg), lambda g: (0, g, 0, g_col)))
        args.append(extra)
    elif epi == "scale":
        specs.append(pl.BlockSpec((1, c), lambda g: (0, 0)))
        args.append(extra)
    if cplx_out:
        ospec = pl.BlockSpec((2, gs, p // 2, c), lambda g: (0, g, 0, 0))
        oshape = jax.ShapeDtypeStruct((2, ng, p // 2, c), out_dtype)
    else:
        ospec = pl.BlockSpec((gs, p, c), lambda g: (g, 0, 0))
        oshape = jax.ShapeDtypeStruct((ng, p, c), out_dtype)
    return pl.pallas_call(
        functools.partial(_bmm_kernel, epi=epi, cplx_out=cplx_out, gs=gs),
        grid=(ng // gs,),
        in_specs=specs,
        out_specs=ospec,
        out_shape=oshape,
        compiler_params=_params(("parallel",), 48),
        name="dft_batched_" + epi,
    )(*args)


def _angle(num, den):
    return (2.0 * math.pi / den) * jnp.mod(num, den).astype(F32)


def _dft_tables(n_lat):
    nn = 2 * n_lat
    n2 = DFT_N2
    n1 = nn // n2
    i1 = jnp.arange(n1, dtype=jnp.int32)
    a1 = _angle(i1[:, None] * i1[None, :], n1)
    c1, s1 = jnp.cos(a1), jnp.sin(a1)
    t = {}
    t["hy_f1"] = jnp.concatenate([c1, -s1], axis=0).astype(BF16)
    t["hy_i2"] = jnp.concatenate([c1[:n1 // 2], -s1[:n1 // 2]], axis=1).astype(BF16)
    i2 = jnp.arange(n2, dtype=jnp.int32)
    kk = i1[:, None, None] + n1 * i2[None, :, None]
    a2 = _angle(kk * i2[None, None, :], nn)
    c2, s2 = jnp.cos(a2), jnp.sin(a2)
    fwd = jnp.concatenate([jnp.concatenate([c2, s2], axis=2), jnp.concatenate([-s2, c2], axis=2)], axis=1)
    t["hy_fwd"] = fwd.astype(BF16)
    t["hy_inv"] = jnp.swapaxes(fwd, 1, 2).astype(BF16)
    m2 = FN_N2
    m1 = n_lat // m2
    j1 = jnp.arange(m1, dtype=jnp.int32)
    b1 = _angle(j1[:, None] * j1[None, :], m1)
    cb, sb = jnp.cos(b1), jnp.sin(b1)
    t["fn_s1"] = jnp.concatenate([jnp.concatenate([cb, sb], axis=1),
                                  jnp.concatenate([-sb, cb], axis=1)], axis=0).astype(BF16)
    j2 = jnp.arange(m2, dtype=jnp.int32)
    kf = j1[:, None, None] + m1 * j2[None, :, None]
    b2 = _angle(kf * j2[None, None, :], n_lat)
    norm = 1.0 / math.sqrt(n_lat * FN_GROUP_DIM)
    t["fn_s3"] = (norm * jnp.concatenate([jnp.cos(b2), jnp.sin(b2)], axis=2)).astype(BF16)
    return t


def _small_dft_tables(n):
    nn = 2 * n
    k = jnp.arange(nn, dtype=jnp.int32)
    a = _angle(k[:, None] * k[None, :], nn)
    c, s = jnp.cos(a), jnp.sin(a)
    t = {}
    t["f_full"] = jnp.concatenate([c, -s], axis=0).astype(BF16)
    t["f_half"] = jnp.concatenate([c[:, :n], -s[:, :n]], axis=0).astype(BF16)
    t["i_half"] = jnp.concatenate([c[:n], -s[:n]], axis=1).astype(BF16)
    j = jnp.arange(n, dtype=jnp.int32)
    b = _angle(j[:, None] * j[None, :], n)
    norm = 1.0 / math.sqrt(n * FN_GROUP_DIM)
    t["fn"] = (norm * jnp.concatenate([jnp.cos(b), jnp.sin(b)], axis=1)).astype(BF16)
    return t


def _hyena_long(hy, short_w, fw, bias, tabs):
    b, n, _ = hy.shape
    cw = b * HY_WIDTH
    n2 = DFT_N2
    n1 = 2 * n // n2
    v, x1, x2 = _sconv(hy, short_w)
    g, gscale = _padded_filter(n, fw)
    half = HY_ORDER * HY_WIDTH
    ga = _colmm(tabs["hy_f1"], g.reshape(n1, n2 * half), BF16)
    gspec = _bmm(tabs["hy_fwd"], ga.reshape(2, n1, n2, half), F32, epi="scale",
                 extra=gscale * (1.0 / (2 * n)))
    z = v
    for o, xg in enumerate((x1, x2)):
        a = _colmm(tabs["hy_f1"][:, :n1 // 2], z.reshape(n1 // 2, n2 * cw), BF16)
        y = _bmm(tabs["hy_fwd"], a.reshape(2, n1, n2, cw), BF16, epi="cmul", extra=gspec, g_col=o)
        bm = _bmm(tabs["hy_inv"], y, BF16)
        brow = jnp.tile(bias[o], n2 * b)[None, :]
        z = _colmm(tabs["hy_i2"], bm.reshape(2 * n1, n2 * cw), F32, epi="gate",
                   extra=(xg.reshape(n1 // 2, n2 * cw), z.reshape(n1 // 2, n2 * cw), brow))
        z = z.reshape(n, cw)
    return z


def _hyena_short(hy, short_w, fw, bias, tabs):
    b, n, _ = hy.shape
    cw = b * HY_WIDTH
    v, x1, x2 = _sconv(hy, short_w)
    g, gscale = _padded_filter(n, fw)
    gspec = _colmm(tabs["f_full"], g, F32, epi="scale", extra=(gscale * (1.0 / (2 * n)),), tn_cap=256)
    z = v
    for o, xg in enumerate((x1, x2)):
        y = _colmm(tabs["f_half"], z, BF16, epi="cmul", extra=(gspec,), tn_cap=HY_WIDTH, g_col=o)
        brow = jnp.tile(bias[o], b)[None, :]
        z = _colmm(tabs["i_half"], y, F32, epi="gate", extra=(xg, z, brow), tn_cap=256)
    return z


def _fn0_kernel(x_ref, w_ref, o_ref):
    acc = _dot(x_ref[0].astype(BF16), w_ref[...])
    o_ref[0] = acc[:, :BRANCH_WIDTH].astype(o_ref.dtype)
    o_ref[1] = acc[:, BRANCH_WIDTH:].astype(o_ref.dtype)


def _fnet_channel_dft(u):
    b, n, w = u.shape
    j = np.arange(FN_GROUP_DIM)
    ang = 2.0 * np.pi * ((j[:, None] * j[None, :]) % FN_GROUP_DIM) / FN_GROUP_DIM
    eye = np.eye(FN_GROUPS)
    wc = np.concatenate([np.kron(eye, np.cos(ang)), -np.kron(eye, np.sin(ang))], axis=1)
    wc = jnp.asarray(wc, F32).astype(BF16)
    tm = _pick_tile(n, 1024, 8)
    return pl.pallas_call(
        _fn0_kernel,
        grid=(b, n // tm),
        in_specs=[pl.BlockSpec((1, tm, w), lambda bb, i: (bb, i, 0)),
                  pl.BlockSpec((w, 2 * w), lambda bb, i: (0, 0))],
        out_specs=pl.BlockSpec((2, tm, w), lambda bb, i: (0, i, bb)),
        out_shape=jax.ShapeDtypeStruct((2, n, b * w), BF16),
        compiler_params=_params(("parallel", "parallel")),
        name="fnet_channel_dft",
    )(u, wc)


def _fnet_long(u, tabs):
    b, n, w = u.shape
    cw = b * w
    m2 = FN_N2
    m1 = n // m2
    wri = _fnet_channel_dft(u)
    a = _colmm(tabs["fn_s1"], wri.reshape(2 * m1, m2 * cw), BF16)
    y = _bmm(tabs["fn_s3"], a.reshape(2, m1, m2, cw), BF16, cplx_out=False)
    return jnp.transpose(y, (1, 0, 2)).reshape(n, cw)


def _fnet_short(u, tabs):
    b, n, w = u.shape
    wri = _fnet_channel_dft(u)
    return _colmm(tabs["fn"], wri.reshape(2 * n, b * w), BF16, tn_cap=256)


def _merge_kernel(x_ref, g_ref, sc_ref, sh_ref, gt_ref, oa_ref, ob_ref, oc_ref, od_ref,
                  wg_ref, wb_ref, wo_ref, o_ref):
    x = x_ref[0]
    hb = _normmod(x, g_ref[...], sc_ref[0], sh_ref[0]).astype(BF16)
    branches = (oa_ref[0], ob_ref[...], oc_ref[...], od_ref[0])
    acc = jnp.zeros(x.shape, F32)
    for nbr, o in enumerate(branches):
        gate = _dot(hb, wg_ref[:, nbr * D_MODEL:(nbr + 1) * D_MODEL])
        proj = _dot(o.astype(BF16), wb_ref[nbr])
        acc = acc + jax.nn.sigmoid(gate) * proj
    m = _dot(acc.astype(BF16), wo_ref[...])
    o_ref[0] = x + gt_ref[0] * m


def _merge(x, g, sc, sh, gt, oa, ob, oc, od, w):
    b, n, _ = x.shape
    tm = _pick_tile(n, 512, 8)
    tok = lambda bb, i: (bb, i, 0)
    row = lambda bb, i: (bb, 0, 0)
    seq = lambda bb, i: (i, bb)
    c2 = lambda bb, i: (0, 0)
    return pl.pallas_call(
        _merge_kernel,
        grid=(b, n // tm),
        in_specs=[pl.BlockSpec((1, tm, D_MODEL), tok),
                  pl.BlockSpec((1, D_MODEL), c2),
                  pl.BlockSpec((1, 1, D_MODEL), row), pl.BlockSpec((1, 1, D_MODEL), row),
                  pl.BlockSpec((1, 1, D_MODEL), row),
                  pl.BlockSpec((1, tm, BRANCH_WIDTH), tok),
                  pl.BlockSpec((tm, BRANCH_WIDTH), seq),
                  pl.BlockSpec((tm, BRANCH_WIDTH), seq),
                  pl.BlockSpec((1, tm, BRANCH_WIDTH), tok),
                  pl.BlockSpec((D_MODEL, N_BRANCH * D_MODEL), c2),
                  pl.BlockSpec((N_BRANCH, BRANCH_WIDTH, D_MODEL), lambda bb, i: (0, 0, 0)),
                  pl.BlockSpec((D_MODEL, D_MODEL), c2)],
        out_specs=pl.BlockSpec((1, tm, D_MODEL), tok),
        out_shape=jax.ShapeDtypeStruct(x.shape, F32),
        compiler_params=_params(("parallel", "parallel"), 56),
        name="branch_merge",
    )(x, g, sc, sh, gt, oa, ob, oc, od, w["wgate"], w["wbranch"], w["wout"])


def _moe_kernel(x_ref, g_ref, sc_ref, sh_ref, gt_ref, wr_ref, br_ref, wg_ref, wu_ref, wd_ref, fg_ref,
                o_ref, h_s, wt_s, acc_s, *, final):
    e = pl.program_id(2)

    @pl.when(e == 0)
    def _():
        hb = _normmod(x_ref[0], g_ref[...], sc_ref[0], sh_ref[0]).astype(BF16)
        h_s[...] = hb
        r = _dot(hb, wr_ref[...]) + br_ref[...]
        lane = lax.broadcasted_iota(jnp.int32, r.shape, 1)
        neg = jnp.float32(-jnp.inf)
        isg = lane < MOE_GROUPS
        gmax = jnp.max(jnp.where(isg, r, neg), axis=1, keepdims=True)
        gsum = jnp.sum(jnp.where(isg, jnp.exp(r - gmax), 0.0), axis=1, keepdims=True)
        g_p = 1.0 / gsum
        gidx = jnp.min(jnp.where(isg & (r == gmax), lane, ROUTER_LANES), axis=1, keepdims=True)
        ise = ((lane >= MOE_GROUPS) & (lane < MOE_GROUPS + MOE_EXPERTS)
               & (jnp.right_shift(lane - MOE_GROUPS, 2) == gidx))
        el = jnp.where(ise, r, neg)
        e1 = jnp.max(el, axis=1, keepdims=True)
        i1 = jnp.min(jnp.where(ise & (r == e1), lane, ROUTER_LANES), axis=1, keepdims=True)
        el2 = jnp.where(lane == i1, neg, el)
        e2 = jnp.max(el2, axis=1, keepdims=True)
        i2 = jnp.min(jnp.where(el2 == e2, lane, ROUTER_LANES), axis=1, keepdims=True)
        d = jnp.exp(e2 - e1)
        w1 = 1.0 / (1.0 + d)
        w2 = d / (1.0 + d)
        wt_s[...] = jnp.where(lane == i1, g_p * w1, jnp.where(lane == i2, g_p * w2, 0.0))
        acc_s[...] = jnp.zeros(acc_s.shape, F32)

    hb = h_s[...]
    a = _dot(hb, wg_ref[0])
    u = _dot(hb, wu_ref[0])
    wt = wt_s[...]
    lane = lax.broadcasted_iota(jnp.int32, wt.shape, 1)
    wcol = jnp.sum(jnp.where(lane == e + MOE_GROUPS, wt, 0.0), axis=1, keepdims=True)
    act = (a * jax.nn.sigmoid(a)) * u * wcol
    acc_s[...] += _dot(act.astype(BF16), wd_ref[0])

    @pl.when(e == MOE_EXPERTS - 1)
    def _():
        y = x_ref[0] + gt_ref[0] * acc_s[...]
        if final:
            y = _rms(y) * fg_ref[...]
        o_ref[0] = y


def _moe(x, g, sc, sh, gt, w, fg, final):
    b, n, _ = x.shape
    tm = _pick_tile(n, 1024, 8)
    tok = lambda bb, i, e: (bb, i, 0)
    row = lambda bb, i, e: (bb, 0, 0)
    c2 = lambda bb, i, e: (0, 0)
    ex = lambda bb, i, e: (e, 0, 0)
    return pl.pallas_call(
        functools.partial(_moe_kernel, final=final),
        grid=(b, n // tm, MOE_EXPERTS),
        in_specs=[pl.BlockSpec((1, tm, D_MODEL), tok),
                  pl.BlockSpec((1, D_MODEL), c2),
                  pl.BlockSpec((1, 1, D_MODEL), row), pl.BlockSpec((1, 1, D_MODEL), row),
                  pl.BlockSpec((1, 1, D_MODEL), row),
                  pl.BlockSpec((D_MODEL, ROUTER_LANES), c2), pl.BlockSpec((1, ROUTER_LANES), c2),
                  pl.BlockSpec((1, D_MODEL, MOE_HIDDEN), ex),
                  pl.BlockSpec((1, D_MODEL, MOE_HIDDEN), ex),
                  pl.BlockSpec((1, MOE_HIDDEN, D_MODEL), ex),
                  pl.BlockSpec((1, D_MODEL), c2)],
        out_specs=pl.BlockSpec((1, tm, D_MODEL), tok),
        out_shape=jax.ShapeDtypeStruct(x.shape, F32),
        scratch_shapes=[pltpu.VMEM((tm, D_MODEL), BF16), pltpu.VMEM((tm, ROUTER_LANES), F32),
                        pltpu.VMEM((tm, D_MODEL), F32)],
        compiler_params=_params(("parallel", "parallel", "arbitrary"), 48),
        name="hier_moe",
    )(x, g, sc, sh, gt, w["wr"], w["br"], w["wg"], w["wu"], w["wd"], fg)


def _layer_weights(l, w_in, mla_q_norm, mla_w_uq, mla_kv_norm, mla_w_ukv, w_branch, w_out,
                   moe_w_group, moe_b_group, moe_w_expert, moe_b_expert, moe_w_gate, moe_w_up, moe_w_down):
    wi = w_in[l]
    b0 = MLA_Q_RANK
    b1 = b0 + MLA_KV_RANK
    b2 = b1 + MLA_ROPE
    b3 = b2 + HY_IN
    b4 = b3 + BRANCH_WIDTH
    b5 = b4 + 3 * BRANCH_WIDTH
    zpad = lambda r, c: jnp.zeros((r, c), F32)
    w_kr = wi[:, b1:b2]
    krp = jnp.concatenate([zpad(D_MODEL, MLA_NOPE), w_kr[:, 0::2], w_kr[:, 1::2],
                           zpad(D_MODEL, MLA_HEAD_PAD - MLA_NOPE - MLA_ROPE)], axis=1)
    w_na = wi[:, b4:b5]
    w_na = jnp.concatenate([w_na[:, :BRANCH_WIDTH] * NA_SCALE, w_na[:, BRANCH_WIDTH:]], axis=1)
    wa = jnp.concatenate([wi[:, :b1], krp, wi[:, b2:b4], w_na], axis=1).astype(BF16)
    uq = mla_w_uq[l].reshape(MLA_Q_RANK, MLA_HEADS, MLA_NOPE + MLA_ROPE) * (MLA_SCALE * LOG2E)
    uq = jnp.concatenate([uq[..., :MLA_NOPE], uq[..., MLA_NOPE::2], uq[..., MLA_NOPE + 1::2],
                          jnp.zeros((MLA_Q_RANK, MLA_HEADS, MLA_HEAD_PAD - MLA_NOPE - MLA_ROPE), F32)], axis=-1)
    ukv = mla_w_ukv[l].reshape(MLA_KV_RANK, MLA_HEADS, MLA_NOPE + MLA_V)
    uk = jnp.concatenate([ukv[..., :MLA_NOPE],
                          jnp.zeros((MLA_KV_RANK, MLA_HEADS, MLA_HEAD_PAD - MLA_NOPE), F32)], axis=-1)
    uv = ukv[..., MLA_NOPE:]
    wr = jnp.concatenate([moe_w_group[l], moe_w_expert[l],
                          zpad(D_MODEL, ROUTER_LANES - MOE_GROUPS - MOE_EXPERTS)], axis=1)
    br = jnp.concatenate([moe_b_group[l], moe_b_expert[l],
                          jnp.zeros((ROUTER_LANES - MOE_GROUPS - MOE_EXPERTS,), F32)])[None, :]
    return dict(
        wa=wa, qn=mla_q_norm[l][None, :], kvn=mla_kv_norm[l][None, :],
        wuq=uq.reshape(MLA_Q_RANK, -1).astype(BF16), wuk=uk.reshape(MLA_KV_RANK, -1).astype(BF16),
        wuv=uv.reshape(MLA_KV_RANK, -1).astype(BF16),
        wgate=wi[:, b5:].astype(BF16), wbranch=w_branch[l].astype(BF16), wout=w_out[l].astype(BF16),
        wr=wr.astype(BF16), br=br, wg=moe_w_gate[l].astype(BF16), wu=moe_w_up[l].astype(BF16),
        wd=moe_w_down[l].astype(BF16))


def _rope_tables(n):
    t = jnp.arange(n, dtype=jnp.int32)
    row = (t // GRID_W).astype(F32)
    col = (t % GRID_W).astype(F32)
    n_freq = MLA_ROPE // 4
    inv = ROPE_BASE ** (-jnp.arange(n_freq, dtype=F32) / n_freq)
    ang = jnp.concatenate([row[:, None] * inv, col[:, None] * inv], axis=-1)
    c, s = jnp.cos(ang), jnp.sin(ang)
    one = jnp.ones((n, MLA_NOPE), F32)
    zero = jnp.zeros((n, MLA_NOPE), F32)
    hr = MLA_ROPE // 2
    tail1 = jnp.ones((n, MLA_HEAD_PAD - MLA_NOPE - MLA_ROPE), F32)
    tail0 = jnp.zeros((n, MLA_HEAD_PAD - MLA_NOPE - MLA_ROPE), F32)
    cos = jnp.concatenate([one, c, c, tail1], axis=1)
    s1 = jnp.concatenate([zero, -s, jnp.zeros((n, hr), F32), tail0], axis=1)
    s2 = jnp.concatenate([zero, jnp.zeros((n, hr), F32), s, tail0], axis=1)
    return cos, s1, s2


def _identity_rope(n):
    return (jnp.ones((n, MLA_HEAD_PAD), F32), jnp.zeros((n, MLA_HEAD_PAD), F32),
            jnp.zeros((n, MLA_HEAD_PAD), F32))


def kernel(x, c, ctx, c_ctx, ada_w, ada_b, norm1_g, norm2_g, w_in, mla_q_norm, mla_w_uq, mla_kv_norm, mla_w_ukv, hy_short_w, hy_w1, hy_b1, hy_freq1, hy_w2, hy_b2, hy_freq2, hy_w3, hy_bias, na_rpb, w_branch, w_out, moe_w_group, moe_b_group, moe_w_expert, moe_b_expert, moe_w_gate, moe_w_up, moe_w_down, final_norm_g):
    b, n_lat, _ = x.shape
    n_ctx = ctx.shape[1]
    assert b <= 4 and n_lat % (NA_QROWS * GRID_W) == 0 and n_lat // GRID_W >= 2 * NA_QROWS

    cvec = jnp.concatenate([c, c_ctx[None, :], jnp.zeros((8 - b - 1, D_MODEL), F32)], axis=0)
    mod = _mod_vectors(cvec, ada_w, ada_b)

    rope_lat = _rope_tables(n_lat)
    rope_ctx = _identity_rope(n_ctx)
    tabs_lat = _dft_tables(n_lat)
    tabs_ctx = _small_dft_tables(n_ctx)
    fg = final_norm_g[None, :]

    for l in range(DEPTH):
        last = l == DEPTH - 1
        w = _layer_weights(l, w_in, mla_q_norm, mla_w_uq, mla_kv_norm, mla_w_ukv, w_branch, w_out,
                           moe_w_group, moe_b_group, moe_w_expert, moe_b_expert,
                           moe_w_gate, moe_w_up, moe_w_down)
        ml = mod[l, :b].reshape(b, 1, 6, D_MODEL)
        mc = jnp.broadcast_to(mod[l, b].reshape(1, 1, 6, D_MODEL), (b, 1, 6, D_MODEL))
        sh1, sc1, gt1, sh2, sc2, gt2 = [ml[:, :, i] for i in range(6)]
        sh1c, sc1c, gt1c, sh2c, sc2c, gt2c = [mc[:, :, i] for i in range(6)]
        g1 = norm1_g[l][None, :]
        g2 = norm2_g[l][None, :]
        fw = dict(
            w1=jnp.concatenate([hy_w1[l], jnp.zeros((HY_EMB_PAD - HY_EMB, HY_HIDDEN), F32)], axis=0).astype(BF16),
            b1=hy_b1[l][None, :], f1=hy_freq1[l][None, :], w2=hy_w2[l].astype(BF16), b2=hy_b2[l][None, :],
            f2=hy_freq2[l][None, :],
            w3=jnp.transpose(hy_w3[l].reshape(HY_HIDDEN, 2, HY_ORDER * HY_WIDTH), (1, 0, 2)).astype(BF16))

        n_keys = n_lat + n_ctx
        q, kall, vtall, hy, fn, naq, nak, nav = _inproj(x, g1, sc1, sh1, w, rope_lat, n_keys, 0)
        qc, kc, vc, kall, vtall, hyc, fnc, naqc, nakc, navc = _inproj(
            ctx, g1, sc1c, sh1c, w, rope_ctx, n_keys, n_lat, kv_bufs=(kall, vtall))

        oa = _flash_t(q, kall, vtall, MLA_V)
        od = _na(naq, nak, nav, nakc, navc, _na_bias_tables(na_rpb[l]))
        ob = _hyena_long(hy, hy_short_w[l], fw, hy_bias[l], tabs_lat)
        oc = _fnet_long(fn, tabs_lat)
        x_new = _merge(x, g1, sc1, sh1, gt1, oa, ob, oc, od, w)

        if not last:
            oa_c = _flash(qc, kc, vc, True)
            heads = lambda t: jnp.transpose(t.reshape(b, n_ctx, NA_HEADS, NA_HEAD_DIM), (0, 2, 1, 3))
            od_c = _flash(heads(naqc), heads(nakc), heads(navc), False)
            ob_c = _hyena_short(hyc, hy_short_w[l], fw, hy_bias[l], tabs_ctx)
            oc_c = _fnet_short(fnc, tabs_ctx)
            ctx = _merge(ctx, g1, sc1c, sh1c, gt1c, oa_c, ob_c, oc_c, od_c, w)
            ctx = _moe(ctx, g2, sc2c, sh2c, gt2c, w, fg, False)
        x = _moe(x_new, g2, sc2, sh2, gt2, w, fg, last)
    return x
```

```python
import functools
import math

import numpy as np
import jax
import jax.numpy as jnp
from jax import lax
from jax.experimental import pallas as pl
from jax.experimental.pallas import tpu as pltpu

F32 = jnp.float32
BF16 = jnp.bfloat16

D_MODEL = 1024
DEPTH = 2
GRID_W = 64
NORM_EPS = 1e-6
MASK_VALUE = -1e30
BRANCH_WIDTH = D_MODEL // 4
N_BRANCH = 4

MLA_HEADS = 4
MLA_Q_RANK = D_MODEL // 4
MLA_KV_RANK = D_MODEL // 8
MLA_NOPE = 64
MLA_ROPE = 32
MLA_V = BRANCH_WIDTH // MLA_HEADS
MLA_SCALE = 1.0 / math.sqrt(MLA_NOPE + MLA_ROPE)
MLA_HEAD_PAD = 128
MLA_VT_ROWS = MLA_V + 16
PV_DTYPE = BF16
ROPE_BASE = 10000.0
LOG2E = 1.4426950408889634

HY_WIDTH = BRANCH_WIDTH
HY_ORDER = 2
HY_BANDS = 16
HY_EMB = 1 + 2 * HY_BANDS
HY_EMB_PAD = 128
HY_HIDDEN = 64
HY_DECAY_TARGET = 1e-2
HY_DECAY_SHORT_PCT = 0.3
HY_DECAY_LONG_PCT = 1.5
HY_IN = (HY_ORDER + 1) * HY_WIDTH

FN_GROUPS = 4
FN_GROUP_DIM = BRANCH_WIDTH // FN_GROUPS

NA_HEADS = 4
NA_HEAD_DIM = BRANCH_WIDTH // NA_HEADS
NA_KH = 8
NA_KW = 16
NA_SCALE = 1.0 / math.sqrt(NA_HEAD_DIM)
NA_QROWS = 8
NA_KROWS = 16

MOE_GROUPS = 4
MOE_PER_GROUP = 4
MOE_EXPERTS = MOE_GROUPS * MOE_PER_GROUP
MOE_HIDDEN = 256
ROUTER_LANES = 128

MLA_IN = MLA_Q_RANK + MLA_KV_RANK + MLA_ROPE
MIX_IN = MLA_IN + HY_IN + BRANCH_WIDTH + 3 * BRANCH_WIDTH
MIX_PAD = MLA_Q_RANK + MLA_KV_RANK + MLA_HEAD_PAD + HY_IN + BRANCH_WIDTH + 3 * BRANCH_WIDTH

DFT_N2 = 256
FN_N2 = 128
BMM_GROUPS_PER_STEP = 4
FLASH_K_TILE = 3328
FLASH_Q_SUB = 512


def _params(sem, vmem_mb=None):
    kw = dict(dimension_semantics=sem)
    if vmem_mb is not None:
        kw["vmem_limit_bytes"] = vmem_mb << 20
    return pltpu.CompilerParams(**kw)


def _pick_tile(n, cap, mult):
    best = None
    for t in range(mult, min(n, cap) + 1, mult):
        if n % t == 0:
            best = t
    return best if best is not None else n


def _rms(x):
    return x * lax.rsqrt(jnp.mean(x * x, axis=-1, keepdims=True) + NORM_EPS)


def _normmod(x, g, sc, sh):
    return (_rms(x) * g) * (1.0 + sc) + sh


def _dot(a, b):
    return jnp.dot(a, b, preferred_element_type=F32)


def _dot_nt(a, b):
    return lax.dot_general(a, b, (((1,), (1,)), ((), ())), preferred_element_type=F32)


def _mod_kernel(c_ref, w_ref, b_ref, o_ref):
    c = c_ref[...]
    s = c * jax.nn.sigmoid(c)
    o_ref[0] = _dot(s.astype(BF16), w_ref[0].astype(BF16)) + b_ref[0]


def _mod_vectors(cvec, ada_w, ada_b):
    n6 = ada_w.shape[-1]
    tn = 1024
    return pl.pallas_call(
        _mod_kernel,
        grid=(DEPTH, n6 // tn),
        in_specs=[pl.BlockSpec((8, D_MODEL), lambda l, j: (0, 0)),
                  pl.BlockSpec((1, D_MODEL, tn), lambda l, j: (l, 0, j)),
                  pl.BlockSpec((1, 1, tn), lambda l, j: (l, 0, j))],
        out_specs=pl.BlockSpec((1, 8, tn), lambda l, j: (l, 0, j)),
        out_shape=jax.ShapeDtypeStruct((DEPTH, 8, n6), F32),
        compiler_params=_params(("parallel", "parallel")),
        name="adaln_vectors",
    )(cvec, ada_w, ada_b.reshape(DEPTH, 1, n6))


def _inproj_kernel(*refs, ctx_mode):
    (x_ref, g_ref, sc_ref, sh_ref, wa_ref, qn_ref, kvn_ref, wuq_ref, wuk_ref, wuv_ref,
     cos_ref, s1_ref, s2_ref) = refs[:13]
    if ctx_mode:
        (q_ref, k_ref, v_ref, kall_ref, vtall_ref, hy_ref, fn_ref, naq_ref, nak_ref, nav_ref) = refs[15:]
    else:
        (q_ref, kall_ref, vtall_ref, hy_ref, fn_ref, naq_ref, nak_ref, nav_ref) = refs[15:]
    h = _normmod(x_ref[0], g_ref[...], sc_ref[0], sh_ref[0]).astype(BF16)
    p = _dot(h, wa_ref[...])
    c0 = MLA_Q_RANK
    c1 = c0 + MLA_KV_RANK
    c2 = c1 + MLA_HEAD_PAD
    c3 = c2 + HY_IN
    c4 = c3 + BRANCH_WIDTH
    c5 = c4 + BRANCH_WIDTH
    c6 = c5 + BRANCH_WIDTH
    cq, ckv, krp = p[:, :c0], p[:, c0:c1], p[:, c1:c2]
    hy_ref[0] = p[:, c2:c3]
    fn_ref[0] = p[:, c3:c4]
    naq_ref[0] = p[:, c4:c5].astype(BF16)
    nak_ref[0] = p[:, c5:c6].astype(BF16)
    nav_ref[0] = p[:, c6:].astype(BF16)
    cqn = (_rms(cq) * qn_ref[...]).astype(BF16)
    ckn = (_rms(ckv) * kvn_ref[...]).astype(BF16)
    q = _dot(cqn, wuq_ref[...])
    kn = _dot(ckn, wuk_ref[...])
    v = _dot(ckn, wuv_ref[...])
    cos, s1, s2 = cos_ref[...], s1_ref[...], s2_ref[...]

    def rope(t):
        return (t * cos + pltpu.roll(t, MLA_HEAD_PAD - MLA_ROPE // 2, 1) * s1
                + pltpu.roll(t, MLA_ROPE // 2, 1) * s2)

    kr = rope(krp)
    vt = v.T
    tail_rows = lax.broadcasted_iota(jnp.int32, (MLA_VT_ROWS - MLA_V, v.shape[0]), 0)
    tail = jnp.where(tail_rows == 0, 1.0, 0.0).astype(PV_DTYPE)
    for hh in range(MLA_HEADS):
        sl = slice(hh * MLA_HEAD_PAD, (hh + 1) * MLA_HEAD_PAD)
        vs = slice(hh * MLA_V, (hh + 1) * MLA_V)
        kh = (kn[:, sl] + kr).astype(BF16)
        q_ref[0, hh] = rope(q[:, sl]).astype(BF16)
        kall_ref[0, hh] = kh
        vtall_ref[0, hh, :MLA_V] = vt[vs].astype(PV_DTYPE)
        vtall_ref[0, hh, MLA_V:] = tail
        if ctx_mode:
            k_ref[0, hh] = kh
            v_ref[0, hh] = v[:, vs].astype(BF16)


def _inproj(x, g, sc, sh, w, rope_tabs, kv_bufs, key_off, ctx_mode):
    b, n, _ = x.shape
    n_keys = kv_bufs[0].shape[2]
    tm = _pick_tile(n, 512, 128)
    assert key_off % tm == 0
    ko = key_off // tm
    cos, s1, s2 = rope_tabs
    tok = lambda bb, i: (bb, i, 0)
    row = lambda bb, i: (bb, 0, 0)
    const2 = lambda bb, i: (0, 0)
    hd = lambda bb, i: (bb, 0, i, 0)
    tab = lambda bb, i: (i, 0)
    hp = MLA_HEADS * MLA_HEAD_PAD
    in_specs = [pl.BlockSpec((1, tm, D_MODEL), tok),
                pl.BlockSpec((1, D_MODEL), const2),
                pl.BlockSpec((1, 1, D_MODEL), row),
                pl.BlockSpec((1, 1, D_MODEL), row),
                pl.BlockSpec((D_MODEL, MIX_PAD), const2),
                pl.BlockSpec((1, MLA_Q_RANK), const2),
                pl.BlockSpec((1, MLA_KV_RANK), const2),
                pl.BlockSpec((MLA_Q_RANK, hp), const2),
                pl.BlockSpec((MLA_KV_RANK, hp), const2),
                pl.BlockSpec((MLA_KV_RANK, BRANCH_WIDTH), const2),
                pl.BlockSpec((tm, MLA_HEAD_PAD), tab),
                pl.BlockSpec((tm, MLA_HEAD_PAD), tab),
                pl.BlockSpec((tm, MLA_HEAD_PAD), tab)]
    args = [x, g, sc, sh, w["wa"], w["qn"], w["kvn"], w["wuq"], w["wuk"], w["wuv"], cos, s1, s2]
    q_spec = pl.BlockSpec((1, MLA_HEADS, tm, MLA_HEAD_PAD), hd)
    q_sd = jax.ShapeDtypeStruct((b, MLA_HEADS, n, MLA_HEAD_PAD), BF16)
    kall_spec = pl.BlockSpec((1, MLA_HEADS, tm, MLA_HEAD_PAD), lambda bb, i: (bb, 0, ko + i, 0))
    vtall_spec = pl.BlockSpec((1, MLA_HEADS, MLA_VT_ROWS, tm), lambda bb, i: (bb, 0, 0, ko + i))
    kall_sd = jax.ShapeDtypeStruct((b, MLA_HEADS, n_keys, MLA_HEAD_PAD), BF16)
    vtall_sd = jax.ShapeDtypeStruct((b, MLA_HEADS, MLA_VT_ROWS, n_keys), PV_DTYPE)
    rest_specs = [pl.BlockSpec((1, tm, HY_IN), tok)] + [pl.BlockSpec((1, tm, BRANCH_WIDTH), tok)] * 4
    rest_sd = [jax.ShapeDtypeStruct((b, n, HY_IN), F32), jax.ShapeDtypeStruct((b, n, BRANCH_WIDTH), F32)] \
        + [jax.ShapeDtypeStruct((b, n, BRANCH_WIDTH), BF16)] * 3
    in_specs += [pl.BlockSpec(memory_space=pl.ANY), pl.BlockSpec(memory_space=pl.ANY)]
    args += list(kv_bufs)
    if ctx_mode:
        out_specs = [q_spec, q_spec, pl.BlockSpec((1, MLA_HEADS, tm, MLA_V), hd), kall_spec, vtall_spec]
        out_shape = [q_sd, q_sd, jax.ShapeDtypeStruct((b, MLA_HEADS, n, MLA_V), BF16), kall_sd, vtall_sd]
        aliases = {13: 3, 14: 4}
    else:
        out_specs = [q_spec, kall_spec, vtall_spec]
        out_shape = [q_sd, kall_sd, vtall_sd]
        aliases = {13: 1, 14: 2}
    return pl.pallas_call(
        functools.partial(_inproj_kernel, ctx_mode=ctx_mode),
        grid=(b, n // tm),
        in_specs=in_specs,
        out_specs=out_specs + rest_specs,
        out_shape=out_shape + rest_sd,
        input_output_aliases=aliases,
        compiler_params=_params(("parallel", "parallel"), 48),
        name="input_projection",
    )(*args)


def _flash_kernel(q_ref, k_ref, v_ref, o_ref, m_ref, l_ref, acc_ref, *, nh, use_exp2):
    j = pl.program_id(2)

    @pl.when(j == 0)
    def _():
        m_ref[...] = jnp.full(m_ref.shape, MASK_VALUE, F32)
        l_ref[...] = jnp.zeros(l_ref.shape, F32)
        acc_ref[...] = jnp.zeros(acc_ref.shape, F32)

    ex = jnp.exp2 if use_exp2 else jnp.exp
    for h in range(nh):
        s = _dot_nt(q_ref[0, h], k_ref[0, h])
        m_prev = m_ref[h]
        m_cur = jnp.maximum(m_prev, jnp.max(s, axis=1, keepdims=True))
        alpha = ex(m_prev - m_cur)
        p = ex(s - m_cur)
        l_ref[h] = alpha * l_ref[h] + jnp.sum(p, axis=1, keepdims=True)
        acc_ref[h] = alpha * acc_ref[h] + _dot(p.astype(BF16), v_ref[0, h])
        m_ref[h] = m_cur

    @pl.when(j == pl.num_programs(2) - 1)
    def _():
        outs = [acc_ref[h] / l_ref[h] for h in range(nh)]
        o_ref[0] = jnp.concatenate(outs, axis=1).astype(o_ref.dtype)


def _flash(q, k, v, use_exp2):
    b, nh, lq, dk = q.shape
    lk, dv = k.shape[2], v.shape[3]
    tq = _pick_tile(lq, 512, 8)
    tk = _pick_tile(lk, 1280, 128)
    return pl.pallas_call(
        functools.partial(_flash_kernel, nh=nh, use_exp2=use_exp2),
        grid=(b, lq // tq, lk // tk),
        in_specs=[pl.BlockSpec((1, nh, tq, dk), lambda bb, i, j: (bb, 0, i, 0)),
                  pl.BlockSpec((1, nh, tk, dk), lambda bb, i, j: (bb, 0, j, 0)),
                  pl.BlockSpec((1, nh, tk, dv), lambda bb, i, j: (bb, 0, j, 0))],
        out_specs=pl.BlockSpec((1, tq, nh * dv), lambda bb, i, j: (bb, i, 0)),
        out_shape=jax.ShapeDtypeStruct((b, lq, nh * dv), BF16),
        scratch_shapes=[pltpu.VMEM((nh, tq, 1), F32), pltpu.VMEM((nh, tq, 1), F32),
                        pltpu.VMEM((nh, tq, dv), F32)],
        compiler_params=_params(("parallel", "parallel", "arbitrary"), 48),
        name="softmax_attention",
    )(q, k, v)


def _flash_t_kernel(q_ref, k_ref, vt_ref, o_ref, m_ref, acc_ref, st_ref, p_ref, *, nh, dv):
    j = pl.program_id(2)

    @pl.when(j == 0)
    def _():
        m_ref[...] = jnp.full(m_ref.shape, MASK_VALUE, F32)
        acc_ref[...] = jnp.zeros(acc_ref.shape, F32)

    rows = vt_ref.shape[2]

    tq = q_ref.shape[2]
    qs = min(tq, FLASH_Q_SUB)
    units = [(h, c) for h in range(nh) for c in range(tq // qs)]

    def scores(i):
        h, c = units[i]
        cols = slice(c * qs, (c + 1) * qs)
        st = _dot_nt(k_ref[0, h], q_ref[0, h, cols, :])
        st_ref[i % 2] = st
        m_prev = m_ref[h, :, cols]
        m_cur = jnp.maximum(m_prev, jnp.max(st, axis=0, keepdims=True))
        m_ref[h, :, cols] = m_cur
        return m_prev, m_cur

    def probs(i, m_cur):
        p_ref[i % 2] = jnp.exp2(st_ref[i % 2] - m_cur).astype(PV_DTYPE)

    def values(i, m_prev, m_cur):
        h, c = units[i]
        cols = slice(c * qs, (c + 1) * qs)
        alpha = jnp.exp2(m_prev - m_cur)
        acc_ref[h, :rows, cols] = alpha * acc_ref[h, :rows, cols] + _dot(vt_ref[0, h], p_ref[i % 2])

    n_units = len(units)
    stats = {0: scores(0)}
    if n_units > 1:
        stats[1] = scores(1)
    probs(0, stats[0][1])
    for i in range(n_units):
        values(i, *stats.pop(i))
        if i + 2 < n_units:
            stats[i + 2] = scores(i + 2)
        if i + 1 < n_units:
            probs(i + 1, stats[i + 1][1])

    @pl.when(j == pl.num_programs(2) - 1)
    def _():
        outs = []
        for h in range(nh):
            a = acc_ref[h]
            o = (a / a[dv:dv + 1]).T
            outs.append(o[:, :dv])
        o_ref[0] = jnp.concatenate(outs, axis=1).astype(o_ref.dtype)


def _flash_t(q, k, vt, dv):
    b, nh, lq, dk = q.shape
    lk, rows = k.shape[2], vt.shape[2]
    tq = _pick_tile(lq, 1024, 128)
    tk = _pick_tile(lk, FLASH_K_TILE, 128)
    return pl.pallas_call(
        functools.partial(_flash_t_kernel, nh=nh, dv=dv),
        grid=(b, lq // tq, lk // tk),
        in_specs=[pl.BlockSpec((1, nh, tq, dk), lambda bb, i, j: (bb, 0, i, 0)),
                  pl.BlockSpec((1, nh, tk, dk), lambda bb, i, j: (bb, 0, j, 0)),
                  pl.BlockSpec((1, nh, rows, tk), lambda bb, i, j: (bb, 0, 0, j))],
        out_specs=pl.BlockSpec((1, tq, nh * dv), lambda bb, i, j: (bb, i, 0)),
        out_shape=jax.ShapeDtypeStruct((b, lq, nh * dv), BF16),
        scratch_shapes=[pltpu.VMEM((nh, 1, tq), F32), pltpu.VMEM((nh, 128, tq), F32),
                        pltpu.VMEM((2, tk, min(tq, FLASH_Q_SUB)), F32),
                        pltpu.VMEM((2, tk, min(tq, FLASH_Q_SUB)), PV_DTYPE)],
        compiler_params=_params(("parallel", "parallel", "arbitrary"), 48),
        name="softmax_attention_t",
    )(q, k, vt)


def _na_kernel(q_ref, k0, k1, k2, k3, v0, v1, v2, v3, kc_ref, vc_ref, b_ref, o_ref):
    q = q_ref[0]
    kcat = jnp.concatenate([k0[0], k1[0], k2[0], k3[0]], axis=0)
    vcat = jnp.concatenate([v0[0], v1[0], v2[0], v3[0]], axis=0)
    kc, vc = kc_ref[0], vc_ref[0]
    outs = []
    for h in range(NA_HEADS):
        sl = slice(h * NA_HEAD_DIM, (h + 1) * NA_HEAD_DIM)
        qh = q[:, sl]
        s = _dot_nt(qh, kcat[:, sl]) + b_ref[0, h]
        sc = _dot_nt(qh, kc[:, sl])
        m = jnp.maximum(jnp.max(s, axis=1, keepdims=True), jnp.max(sc, axis=1, keepdims=True))
        p = jnp.exp(s - m)
        pc = jnp.exp(sc - m)
        l = jnp.sum(p, axis=1, keepdims=True) + jnp.sum(pc, axis=1, keepdims=True)
        o = _dot(p.astype(BF16), vcat[:, sl]) + _dot(pc.astype(BF16), vc[:, sl])
        outs.append(o / l)
    o_ref[0] = jnp.concatenate(outs, axis=1).astype(o_ref.dtype)


def _na_bias_tables(rpb):
    j = np.arange(NA_QROWS)[:, None]
    m = np.arange(NA_KROWS)[None, :]
    valid = np.zeros((3, NA_QROWS, NA_KROWS), bool)
    st0 = np.maximum(j - NA_KH // 2, 0)
    valid[0] = (m - 4 >= st0) & (m - 4 < st0 + NA_KH)
    valid[1] = (m - j >= 0) & (m - j < NA_KH)
    st2 = 4 + np.minimum(j - NA_KH // 2, 0)
    valid[2] = (m < 12) & (m >= st2) & (m < st2 + NA_KH)
    dr_idx = np.clip(m - 4 - j + NA_KH - 1, 0, 2 * NA_KH - 2)
    qc = np.arange(GRID_W)[:, None]
    kc = np.arange(GRID_W)[None, :]
    dc_idx = np.clip(kc - qc, -(NA_KW - 1), NA_KW - 1) + NA_KW - 1
    cs = np.clip(qc - NA_KW // 2, 0, GRID_W - NA_KW)
    col_ok = (kc >= cs) & (kc < cs + NA_KW)
    t = rpb.astype(F32)[:, dr_idx]
    t = t[..., dc_idx]
    t = jnp.transpose(t, (0, 1, 3, 2, 4))
    ok = valid[:, None, :, None, :, None] & col_ok[None, None, None, :, None, :]
    out = jnp.where(jnp.asarray(ok), t[None], MASK_VALUE)
    return out.reshape(3, NA_HEADS, NA_QROWS * GRID_W, NA_KROWS * GRID_W)


def _na(q, k, v, kc, vc, bias):
    b, n, w = q.shape
    c = kc.shape[1]
    tq = NA_QROWS * GRID_W
    tkb = 4 * GRID_W
    nb = n // tq
    nkb = n // tkb
    kmaps = [lambda bb, i: (bb, jnp.maximum(2 * i - 1, 0), 0),
             lambda bb, i: (bb, 2 * i, 0),
             lambda bb, i: (bb, 2 * i + 1, 0),
             lambda bb, i: (bb, jnp.minimum(2 * i + 2, nkb - 1), 0)]
    kspecs = [pl.BlockSpec((1, tkb, w), mp) for mp in kmaps]
    return pl.pallas_call(
        _na_kernel,
        grid=(b, nb),
        in_specs=[pl.BlockSpec((1, tq, w), lambda bb, i: (bb, i, 0))] + kspecs + kspecs
                 + [pl.BlockSpec((1, c, w), lambda bb, i: (bb, 0, 0)),
                    pl.BlockSpec((1, c, w), lambda bb, i: (bb, 0, 0)),
                    pl.BlockSpec((1, NA_HEADS, tq, NA_KROWS * GRID_W),
                                 lambda bb, i: (jnp.where(i == 0, 0, jnp.where(i == nb - 1, 2, 1)), 0, 0, 0))],
        out_specs=pl.BlockSpec((1, tq, w), lambda bb, i: (bb, i, 0)),
        out_shape=jax.ShapeDtypeStruct((b, n, w), BF16),
        compiler_params=_params(("parallel", "arbitrary"), 56),
        name="neighbourhood_attention",
    )(q, k, k, k, k, v, v, v, v, kc, vc, bias)


def _sconv_kernel(u_ref, pv_ref, nx_ref, w_ref, v_ref, x1_ref, x2_ref, *, nt):
    i = pl.program_id(1)
    u = u_ref[0]
    tl = u.shape[0]
    prev = jnp.where(i > 0, pv_ref[0, 7:8, :], 0.0)
    nxt = jnp.where(i < nt - 1, nx_ref[0, 0:1, :], 0.0)
    rows = lax.broadcasted_iota(jnp.int32, u.shape, 0)
    um = jnp.where(rows == 0, prev, pltpu.roll(u, 1, 0))
    up = jnp.where(rows == tl - 1, nxt, pltpu.roll(u, tl - 1, 0))
    w = w_ref[...]
    y = um * w[0:1] + u * w[1:2] + up * w[2:3]
    v_ref[...] = y[:, :HY_WIDTH]
    x1_ref[...] = y[:, HY_WIDTH:2 * HY_WIDTH]
    x2_ref[...] = y[:, 2 * HY_WIDTH:]


def _sconv(u, w):
    b, n, _ = u.shape
    tl = _pick_tile(n, 512, 8)
    nt = n // tl
    r8 = tl // 8
    osd = jax.ShapeDtypeStruct((n, b * HY_WIDTH), F32)
    ospec = pl.BlockSpec((tl, HY_WIDTH), lambda bb, i: (i, bb))
    return pl.pallas_call(
        functools.partial(_sconv_kernel, nt=nt),
        grid=(b, nt),
        in_specs=[pl.BlockSpec((1, tl, HY_IN), lambda bb, i: (bb, i, 0)),
                  pl.BlockSpec((1, 8, HY_IN), lambda bb, i: (bb, jnp.maximum(i * r8 - 1, 0), 0)),
                  pl.BlockSpec((1, 8, HY_IN), lambda bb, i: (bb, jnp.minimum((i + 1) * r8, n // 8 - 1), 0)),
                  pl.BlockSpec((3, HY_IN), lambda bb, i: (0, 0))],
        out_specs=[ospec, ospec, ospec],
        out_shape=[osd, osd, osd],
        compiler_params=_params(("parallel", "parallel")),
        name="hyena_short_conv",
    )(u, u, u, w)


def _filt_kernel(z_ref, w1_ref, b1_ref, f1_ref, w2_ref, b2_ref, f2_ref, w3_ref, win_ref, g_ref, ss_ref):
    d = pl.program_id(0)
    i = pl.program_id(1)
    h = jnp.sin(f1_ref[...] * (_dot(z_ref[0].astype(BF16), w1_ref[...]) + b1_ref[...]))
    h = jnp.sin(f2_ref[...] * (_dot(h.astype(BF16), w2_ref[...]) + b2_ref[...]))
    h = _dot(h.astype(BF16), w3_ref[0])
    h = h * jnp.concatenate([win_ref[0]] * HY_ORDER, axis=1)
    rows = lax.broadcasted_iota(jnp.int32, h.shape, 0)
    h = jnp.where((rows == 0) & (i == 0) & (d == 1), 0.0, h)
    g_ref[0] = h
    ss = jnp.sum(h * h, axis=0, keepdims=True)
    first = (i == 0) & (d == 0)

    @pl.when(first)
    def _():
        ss_ref[...] = ss

    @pl.when(jnp.logical_not(first))
    def _():
        ss_ref[...] += ss


def _padded_filter(n, fw):
    r = jnp.arange(n, dtype=jnp.int32)
    idx = jnp.stack([r, jnp.mod(n - r, n)], axis=0).astype(F32)[:, :, None]
    t = idx / (n - 1)
    bands = jnp.linspace(1e-4, HY_BANDS - 1, HY_BANDS, dtype=F32)
    ang = (2.0 * math.pi / n) * idx * bands
    z = jnp.concatenate([t, jnp.cos(ang), -jnp.sin(ang), jnp.zeros((2, n, HY_EMB_PAD - HY_EMB), F32)], axis=-1)
    deltas = jnp.linspace(math.log(HY_DECAY_TARGET) / HY_DECAY_LONG_PCT,
                          math.log(HY_DECAY_TARGET) / HY_DECAY_SHORT_PCT, HY_WIDTH, dtype=F32)
    window = jnp.exp(-t * jnp.abs(deltas))
    tl = _pick_tile(n, 512, 8)
    half = HY_ORDER * HY_WIDTH
    c2 = lambda d, i: (0, 0)
    g, ss = pl.pallas_call(
        _filt_kernel,
        grid=(2, n // tl),
        in_specs=[pl.BlockSpec((1, tl, HY_EMB_PAD), lambda d, i: (d, i, 0)),
                  pl.BlockSpec((HY_EMB_PAD, HY_HIDDEN), c2), pl.BlockSpec((1, HY_HIDDEN), c2),
                  pl.BlockSpec((1, HY_HIDDEN), c2),
                  pl.BlockSpec((HY_HIDDEN, HY_HIDDEN), c2), pl.BlockSpec((1, HY_HIDDEN), c2),
                  pl.BlockSpec((1, HY_HIDDEN), c2),
                  pl.BlockSpec((1, HY_HIDDEN, half), lambda d, i: (d, 0, 0)),
                  pl.BlockSpec((1, tl, HY_WIDTH), lambda d, i: (d, i, 0))],
        out_specs=[pl.BlockSpec((1, tl, half), lambda d, i: (d, i, 0)), pl.BlockSpec((1, half), c2)],
        out_shape=[jax.ShapeDtypeStruct((2, n, half), F32), jax.ShapeDtypeStruct((1, half), F32)],
        compiler_params=_params(("arbitrary", "arbitrary")),
        name="hyena_filter_mlp",
    )(z, fw["w1"], fw["b1"], fw["f1"], fw["w2"], fw["b2"], fw["f2"], fw["w3"], window)
    return g.reshape(2 * n, half), lax.rsqrt(ss + NORM_EPS)


def _colmm_kernel(*refs, epi):
    w_ref, x_ref = refs[0], refs[1]
    o_ref = refs[-1]
    acc = _dot(w_ref[...], x_ref[...].astype(BF16))
    if epi == "gate":
        xg_ref, z_ref, b_ref = refs[2:5]
        acc = xg_ref[...] * (acc + z_ref[...] * b_ref[...])
    elif epi == "scale":
        acc = acc * refs[2][...]
    elif epi == "cmul":
        g = refs[2][...]
        mh = acc.shape[0] // 2
        ar, ai, gr, gi = acc[:mh], acc[mh:], g[:mh], g[mh:]
        acc = jnp.concatenate([ar * gr - ai * gi, ar * gi + ai * gr], axis=0)
    o_ref[...] = acc.astype(o_ref.dtype)


def _colmm(w, x, out_dtype, epi="none", extra=(), tn_cap=4096, g_col=None):
    m, k = w.shape
    nc = x.shape[1]
    tn = _pick_tile(nc, tn_cap, 128)
    col = lambda j: (0, j)
    specs = [pl.BlockSpec((m, k), lambda j: (0, 0)), pl.BlockSpec((k, tn), col)]
    if epi == "gate":
        specs += [pl.BlockSpec((m, tn), col), pl.BlockSpec((m, tn), col), pl.BlockSpec((1, tn), col)]
    elif epi == "scale":
        specs += [pl.BlockSpec((1, tn), col)]
    elif epi == "cmul":
        specs += [pl.BlockSpec((m, tn), lambda j: (0, g_col))]
    return pl.pallas_call(
        functools.partial(_colmm_kernel, epi=epi),
        grid=(nc // tn,),
        in_specs=specs,
        out_specs=pl.BlockSpec((m, tn), col),
        out_shape=jax.ShapeDtypeStruct((m, nc), out_dtype),
        compiler_params=_params(("parallel",), 48),
        name="dft_stage_" + epi,
    )(w, x, *extra)


def _bmm_kernel(*refs, epi, cplx_out, gs):
    m_ref, x_ref = refs[0], refs[1]
    o_ref = refs[-1]
    for s in range(gs):
        x = x_ref[:, s]
        x = x.reshape(x.shape[0] * x.shape[1], x.shape[2])
        acc = _dot(m_ref[s], x)
        ph = acc.shape[0] // 2
        if epi == "cmul":
            g_ref = refs[2]
            gr, gi = g_ref[0, s], g_ref[1, s]
            cg = gr.shape[1]
            res_r, res_i = [], []
            for t in range(acc.shape[1] // cg):
                ar, ai = acc[:ph, t * cg:(t + 1) * cg], acc[ph:, t * cg:(t + 1) * cg]
                res_r.append(ar * gr - ai * gi)
                res_i.append(ar * gi + ai * gr)
            o_ref[0, s] = jnp.concatenate(res_r, axis=1).astype(o_ref.dtype)
            o_ref[1, s] = jnp.concatenate(res_i, axis=1).astype(o_ref.dtype)
        elif epi == "scale":
            sc = refs[2][...]
            o_ref[0, s] = (acc[:ph] * sc).astype(o_ref.dtype)
            o_ref[1, s] = (acc[ph:] * sc).astype(o_ref.dtype)
        elif cplx_out:
            o_ref[0, s] = acc[:ph].astype(o_ref.dtype)
            o_ref[1, s] = acc[ph:].astype(o_ref.dtype)
        else:
            o_ref[s] = acc.astype(o_ref.dtype)


def _bmm(mats, x, out_dtype, epi="none", extra=None, g_col=0, cplx_out=True):
    ng, p, qq = mats.shape
    c = x.shape[3]
    gs = _pick_tile(ng, BMM_GROUPS_PER_STEP, 1)
    specs = [pl.BlockSpec((gs, p, qq), lambda g: (g, 0, 0)),
             pl.BlockSpec((2, gs, qq // 2, c), lambda g: (0, g, 0, 0))]
    args = [mats, x]
    if epi == "cmul":
        cg = HY_WIDTH
        specs.append(pl.BlockSpec((2, gs, p // 2, cg), lambda g: (0, g, 0, g_col)))
        args.append(extra)
    elif epi == "scale":
        specs.append(pl.BlockSpec((1, c), lambda g: (0, 0)))
        args.append(extra)
    if cplx_out:
        ospec = pl.BlockSpec((2, gs, p // 2, c), lambda g: (0, g, 0, 0))
        oshape = jax.ShapeDtypeStruct((2, ng, p // 2, c), out_dtype)
    else:
        ospec = pl.BlockSpec((gs, p, c), lambda g: (g, 0, 0))
        oshape = jax.ShapeDtypeStruct((ng, p, c), out_dtype)
    return pl.pallas_call(
        functools.partial(_bmm_kernel, epi=epi, cplx_out=cplx_out, gs=gs),
        grid=(ng // gs,),
        in_specs=specs,
        out_specs=ospec,
        out_shape=oshape,
        compiler_params=_params(("parallel",), 48),
        name="dft_batched_" + epi,
    )(*args)


def _angle(num, den):
    return (2.0 * math.pi / den) * jnp.mod(num, den).astype(F32)


def _dft_tables(n_lat):
    nn = 2 * n_lat
    n2 = DFT_N2
    n1 = nn // n2
    i1 = jnp.arange(n1, dtype=jnp.int32)
    a1 = _angle(i1[:, None] * i1[None, :], n1)
    c1, s1 = jnp.cos(a1), jnp.sin(a1)
    t = {}
    t["hy_f1"] = jnp.concatenate([c1, -s1], axis=0).astype(BF16)
    t["hy_i2"] = jnp.concatenate([c1[:n1 // 2], -s1[:n1 // 2]], axis=1).astype(BF16)
    i2 = jnp.arange(n2, dtype=jnp.int32)
    kk = i1[:, None, None] + n1 * i2[None, :, None]
    a2 = _angle(kk * i2[None, None, :], nn)
    c2, s2 = jnp.cos(a2), jnp.sin(a2)
    fwd = jnp.concatenate([jnp.concatenate([c2, s2], axis=2), jnp.concatenate([-s2, c2], axis=2)], axis=1)
    t["hy_fwd"] = fwd.astype(BF16)
    t["hy_inv"] = jnp.swapaxes(fwd, 1, 2).astype(BF16)
    m2 = FN_N2
    m1 = n_lat // m2
    j1 = jnp.arange(m1, dtype=jnp.int32)
    b1 = _angle(j1[:, None] * j1[None, :], m1)
    cb, sb = jnp.cos(b1), jnp.sin(b1)
    t["fn_s1"] = jnp.concatenate([jnp.concatenate([cb, sb], axis=1),
                                  jnp.concatenate([-sb, cb], axis=1)], axis=0).astype(BF16)
    j2 = jnp.arange(m2, dtype=jnp.int32)
    kf = j1[:, None, None] + m1 * j2[None, :, None]
    b2 = _angle(kf * j2[None, None, :], n_lat)
    norm = 1.0 / math.sqrt(n_lat * FN_GROUP_DIM)
    t["fn_s3"] = (norm * jnp.concatenate([jnp.cos(b2), jnp.sin(b2)], axis=2)).astype(BF16)
    return t


def _small_dft_tables(n):
    nn = 2 * n
    k = jnp.arange(nn, dtype=jnp.int32)
    a = _angle(k[:, None] * k[None, :], nn)
    c, s = jnp.cos(a), jnp.sin(a)
    t = {}
    t["f_full"] = jnp.concatenate([c, -s], axis=0).astype(BF16)
    t["f_half"] = jnp.concatenate([c[:, :n], -s[:, :n]], axis=0).astype(BF16)
    t["i_half"] = jnp.concatenate([c[:n], -s[:n]], axis=1).astype(BF16)
    j = jnp.arange(n, dtype=jnp.int32)
    b = _angle(j[:, None] * j[None, :], n)
    norm = 1.0 / math.sqrt(n * FN_GROUP_DIM)
    t["fn"] = (norm * jnp.concatenate([jnp.cos(b), jnp.sin(b)], axis=1)).astype(BF16)
    return t


def _hyena_long(hy, short_w, fw, bias, tabs):
    b, n, _ = hy.shape
    cw = b * HY_WIDTH
    n2 = DFT_N2
    n1 = 2 * n // n2
    v, x1, x2 = _sconv(hy, short_w)
    g, gscale = _padded_filter(n, fw)
    half = HY_ORDER * HY_WIDTH
    ga = _colmm(tabs["hy_f1"], g.reshape(n1, n2 * half), BF16)
    gspec = _bmm(tabs["hy_fwd"], ga.reshape(2, n1, n2, half), F32, epi="scale",
                 extra=gscale * (1.0 / (2 * n)))
    z = v
    for o, xg in enumerate((x1, x2)):
        a = _colmm(tabs["hy_f1"][:, :n1 // 2], z.reshape(n1 // 2, n2 * cw), BF16)
        y = _bmm(tabs["hy_fwd"], a.reshape(2, n1, n2, cw), BF16, epi="cmul", extra=gspec, g_col=o)
        bm = _bmm(tabs["hy_inv"], y, BF16)
        brow = jnp.tile(bias[o], n2 * b)[None, :]
        z = _colmm(tabs["hy_i2"], bm.reshape(2 * n1, n2 * cw), F32, epi="gate",
                   extra=(xg.reshape(n1 // 2, n2 * cw), z.reshape(n1 // 2, n2 * cw), brow))
        z = z.reshape(n, cw)
    return z


def _hyena_short(hy, short_w, fw, bias, tabs):
    b, n, _ = hy.shape
    cw = b * HY_WIDTH
    v, x1, x2 = _sconv(hy, short_w)
    g, gscale = _padded_filter(n, fw)
    gspec = _colmm(tabs["f_full"], g, F32, epi="scale", extra=(gscale * (1.0 / (2 * n)),), tn_cap=256)
    z = v
    for o, xg in enumerate((x1, x2)):
        y = _colmm(tabs["f_half"], z, BF16, epi="cmul", extra=(gspec,), tn_cap=HY_WIDTH, g_col=o)
        brow = jnp.tile(bias[o], b)[None, :]
        z = _colmm(tabs["i_half"], y, F32, epi="gate", extra=(xg, z, brow), tn_cap=256)
    return z


def _fn0_kernel(x_ref, w_ref, o_ref):
    acc = _dot(x_ref[0].astype(BF16), w_ref[...])
    o_ref[0] = acc[:, :BRANCH_WIDTH].astype(o_ref.dtype)
    o_ref[1] = acc[:, BRANCH_WIDTH:].astype(o_ref.dtype)


def _fnet_channel_dft(u):
    b, n, w = u.shape
    j = np.arange(FN_GROUP_DIM)
    ang = 2.0 * np.pi * ((j[:, None] * j[None, :]) % FN_GROUP_DIM) / FN_GROUP_DIM
    eye = np.eye(FN_GROUPS)
    wc = np.concatenate([np.kron(eye, np.cos(ang)), -np.kron(eye, np.sin(ang))], axis=1)
    wc = jnp.asarray(wc, F32).astype(BF16)
    tm = _pick_tile(n, 1024, 8)
    return pl.pallas_call(
        _fn0_kernel,
        grid=(b, n // tm),
        in_specs=[pl.BlockSpec((1, tm, w), lambda bb, i: (bb, i, 0)),
                  pl.BlockSpec((w, 2 * w), lambda bb, i: (0, 0))],
        out_specs=pl.BlockSpec((2, tm, w), lambda bb, i: (0, i, bb)),
        out_shape=jax.ShapeDtypeStruct((2, n, b * w), BF16),
        compiler_params=_params(("parallel", "parallel")),
        name="fnet_channel_dft",
    )(u, wc)


def _fnet_long(u, tabs):
    b, n, w = u.shape
    cw = b * w
    m2 = FN_N2
    m1 = n // m2
    wri = _fnet_channel_dft(u)
    a = _colmm(tabs["fn_s1"], wri.reshape(2 * m1, m2 * cw), BF16)
    y = _bmm(tabs["fn_s3"], a.reshape(2, m1, m2, cw), BF16, cplx_out=False)
    return jnp.transpose(y, (1, 0, 2)).reshape(n, cw)


def _fnet_short(u, tabs):
    b, n, w = u.shape
    wri = _fnet_channel_dft(u)
    return _colmm(tabs["fn"], wri.reshape(2 * n, b * w), BF16, tn_cap=256)


def _merge_kernel(x_ref, g_ref, sc_ref, sh_ref, gt_ref, oa_ref, ob_ref, oc_ref, od_ref,
                  wg_ref, wb_ref, wo_ref, o_ref):
    x = x_ref[0]
    hb = _normmod(x, g_ref[...], sc_ref[0], sh_ref[0]).astype(BF16)
    branches = (oa_ref[0], ob_ref[...], oc_ref[...], od_ref[0])
    acc = jnp.zeros(x.shape, F32)
    for nbr, o in enumerate(branches):
        gate = _dot(hb, wg_ref[:, nbr * D_MODEL:(nbr + 1) * D_MODEL])
        proj = _dot(o.astype(BF16), wb_ref[nbr])
        acc = acc + jax.nn.sigmoid(gate) * proj
    m = _dot(acc.astype(BF16), wo_ref[...])
    o_ref[0] = x + gt_ref[0] * m


def _merge(x, g, sc, sh, gt, oa, ob, oc, od, w):
    b, n, _ = x.shape
    tm = _pick_tile(n, 512, 8)
    tok = lambda bb, i: (bb, i, 0)
    row = lambda bb, i: (bb, 0, 0)
    seq = lambda bb, i: (i, bb)
    c2 = lambda bb, i: (0, 0)
    return pl.pallas_call(
        _merge_kernel,
        grid=(b, n // tm),
        in_specs=[pl.BlockSpec((1, tm, D_MODEL), tok),
                  pl.BlockSpec((1, D_MODEL), c2),
                  pl.BlockSpec((1, 1, D_MODEL), row), pl.BlockSpec((1, 1, D_MODEL), row),
                  pl.BlockSpec((1, 1, D_MODEL), row),
                  pl.BlockSpec((1, tm, BRANCH_WIDTH), tok),
                  pl.BlockSpec((tm, BRANCH_WIDTH), seq),
                  pl.BlockSpec((tm, BRANCH_WIDTH), seq),
                  pl.BlockSpec((1, tm, BRANCH_WIDTH), tok),
                  pl.BlockSpec((D_MODEL, N_BRANCH * D_MODEL), c2),
                  pl.BlockSpec((N_BRANCH, BRANCH_WIDTH, D_MODEL), lambda bb, i: (0, 0, 0)),
                  pl.BlockSpec((D_MODEL, D_MODEL), c2)],
        out_specs=pl.BlockSpec((1, tm, D_MODEL), tok),
        out_shape=jax.ShapeDtypeStruct(x.shape, F32),
        compiler_params=_params(("parallel", "parallel"), 56),
        name="branch_merge",
    )(x, g, sc, sh, gt, oa, ob, oc, od, w["wgate"], w["wbranch"], w["wout"])


def _moe_kernel(x_ref, g_ref, sc_ref, sh_ref, gt_ref, wr_ref, br_ref, wg_ref, wu_ref, wd_ref, fg_ref,
                o_ref, h_s, wt_s, acc_s, *, final):
    e = pl.program_id(2)

    @pl.when(e == 0)
    def _():
        hb = _normmod(x_ref[0], g_ref[...], sc_ref[0], sh_ref[0]).astype(BF16)
        h_s[...] = hb
        r = _dot(hb, wr_ref[...]) + br_ref[...]
        lane = lax.broadcasted_iota(jnp.int32, r.shape, 1)
        neg = jnp.float32(-jnp.inf)
        isg = lane < MOE_GROUPS
        gmax = jnp.max(jnp.where(isg, r, neg), axis=1, keepdims=True)
        gsum = jnp.sum(jnp.where(isg, jnp.exp(r - gmax), 0.0), axis=1, keepdims=True)
        g_p = 1.0 / gsum
        gidx = jnp.min(jnp.where(isg & (r == gmax), lane, ROUTER_LANES), axis=1, keepdims=True)
        ise = ((lane >= MOE_GROUPS) & (lane < MOE_GROUPS + MOE_EXPERTS)
               & (jnp.right_shift(lane - MOE_GROUPS, 2) == gidx))
        el = jnp.where(ise, r, neg)
        e1 = jnp.max(el, axis=1, keepdims=True)
        i1 = jnp.min(jnp.where(ise & (r == e1), lane, ROUTER_LANES), axis=1, keepdims=True)
        el2 = jnp.where(lane == i1, neg, el)
        e2 = jnp.max(el2, axis=1, keepdims=True)
        i2 = jnp.min(jnp.where(el2 == e2, lane, ROUTER_LANES), axis=1, keepdims=True)
        d = jnp.exp(e2 - e1)
        w1 = 1.0 / (1.0 + d)
        w2 = d / (1.0 + d)
        wt_s[...] = jnp.where(lane == i1, g_p * w1, jnp.where(lane == i2, g_p * w2, 0.0))
        acc_s[...] = jnp.zeros(acc_s.shape, F32)

    hb = h_s[...]
    wt = wt_s[...]
    lane = lax.broadcasted_iota(jnp.int32, wt.shape, 1)
    acts = []
    for k in range(MOE_PER_GROUP):
        a = _dot(hb, wg_ref[k])
        u = _dot(hb, wu_ref[k])
        col = MOE_GROUPS + e * MOE_PER_GROUP + k
        wcol = jnp.sum(jnp.where(lane == col, wt, 0.0), axis=1, keepdims=True)
        acts.append(((a * jax.nn.sigmoid(a)) * u * wcol).astype(BF16))
    act = jnp.concatenate(acts, axis=1)
    wd = wd_ref[...].reshape(MOE_PER_GROUP * MOE_HIDDEN, D_MODEL)
    acc_s[...] += _dot(act, wd)

    @pl.when(e == MOE_GROUPS - 1)
    def _():
        y = x_ref[0] + gt_ref[0] * acc_s[...]
        if final:
            y = _rms(y) * fg_ref[...]
        o_ref[0] = y


def _moe(x, g, sc, sh, gt, w, fg, final):
    b, n, _ = x.shape
    tm = _pick_tile(n, 1024, 8)
    tok = lambda bb, i, e: (bb, i, 0)
    row = lambda bb, i, e: (bb, 0, 0)
    c2 = lambda bb, i, e: (0, 0)
    ex = lambda bb, i, e: (e, 0, 0)
    return pl.pallas_call(
        functools.partial(_moe_kernel, final=final),
        grid=(b, n // tm, MOE_GROUPS),
        in_specs=[pl.BlockSpec((1, tm, D_MODEL), tok),
                  pl.BlockSpec((1, D_MODEL), c2),
                  pl.BlockSpec((1, 1, D_MODEL), row), pl.BlockSpec((1, 1, D_MODEL), row),
                  pl.BlockSpec((1, 1, D_MODEL), row),
                  pl.BlockSpec((D_MODEL, ROUTER_LANES), c2), pl.BlockSpec((1, ROUTER_LANES), c2),
                  pl.BlockSpec((MOE_PER_GROUP, D_MODEL, MOE_HIDDEN), ex),
                  pl.BlockSpec((MOE_PER_GROUP, D_MODEL, MOE_HIDDEN), ex),
                  pl.BlockSpec((MOE_PER_GROUP, MOE_HIDDEN, D_MODEL), ex),
                  pl.BlockSpec((1, D_MODEL), c2)],
        out_specs=pl.BlockSpec((1, tm, D_MODEL), tok),
        out_shape=jax.ShapeDtypeStruct(x.shape, F32),
        scratch_shapes=[pltpu.VMEM((tm, D_MODEL), BF16), pltpu.VMEM((tm, ROUTER_LANES), F32),
                        pltpu.VMEM((tm, D_MODEL), F32)],
        compiler_params=_params(("parallel", "parallel", "arbitrary"), 56),
        name="hier_moe",
    )(x, g, sc, sh, gt, w["wr"], w["br"], w["wg"], w["wu"], w["wd"], fg)


def _layer_weights(l, w_in, mla_q_norm, mla_w_uq, mla_kv_norm, mla_w_ukv, w_branch, w_out,
                   moe_w_group, moe_b_group, moe_w_expert, moe_b_expert, moe_w_gate, moe_w_up, moe_w_down):
    wi = w_in[l]
    b0 = MLA_Q_RANK
    b1 = b0 + MLA_KV_RANK
    b2 = b1 + MLA_ROPE
    b3 = b2 + HY_IN
    b4 = b3 + BRANCH_WIDTH
    b5 = b4 + 3 * BRANCH_WIDTH
    zpad = lambda r, c: jnp.zeros((r, c), F32)
    w_kr = wi[:, b1:b2]
    krp = jnp.concatenate([zpad(D_MODEL, MLA_NOPE), w_kr[:, 0::2], w_kr[:, 1::2],
                           zpad(D_MODEL, MLA_HEAD_PAD - MLA_NOPE - MLA_ROPE)], axis=1)
    w_na = wi[:, b4:b5]
    w_na = jnp.concatenate([w_na[:, :BRANCH_WIDTH] * NA_SCALE, w_na[:, BRANCH_WIDTH:]], axis=1)
    wa = jnp.concatenate([wi[:, :b1], krp, wi[:, b2:b4], w_na], axis=1).astype(BF16)
    uq = mla_w_uq[l].reshape(MLA_Q_RANK, MLA_HEADS, MLA_NOPE + MLA_ROPE) * (MLA_SCALE * LOG2E)
    uq = jnp.concatenate([uq[..., :MLA_NOPE], uq[..., MLA_NOPE::2], uq[..., MLA_NOPE + 1::2],
                          jnp.zeros((MLA_Q_RANK, MLA_HEADS, MLA_HEAD_PAD - MLA_NOPE - MLA_ROPE), F32)], axis=-1)
    ukv = mla_w_ukv[l].reshape(MLA_KV_RANK, MLA_HEADS, MLA_NOPE + MLA_V)
    uk = jnp.concatenate([ukv[..., :MLA_NOPE],
                          jnp.zeros((MLA_KV_RANK, MLA_HEADS, MLA_HEAD_PAD - MLA_NOPE), F32)], axis=-1)
    uv = ukv[..., MLA_NOPE:]
    wr = jnp.concatenate([moe_w_group[l], moe_w_expert[l],
                          zpad(D_MODEL, ROUTER_LANES - MOE_GROUPS - MOE_EXPERTS)], axis=1)
    br = jnp.concatenate([moe_b_group[l], moe_b_expert[l],
                          jnp.zeros((ROUTER_LANES - MOE_GROUPS - MOE_EXPERTS,), F32)])[None, :]
    return dict(
        wa=wa, qn=mla_q_norm[l][None, :], kvn=mla_kv_norm[l][None, :],
        wuq=uq.reshape(MLA_Q_RANK, -1).astype(BF16), wuk=uk.reshape(MLA_KV_RANK, -1).astype(BF16),
        wuv=uv.reshape(MLA_KV_RANK, -1).astype(BF16),
        wgate=wi[:, b5:].astype(BF16), wbranch=w_branch[l].astype(BF16), wout=w_out[l].astype(BF16),
        wr=wr.astype(BF16), br=br, wg=moe_w_gate[l].astype(BF16), wu=moe_w_up[l].astype(BF16),
        wd=moe_w_down[l].astype(BF16))


def _rope_tables(n):
    t = jnp.arange(n, dtype=jnp.int32)
    row = (t // GRID_W).astype(F32)
    col = (t % GRID_W).astype(F32)
    n_freq = MLA_ROPE // 4
    inv = ROPE_BASE ** (-jnp.arange(n_freq, dtype=F32) / n_freq)
    ang = jnp.concatenate([row[:, None] * inv, col[:, None] * inv], axis=-1)
    c, s = jnp.cos(ang), jnp.sin(ang)
    one = jnp.ones((n, MLA_NOPE), F32)
    zero = jnp.zeros((n, MLA_NOPE), F32)
    hr = MLA_ROPE // 2
    tail1 = jnp.ones((n, MLA_HEAD_PAD - MLA_NOPE - MLA_ROPE), F32)
    tail0 = jnp.zeros((n, MLA_HEAD_PAD - MLA_NOPE - MLA_ROPE), F32)
    cos = jnp.concatenate([one, c, c, tail1], axis=1)
    s1 = jnp.concatenate([zero, -s, jnp.zeros((n, hr), F32), tail0], axis=1)
    s2 = jnp.concatenate([zero, jnp.zeros((n, hr), F32), s, tail0], axis=1)
    return cos, s1, s2


def _identity_rope(n):
    return (jnp.ones((n, MLA_HEAD_PAD), F32), jnp.zeros((n, MLA_HEAD_PAD), F32),
            jnp.zeros((n, MLA_HEAD_PAD), F32))


def kernel(x, c, ctx, c_ctx, ada_w, ada_b, norm1_g, norm2_g, w_in, mla_q_norm, mla_w_uq, mla_kv_norm, mla_w_ukv, hy_short_w, hy_w1, hy_b1, hy_freq1, hy_w2, hy_b2, hy_freq2, hy_w3, hy_bias, na_rpb, w_branch, w_out, moe_w_group, moe_b_group, moe_w_expert, moe_b_expert, moe_w_gate, moe_w_up, moe_w_down, final_norm_g):
    b, n_lat, _ = x.shape
    n_ctx = ctx.shape[1]
    assert b <= 4 and n_lat % (NA_QROWS * GRID_W) == 0 and n_lat // GRID_W >= 2 * NA_QROWS

    cvec = jnp.concatenate([c, c_ctx[None, :], jnp.zeros((8 - b - 1, D_MODEL), F32)], axis=0)
    mod = _mod_vectors(cvec, ada_w, ada_b)

    rope_lat = _rope_tables(n_lat)
    rope_ctx = _identity_rope(n_ctx)
    tabs_lat = _dft_tables(n_lat)
    tabs_ctx = _small_dft_tables(n_ctx)
    fg = final_norm_g[None, :]

    for l in range(DEPTH):
        last = l == DEPTH - 1
        w = _layer_weights(l, w_in, mla_q_norm, mla_w_uq, mla_kv_norm, mla_w_ukv, w_branch, w_out,
                           moe_w_group, moe_b_group, moe_w_expert, moe_b_expert,
                           moe_w_gate, moe_w_up, moe_w_down)
        ml = mod[l, :b].reshape(b, 1, 6, D_MODEL)
        mc = jnp.broadcast_to(mod[l, b].reshape(1, 1, 6, D_MODEL), (b, 1, 6, D_MODEL))
        sh1, sc1, gt1, sh2, sc2, gt2 = [ml[:, :, i] for i in range(6)]
        sh1c, sc1c, gt1c, sh2c, sc2c, gt2c = [mc[:, :, i] for i in range(6)]
        g1 = norm1_g[l][None, :]
        g2 = norm2_g[l][None, :]
        fw = dict(
            w1=jnp.concatenate([hy_w1[l], jnp.zeros((HY_EMB_PAD - HY_EMB, HY_HIDDEN), F32)], axis=0).astype(BF16),
            b1=hy_b1[l][None, :], f1=hy_freq1[l][None, :], w2=hy_w2[l].astype(BF16), b2=hy_b2[l][None, :],
            f2=hy_freq2[l][None, :],
            w3=jnp.transpose(hy_w3[l].reshape(HY_HIDDEN, 2, HY_ORDER * HY_WIDTH), (1, 0, 2)).astype(BF16))

        n_keys = n_lat + n_ctx
        kv_bufs = (jnp.zeros((b, MLA_HEADS, n_keys, MLA_HEAD_PAD), BF16),
                   jnp.zeros((b, MLA_HEADS, MLA_VT_ROWS, n_keys), PV_DTYPE))
        q, kall, vtall, hy, fn, naq, nak, nav = _inproj(x, g1, sc1, sh1, w, rope_lat, kv_bufs, 0, False)
        qc, kc, vc, kall, vtall, hyc, fnc, naqc, nakc, navc = _inproj(
            ctx, g1, sc1c, sh1c, w, rope_ctx, (kall, vtall), n_lat, True)

        oa = _flash_t(q, kall, vtall, MLA_V)
        od = _na(naq, nak, nav, nakc, navc, _na_bias_tables(na_rpb[l]))
        ob = _hyena_long(hy, hy_short_w[l], fw, hy_bias[l], tabs_lat)
        oc = _fnet_long(fn, tabs_lat)
        x_new = _merge(x, g1, sc1, sh1, gt1, oa, ob, oc, od, w)

        if not last:
            oa_c = _flash(qc, kc, vc, True)
            heads = lambda t: jnp.transpose(t.reshape(b, n_ctx, NA_HEADS, NA_HEAD_DIM), (0, 2, 1, 3))
            od_c = _flash(heads(naqc), heads(nakc), heads(navc), False)
            ob_c = _hyena_short(hyc, hy_short_w[l], fw, hy_bias[l], tabs_ctx)
            oc_c = _fnet_short(fnc, tabs_ctx)
            ctx = _merge(ctx, g1, sc1c, sh1c, gt1c, oa_c, ob_c, oc_c, od_c, w)
            ctx = _moe(ctx, g2, sc2c, sh2c, gt2c, w, fg, False)
        x = _moe(x_new, g2, sc2, sh2, gt2, w, fg, last)
    return x
```

```python
import functools
import math

import numpy as np
import jax
import jax.numpy as jnp
from jax import lax
from jax.experimental import pallas as pl
from jax.experimental.pallas import tpu as pltpu

F32 = jnp.float32
BF16 = jnp.bfloat16

D_MODEL = 1024
DEPTH = 2
GRID_W = 64
NORM_EPS = 1e-6
MASK_VALUE = -1e30
BRANCH_WIDTH = D_MODEL // 4
N_BRANCH = 4

MLA_HEADS = 4
MLA_Q_RANK = D_MODEL // 4
MLA_KV_RANK = D_MODEL // 8
MLA_NOPE = 64
MLA_ROPE = 32
MLA_V = BRANCH_WIDTH // MLA_HEADS
MLA_SCALE = 1.0 / math.sqrt(MLA_NOPE + MLA_ROPE)
MLA_HEAD_PAD = 128
MLA_VT_ROWS = MLA_V + 16
PV_DTYPE = BF16
ROPE_BASE = 10000.0
LOG2E = 1.4426950408889634

HY_WIDTH = BRANCH_WIDTH
HY_ORDER = 2
HY_BANDS = 16
HY_EMB = 1 + 2 * HY_BANDS
HY_EMB_PAD = 128
HY_HIDDEN = 64
HY_DECAY_TARGET = 1e-2
HY_DECAY_SHORT_PCT = 0.3
HY_DECAY_LONG_PCT = 1.5
HY_IN = (HY_ORDER + 1) * HY_WIDTH

FN_GROUPS = 4
FN_GROUP_DIM = BRANCH_WIDTH // FN_GROUPS

NA_HEADS = 4
NA_HEAD_DIM = BRANCH_WIDTH // NA_HEADS
NA_KH = 8
NA_KW = 16
NA_SCALE = 1.0 / math.sqrt(NA_HEAD_DIM)
NA_QROWS = 8
NA_KROWS = 16

MOE_GROUPS = 4
MOE_PER_GROUP = 4
MOE_EXPERTS = MOE_GROUPS * MOE_PER_GROUP
MOE_HIDDEN = 256
ROUTER_LANES = 128

MLA_IN = MLA_Q_RANK + MLA_KV_RANK + MLA_ROPE
MIX_IN = MLA_IN + HY_IN + BRANCH_WIDTH + 3 * BRANCH_WIDTH
MIX_PAD = MLA_Q_RANK + MLA_KV_RANK + MLA_HEAD_PAD + HY_IN + BRANCH_WIDTH + 3 * BRANCH_WIDTH

DFT_N2 = 256
FN_N2 = 128
BMM_GROUPS_PER_STEP = 4
FLASH_K_TILE = 3328
FLASH_Q_TILE = 1024
FLASH_Q_SUB = 512


def _params(sem, vmem_mb=None):
    kw = dict(dimension_semantics=sem)
    if vmem_mb is not None:
        kw["vmem_limit_bytes"] = vmem_mb << 20
    return pltpu.CompilerParams(**kw)


def _pick_tile(n, cap, mult):
    best = None
    for t in range(mult, min(n, cap) + 1, mult):
        if n % t == 0:
            best = t
    return best if best is not None else n


def _rms(x):
    return x * lax.rsqrt(jnp.mean(x * x, axis=-1, keepdims=True) + NORM_EPS)


def _normmod(x, g, sc, sh):
    return (_rms(x) * g) * (1.0 + sc) + sh


def _dot(a, b):
    return jnp.dot(a, b, preferred_element_type=F32)


def _dot_nt(a, b):
    return lax.dot_general(a, b, (((1,), (1,)), ((), ())), preferred_element_type=F32)


def _mod_kernel(c_ref, w_ref, b_ref, o_ref):
    c = c_ref[...]
    s = c * jax.nn.sigmoid(c)
    o_ref[0] = _dot(s.astype(BF16), w_ref[0].astype(BF16)) + b_ref[0]


def _mod_vectors(cvec, ada_w, ada_b):
    n6 = ada_w.shape[-1]
    tn = 1024
    return pl.pallas_call(
        _mod_kernel,
        grid=(DEPTH, n6 // tn),
        in_specs=[pl.BlockSpec((8, D_MODEL), lambda l, j: (0, 0)),
                  pl.BlockSpec((1, D_MODEL, tn), lambda l, j: (l, 0, j)),
                  pl.BlockSpec((1, 1, tn), lambda l, j: (l, 0, j))],
        out_specs=pl.BlockSpec((1, 8, tn), lambda l, j: (l, 0, j)),
        out_shape=jax.ShapeDtypeStruct((DEPTH, 8, n6), F32),
        compiler_params=_params(("parallel", "parallel")),
        name="adaln_vectors",
    )(cvec, ada_w, ada_b.reshape(DEPTH, 1, n6))


def _inproj_kernel(*refs, ctx_mode):
    (x_ref, g_ref, sc_ref, sh_ref, wa_ref, qn_ref, kvn_ref, wuq_ref, wuk_ref, wuv_ref,
     cos_ref, s1_ref, s2_ref) = refs[:13]
    if ctx_mode:
        (q_ref, k_ref, v_ref, kall_ref, vtall_ref, hy_ref, fn_ref, naq_ref, nak_ref, nav_ref) = refs[15:]
    else:
        (q_ref, kall_ref, vtall_ref, hy_ref, fn_ref, naq_ref, nak_ref, nav_ref) = refs[15:]
    h = _normmod(x_ref[0], g_ref[...], sc_ref[0], sh_ref[0]).astype(BF16)
    p = _dot(h, wa_ref[...])
    c0 = MLA_Q_RANK
    c1 = c0 + MLA_KV_RANK
    c2 = c1 + MLA_HEAD_PAD
    c3 = c2 + HY_IN
    c4 = c3 + BRANCH_WIDTH
    c5 = c4 + BRANCH_WIDTH
    c6 = c5 + BRANCH_WIDTH
    cq, ckv, krp = p[:, :c0], p[:, c0:c1], p[:, c1:c2]
    hy_ref[0] = p[:, c2:c3]
    fn_ref[0] = p[:, c3:c4]
    naq_ref[0] = p[:, c4:c5].astype(BF16)
    nak_ref[0] = p[:, c5:c6].astype(BF16)
    nav_ref[0] = p[:, c6:].astype(BF16)
    cqn = (_rms(cq) * qn_ref[...]).astype(BF16)
    ckn = (_rms(ckv) * kvn_ref[...]).astype(BF16)
    q = _dot(cqn, wuq_ref[...])
    kn = _dot(ckn, wuk_ref[...])
    v = _dot(ckn, wuv_ref[...])
    cos, s1, s2 = cos_ref[...], s1_ref[...], s2_ref[...]

    def rope(t):
        return (t * cos + pltpu.roll(t, MLA_HEAD_PAD - MLA_ROPE // 2, 1) * s1
                + pltpu.roll(t, MLA_ROPE // 2, 1) * s2)

    kr = rope(krp)
    vt = v.T
    tail_rows = lax.broadcasted_iota(jnp.int32, (MLA_VT_ROWS - MLA_V, v.shape[0]), 0)
    tail = jnp.where(tail_rows == 0, 1.0, 0.0).astype(PV_DTYPE)
    for hh in range(MLA_HEADS):
        sl = slice(hh * MLA_HEAD_PAD, (hh + 1) * MLA_HEAD_PAD)
        vs = slice(hh * MLA_V, (hh + 1) * MLA_V)
        kh = (kn[:, sl] + kr).astype(BF16)
        q_ref[0, hh] = rope(q[:, sl]).astype(BF16)
        kall_ref[0, hh] = kh
        vtall_ref[0, hh, :MLA_V] = vt[vs].astype(PV_DTYPE)
        vtall_ref[0, hh, MLA_V:] = tail
        if ctx_mode:
            k_ref[0, hh] = kh
            v_ref[0, hh] = v[:, vs].astype(BF16)


def _inproj(x, g, sc, sh, w, rope_tabs, kv_bufs, key_off, ctx_mode):
    b, n, _ = x.shape
    n_keys = kv_bufs[0].shape[2]
    tm = _pick_tile(n, 512, 128)
    assert key_off % tm == 0
    ko = key_off // tm
    cos, s1, s2 = rope_tabs
    tok = lambda bb, i: (bb, i, 0)
    row = lambda bb, i: (bb, 0, 0)
    const2 = lambda bb, i: (0, 0)
    hd = lambda bb, i: (bb, 0, i, 0)
    tab = lambda bb, i: (i, 0)
    hp = MLA_HEADS * MLA_HEAD_PAD
    in_specs = [pl.BlockSpec((1, tm, D_MODEL), tok),
                pl.BlockSpec((1, D_MODEL), const2),
                pl.BlockSpec((1, 1, D_MODEL), row),
                pl.BlockSpec((1, 1, D_MODEL), row),
                pl.BlockSpec((D_MODEL, MIX_PAD), const2),
                pl.BlockSpec((1, MLA_Q_RANK), const2),
                pl.BlockSpec((1, MLA_KV_RANK), const2),
                pl.BlockSpec((MLA_Q_RANK, hp), const2),
                pl.BlockSpec((MLA_KV_RANK, hp), const2),
                pl.BlockSpec((MLA_KV_RANK, BRANCH_WIDTH), const2),
                pl.BlockSpec((tm, MLA_HEAD_PAD), tab),
                pl.BlockSpec((tm, MLA_HEAD_PAD), tab),
                pl.BlockSpec((tm, MLA_HEAD_PAD), tab)]
    args = [x, g, sc, sh, w["wa"], w["qn"], w["kvn"], w["wuq"], w["wuk"], w["wuv"], cos, s1, s2]
    q_spec = pl.BlockSpec((1, MLA_HEADS, tm, MLA_HEAD_PAD), hd)
    q_sd = jax.ShapeDtypeStruct((b, MLA_HEADS, n, MLA_HEAD_PAD), BF16)
    kall_spec = pl.BlockSpec((1, MLA_HEADS, tm, MLA_HEAD_PAD), lambda bb, i: (bb, 0, ko + i, 0))
    vtall_spec = pl.BlockSpec((1, MLA_HEADS, MLA_VT_ROWS, tm), lambda bb, i: (bb, 0, 0, ko + i))
    kall_sd = jax.ShapeDtypeStruct((b, MLA_HEADS, n_keys, MLA_HEAD_PAD), BF16)
    vtall_sd = jax.ShapeDtypeStruct((b, MLA_HEADS, MLA_VT_ROWS, n_keys), PV_DTYPE)
    rest_specs = [pl.BlockSpec((1, tm, HY_IN), tok)] + [pl.BlockSpec((1, tm, BRANCH_WIDTH), tok)] * 4
    rest_sd = [jax.ShapeDtypeStruct((b, n, HY_IN), F32), jax.ShapeDtypeStruct((b, n, BRANCH_WIDTH), F32)] \
        + [jax.ShapeDtypeStruct((b, n, BRANCH_WIDTH), BF16)] * 3
    in_specs += [pl.BlockSpec(memory_space=pl.ANY), pl.BlockSpec(memory_space=pl.ANY)]
    args += list(kv_bufs)
    if ctx_mode:
        out_specs = [q_spec, q_spec, pl.BlockSpec((1, MLA_HEADS, tm, MLA_V), hd), kall_spec, vtall_spec]
        out_shape = [q_sd, q_sd, jax.ShapeDtypeStruct((b, MLA_HEADS, n, MLA_V), BF16), kall_sd, vtall_sd]
        aliases = {13: 3, 14: 4}
    else:
        out_specs = [q_spec, kall_spec, vtall_spec]
        out_shape = [q_sd, kall_sd, vtall_sd]
        aliases = {13: 1, 14: 2}
    return pl.pallas_call(
        functools.partial(_inproj_kernel, ctx_mode=ctx_mode),
        grid=(b, n // tm),
        in_specs=in_specs,
        out_specs=out_specs + rest_specs,
        out_shape=out_shape + rest_sd,
        input_output_aliases=aliases,
        compiler_params=_params(("parallel", "parallel"), 48),
        name="input_projection",
    )(*args)


def _flash_kernel(q_ref, k_ref, v_ref, o_ref, m_ref, l_ref, acc_ref, *, nh, use_exp2):
    j = pl.program_id(2)

    @pl.when(j == 0)
    def _():
        m_ref[...] = jnp.full(m_ref.shape, MASK_VALUE, F32)
        l_ref[...] = jnp.zeros(l_ref.shape, F32)
        acc_ref[...] = jnp.zeros(acc_ref.shape, F32)

    ex = jnp.exp2 if use_exp2 else jnp.exp
    for h in range(nh):
        s = _dot_nt(q_ref[0, h], k_ref[0, h])
        m_prev = m_ref[h]
        m_cur = jnp.maximum(m_prev, jnp.max(s, axis=1, keepdims=True))
        alpha = ex(m_prev - m_cur)
        p = ex(s - m_cur)
        l_ref[h] = alpha * l_ref[h] + jnp.sum(p, axis=1, keepdims=True)
        acc_ref[h] = alpha * acc_ref[h] + _dot(p.astype(BF16), v_ref[0, h])
        m_ref[h] = m_cur

    @pl.when(j == pl.num_programs(2) - 1)
    def _():
        outs = [acc_ref[h] / l_ref[h] for h in range(nh)]
        o_ref[0] = jnp.concatenate(outs, axis=1).astype(o_ref.dtype)


def _flash(q, k, v, use_exp2):
    b, nh, lq, dk = q.shape
    lk, dv = k.shape[2], v.shape[3]
    tq = _pick_tile(lq, 512, 8)
    tk = _pick_tile(lk, 1280, 128)
    return pl.pallas_call(
        functools.partial(_flash_kernel, nh=nh, use_exp2=use_exp2),
        grid=(b, lq // tq, lk // tk),
        in_specs=[pl.BlockSpec((1, nh, tq, dk), lambda bb, i, j: (bb, 0, i, 0)),
                  pl.BlockSpec((1, nh, tk, dk), lambda bb, i, j: (bb, 0, j, 0)),
                  pl.BlockSpec((1, nh, tk, dv), lambda bb, i, j: (bb, 0, j, 0))],
        out_specs=pl.BlockSpec((1, tq, nh * dv), lambda bb, i, j: (bb, i, 0)),
        out_shape=jax.ShapeDtypeStruct((b, lq, nh * dv), BF16),
        scratch_shapes=[pltpu.VMEM((nh, tq, 1), F32), pltpu.VMEM((nh, tq, 1), F32),
                        pltpu.VMEM((nh, tq, dv), F32)],
        compiler_params=_params(("parallel", "parallel", "arbitrary"), 48),
        name="softmax_attention",
    )(q, k, v)


def _flash_t_kernel(q_ref, k_ref, vt_ref, o_ref, m_ref, acc_ref, st_ref, p_ref, *, nh, dv):
    j = pl.program_id(2)

    @pl.when(j == 0)
    def _():
        m_ref[...] = jnp.full(m_ref.shape, MASK_VALUE, F32)
        acc_ref[...] = jnp.zeros(acc_ref.shape, F32)

    rows = vt_ref.shape[2]

    tq = q_ref.shape[2]
    qs = min(tq, FLASH_Q_SUB)
    units = [(h, c) for h in range(nh) for c in range(tq // qs)]

    def scores(i):
        h, c = units[i]
        cols = slice(c * qs, (c + 1) * qs)
        st = _dot_nt(k_ref[0, h], q_ref[0, h, cols, :])
        st_ref[i % 2] = st
        m_prev = m_ref[h, :, cols]
        m_cur = jnp.maximum(m_prev, jnp.max(st, axis=0, keepdims=True))
        m_ref[h, :, cols] = m_cur
        return m_prev, m_cur

    def probs(i, m_cur):
        p_ref[i % 2] = jnp.exp2(st_ref[i % 2] - m_cur).astype(PV_DTYPE)

    def values(i, m_prev, m_cur):
        h, c = units[i]
        cols = slice(c * qs, (c + 1) * qs)
        alpha = jnp.exp2(m_prev - m_cur)
        acc_ref[h, :rows, cols] = alpha * acc_ref[h, :rows, cols] + _dot(vt_ref[0, h], p_ref[i % 2])

    n_units = len(units)
    stats = {0: scores(0)}
    if n_units > 1:
        stats[1] = scores(1)
    probs(0, stats[0][1])
    for i in range(n_units):
        values(i, *stats.pop(i))
        if i + 2 < n_units:
            stats[i + 2] = scores(i + 2)
        if i + 1 < n_units:
            probs(i + 1, stats[i + 1][1])

    @pl.when(j == pl.num_programs(2) - 1)
    def _():
        outs = []
        for h in range(nh):
            a = acc_ref[h]
            o = (a / a[dv:dv + 1]).T
            outs.append(o[:, :dv])
        o_ref[0] = jnp.concatenate(outs, axis=1).astype(o_ref.dtype)


def _flash_t(q, k, vt, dv):
    b, nh, lq, dk = q.shape
    lk, rows = k.shape[2], vt.shape[2]
    tq = _pick_tile(lq, FLASH_Q_TILE, 128)
    tk = _pick_tile(lk, FLASH_K_TILE, 128)
    return pl.pallas_call(
        functools.partial(_flash_t_kernel, nh=nh, dv=dv),
        grid=(b, lq // tq, lk // tk),
        in_specs=[pl.BlockSpec((1, nh, tq, dk), lambda bb, i, j: (bb, 0, i, 0)),
                  pl.BlockSpec((1, nh, tk, dk), lambda bb, i, j: (bb, 0, j, 0)),
                  pl.BlockSpec((1, nh, rows, tk), lambda bb, i, j: (bb, 0, 0, j))],
        out_specs=pl.BlockSpec((1, tq, nh * dv), lambda bb, i, j: (bb, i, 0)),
        out_shape=jax.ShapeDtypeStruct((b, lq, nh * dv), BF16),
        scratch_shapes=[pltpu.VMEM((nh, 1, tq), F32), pltpu.VMEM((nh, 128, tq), F32),
                        pltpu.VMEM((2, tk, min(tq, FLASH_Q_SUB)), F32),
                        pltpu.VMEM((2, tk, min(tq, FLASH_Q_SUB)), PV_DTYPE)],
        compiler_params=_params(("parallel", "parallel", "arbitrary"), 56),
        name="softmax_attention_t",
    )(q, k, vt)


def _na_kernel(q_ref, k0, k1, k2, k3, v0, v1, v2, v3, kc_ref, vc_ref, b_ref, o_ref):
    q = q_ref[0]
    kcat = jnp.concatenate([k0[0], k1[0], k2[0], k3[0]], axis=0)
    vcat = jnp.concatenate([v0[0], v1[0], v2[0], v3[0]], axis=0)
    kc, vc = kc_ref[0], vc_ref[0]
    outs = []
    for h in range(NA_HEADS):
        sl = slice(h * NA_HEAD_DIM, (h + 1) * NA_HEAD_DIM)
        qh = q[:, sl]
        s = _dot_nt(qh, kcat[:, sl]) + b_ref[0, h]
        sc = _dot_nt(qh, kc[:, sl])
        m = jnp.maximum(jnp.max(s, axis=1, keepdims=True), jnp.max(sc, axis=1, keepdims=True))
        p = jnp.exp(s - m)
        pc = jnp.exp(sc - m)
        l = jnp.sum(p, axis=1, keepdims=True) + jnp.sum(pc, axis=1, keepdims=True)
        o = _dot(p.astype(BF16), vcat[:, sl]) + _dot(pc.astype(BF16), vc[:, sl])
        outs.append(o / l)
    o_ref[0] = jnp.concatenate(outs, axis=1).astype(o_ref.dtype)


def _na_bias_tables(rpb):
    j = np.arange(NA_QROWS)[:, None]
    m = np.arange(NA_KROWS)[None, :]
    valid = np.zeros((3, NA_QROWS, NA_KROWS), bool)
    st0 = np.maximum(j - NA_KH // 2, 0)
    valid[0] = (m - 4 >= st0) & (m - 4 < st0 + NA_KH)
    valid[1] = (m - j >= 0) & (m - j < NA_KH)
    st2 = 4 + np.minimum(j - NA_KH // 2, 0)
    valid[2] = (m < 12) & (m >= st2) & (m < st2 + NA_KH)
    dr_idx = np.clip(m - 4 - j + NA_KH - 1, 0, 2 * NA_KH - 2)
    qc = np.arange(GRID_W)[:, None]
    kc = np.arange(GRID_W)[None, :]
    dc_idx = np.clip(kc - qc, -(NA_KW - 1), NA_KW - 1) + NA_KW - 1
    cs = np.clip(qc - NA_KW // 2, 0, GRID_W - NA_KW)
    col_ok = (kc >= cs) & (kc < cs + NA_KW)
    t = rpb.astype(F32)[:, dr_idx]
    t = t[..., dc_idx]
    t = jnp.transpose(t, (0, 1, 3, 2, 4))
    ok = valid[:, None, :, None, :, None] & col_ok[None, None, None, :, None, :]
    out = jnp.where(jnp.asarray(ok), t[None], MASK_VALUE)
    return out.reshape(3, NA_HEADS, NA_QROWS * GRID_W, NA_KROWS * GRID_W)


def _na(q, k, v, kc, vc, bias):
    b, n, w = q.shape
    c = kc.shape[1]
    tq = NA_QROWS * GRID_W
    tkb = 4 * GRID_W
    nb = n // tq
    nkb = n // tkb
    kmaps = [lambda bb, i: (bb, jnp.maximum(2 * i - 1, 0), 0),
             lambda bb, i: (bb, 2 * i, 0),
             lambda bb, i: (bb, 2 * i + 1, 0),
             lambda bb, i: (bb, jnp.minimum(2 * i + 2, nkb - 1), 0)]
    kspecs = [pl.BlockSpec((1, tkb, w), mp) for mp in kmaps]
    return pl.pallas_call(
        _na_kernel,
        grid=(b, nb),
        in_specs=[pl.BlockSpec((1, tq, w), lambda bb, i: (bb, i, 0))] + kspecs + kspecs
                 + [pl.BlockSpec((1, c, w), lambda bb, i: (bb, 0, 0)),
                    pl.BlockSpec((1, c, w), lambda bb, i: (bb, 0, 0)),
                    pl.BlockSpec((1, NA_HEADS, tq, NA_KROWS * GRID_W),
                                 lambda bb, i: (jnp.where(i == 0, 0, jnp.where(i == nb - 1, 2, 1)), 0, 0, 0))],
        out_specs=pl.BlockSpec((1, tq, w), lambda bb, i: (bb, i, 0)),
        out_shape=jax.ShapeDtypeStruct((b, n, w), BF16),
        compiler_params=_params(("parallel", "arbitrary"), 56),
        name="neighbourhood_attention",
    )(q, k, k, k, k, v, v, v, v, kc, vc, bias)


def _sconv_kernel(u_ref, pv_ref, nx_ref, w_ref, v_ref, x1_ref, x2_ref, *, nt):
    i = pl.program_id(1)
    u = u_ref[0]
    tl = u.shape[0]
    prev = jnp.where(i > 0, pv_ref[0, 7:8, :], 0.0)
    nxt = jnp.where(i < nt - 1, nx_ref[0, 0:1, :], 0.0)
    rows = lax.broadcasted_iota(jnp.int32, u.shape, 0)
    um = jnp.where(rows == 0, prev, pltpu.roll(u, 1, 0))
    up = jnp.where(rows == tl - 1, nxt, pltpu.roll(u, tl - 1, 0))
    w = w_ref[...]
    y = um * w[0:1] + u * w[1:2] + up * w[2:3]
    v_ref[...] = y[:, :HY_WIDTH]
    x1_ref[...] = y[:, HY_WIDTH:2 * HY_WIDTH]
    x2_ref[...] = y[:, 2 * HY_WIDTH:]


def _sconv(u, w):
    b, n, _ = u.shape
    tl = _pick_tile(n, 512, 8)
    nt = n // tl
    r8 = tl // 8
    osd = jax.ShapeDtypeStruct((n, b * HY_WIDTH), F32)
    ospec = pl.BlockSpec((tl, HY_WIDTH), lambda bb, i: (i, bb))
    return pl.pallas_call(
        functools.partial(_sconv_kernel, nt=nt),
        grid=(b, nt),
        in_specs=[pl.BlockSpec((1, tl, HY_IN), lambda bb, i: (bb, i, 0)),
                  pl.BlockSpec((1, 8, HY_IN), lambda bb, i: (bb, jnp.maximum(i * r8 - 1, 0), 0)),
                  pl.BlockSpec((1, 8, HY_IN), lambda bb, i: (bb, jnp.minimum((i + 1) * r8, n // 8 - 1), 0)),
                  pl.BlockSpec((3, HY_IN), lambda bb, i: (0, 0))],
        out_specs=[ospec, ospec, ospec],
        out_shape=[osd, osd, osd],
        compiler_params=_params(("parallel", "parallel")),
        name="hyena_short_conv",
    )(u, u, u, w)


def _filt_kernel(z_ref, w1_ref, b1_ref, f1_ref, w2_ref, b2_ref, f2_ref, w3_ref, win_ref, g_ref, ss_ref):
    d = pl.program_id(0)
    i = pl.program_id(1)
    h = jnp.sin(f1_ref[...] * (_dot(z_ref[0].astype(BF16), w1_ref[...]) + b1_ref[...]))
    h = jnp.sin(f2_ref[...] * (_dot(h.astype(BF16), w2_ref[...]) + b2_ref[...]))
    h = _dot(h.astype(BF16), w3_ref[0])
    h = h * jnp.concatenate([win_ref[0]] * HY_ORDER, axis=1)
    rows = lax.broadcasted_iota(jnp.int32, h.shape, 0)
    h = jnp.where((rows == 0) & (i == 0) & (d == 1), 0.0, h)
    g_ref[0] = h
    ss = jnp.sum(h * h, axis=0, keepdims=True)
    first = (i == 0) & (d == 0)

    @pl.when(first)
    def _():
        ss_ref[...] = ss

    @pl.when(jnp.logical_not(first))
    def _():
        ss_ref[...] += ss


def _padded_filter(n, fw):
    r = jnp.arange(n, dtype=jnp.int32)
    idx = jnp.stack([r, jnp.mod(n - r, n)], axis=0).astype(F32)[:, :, None]
    t = idx / (n - 1)
    bands = jnp.linspace(1e-4, HY_BANDS - 1, HY_BANDS, dtype=F32)
    ang = (2.0 * math.pi / n) * idx * bands
    z = jnp.concatenate([t, jnp.cos(ang), -jnp.sin(ang), jnp.zeros((2, n, HY_EMB_PAD - HY_EMB), F32)], axis=-1)
    deltas = jnp.linspace(math.log(HY_DECAY_TARGET) / HY_DECAY_LONG_PCT,
                          math.log(HY_DECAY_TARGET) / HY_DECAY_SHORT_PCT, HY_WIDTH, dtype=F32)
    window = jnp.exp(-t * jnp.abs(deltas))
    tl = _pick_tile(n, 512, 8)
    half = HY_ORDER * HY_WIDTH
    c2 = lambda d, i: (0, 0)
    g, ss = pl.pallas_call(
        _filt_kernel,
        grid=(2, n // tl),
        in_specs=[pl.BlockSpec((1, tl, HY_EMB_PAD), lambda d, i: (d, i, 0)),
                  pl.BlockSpec((HY_EMB_PAD, HY_HIDDEN), c2), pl.BlockSpec((1, HY_HIDDEN), c2),
                  pl.BlockSpec((1, HY_HIDDEN), c2),
                  pl.BlockSpec((HY_HIDDEN, HY_HIDDEN), c2), pl.BlockSpec((1, HY_HIDDEN), c2),
                  pl.BlockSpec((1, HY_HIDDEN), c2),
                  pl.BlockSpec((1, HY_HIDDEN, half), lambda d, i: (d, 0, 0)),
                  pl.BlockSpec((1, tl, HY_WIDTH), lambda d, i: (d, i, 0))],
        out_specs=[pl.BlockSpec((1, tl, half), lambda d, i: (d, i, 0)), pl.BlockSpec((1, half), c2)],
        out_shape=[jax.ShapeDtypeStruct((2, n, half), F32), jax.ShapeDtypeStruct((1, half), F32)],
        compiler_params=_params(("arbitrary", "arbitrary")),
        name="hyena_filter_mlp",
    )(z, fw["w1"], fw["b1"], fw["f1"], fw["w2"], fw["b2"], fw["f2"], fw["w3"], window)
    return g.reshape(2 * n, half), lax.rsqrt(ss + NORM_EPS)


def _colmm_kernel(*refs, epi):
    w_ref, x_ref = refs[0], refs[1]
    o_ref = refs[-1]
    acc = _dot(w_ref[...], x_ref[...].astype(BF16))
    if epi == "gate":
        xg_ref, z_ref, b_ref = refs[2:5]
        acc = xg_ref[...] * (acc + z_ref[...] * b_ref[...])
    elif epi == "scale":
        acc = acc * refs[2][...]
    elif epi == "cmul":
        g = refs[2][...]
        mh = acc.shape[0] // 2
        ar, ai, gr, gi = acc[:mh], acc[mh:], g[:mh], g[mh:]
        acc = jnp.concatenate([ar * gr - ai * gi, ar * gi + ai * gr], axis=0)
    o_ref[...] = acc.astype(o_ref.dtype)


def _colmm(w, x, out_dtype, epi="none", extra=(), tn_cap=4096, g_col=None):
    m, k = w.shape
    nc = x.shape[1]
    tn = _pick_tile(nc, tn_cap, 128)
    col = lambda j: (0, j)
    specs = [pl.BlockSpec((m, k), lambda j: (0, 0)), pl.BlockSpec((k, tn), col)]
    if epi == "gate":
        specs += [pl.BlockSpec((m, tn), col), pl.BlockSpec((m, tn), col), pl.BlockSpec((1, tn), col)]
    elif epi == "scale":
        specs += [pl.BlockSpec((1, tn), col)]
    elif epi == "cmul":
        specs += [pl.BlockSpec((m, tn), lambda j: (0, g_col))]
    return pl.pallas_call(
        functools.partial(_colmm_kernel, epi=epi),
        grid=(nc // tn,),
        in_specs=specs,
        out_specs=pl.BlockSpec((m, tn), col),
        out_shape=jax.ShapeDtypeStruct((m, nc), out_dtype),
        compiler_params=_params(("parallel",), 48),
        name="dft_stage_" + epi,
    )(w, x, *extra)


def _bmm_kernel(*refs, epi, cplx_out, gs):
    m_ref, x_ref = refs[0], refs[1]
    o_ref = refs[-1]
    for s in range(gs):
        x = x_ref[:, s]
        x = x.reshape(x.shape[0] * x.shape[1], x.shape[2])
        acc = _dot(m_ref[s], x)
        ph = acc.shape[0] // 2
        if epi in ("cmul", "conv"):
            g_ref = refs[2]
            gr, gi = g_ref[0, s], g_ref[1, s]
            cg = gr.shape[1]
            res_r, res_i = [], []
            for t in range(acc.shape[1] // cg):
                ar, ai = acc[:ph, t * cg:(t + 1) * cg], acc[ph:, t * cg:(t + 1) * cg]
                res_r.append(ar * gr - ai * gi)
                res_i.append(ar * gi + ai * gr)
            yr = jnp.concatenate(res_r, axis=1)
            yi = jnp.concatenate(res_i, axis=1)
            if epi == "conv":
                y = jnp.concatenate([yr, yi], axis=0).astype(BF16)
                back = _dot(refs[3][s], y)
                yr, yi = back[:ph], back[ph:]
            o_ref[0, s] = yr.astype(o_ref.dtype)
            o_ref[1, s] = yi.astype(o_ref.dtype)
        elif epi == "scale":
            sc = refs[2][...]
            o_ref[0, s] = (acc[:ph] * sc).astype(o_ref.dtype)
            o_ref[1, s] = (acc[ph:] * sc).astype(o_ref.dtype)
        elif cplx_out:
            o_ref[0, s] = acc[:ph].astype(o_ref.dtype)
            o_ref[1, s] = acc[ph:].astype(o_ref.dtype)
        else:
            c = acc.shape[1]
            o_ref[:, s * c:(s + 1) * c] = acc.astype(o_ref.dtype)


def _bmm(mats, x, out_dtype, epi="none", extra=None, g_col=0, cplx_out=True):
    ng, p, qq = mats.shape
    c = x.shape[3]
    gs = _pick_tile(ng, BMM_GROUPS_PER_STEP, 1)
    specs = [pl.BlockSpec((gs, p, qq), lambda g: (g, 0, 0)),
             pl.BlockSpec((2, gs, qq // 2, c), lambda g: (0, g, 0, 0))]
    args = [mats, x]
    if epi in ("cmul", "conv"):
        cg = HY_WIDTH
        gspec = extra[0] if epi == "conv" else extra
        specs.append(pl.BlockSpec((2, gs, p // 2, cg), lambda g: (0, g, 0, g_col)))
        args.append(gspec)
        if epi == "conv":
            specs.append(pl.BlockSpec((gs, qq, p), lambda g: (g, 0, 0)))
            args.append(extra[1])
    elif epi == "scale":
        specs.append(pl.BlockSpec((1, c), lambda g: (0, 0)))
        args.append(extra)
    if cplx_out:
        ospec = pl.BlockSpec((2, gs, p // 2, c), lambda g: (0, g, 0, 0))
        oshape = jax.ShapeDtypeStruct((2, ng, p // 2, c), out_dtype)
    else:
        ospec = pl.BlockSpec((p, gs * c), lambda g: (0, g))
        oshape = jax.ShapeDtypeStruct((p, ng * c), out_dtype)
    return pl.pallas_call(
        functools.partial(_bmm_kernel, epi=epi, cplx_out=cplx_out, gs=gs),
        grid=(ng // gs,),
        in_specs=specs,
        out_specs=ospec,
        out_shape=oshape,
        compiler_params=_params(("parallel",), 48),
        name="dft_batched_" + epi,
    )(*args)


def _angle(num, den):
    return (2.0 * math.pi / den) * jnp.mod(num, den).astype(F32)


def _dft_tables(n_lat):
    nn = 2 * n_lat
    n2 = DFT_N2
    n1 = nn // n2
    i1 = jnp.arange(n1, dtype=jnp.int32)
    a1 = _angle(i1[:, None] * i1[None, :], n1)
    c1, s1 = jnp.cos(a1), jnp.sin(a1)
    t = {}
    t["hy_f1"] = jnp.concatenate([c1, -s1], axis=0).astype(BF16)
    t["hy_i2"] = jnp.concatenate([c1[:n1 // 2], -s1[:n1 // 2]], axis=1).astype(BF16)
    i2 = jnp.arange(n2, dtype=jnp.int32)
    kk = i1[:, None, None] + n1 * i2[None, :, None]
    a2 = _angle(kk * i2[None, None, :], nn)
    c2, s2 = jnp.cos(a2), jnp.sin(a2)
    fwd = jnp.concatenate([jnp.concatenate([c2, s2], axis=2), jnp.concatenate([-s2, c2], axis=2)], axis=1)
    t["hy_fwd"] = fwd.astype(BF16)
    t["hy_inv"] = jnp.swapaxes(fwd, 1, 2).astype(BF16)
    m2 = FN_N2
    m1 = n_lat // m2
    j1 = jnp.arange(m1, dtype=jnp.int32)
    b1 = _angle(j1[:, None] * j1[None, :], m1)
    cb, sb = jnp.cos(b1), jnp.sin(b1)
    t["fn_s1"] = jnp.concatenate([jnp.concatenate([cb, sb], axis=1),
                                  jnp.concatenate([-sb, cb], axis=1)], axis=0).astype(BF16)
    j2 = jnp.arange(m2, dtype=jnp.int32)
    kf = j1[:, None, None] + m1 * j2[None, :, None]
    b2 = _angle(kf * j2[None, None, :], n_lat)
    norm = 1.0 / math.sqrt(n_lat * FN_GROUP_DIM)
    t["fn_s3"] = (norm * jnp.concatenate([jnp.cos(b2), jnp.sin(b2)], axis=2)).astype(BF16)
    return t


def _small_dft_tables(n):
    nn = 2 * n
    k = jnp.arange(nn, dtype=jnp.int32)
    a = _angle(k[:, None] * k[None, :], nn)
    c, s = jnp.cos(a), jnp.sin(a)
    t = {}
    t["f_full"] = jnp.concatenate([c, -s], axis=0).astype(BF16)
    t["f_half"] = jnp.concatenate([c[:, :n], -s[:, :n]], axis=0).astype(BF16)
    t["i_half"] = jnp.concatenate([c[:n], -s[:n]], axis=1).astype(BF16)
    j = jnp.arange(n, dtype=jnp.int32)
    b = _angle(j[:, None] * j[None, :], n)
    norm = 1.0 / math.sqrt(n * FN_GROUP_DIM)
    t["fn"] = (norm * jnp.concatenate([jnp.cos(b), jnp.sin(b)], axis=1)).astype(BF16)
    return t


def _hyena_long(hy, short_w, fw, bias, tabs):
    b, n, _ = hy.shape
    cw = b * HY_WIDTH
    n2 = DFT_N2
    n1 = 2 * n // n2
    v, x1, x2 = _sconv(hy, short_w)
    g, gscale = _padded_filter(n, fw)
    half = HY_ORDER * HY_WIDTH
    ga = _colmm(tabs["hy_f1"], g.reshape(n1, n2 * half), BF16)
    gspec = _bmm(tabs["hy_fwd"], ga.reshape(2, n1, n2, half), F32, epi="scale",
                 extra=gscale * (1.0 / (2 * n)))
    z = v
    for o, xg in enumerate((x1, x2)):
        a = _colmm(tabs["hy_f1"][:, :n1 // 2], z.reshape(n1 // 2, n2 * cw), BF16)
        bm = _bmm(tabs["hy_fwd"], a.reshape(2, n1, n2, cw), BF16, epi="conv",
                  extra=(gspec, tabs["hy_inv"]), g_col=o)
        brow = jnp.tile(bias[o], n2 * b)[None, :]
        z = _colmm(tabs["hy_i2"], bm.reshape(2 * n1, n2 * cw), F32, epi="gate",
                   extra=(xg.reshape(n1 // 2, n2 * cw), z.reshape(n1 // 2, n2 * cw), brow))
        z = z.reshape(n, cw)
    return z


def _hyena_short(hy, short_w, fw, bias, tabs):
    b, n, _ = hy.shape
    cw = b * HY_WIDTH
    v, x1, x2 = _sconv(hy, short_w)
    g, gscale = _padded_filter(n, fw)
    gspec = _colmm(tabs["f_full"], g, F32, epi="scale", extra=(gscale * (1.0 / (2 * n)),), tn_cap=256)
    z = v
    for o, xg in enumerate((x1, x2)):
        y = _colmm(tabs["f_half"], z, BF16, epi="cmul", extra=(gspec,), tn_cap=HY_WIDTH, g_col=o)
        brow = jnp.tile(bias[o], b)[None, :]
        z = _colmm(tabs["i_half"], y, F32, epi="gate", extra=(xg, z, brow), tn_cap=256)
    return z


def _fn0_kernel(x_ref, w_ref, o_ref):
    acc = _dot(x_ref[0].astype(BF16), w_ref[...])
    o_ref[0] = acc[:, :BRANCH_WIDTH].astype(o_ref.dtype)
    o_ref[1] = acc[:, BRANCH_WIDTH:].astype(o_ref.dtype)


def _fnet_channel_dft(u):
    b, n, w = u.shape
    j = np.arange(FN_GROUP_DIM)
    ang = 2.0 * np.pi * ((j[:, None] * j[None, :]) % FN_GROUP_DIM) / FN_GROUP_DIM
    eye = np.eye(FN_GROUPS)
    wc = np.concatenate([np.kron(eye, np.cos(ang)), -np.kron(eye, np.sin(ang))], axis=1)
    wc = jnp.asarray(wc, F32).astype(BF16)
    tm = _pick_tile(n, 1024, 8)
    return pl.pallas_call(
        _fn0_kernel,
        grid=(b, n // tm),
        in_specs=[pl.BlockSpec((1, tm, w), lambda bb, i: (bb, i, 0)),
                  pl.BlockSpec((w, 2 * w), lambda bb, i: (0, 0))],
        out_specs=pl.BlockSpec((2, tm, w), lambda bb, i: (0, i, bb)),
        out_shape=jax.ShapeDtypeStruct((2, n, b * w), BF16),
        compiler_params=_params(("parallel", "parallel")),
        name="fnet_channel_dft",
    )(u, wc)


def _fnet_long(u, tabs):
    b, n, w = u.shape
    cw = b * w
    m2 = FN_N2
    m1 = n // m2
    wri = _fnet_channel_dft(u)
    a = _colmm(tabs["fn_s1"], wri.reshape(2 * m1, m2 * cw), BF16)
    y = _bmm(tabs["fn_s3"], a.reshape(2, m1, m2, cw), BF16, cplx_out=False)
    return y.reshape(n, cw)


def _fnet_short(u, tabs):
    b, n, w = u.shape
    wri = _fnet_channel_dft(u)
    return _colmm(tabs["fn"], wri.reshape(2 * n, b * w), BF16, tn_cap=256)


def _merge_kernel(x_ref, g_ref, sc_ref, sh_ref, gt_ref, oa_ref, ob_ref, oc_ref, od_ref,
                  wg_ref, wb_ref, wo_ref, o_ref):
    x = x_ref[0]
    hb = _normmod(x, g_ref[...], sc_ref[0], sh_ref[0]).astype(BF16)
    branches = (oa_ref[0], ob_ref[...], oc_ref[...], od_ref[0])
    acc = jnp.zeros(x.shape, F32)
    for nbr, o in enumerate(branches):
        gate = _dot(hb, wg_ref[:, nbr * D_MODEL:(nbr + 1) * D_MODEL])
        proj = _dot(o.astype(BF16), wb_ref[nbr])
        acc = acc + jax.nn.sigmoid(gate) * proj
    m = _dot(acc.astype(BF16), wo_ref[...])
    o_ref[0] = x + gt_ref[0] * m


def _merge(x, g, sc, sh, gt, oa, ob, oc, od, w):
    b, n, _ = x.shape
    tm = _pick_tile(n, 512, 8)
    tok = lambda bb, i: (bb, i, 0)
    row = lambda bb, i: (bb, 0, 0)
    seq = lambda bb, i: (i, bb)
    c2 = lambda bb, i: (0, 0)
    return pl.pallas_call(
        _merge_kernel,
        grid=(b, n // tm),
        in_specs=[pl.BlockSpec((1, tm, D_MODEL), tok),
                  pl.BlockSpec((1, D_MODEL), c2),
                  pl.BlockSpec((1, 1, D_MODEL), row), pl.BlockSpec((1, 1, D_MODEL), row),
                  pl.BlockSpec((1, 1, D_MODEL), row),
                  pl.BlockSpec((1, tm, BRANCH_WIDTH), tok),
                  pl.BlockSpec((tm, BRANCH_WIDTH), seq),
                  pl.BlockSpec((tm, BRANCH_WIDTH), seq),
                  pl.BlockSpec((1, tm, BRANCH_WIDTH), tok),
                  pl.BlockSpec((D_MODEL, N_BRANCH * D_MODEL), c2),
                  pl.BlockSpec((N_BRANCH, BRANCH_WIDTH, D_MODEL), lambda bb, i: (0, 0, 0)),
                  pl.BlockSpec((D_MODEL, D_MODEL), c2)],
        out_specs=pl.BlockSpec((1, tm, D_MODEL), tok),
        out_shape=jax.ShapeDtypeStruct(x.shape, F32),
        compiler_params=_params(("parallel", "parallel"), 56),
        name="branch_merge",
    )(x, g, sc, sh, gt, oa, ob, oc, od, w["wgate"], w["wbranch"], w["wout"])


def _moe_kernel(x_ref, g_ref, sc_ref, sh_ref, gt_ref, wr_ref, br_ref, wg_ref, wu_ref, wd_ref, fg_ref,
                o_ref, h_s, wt_s, acc_s, *, final):
    e = pl.program_id(2)

    @pl.when(e == 0)
    def _():
        hb = _normmod(x_ref[0], g_ref[...], sc_ref[0], sh_ref[0]).astype(BF16)
        h_s[...] = hb
        r = _dot(hb, wr_ref[...]) + br_ref[...]
        lane = lax.broadcasted_iota(jnp.int32, r.shape, 1)
        neg = jnp.float32(-jnp.inf)
        isg = lane < MOE_GROUPS
        gmax = jnp.max(jnp.where(isg, r, neg), axis=1, keepdims=True)
        gsum = jnp.sum(jnp.where(isg, jnp.exp(r - gmax), 0.0), axis=1, keepdims=True)
        g_p = 1.0 / gsum
        gidx = jnp.min(jnp.where(isg & (r == gmax), lane, ROUTER_LANES), axis=1, keepdims=True)
        ise = ((lane >= MOE_GROUPS) & (lane < MOE_GROUPS + MOE_EXPERTS)
               & (jnp.right_shift(lane - MOE_GROUPS, 2) == gidx))
        el = jnp.where(ise, r, neg)
        e1 = jnp.max(el, axis=1, keepdims=True)
        i1 = jnp.min(jnp.where(ise & (r == e1), lane, ROUTER_LANES), axis=1, keepdims=True)
        el2 = jnp.where(lane == i1, neg, el)
        e2 = jnp.max(el2, axis=1, keepdims=True)
        i2 = jnp.min(jnp.where(el2 == e2, lane, ROUTER_LANES), axis=1, keepdims=True)
        d = jnp.exp(e2 - e1)
        w1 = 1.0 / (1.0 + d)
        w2 = d / (1.0 + d)
        wt_s[...] = jnp.where(lane == i1, g_p * w1, jnp.where(lane == i2, g_p * w2, 0.0))
        acc_s[...] = jnp.zeros(acc_s.shape, F32)

    hb = h_s[...]
    wt = wt_s[...]
    lane = lax.broadcasted_iota(jnp.int32, wt.shape, 1)
    acts = []
    for k in range(MOE_PER_GROUP):
        a = _dot(hb, wg_ref[k])
        u = _dot(hb, wu_ref[k])
        col = MOE_GROUPS + e * MOE_PER_GROUP + k
        wcol = jnp.sum(jnp.where(lane == col, wt, 0.0), axis=1, keepdims=True)
        acts.append(((a * jax.nn.sigmoid(a)) * u * wcol).astype(BF16))
    act = jnp.concatenate(acts, axis=1)
    wd = wd_ref[...].reshape(MOE_PER_GROUP * MOE_HIDDEN, D_MODEL)
    acc_s[...] += _dot(act, wd)

    @pl.when(e == MOE_GROUPS - 1)
    def _():
        y = x_ref[0] + gt_ref[0] * acc_s[...]
        if final:
            y = _rms(y) * fg_ref[...]
        o_ref[0] = y


def _moe(x, g, sc, sh, gt, w, fg, final):
    b, n, _ = x.shape
    tm = _pick_tile(n, 1024, 8)
    tok = lambda bb, i, e: (bb, i, 0)
    row = lambda bb, i, e: (bb, 0, 0)
    c2 = lambda bb, i, e: (0, 0)
    ex = lambda bb, i, e: (e, 0, 0)
    return pl.pallas_call(
        functools.partial(_moe_kernel, final=final),
        grid=(b, n // tm, MOE_GROUPS),
        in_specs=[pl.BlockSpec((1, tm, D_MODEL), tok),
                  pl.BlockSpec((1, D_MODEL), c2),
                  pl.BlockSpec((1, 1, D_MODEL), row), pl.BlockSpec((1, 1, D_MODEL), row),
                  pl.BlockSpec((1, 1, D_MODEL), row),
                  pl.BlockSpec((D_MODEL, ROUTER_LANES), c2), pl.BlockSpec((1, ROUTER_LANES), c2),
                  pl.BlockSpec((MOE_PER_GROUP, D_MODEL, MOE_HIDDEN), ex),
                  pl.BlockSpec((MOE_PER_GROUP, D_MODEL, MOE_HIDDEN), ex),
                  pl.BlockSpec((MOE_PER_GROUP, MOE_HIDDEN, D_MODEL), ex),
                  pl.BlockSpec((1, D_MODEL), c2)],
        out_specs=pl.BlockSpec((1, tm, D_MODEL), tok),
        out_shape=jax.ShapeDtypeStruct(x.shape, F32),
        scratch_shapes=[pltpu.VMEM((tm, D_MODEL), BF16), pltpu.VMEM((tm, ROUTER_LANES), F32),
                        pltpu.VMEM((tm, D_MODEL), F32)],
        compiler_params=_params(("parallel", "parallel", "arbitrary"), 56),
        name="hier_moe",
    )(x, g, sc, sh, gt, w["wr"], w["br"], w["wg"], w["wu"], w["wd"], fg)


def _layer_weights(l, w_in, mla_q_norm, mla_w_uq, mla_kv_norm, mla_w_ukv, w_branch, w_out,
                   moe_w_group, moe_b_group, moe_w_expert, moe_b_expert, moe_w_gate, moe_w_up, moe_w_down):
    wi = w_in[l]
    b0 = MLA_Q_RANK
    b1 = b0 + MLA_KV_RANK
    b2 = b1 + MLA_ROPE
    b3 = b2 + HY_IN
    b4 = b3 + BRANCH_WIDTH
    b5 = b4 + 3 * BRANCH_WIDTH
    zpad = lambda r, c: jnp.zeros((r, c), F32)
    w_kr = wi[:, b1:b2]
    krp = jnp.concatenate([zpad(D_MODEL, MLA_NOPE), w_kr[:, 0::2], w_kr[:, 1::2],
                           zpad(D_MODEL, MLA_HEAD_PAD - MLA_NOPE - MLA_ROPE)], axis=1)
    w_na = wi[:, b4:b5]
    w_na = jnp.concatenate([w_na[:, :BRANCH_WIDTH] * NA_SCALE, w_na[:, BRANCH_WIDTH:]], axis=1)
    wa = jnp.concatenate([wi[:, :b1], krp, wi[:, b2:b4], w_na], axis=1).astype(BF16)
    uq = mla_w_uq[l].reshape(MLA_Q_RANK, MLA_HEADS, MLA_NOPE + MLA_ROPE) * (MLA_SCALE * LOG2E)
    uq = jnp.concatenate([uq[..., :MLA_NOPE], uq[..., MLA_NOPE::2], uq[..., MLA_NOPE + 1::2],
                          jnp.zeros((MLA_Q_RANK, MLA_HEADS, MLA_HEAD_PAD - MLA_NOPE - MLA_ROPE), F32)], axis=-1)
    ukv = mla_w_ukv[l].reshape(MLA_KV_RANK, MLA_HEADS, MLA_NOPE + MLA_V)
    uk = jnp.concatenate([ukv[..., :MLA_NOPE],
                          jnp.zeros((MLA_KV_RANK, MLA_HEADS, MLA_HEAD_PAD - MLA_NOPE), F32)], axis=-1)
    uv = ukv[..., MLA_NOPE:]
    wr = jnp.concatenate([moe_w_group[l], moe_w_expert[l],
                          zpad(D_MODEL, ROUTER_LANES - MOE_GROUPS - MOE_EXPERTS)], axis=1)
    br = jnp.concatenate([moe_b_group[l], moe_b_expert[l],
                          jnp.zeros((ROUTER_LANES - MOE_GROUPS - MOE_EXPERTS,), F32)])[None, :]
    return dict(
        wa=wa, qn=mla_q_norm[l][None, :], kvn=mla_kv_norm[l][None, :],
        wuq=uq.reshape(MLA_Q_RANK, -1).astype(BF16), wuk=uk.reshape(MLA_KV_RANK, -1).astype(BF16),
        wuv=uv.reshape(MLA_KV_RANK, -1).astype(BF16),
        wgate=wi[:, b5:].astype(BF16), wbranch=w_branch[l].astype(BF16), wout=w_out[l].astype(BF16),
        wr=wr.astype(BF16), br=br, wg=moe_w_gate[l].astype(BF16), wu=moe_w_up[l].astype(BF16),
        wd=moe_w_down[l].astype(BF16))


def _rope_tables(n):
    t = jnp.arange(n, dtype=jnp.int32)
    row = (t // GRID_W).astype(F32)
    col = (t % GRID_W).astype(F32)
    n_freq = MLA_ROPE // 4
    inv = ROPE_BASE ** (-jnp.arange(n_freq, dtype=F32) / n_freq)
    ang = jnp.concatenate([row[:, None] * inv, col[:, None] * inv], axis=-1)
    c, s = jnp.cos(ang), jnp.sin(ang)
    one = jnp.ones((n, MLA_NOPE), F32)
    zero = jnp.zeros((n, MLA_NOPE), F32)
    hr = MLA_ROPE // 2
    tail1 = jnp.ones((n, MLA_HEAD_PAD - MLA_NOPE - MLA_ROPE), F32)
    tail0 = jnp.zeros((n, MLA_HEAD_PAD - MLA_NOPE - MLA_ROPE), F32)
    cos = jnp.concatenate([one, c, c, tail1], axis=1)
    s1 = jnp.concatenate([zero, -s, jnp.zeros((n, hr), F32), tail0], axis=1)
    s2 = jnp.concatenate([zero, jnp.zeros((n, hr), F32), s, tail0], axis=1)
    return cos, s1, s2


def _identity_rope(n):
    return (jnp.ones((n, MLA_HEAD_PAD), F32), jnp.zeros((n, MLA_HEAD_PAD), F32),
            jnp.zeros((n, MLA_HEAD_PAD), F32))


def kernel(x, c, ctx, c_ctx, ada_w, ada_b, norm1_g, norm2_g, w_in, mla_q_norm, mla_w_uq, mla_kv_norm, mla_w_ukv, hy_short_w, hy_w1, hy_b1, hy_freq1, hy_w2, hy_b2, hy_freq2, hy_w3, hy_bias, na_rpb, w_branch, w_out, moe_w_group, moe_b_group, moe_w_expert, moe_b_expert, moe_w_gate, moe_w_up, moe_w_down, final_norm_g):
    b, n_lat, _ = x.shape
    n_ctx = ctx.shape[1]
    assert b <= 4 and n_lat % (NA_QROWS * GRID_W) == 0 and n_lat // GRID_W >= 2 * NA_QROWS

    cvec = jnp.concatenate([c, c_ctx[None, :], jnp.zeros((8 - b - 1, D_MODEL), F32)], axis=0)
    mod = _mod_vectors(cvec, ada_w, ada_b)

    rope_lat = _rope_tables(n_lat)
    rope_ctx = _identity_rope(n_ctx)
    tabs_lat = _dft_tables(n_lat)
    tabs_ctx = _small_dft_tables(n_ctx)
    fg = final_norm_g[None, :]

    for l in range(DEPTH):
        last = l == DEPTH - 1
        w = _layer_weights(l, w_in, mla_q_norm, mla_w_uq, mla_kv_norm, mla_w_ukv, w_branch, w_out,
                           moe_w_group, moe_b_group, moe_w_expert, moe_b_expert,
                           moe_w_gate, moe_w_up, moe_w_down)
        ml = mod[l, :b].reshape(b, 1, 6, D_MODEL)
        mc = jnp.broadcast_to(mod[l, b].reshape(1, 1, 6, D_MODEL), (b, 1, 6, D_MODEL))
        sh1, sc1, gt1, sh2, sc2, gt2 = [ml[:, :, i] for i in range(6)]
        sh1c, sc1c, gt1c, sh2c, sc2c, gt2c = [mc[:, :, i] for i in range(6)]
        g1 = norm1_g[l][None, :]
        g2 = norm2_g[l][None, :]
        fw = dict(
            w1=jnp.concatenate([hy_w1[l], jnp.zeros((HY_EMB_PAD - HY_EMB, HY_HIDDEN), F32)], axis=0).astype(BF16),
            b1=hy_b1[l][None, :], f1=hy_freq1[l][None, :], w2=hy_w2[l].astype(BF16), b2=hy_b2[l][None, :],
            f2=hy_freq2[l][None, :],
            w3=jnp.transpose(hy_w3[l].reshape(HY_HIDDEN, 2, HY_ORDER * HY_WIDTH), (1, 0, 2)).astype(BF16))

        n_keys = n_lat + n_ctx
        kv_bufs = (jnp.zeros((b, MLA_HEADS, n_keys, MLA_HEAD_PAD), BF16),
                   jnp.zeros((b, MLA_HEADS, MLA_VT_ROWS, n_keys), PV_DTYPE))
        q, kall, vtall, hy, fn, naq, nak, nav = _inproj(x, g1, sc1, sh1, w, rope_lat, kv_bufs, 0, False)
        qc, kc, vc, kall, vtall, hyc, fnc, naqc, nakc, navc = _inproj(
            ctx, g1, sc1c, sh1c, w, rope_ctx, (kall, vtall), n_lat, True)

        oa = _flash_t(q, kall, vtall, MLA_V)
        od = _na(naq, nak, nav, nakc, navc, _na_bias_tables(na_rpb[l]))
        ob = _hyena_long(hy, hy_short_w[l], fw, hy_bias[l], tabs_lat)
        oc = _fnet_long(fn, tabs_lat)
        x_new = _merge(x, g1, sc1, sh1, gt1, oa, ob, oc, od, w)

        if not last:
            oa_c = _flash(qc, kc, vc, True)
            heads = lambda t: jnp.transpose(t.reshape(b, n_ctx, NA_HEADS, NA_HEAD_DIM), (0, 2, 1, 3))
            od_c = _flash(heads(naqc), heads(nakc), heads(navc), False)
            ob_c = _hyena_short(hyc, hy_short_w[l], fw, hy_bias[l], tabs_ctx)
            oc_c = _fnet_short(fnc, tabs_ctx)
            ctx = _merge(ctx, g1, sc1c, sh1c, gt1c, oa_c, ob_c, oc_c, od_c, w)
            ctx = _moe(ctx, g2, sc2c, sh2c, gt2c, w, fg, False)
        x = _moe(x_new, g2, sc2, sh2, gt2, w, fg, last)
    return x
```

```python
import functools
import math

import numpy as np
import jax
import jax.numpy as jnp
from jax import lax
from jax.experimental import pallas as pl
from jax.experimental.pallas import tpu as pltpu

F32 = jnp.float32
BF16 = jnp.bfloat16

D_MODEL = 1024
DEPTH = 2
GRID_W = 64
NORM_EPS = 1e-6
MASK_VALUE = -1e30
BRANCH_WIDTH = D_MODEL // 4
N_BRANCH = 4

MLA_HEADS = 4
MLA_Q_RANK = D_MODEL // 4
MLA_KV_RANK = D_MODEL // 8
MLA_NOPE = 64
MLA_ROPE = 32
MLA_V = BRANCH_WIDTH // MLA_HEADS
MLA_SCALE = 1.0 / math.sqrt(MLA_NOPE + MLA_ROPE)
MLA_HEAD_PAD = 128
MLA_VT_ROWS = MLA_V + 16
PV_DTYPE = BF16
ROPE_BASE = 10000.0
LOG2E = 1.4426950408889634

HY_WIDTH = BRANCH_WIDTH
HY_ORDER = 2
HY_BANDS = 16
HY_EMB = 1 + 2 * HY_BANDS
HY_EMB_PAD = 128
HY_HIDDEN = 64
HY_DECAY_TARGET = 1e-2
HY_DECAY_SHORT_PCT = 0.3
HY_DECAY_LONG_PCT = 1.5
HY_IN = (HY_ORDER + 1) * HY_WIDTH

FN_GROUPS = 4
FN_GROUP_DIM = BRANCH_WIDTH // FN_GROUPS

NA_HEADS = 4
NA_HEAD_DIM = BRANCH_WIDTH // NA_HEADS
NA_KH = 8
NA_KW = 16
NA_SCALE = 1.0 / math.sqrt(NA_HEAD_DIM)
NA_QROWS = 8
NA_KROWS = 16

MOE_GROUPS = 4
MOE_PER_GROUP = 4
MOE_EXPERTS = MOE_GROUPS * MOE_PER_GROUP
MOE_HIDDEN = 256
ROUTER_LANES = 128

MLA_IN = MLA_Q_RANK + MLA_KV_RANK + MLA_ROPE
MIX_IN = MLA_IN + HY_IN + BRANCH_WIDTH + 3 * BRANCH_WIDTH
MIX_PAD = MLA_Q_RANK + MLA_KV_RANK + MLA_HEAD_PAD + HY_IN + BRANCH_WIDTH + 3 * BRANCH_WIDTH

DFT_N2 = 256
FN_N2 = 128
BMM_GROUPS_PER_STEP = 4
FLASH_K_TILE = 3328
FLASH_Q_TILE = 1024
FLASH_Q_SUB = 512


def _params(sem, vmem_mb=None):
    kw = dict(dimension_semantics=sem)
    if vmem_mb is not None:
        kw["vmem_limit_bytes"] = vmem_mb << 20
    return pltpu.CompilerParams(**kw)


def _pick_tile(n, cap, mult):
    best = None
    for t in range(mult, min(n, cap) + 1, mult):
        if n % t == 0:
            best = t
    return best if best is not None else n


def _rms(x):
    return x * lax.rsqrt(jnp.mean(x * x, axis=-1, keepdims=True) + NORM_EPS)


def _normmod(x, g, sc, sh):
    return (_rms(x) * g) * (1.0 + sc) + sh


def _dot(a, b):
    return jnp.dot(a, b, preferred_element_type=F32)


def _dot_nt(a, b):
    return lax.dot_general(a, b, (((1,), (1,)), ((), ())), preferred_element_type=F32)


def _mod_kernel(c_ref, w_ref, b_ref, o_ref):
    c = c_ref[...]
    s = c * jax.nn.sigmoid(c)
    o_ref[0] = _dot(s.astype(BF16), w_ref[0].astype(BF16)) + b_ref[0]


def _mod_vectors(cvec, ada_w, ada_b):
    n6 = ada_w.shape[-1]
    tn = 1024
    return pl.pallas_call(
        _mod_kernel,
        grid=(DEPTH, n6 // tn),
        in_specs=[pl.BlockSpec((8, D_MODEL), lambda l, j: (0, 0)),
                  pl.BlockSpec((1, D_MODEL, tn), lambda l, j: (l, 0, j)),
                  pl.BlockSpec((1, 1, tn), lambda l, j: (l, 0, j))],
        out_specs=pl.BlockSpec((1, 8, tn), lambda l, j: (l, 0, j)),
        out_shape=jax.ShapeDtypeStruct((DEPTH, 8, n6), F32),
        compiler_params=_params(("parallel", "parallel")),
        name="adaln_vectors",
    )(cvec, ada_w, ada_b.reshape(DEPTH, 1, n6))


def _inproj_kernel(*refs, ctx_mode):
    (x_ref, g_ref, sc_ref, sh_ref, wa_ref, qn_ref, kvn_ref, wuq_ref, wuk_ref, wuv_ref,
     cos_ref, s1_ref, s2_ref) = refs[:13]
    if ctx_mode:
        (q_ref, k_ref, v_ref, kall_ref, vtall_ref, hy_ref, fn_ref, naq_ref, nak_ref, nav_ref) = refs[15:]
    else:
        (q_ref, kall_ref, vtall_ref, hy_ref, fn_ref, naq_ref, nak_ref, nav_ref) = refs[15:]
    h = _normmod(x_ref[0], g_ref[...], sc_ref[0], sh_ref[0]).astype(BF16)
    p = _dot(h, wa_ref[...])
    c0 = MLA_Q_RANK
    c1 = c0 + MLA_KV_RANK
    c2 = c1 + MLA_HEAD_PAD
    c3 = c2 + HY_IN
    c4 = c3 + BRANCH_WIDTH
    c5 = c4 + BRANCH_WIDTH
    c6 = c5 + BRANCH_WIDTH
    cq, ckv, krp = p[:, :c0], p[:, c0:c1], p[:, c1:c2]
    hy_ref[0] = p[:, c2:c3]
    fn_ref[0] = p[:, c3:c4]
    naq_ref[0] = p[:, c4:c5].astype(BF16)
    nak_ref[0] = p[:, c5:c6].astype(BF16)
    nav_ref[0] = p[:, c6:].astype(BF16)
    cqn = (_rms(cq) * qn_ref[...]).astype(BF16)
    ckn = (_rms(ckv) * kvn_ref[...]).astype(BF16)
    q = _dot(cqn, wuq_ref[...])
    kn = _dot(ckn, wuk_ref[...])
    v = _dot(ckn, wuv_ref[...])
    cos, s1, s2 = cos_ref[...], s1_ref[...], s2_ref[...]

    def rope(t):
        return (t * cos + pltpu.roll(t, MLA_HEAD_PAD - MLA_ROPE // 2, 1) * s1
                + pltpu.roll(t, MLA_ROPE // 2, 1) * s2)

    kr = rope(krp)
    vt = v.T
    tail_rows = lax.broadcasted_iota(jnp.int32, (MLA_VT_ROWS - MLA_V, v.shape[0]), 0)
    tail = jnp.where(tail_rows == 0, 1.0, 0.0).astype(PV_DTYPE)
    for hh in range(MLA_HEADS):
        sl = slice(hh * MLA_HEAD_PAD, (hh + 1) * MLA_HEAD_PAD)
        vs = slice(hh * MLA_V, (hh + 1) * MLA_V)
        kh = (kn[:, sl] + kr).astype(BF16)
        q_ref[0, hh] = rope(q[:, sl]).astype(BF16)
        kall_ref[0, hh] = kh
        vtall_ref[0, hh, :MLA_V] = vt[vs].astype(PV_DTYPE)
        vtall_ref[0, hh, MLA_V:] = tail
        if ctx_mode:
            k_ref[0, hh] = kh
            v_ref[0, hh] = v[:, vs].astype(BF16)


def _inproj(x, g, sc, sh, w, rope_tabs, kv_bufs, key_off, ctx_mode):
    b, n, _ = x.shape
    n_keys = kv_bufs[0].shape[2]
    tm = _pick_tile(n, 512, 128)
    assert key_off % tm == 0
    ko = key_off // tm
    cos, s1, s2 = rope_tabs
    tok = lambda bb, i: (bb, i, 0)
    row = lambda bb, i: (bb, 0, 0)
    const2 = lambda bb, i: (0, 0)
    hd = lambda bb, i: (bb, 0, i, 0)
    tab = lambda bb, i: (i, 0)
    hp = MLA_HEADS * MLA_HEAD_PAD
    in_specs = [pl.BlockSpec((1, tm, D_MODEL), tok),
                pl.BlockSpec((1, D_MODEL), const2),
                pl.BlockSpec((1, 1, D_MODEL), row),
                pl.BlockSpec((1, 1, D_MODEL), row),
                pl.BlockSpec((D_MODEL, MIX_PAD), const2),
                pl.BlockSpec((1, MLA_Q_RANK), const2),
                pl.BlockSpec((1, MLA_KV_RANK), const2),
                pl.BlockSpec((MLA_Q_RANK, hp), const2),
                pl.BlockSpec((MLA_KV_RANK, hp), const2),
                pl.BlockSpec((MLA_KV_RANK, BRANCH_WIDTH), const2),
                pl.BlockSpec((tm, MLA_HEAD_PAD), tab),
                pl.BlockSpec((tm, MLA_HEAD_PAD), tab),
                pl.BlockSpec((tm, MLA_HEAD_PAD), tab)]
    args = [x, g, sc, sh, w["wa"], w["qn"], w["kvn"], w["wuq"], w["wuk"], w["wuv"], cos, s1, s2]
    q_spec = pl.BlockSpec((1, MLA_HEADS, tm, MLA_HEAD_PAD), hd)
    q_sd = jax.ShapeDtypeStruct((b, MLA_HEADS, n, MLA_HEAD_PAD), BF16)
    kall_spec = pl.BlockSpec((1, MLA_HEADS, tm, MLA_HEAD_PAD), lambda bb, i: (bb, 0, ko + i, 0))
    vtall_spec = pl.BlockSpec((1, MLA_HEADS, MLA_VT_ROWS, tm), lambda bb, i: (bb, 0, 0, ko + i))
    kall_sd = jax.ShapeDtypeStruct((b, MLA_HEADS, n_keys, MLA_HEAD_PAD), BF16)
    vtall_sd = jax.ShapeDtypeStruct((b, MLA_HEADS, MLA_VT_ROWS, n_keys), PV_DTYPE)
    rest_specs = [pl.BlockSpec((1, tm, HY_IN), tok)] + [pl.BlockSpec((1, tm, BRANCH_WIDTH), tok)] * 4
    rest_sd = [jax.ShapeDtypeStruct((b, n, HY_IN), F32), jax.ShapeDtypeStruct((b, n, BRANCH_WIDTH), F32)] \
        + [jax.ShapeDtypeStruct((b, n, BRANCH_WIDTH), BF16)] * 3
    in_specs += [pl.BlockSpec(memory_space=pl.ANY), pl.BlockSpec(memory_space=pl.ANY)]
    args += list(kv_bufs)
    if ctx_mode:
        out_specs = [q_spec, q_spec, pl.BlockSpec((1, MLA_HEADS, tm, MLA_V), hd), kall_spec, vtall_spec]
        out_shape = [q_sd, q_sd, jax.ShapeDtypeStruct((b, MLA_HEADS, n, MLA_V), BF16), kall_sd, vtall_sd]
        aliases = {13: 3, 14: 4}
    else:
        out_specs = [q_spec, kall_spec, vtall_spec]
        out_shape = [q_sd, kall_sd, vtall_sd]
        aliases = {13: 1, 14: 2}
    return pl.pallas_call(
        functools.partial(_inproj_kernel, ctx_mode=ctx_mode),
        grid=(b, n // tm),
        in_specs=in_specs,
        out_specs=out_specs + rest_specs,
        out_shape=out_shape + rest_sd,
        input_output_aliases=aliases,
        compiler_params=_params(("parallel", "parallel"), 48),
        name="input_projection",
    )(*args)


def _flash_kernel(q_ref, k_ref, v_ref, o_ref, m_ref, l_ref, acc_ref, *, nh, use_exp2):
    j = pl.program_id(2)

    @pl.when(j == 0)
    def _():
        m_ref[...] = jnp.full(m_ref.shape, MASK_VALUE, F32)
        l_ref[...] = jnp.zeros(l_ref.shape, F32)
        acc_ref[...] = jnp.zeros(acc_ref.shape, F32)

    ex = jnp.exp2 if use_exp2 else jnp.exp
    for h in range(nh):
        s = _dot_nt(q_ref[0, h], k_ref[0, h])
        m_prev = m_ref[h]
        m_cur = jnp.maximum(m_prev, jnp.max(s, axis=1, keepdims=True))
        alpha = ex(m_prev - m_cur)
        p = ex(s - m_cur)
        l_ref[h] = alpha * l_ref[h] + jnp.sum(p, axis=1, keepdims=True)
        acc_ref[h] = alpha * acc_ref[h] + _dot(p.astype(BF16), v_ref[0, h])
        m_ref[h] = m_cur

    @pl.when(j == pl.num_programs(2) - 1)
    def _():
        outs = [acc_ref[h] / l_ref[h] for h in range(nh)]
        o_ref[0] = jnp.concatenate(outs, axis=1).astype(o_ref.dtype)


def _flash(q, k, v, use_exp2):
    b, nh, lq, dk = q.shape
    lk, dv = k.shape[2], v.shape[3]
    tq = _pick_tile(lq, 512, 8)
    tk = _pick_tile(lk, 1280, 128)
    return pl.pallas_call(
        functools.partial(_flash_kernel, nh=nh, use_exp2=use_exp2),
        grid=(b, lq // tq, lk // tk),
        in_specs=[pl.BlockSpec((1, nh, tq, dk), lambda bb, i, j: (bb, 0, i, 0)),
                  pl.BlockSpec((1, nh, tk, dk), lambda bb, i, j: (bb, 0, j, 0)),
                  pl.BlockSpec((1, nh, tk, dv), lambda bb, i, j: (bb, 0, j, 0))],
        out_specs=pl.BlockSpec((1, tq, nh * dv), lambda bb, i, j: (bb, i, 0)),
        out_shape=jax.ShapeDtypeStruct((b, lq, nh * dv), BF16),
        scratch_shapes=[pltpu.VMEM((nh, tq, 1), F32), pltpu.VMEM((nh, tq, 1), F32),
                        pltpu.VMEM((nh, tq, dv), F32)],
        compiler_params=_params(("parallel", "parallel", "arbitrary"), 48),
        name="softmax_attention",
    )(q, k, v)


def _flash_t_kernel(q_ref, k_ref, vt_ref, o_ref, m_ref, acc_ref, st_ref, p_ref, *, nh, dv):
    j = pl.program_id(2)

    @pl.when(j == 0)
    def _():
        m_ref[...] = jnp.full(m_ref.shape, MASK_VALUE, F32)
        acc_ref[...] = jnp.zeros(acc_ref.shape, F32)

    rows = vt_ref.shape[2]

    tq = q_ref.shape[2]
    qs = min(tq, FLASH_Q_SUB)
    units = [(h, c) for h in range(nh) for c in range(tq // qs)]

    def scores(i):
        h, c = units[i]
        cols = slice(c * qs, (c + 1) * qs)
        st = _dot_nt(k_ref[0, h], q_ref[0, h, cols, :])
        st_ref[i % 2] = st
        m_prev = m_ref[h, :, cols]
        m_cur = jnp.maximum(m_prev, jnp.max(st, axis=0, keepdims=True))
        m_ref[h, :, cols] = m_cur
        return m_prev, m_cur

    def probs(i, m_cur):
        p_ref[i % 2] = jnp.exp2(st_ref[i % 2] - m_cur).astype(PV_DTYPE)

    def values(i, m_prev, m_cur):
        h, c = units[i]
        cols = slice(c * qs, (c + 1) * qs)
        alpha = jnp.exp2(m_prev - m_cur)
        acc_ref[h, :rows, cols] = alpha * acc_ref[h, :rows, cols] + _dot(vt_ref[0, h], p_ref[i % 2])

    n_units = len(units)
    stats = {0: scores(0)}
    if n_units > 1:
        stats[1] = scores(1)
    probs(0, stats[0][1])
    for i in range(n_units):
        values(i, *stats.pop(i))
        if i + 2 < n_units:
            stats[i + 2] = scores(i + 2)
        if i + 1 < n_units:
            probs(i + 1, stats[i + 1][1])

    @pl.when(j == pl.num_programs(2) - 1)
    def _():
        outs = []
        for h in range(nh):
            a = acc_ref[h]
            o = (a / a[dv:dv + 1]).T
            outs.append(o[:, :dv])
        o_ref[0] = jnp.concatenate(outs, axis=1).astype(o_ref.dtype)


def _flash_t(q, k, vt, dv):
    b, nh, lq, dk = q.shape
    lk, rows = k.shape[2], vt.shape[2]
    tq = _pick_tile(lq, FLASH_Q_TILE, 128)
    tk = _pick_tile(lk, FLASH_K_TILE, 128)
    return pl.pallas_call(
        functools.partial(_flash_t_kernel, nh=nh, dv=dv),
        grid=(b, lq // tq, lk // tk),
        in_specs=[pl.BlockSpec((1, nh, tq, dk), lambda bb, i, j: (bb, 0, i, 0)),
                  pl.BlockSpec((1, nh, tk, dk), lambda bb, i, j: (bb, 0, j, 0)),
                  pl.BlockSpec((1, nh, rows, tk), lambda bb, i, j: (bb, 0, 0, j))],
        out_specs=pl.BlockSpec((1, tq, nh * dv), lambda bb, i, j: (bb, i, 0)),
        out_shape=jax.ShapeDtypeStruct((b, lq, nh * dv), BF16),
        scratch_shapes=[pltpu.VMEM((nh, 1, tq), F32), pltpu.VMEM((nh, 128, tq), F32),
                        pltpu.VMEM((2, tk, min(tq, FLASH_Q_SUB)), F32),
                        pltpu.VMEM((2, tk, min(tq, FLASH_Q_SUB)), PV_DTYPE)],
        compiler_params=_params(("parallel", "parallel", "arbitrary"), 56),
        name="softmax_attention_t",
    )(q, k, vt)


def _na_kernel(q_ref, k0, k1, k2, k3, v0, v1, v2, v3, kc_ref, vc_ref, b_ref, o_ref):
    q = q_ref[0]
    kcat = jnp.concatenate([k0[0], k1[0], k2[0], k3[0]], axis=0)
    vcat = jnp.concatenate([v0[0], v1[0], v2[0], v3[0]], axis=0)
    kc, vc = kc_ref[0], vc_ref[0]
    outs = []
    for h in range(NA_HEADS):
        sl = slice(h * NA_HEAD_DIM, (h + 1) * NA_HEAD_DIM)
        qh = q[:, sl]
        s = _dot_nt(qh, kcat[:, sl]) + b_ref[0, h]
        sc = _dot_nt(qh, kc[:, sl])
        m = jnp.maximum(jnp.max(s, axis=1, keepdims=True), jnp.max(sc, axis=1, keepdims=True))
        p = jnp.exp(s - m)
        pc = jnp.exp(sc - m)
        l = jnp.sum(p, axis=1, keepdims=True) + jnp.sum(pc, axis=1, keepdims=True)
        o = _dot(p.astype(BF16), vcat[:, sl]) + _dot(pc.astype(BF16), vc[:, sl])
        outs.append(o / l)
    o_ref[0] = jnp.concatenate(outs, axis=1).astype(o_ref.dtype)


def _na_bias_tables(rpb):
    j = np.arange(NA_QROWS)[:, None]
    m = np.arange(NA_KROWS)[None, :]
    valid = np.zeros((3, NA_QROWS, NA_KROWS), bool)
    st0 = np.maximum(j - NA_KH // 2, 0)
    valid[0] = (m - 4 >= st0) & (m - 4 < st0 + NA_KH)
    valid[1] = (m - j >= 0) & (m - j < NA_KH)
    st2 = 4 + np.minimum(j - NA_KH // 2, 0)
    valid[2] = (m < 12) & (m >= st2) & (m < st2 + NA_KH)
    dr_idx = np.clip(m - 4 - j + NA_KH - 1, 0, 2 * NA_KH - 2)
    qc = np.arange(GRID_W)[:, None]
    kc = np.arange(GRID_W)[None, :]
    dc_idx = np.clip(kc - qc, -(NA_KW - 1), NA_KW - 1) + NA_KW - 1
    cs = np.clip(qc - NA_KW // 2, 0, GRID_W - NA_KW)
    col_ok = (kc >= cs) & (kc < cs + NA_KW)
    t = rpb.astype(F32)[:, dr_idx]
    t = t[..., dc_idx]
    t = jnp.transpose(t, (0, 1, 3, 2, 4))
    ok = valid[:, None, :, None, :, None] & col_ok[None, None, None, :, None, :]
    out = jnp.where(jnp.asarray(ok), t[None], MASK_VALUE)
    return out.reshape(3, NA_HEADS, NA_QROWS * GRID_W, NA_KROWS * GRID_W)


def _na(q, k, v, kc, vc, bias):
    b, n, w = q.shape
    c = kc.shape[1]
    tq = NA_QROWS * GRID_W
    tkb = 4 * GRID_W
    nb = n // tq
    nkb = n // tkb
    kmaps = [lambda bb, i: (bb, jnp.maximum(2 * i - 1, 0), 0),
             lambda bb, i: (bb, 2 * i, 0),
             lambda bb, i: (bb, 2 * i + 1, 0),
             lambda bb, i: (bb, jnp.minimum(2 * i + 2, nkb - 1), 0)]
    kspecs = [pl.BlockSpec((1, tkb, w), mp) for mp in kmaps]
    return pl.pallas_call(
        _na_kernel,
        grid=(b, nb),
        in_specs=[pl.BlockSpec((1, tq, w), lambda bb, i: (bb, i, 0))] + kspecs + kspecs
                 + [pl.BlockSpec((1, c, w), lambda bb, i: (bb, 0, 0)),
                    pl.BlockSpec((1, c, w), lambda bb, i: (bb, 0, 0)),
                    pl.BlockSpec((1, NA_HEADS, tq, NA_KROWS * GRID_W),
                                 lambda bb, i: (jnp.where(i == 0, 0, jnp.where(i == nb - 1, 2, 1)), 0, 0, 0))],
        out_specs=pl.BlockSpec((1, tq, w), lambda bb, i: (bb, i, 0)),
        out_shape=jax.ShapeDtypeStruct((b, n, w), BF16),
        compiler_params=_params(("parallel", "arbitrary"), 56),
        name="neighbourhood_attention",
    )(q, k, k, k, k, v, v, v, v, kc, vc, bias)


def _sconv_kernel(u_ref, pv_ref, nx_ref, w_ref, v_ref, x1_ref, x2_ref, *, nt):
    i = pl.program_id(1)
    u = u_ref[0]
    tl = u.shape[0]
    prev = jnp.where(i > 0, pv_ref[0, 7:8, :], 0.0)
    nxt = jnp.where(i < nt - 1, nx_ref[0, 0:1, :], 0.0)
    rows = lax.broadcasted_iota(jnp.int32, u.shape, 0)
    um = jnp.where(rows == 0, prev, pltpu.roll(u, 1, 0))
    up = jnp.where(rows == tl - 1, nxt, pltpu.roll(u, tl - 1, 0))
    w = w_ref[...]
    y = um * w[0:1] + u * w[1:2] + up * w[2:3]
    v_ref[...] = y[:, :HY_WIDTH]
    x1_ref[...] = y[:, HY_WIDTH:2 * HY_WIDTH]
    x2_ref[...] = y[:, 2 * HY_WIDTH:]


def _sconv(u, w, seq_major):
    b, n, _ = u.shape
    tl = _pick_tile(n, 512, 8)
    nt = n // tl
    r8 = tl // 8
    if seq_major:
        osd = jax.ShapeDtypeStruct((n, b * HY_WIDTH), F32)
        ospec = pl.BlockSpec((tl, HY_WIDTH), lambda bb, i: (i, bb))
    else:
        osd = jax.ShapeDtypeStruct((b * n, HY_WIDTH), F32)
        ospec = pl.BlockSpec((tl, HY_WIDTH), lambda bb, i: (bb * nt + i, 0))
    return pl.pallas_call(
        functools.partial(_sconv_kernel, nt=nt),
        grid=(b, nt),
        in_specs=[pl.BlockSpec((1, tl, HY_IN), lambda bb, i: (bb, i, 0)),
                  pl.BlockSpec((1, 8, HY_IN), lambda bb, i: (bb, jnp.maximum(i * r8 - 1, 0), 0)),
                  pl.BlockSpec((1, 8, HY_IN), lambda bb, i: (bb, jnp.minimum((i + 1) * r8, n // 8 - 1), 0)),
                  pl.BlockSpec((3, HY_IN), lambda bb, i: (0, 0))],
        out_specs=[ospec, ospec, ospec],
        out_shape=[osd, osd, osd],
        compiler_params=_params(("parallel", "parallel")),
        name="hyena_short_conv",
    )(u, u, u, w)


def _filt_kernel(z_ref, w1_ref, b1_ref, f1_ref, w2_ref, b2_ref, f2_ref, w3_ref, win_ref, g_ref, ss_ref):
    d = pl.program_id(0)
    i = pl.program_id(1)
    h = jnp.sin(f1_ref[...] * (_dot(z_ref[0].astype(BF16), w1_ref[...]) + b1_ref[...]))
    h = jnp.sin(f2_ref[...] * (_dot(h.astype(BF16), w2_ref[...]) + b2_ref[...]))
    h = _dot(h.astype(BF16), w3_ref[0])
    h = h * jnp.concatenate([win_ref[0]] * HY_ORDER, axis=1)
    rows = lax.broadcasted_iota(jnp.int32, h.shape, 0)
    h = jnp.where((rows == 0) & (i == 0) & (d == 1), 0.0, h)
    g_ref[0] = h
    ss = jnp.sum(h * h, axis=0, keepdims=True)
    first = (i == 0) & (d == 0)

    @pl.when(first)
    def _():
        ss_ref[...] = ss

    @pl.when(jnp.logical_not(first))
    def _():
        ss_ref[...] += ss


def _padded_filter(n, fw):
    r = jnp.arange(n, dtype=jnp.int32)
    idx = jnp.stack([r, jnp.mod(n - r, n)], axis=0).astype(F32)[:, :, None]
    t = idx / (n - 1)
    bands = jnp.linspace(1e-4, HY_BANDS - 1, HY_BANDS, dtype=F32)
    ang = (2.0 * math.pi / n) * idx * bands
    z = jnp.concatenate([t, jnp.cos(ang), -jnp.sin(ang), jnp.zeros((2, n, HY_EMB_PAD - HY_EMB), F32)], axis=-1)
    deltas = jnp.linspace(math.log(HY_DECAY_TARGET) / HY_DECAY_LONG_PCT,
                          math.log(HY_DECAY_TARGET) / HY_DECAY_SHORT_PCT, HY_WIDTH, dtype=F32)
    window = jnp.exp(-t * jnp.abs(deltas))
    tl = _pick_tile(n, 512, 8)
    half = HY_ORDER * HY_WIDTH
    c2 = lambda d, i: (0, 0)
    g, ss = pl.pallas_call(
        _filt_kernel,
        grid=(2, n // tl),
        in_specs=[pl.BlockSpec((1, tl, HY_EMB_PAD), lambda d, i: (d, i, 0)),
                  pl.BlockSpec((HY_EMB_PAD, HY_HIDDEN), c2), pl.BlockSpec((1, HY_HIDDEN), c2),
                  pl.BlockSpec((1, HY_HIDDEN), c2),
                  pl.BlockSpec((HY_HIDDEN, HY_HIDDEN), c2), pl.BlockSpec((1, HY_HIDDEN), c2),
                  pl.BlockSpec((1, HY_HIDDEN), c2),
                  pl.BlockSpec((1, HY_HIDDEN, half), lambda d, i: (d, 0, 0)),
                  pl.BlockSpec((1, tl, HY_WIDTH), lambda d, i: (d, i, 0))],
        out_specs=[pl.BlockSpec((1, tl, half), lambda d, i: (d, i, 0)), pl.BlockSpec((1, half), c2)],
        out_shape=[jax.ShapeDtypeStruct((2, n, half), F32), jax.ShapeDtypeStruct((1, half), F32)],
        compiler_params=_params(("arbitrary", "arbitrary")),
        name="hyena_filter_mlp",
    )(z, fw["w1"], fw["b1"], fw["f1"], fw["w2"], fw["b2"], fw["f2"], fw["w3"], window)
    return g.reshape(2 * n, half), lax.rsqrt(ss + NORM_EPS)


def _colmm_kernel(*refs, epi):
    w_ref, x_ref = refs[0], refs[1]
    o_ref = refs[-1]
    acc = _dot(w_ref[...], x_ref[...].astype(BF16))
    if epi == "gate":
        xg_ref, z_ref, b_ref = refs[2:5]
        acc = xg_ref[...] * (acc + z_ref[...] * b_ref[...])
    elif epi == "scale":
        acc = acc * refs[2][...]
    elif epi == "cmul":
        g = refs[2][...]
        mh = acc.shape[0] // 2
        ar, ai, gr, gi = acc[:mh], acc[mh:], g[:mh], g[mh:]
        acc = jnp.concatenate([ar * gr - ai * gi, ar * gi + ai * gr], axis=0)
    o_ref[...] = acc.astype(o_ref.dtype)


def _colmm(w, x, out_dtype, epi="none", extra=(), tn_cap=4096, g_col=None):
    m, k = w.shape
    nc = x.shape[1]
    tn = _pick_tile(nc, tn_cap, 128)
    col = lambda j: (0, j)
    specs = [pl.BlockSpec((m, k), lambda j: (0, 0)), pl.BlockSpec((k, tn), col)]
    if epi == "gate":
        specs += [pl.BlockSpec((m, tn), col), pl.BlockSpec((m, tn), col), pl.BlockSpec((1, tn), col)]
    elif epi == "scale":
        specs += [pl.BlockSpec((1, tn), col)]
    elif epi == "cmul":
        specs += [pl.BlockSpec((m, tn), lambda j: (0, g_col))]
    return pl.pallas_call(
        functools.partial(_colmm_kernel, epi=epi),
        grid=(nc // tn,),
        in_specs=specs,
        out_specs=pl.BlockSpec((m, tn), col),
        out_shape=jax.ShapeDtypeStruct((m, nc), out_dtype),
        compiler_params=_params(("parallel",), 48),
        name="dft_stage_" + epi,
    )(w, x, *extra)


def _bmm_kernel(*refs, epi, cplx_out, gs):
    m_ref, x_ref = refs[0], refs[1]
    o_ref = refs[-1]
    for s in range(gs):
        x = x_ref[:, s]
        x = x.reshape(x.shape[0] * x.shape[1], x.shape[2])
        acc = _dot(m_ref[s], x)
        ph = acc.shape[0] // 2
        if epi in ("cmul", "conv"):
            g_ref = refs[2]
            gr, gi = g_ref[0, s], g_ref[1, s]
            cg = gr.shape[1]
            res_r, res_i = [], []
            for t in range(acc.shape[1] // cg):
                ar, ai = acc[:ph, t * cg:(t + 1) * cg], acc[ph:, t * cg:(t + 1) * cg]
                res_r.append(ar * gr - ai * gi)
                res_i.append(ar * gi + ai * gr)
            yr = jnp.concatenate(res_r, axis=1)
            yi = jnp.concatenate(res_i, axis=1)
            if epi == "conv":
                y = jnp.concatenate([yr, yi], axis=0).astype(BF16)
                back = _dot(refs[3][s], y)
                yr, yi = back[:ph], back[ph:]
            o_ref[0, s] = yr.astype(o_ref.dtype)
            o_ref[1, s] = yi.astype(o_ref.dtype)
        elif epi == "scale":
            sc = refs[2][...]
            o_ref[0, s] = (acc[:ph] * sc).astype(o_ref.dtype)
            o_ref[1, s] = (acc[ph:] * sc).astype(o_ref.dtype)
        elif cplx_out:
            o_ref[0, s] = acc[:ph].astype(o_ref.dtype)
            o_ref[1, s] = acc[ph:].astype(o_ref.dtype)
        else:
            c = acc.shape[1]
            o_ref[:, s * c:(s + 1) * c] = acc.astype(o_ref.dtype)


def _bmm(mats, x, out_dtype, epi="none", extra=None, g_col=0, cplx_out=True):
    ng, p, qq = mats.shape
    c = x.shape[3]
    gs = _pick_tile(ng, BMM_GROUPS_PER_STEP, 1)
    specs = [pl.BlockSpec((gs, p, qq), lambda g: (g, 0, 0)),
             pl.BlockSpec((2, gs, qq // 2, c), lambda g: (0, g, 0, 0))]
    args = [mats, x]
    if epi in ("cmul", "conv"):
        cg = HY_WIDTH
        gspec = extra[0] if epi == "conv" else extra
        specs.append(pl.BlockSpec((2, gs, p // 2, cg), lambda g: (0, g, 0, g_col)))
        args.append(gspec)
        if epi == "conv":
            specs.append(pl.BlockSpec((gs, qq, p), lambda g: (g, 0, 0)))
            args.append(extra[1])
    elif epi == "scale":
        specs.append(pl.BlockSpec((1, c), lambda g: (0, 0)))
        args.append(extra)
    if cplx_out:
        ospec = pl.BlockSpec((2, gs, p // 2, c), lambda g: (0, g, 0, 0))
        oshape = jax.ShapeDtypeStruct((2, ng, p // 2, c), out_dtype)
    else:
        ospec = pl.BlockSpec((p, gs * c), lambda g: (0, g))
        oshape = jax.ShapeDtypeStruct((p, ng * c), out_dtype)
    return pl.pallas_call(
        functools.partial(_bmm_kernel, epi=epi, cplx_out=cplx_out, gs=gs),
        grid=(ng // gs,),
        in_specs=specs,
        out_specs=ospec,
        out_shape=oshape,
        compiler_params=_params(("parallel",), 48),
        name="dft_batched_" + epi,
    )(*args)


def _angle(num, den):
    return (2.0 * math.pi / den) * jnp.mod(num, den).astype(F32)


def _dft_tables(n_lat):
    nn = 2 * n_lat
    n2 = DFT_N2
    n1 = nn // n2
    i1 = jnp.arange(n1, dtype=jnp.int32)
    a1 = _angle(i1[:, None] * i1[None, :], n1)
    c1, s1 = jnp.cos(a1), jnp.sin(a1)
    t = {}
    t["hy_f1"] = jnp.concatenate([c1, -s1], axis=0).astype(BF16)
    ch, sh = c1[:, :n1 // 2], s1[:, :n1 // 2]
    t["hy_f1c"] = jnp.concatenate([jnp.concatenate([ch, sh], axis=1),
                                   jnp.concatenate([-sh, ch], axis=1)], axis=0).astype(BF16)
    t["hy_i2c"] = jnp.concatenate([jnp.concatenate([ch.T, -sh.T], axis=1),
                                   jnp.concatenate([sh.T, ch.T], axis=1)], axis=0).astype(BF16)
    i2 = jnp.arange(n2, dtype=jnp.int32)
    kk = i1[:, None, None] + n1 * i2[None, :, None]
    a2 = _angle(kk * i2[None, None, :], nn)
    c2, s2 = jnp.cos(a2), jnp.sin(a2)
    fwd = jnp.concatenate([jnp.concatenate([c2, s2], axis=2), jnp.concatenate([-s2, c2], axis=2)], axis=1)
    t["hy_fwd"] = fwd.astype(BF16)
    t["hy_inv"] = jnp.swapaxes(fwd, 1, 2).astype(BF16)
    m2 = FN_N2
    m1 = n_lat // m2
    j1 = jnp.arange(m1, dtype=jnp.int32)
    b1 = _angle(j1[:, None] * j1[None, :], m1)
    cb, sb = jnp.cos(b1), jnp.sin(b1)
    t["fn_s1"] = jnp.concatenate([jnp.concatenate([cb, sb], axis=1),
                                  jnp.concatenate([-sb, cb], axis=1)], axis=0).astype(BF16)
    j2 = jnp.arange(m2, dtype=jnp.int32)
    kf = j1[:, None, None] + m1 * j2[None, :, None]
    b2 = _angle(kf * j2[None, None, :], n_lat)
    norm = 1.0 / math.sqrt(n_lat * FN_GROUP_DIM)
    t["fn_s3"] = (norm * jnp.concatenate([jnp.cos(b2), jnp.sin(b2)], axis=2)).astype(BF16)
    return t


def _small_dft_tables(n):
    nn = 2 * n
    k = jnp.arange(nn, dtype=jnp.int32)
    a = _angle(k[:, None] * k[None, :], nn)
    c, s = jnp.cos(a), jnp.sin(a)
    t = {}
    t["f_full"] = jnp.concatenate([c, -s], axis=0).astype(BF16)
    t["f_half"] = jnp.concatenate([c[:, :n], -s[:, :n]], axis=0).astype(BF16)
    t["i_half"] = jnp.concatenate([c[:n], -s[:n]], axis=1).astype(BF16)
    j = jnp.arange(n, dtype=jnp.int32)
    b = _angle(j[:, None] * j[None, :], n)
    norm = 1.0 / math.sqrt(n * FN_GROUP_DIM)
    t["fn"] = (norm * jnp.concatenate([jnp.cos(b), jnp.sin(b)], axis=1)).astype(BF16)
    return t


def _hyena_long(hy, short_w, fw, bias, tabs):
    b, n, _ = hy.shape
    assert b == 2
    cw = HY_WIDTH
    n2 = DFT_N2
    n1 = 2 * n // n2
    v, x1, x2 = _sconv(hy, short_w, False)
    g, gscale = _padded_filter(n, fw)
    half = HY_ORDER * HY_WIDTH
    ga = _colmm(tabs["hy_f1"], g.reshape(n1, n2 * half), BF16)
    gspec = _bmm(tabs["hy_fwd"], ga.reshape(2, n1, n2, half), F32, epi="scale",
                 extra=gscale * (1.0 / (2 * n)))
    z = v.reshape(n1, n2 * cw)
    for o, xg in enumerate((x1, x2)):
        a = _colmm(tabs["hy_f1c"], z, BF16)
        bm = _bmm(tabs["hy_fwd"], a.reshape(2, n1, n2, cw), BF16, epi="conv",
                  extra=(gspec, tabs["hy_inv"]), g_col=o)
        brow = jnp.tile(bias[o], n2)[None, :]
        z = _colmm(tabs["hy_i2c"], bm.reshape(2 * n1, n2 * cw), F32, epi="gate",
                   extra=(xg.reshape(n1, n2 * cw), z, brow))
    return z.reshape(b, n, cw)


def _hyena_short(hy, short_w, fw, bias, tabs):
    b, n, _ = hy.shape
    cw = b * HY_WIDTH
    v, x1, x2 = _sconv(hy, short_w, True)
    g, gscale = _padded_filter(n, fw)
    gspec = _colmm(tabs["f_full"], g, F32, epi="scale", extra=(gscale * (1.0 / (2 * n)),), tn_cap=256)
    z = v
    for o, xg in enumerate((x1, x2)):
        y = _colmm(tabs["f_half"], z, BF16, epi="cmul", extra=(gspec,), tn_cap=HY_WIDTH, g_col=o)
        brow = jnp.tile(bias[o], b)[None, :]
        z = _colmm(tabs["i_half"], y, F32, epi="gate", extra=(xg, z, brow), tn_cap=256)
    return z


def _fn0_kernel(x_ref, w_ref, o_ref):
    acc = _dot(x_ref[0].astype(BF16), w_ref[...])
    o_ref[0] = acc[:, :BRANCH_WIDTH].astype(o_ref.dtype)
    o_ref[1] = acc[:, BRANCH_WIDTH:].astype(o_ref.dtype)


def _fnet_channel_dft(u):
    b, n, w = u.shape
    j = np.arange(FN_GROUP_DIM)
    ang = 2.0 * np.pi * ((j[:, None] * j[None, :]) % FN_GROUP_DIM) / FN_GROUP_DIM
    eye = np.eye(FN_GROUPS)
    wc = np.concatenate([np.kron(eye, np.cos(ang)), -np.kron(eye, np.sin(ang))], axis=1)
    wc = jnp.asarray(wc, F32).astype(BF16)
    tm = _pick_tile(n, 1024, 8)
    return pl.pallas_call(
        _fn0_kernel,
        grid=(b, n // tm),
        in_specs=[pl.BlockSpec((1, tm, w), lambda bb, i: (bb, i, 0)),
                  pl.BlockSpec((w, 2 * w), lambda bb, i: (0, 0))],
        out_specs=pl.BlockSpec((2, tm, w), lambda bb, i: (0, i, bb)),
        out_shape=jax.ShapeDtypeStruct((2, n, b * w), BF16),
        compiler_params=_params(("parallel", "parallel")),
        name="fnet_channel_dft",
    )(u, wc)


def _fnet_long(u, tabs):
    b, n, w = u.shape
    cw = b * w
    m2 = FN_N2
    m1 = n // m2
    wri = _fnet_channel_dft(u)
    a = _colmm(tabs["fn_s1"], wri.reshape(2 * m1, m2 * cw), BF16)
    y = _bmm(tabs["fn_s3"], a.reshape(2, m1, m2, cw), BF16, cplx_out=False)
    return y.reshape(n, cw)


def _fnet_short(u, tabs):
    b, n, w = u.shape
    wri = _fnet_channel_dft(u)
    return _colmm(tabs["fn"], wri.reshape(2 * n, b * w), BF16, tn_cap=256)


def _merge_kernel(x_ref, g_ref, sc_ref, sh_ref, gt_ref, oa_ref, ob_ref, oc_ref, od_ref,
                  wg_ref, wb_ref, wo_ref, o_ref):
    x = x_ref[0]
    hb = _normmod(x, g_ref[...], sc_ref[0], sh_ref[0]).astype(BF16)
    branches = (oa_ref[0], ob_ref[0], oc_ref[...], od_ref[0])
    acc = jnp.zeros(x.shape, F32)
    for nbr, o in enumerate(branches):
        gate = _dot(hb, wg_ref[:, nbr * D_MODEL:(nbr + 1) * D_MODEL])
        proj = _dot(o.astype(BF16), wb_ref[nbr])
        acc = acc + jax.nn.sigmoid(gate) * proj
    m = _dot(acc.astype(BF16), wo_ref[...])
    o_ref[0] = x + gt_ref[0] * m


def _merge(x, g, sc, sh, gt, oa, ob, oc, od, w):
    b, n, _ = x.shape
    tm = _pick_tile(n, 512, 8)
    tok = lambda bb, i: (bb, i, 0)
    row = lambda bb, i: (bb, 0, 0)
    seq = lambda bb, i: (i, bb)
    c2 = lambda bb, i: (0, 0)
    return pl.pallas_call(
        _merge_kernel,
        grid=(b, n // tm),
        in_specs=[pl.BlockSpec((1, tm, D_MODEL), tok),
                  pl.BlockSpec((1, D_MODEL), c2),
                  pl.BlockSpec((1, 1, D_MODEL), row), pl.BlockSpec((1, 1, D_MODEL), row),
                  pl.BlockSpec((1, 1, D_MODEL), row),
                  pl.BlockSpec((1, tm, BRANCH_WIDTH), tok),
                  pl.BlockSpec((1, tm, BRANCH_WIDTH), tok),
                  pl.BlockSpec((tm, BRANCH_WIDTH), seq),
                  pl.BlockSpec((1, tm, BRANCH_WIDTH), tok),
                  pl.BlockSpec((D_MODEL, N_BRANCH * D_MODEL), c2),
                  pl.BlockSpec((N_BRANCH, BRANCH_WIDTH, D_MODEL), lambda bb, i: (0, 0, 0)),
                  pl.BlockSpec((D_MODEL, D_MODEL), c2)],
        out_specs=pl.BlockSpec((1, tm, D_MODEL), tok),
        out_shape=jax.ShapeDtypeStruct(x.shape, F32),
        compiler_params=_params(("parallel", "parallel"), 56),
        name="branch_merge",
    )(x, g, sc, sh, gt, oa, ob, oc, od, w["wgate"], w["wbranch"], w["wout"])


def _moe_kernel(x_ref, g_ref, sc_ref, sh_ref, gt_ref, wr_ref, br_ref, wg_ref, wu_ref, wd_ref, fg_ref,
                o_ref, h_s, wt_s, acc_s, *, final):
    e = pl.program_id(2)

    @pl.when(e == 0)
    def _():
        hb = _normmod(x_ref[0], g_ref[...], sc_ref[0], sh_ref[0]).astype(BF16)
        h_s[...] = hb
        r = _dot(hb, wr_ref[...]) + br_ref[...]
        lane = lax.broadcasted_iota(jnp.int32, r.shape, 1)
        neg = jnp.float32(-jnp.inf)
        isg = lane < MOE_GROUPS
        gmax = jnp.max(jnp.where(isg, r, neg), axis=1, keepdims=True)
        gsum = jnp.sum(jnp.where(isg, jnp.exp(r - gmax), 0.0), axis=1, keepdims=True)
        g_p = 1.0 / gsum
        gidx = jnp.min(jnp.where(isg & (r == gmax), lane, ROUTER_LANES), axis=1, keepdims=True)
        ise = ((lane >= MOE_GROUPS) & (lane < MOE_GROUPS + MOE_EXPERTS)
               & (jnp.right_shift(lane - MOE_GROUPS, 2) == gidx))
        el = jnp.where(ise, r, neg)
        e1 = jnp.max(el, axis=1, keepdims=True)
        i1 = jnp.min(jnp.where(ise & (r == e1), lane, ROUTER_LANES), axis=1, keepdims=True)
        el2 = jnp.where(lane == i1, neg, el)
        e2 = jnp.max(el2, axis=1, keepdims=True)
        i2 = jnp.min(jnp.where(el2 == e2, lane, ROUTER_LANES), axis=1, keepdims=True)
        d = jnp.exp(e2 - e1)
        w1 = 1.0 / (1.0 + d)
        w2 = d / (1.0 + d)
        wt_s[...] = jnp.where(lane == i1, g_p * w1, jnp.where(lane == i2, g_p * w2, 0.0))
        acc_s[...] = jnp.zeros(acc_s.shape, F32)

    hb = h_s[...]
    wt = wt_s[...]
    lane = lax.broadcasted_iota(jnp.int32, wt.shape, 1)
    acts = []
    for k in range(MOE_PER_GROUP):
        a = _dot(hb, wg_ref[k])
        u = _dot(hb, wu_ref[k])
        col = MOE_GROUPS + e * MOE_PER_GROUP + k
        wcol = jnp.sum(jnp.where(lane == col, wt, 0.0), axis=1, keepdims=True)
        acts.append(((a * jax.nn.sigmoid(a)) * u * wcol).astype(BF16))
    act = jnp.concatenate(acts, axis=1)
    wd = wd_ref[...].reshape(MOE_PER_GROUP * MOE_HIDDEN, D_MODEL)
    acc_s[...] += _dot(act, wd)

    @pl.when(e == MOE_GROUPS - 1)
    def _():
        y = x_ref[0] + gt_ref[0] * acc_s[...]
        if final:
            y = _rms(y) * fg_ref[...]
        o_ref[0] = y


def _moe(x, g, sc, sh, gt, w, fg, final):
    b, n, _ = x.shape
    tm = _pick_tile(n, 1024, 8)
    tok = lambda bb, i, e: (bb, i, 0)
    row = lambda bb, i, e: (bb, 0, 0)
    c2 = lambda bb, i, e: (0, 0)
    ex = lambda bb, i, e: (e, 0, 0)
    return pl.pallas_call(
        functools.partial(_moe_kernel, final=final),
        grid=(b, n // tm, MOE_GROUPS),
        in_specs=[pl.BlockSpec((1, tm, D_MODEL), tok),
                  pl.BlockSpec((1, D_MODEL), c2),
                  pl.BlockSpec((1, 1, D_MODEL), row), pl.BlockSpec((1, 1, D_MODEL), row),
                  pl.BlockSpec((1, 1, D_MODEL), row),
                  pl.BlockSpec((D_MODEL, ROUTER_LANES), c2), pl.BlockSpec((1, ROUTER_LANES), c2),
                  pl.BlockSpec((MOE_PER_GROUP, D_MODEL, MOE_HIDDEN), ex),
                  pl.BlockSpec((MOE_PER_GROUP, D_MODEL, MOE_HIDDEN), ex),
                  pl.BlockSpec((MOE_PER_GROUP, MOE_HIDDEN, D_MODEL), ex),
                  pl.BlockSpec((1, D_MODEL), c2)],
        out_specs=pl.BlockSpec((1, tm, D_MODEL), tok),
        out_shape=jax.ShapeDtypeStruct(x.shape, F32),
        scratch_shapes=[pltpu.VMEM((tm, D_MODEL), BF16), pltpu.VMEM((tm, ROUTER_LANES), F32),
                        pltpu.VMEM((tm, D_MODEL), F32)],
        compiler_params=_params(("parallel", "parallel", "arbitrary"), 56),
        name="hier_moe",
    )(x, g, sc, sh, gt, w["wr"], w["br"], w["wg"], w["wu"], w["wd"], fg)


def _layer_weights(l, w_in, mla_q_norm, mla_w_uq, mla_kv_norm, mla_w_ukv, w_branch, w_out,
                   moe_w_group, moe_b_group, moe_w_expert, moe_b_expert, moe_w_gate, moe_w_up, moe_w_down):
    wi = w_in[l]
    b0 = MLA_Q_RANK
    b1 = b0 + MLA_KV_RANK
    b2 = b1 + MLA_ROPE
    b3 = b2 + HY_IN
    b4 = b3 + BRANCH_WIDTH
    b5 = b4 + 3 * BRANCH_WIDTH
    zpad = lambda r, c: jnp.zeros((r, c), F32)
    w_kr = wi[:, b1:b2]
    krp = jnp.concatenate([zpad(D_MODEL, MLA_NOPE), w_kr[:, 0::2], w_kr[:, 1::2],
                           zpad(D_MODEL, MLA_HEAD_PAD - MLA_NOPE - MLA_ROPE)], axis=1)
    w_na = wi[:, b4:b5]
    w_na = jnp.concatenate([w_na[:, :BRANCH_WIDTH] * NA_SCALE, w_na[:, BRANCH_WIDTH:]], axis=1)
    wa = jnp.concatenate([wi[:, :b1], krp, wi[:, b2:b4], w_na], axis=1).astype(BF16)
    uq = mla_w_uq[l].reshape(MLA_Q_RANK, MLA_HEADS, MLA_NOPE + MLA_ROPE) * (MLA_SCALE * LOG2E)
    uq = jnp.concatenate([uq[..., :MLA_NOPE], uq[..., MLA_NOPE::2], uq[..., MLA_NOPE + 1::2],
                          jnp.zeros((MLA_Q_RANK, MLA_HEADS, MLA_HEAD_PAD - MLA_NOPE - MLA_ROPE), F32)], axis=-1)
    ukv = mla_w_ukv[l].reshape(MLA_KV_RANK, MLA_HEADS, MLA_NOPE + MLA_V)
    uk = jnp.concatenate([ukv[..., :MLA_NOPE],
                          jnp.zeros((MLA_KV_RANK, MLA_HEADS, MLA_HEAD_PAD - MLA_NOPE), F32)], axis=-1)
    uv = ukv[..., MLA_NOPE:]
    wr = jnp.concatenate([moe_w_group[l], moe_w_expert[l],
                          zpad(D_MODEL, ROUTER_LANES - MOE_GROUPS - MOE_EXPERTS)], axis=1)
    br = jnp.concatenate([moe_b_group[l], moe_b_expert[l],
                          jnp.zeros((ROUTER_LANES - MOE_GROUPS - MOE_EXPERTS,), F32)])[None, :]
    return dict(
        wa=wa, qn=mla_q_norm[l][None, :], kvn=mla_kv_norm[l][None, :],
        wuq=uq.reshape(MLA_Q_RANK, -1).astype(BF16), wuk=uk.reshape(MLA_KV_RANK, -1).astype(BF16),
        wuv=uv.reshape(MLA_KV_RANK, -1).astype(BF16),
        wgate=wi[:, b5:].astype(BF16), wbranch=w_branch[l].astype(BF16), wout=w_out[l].astype(BF16),
        wr=wr.astype(BF16), br=br, wg=moe_w_gate[l].astype(BF16), wu=moe_w_up[l].astype(BF16),
        wd=moe_w_down[l].astype(BF16))


def _rope_tables(n):
    t = jnp.arange(n, dtype=jnp.int32)
    row = (t // GRID_W).astype(F32)
    col = (t % GRID_W).astype(F32)
    n_freq = MLA_ROPE // 4
    inv = ROPE_BASE ** (-jnp.arange(n_freq, dtype=F32) / n_freq)
    ang = jnp.concatenate([row[:, None] * inv, col[:, None] * inv], axis=-1)
    c, s = jnp.cos(ang), jnp.sin(ang)
    one = jnp.ones((n, MLA_NOPE), F32)
    zero = jnp.zeros((n, MLA_NOPE), F32)
    hr = MLA_ROPE // 2
    tail1 = jnp.ones((n, MLA_HEAD_PAD - MLA_NOPE - MLA_ROPE), F32)
    tail0 = jnp.zeros((n, MLA_HEAD_PAD - MLA_NOPE - MLA_ROPE), F32)
    cos = jnp.concatenate([one, c, c, tail1], axis=1)
    s1 = jnp.concatenate([zero, -s, jnp.zeros((n, hr), F32), tail0], axis=1)
    s2 = jnp.concatenate([zero, jnp.zeros((n, hr), F32), s, tail0], axis=1)
    return cos, s1, s2


def _identity_rope(n):
    return (jnp.ones((n, MLA_HEAD_PAD), F32), jnp.zeros((n, MLA_HEAD_PAD), F32),
            jnp.zeros((n, MLA_HEAD_PAD), F32))


def kernel(x, c, ctx, c_ctx, ada_w, ada_b, norm1_g, norm2_g, w_in, mla_q_norm, mla_w_uq, mla_kv_norm, mla_w_ukv, hy_short_w, hy_w1, hy_b1, hy_freq1, hy_w2, hy_b2, hy_freq2, hy_w3, hy_bias, na_rpb, w_branch, w_out, moe_w_group, moe_b_group, moe_w_expert, moe_b_expert, moe_w_gate, moe_w_up, moe_w_down, final_norm_g):
    b, n_lat, _ = x.shape
    n_ctx = ctx.shape[1]
    assert b <= 4 and n_lat % (NA_QROWS * GRID_W) == 0 and n_lat // GRID_W >= 2 * NA_QROWS

    cvec = jnp.concatenate([c, c_ctx[None, :], jnp.zeros((8 - b - 1, D_MODEL), F32)], axis=0)
    mod = _mod_vectors(cvec, ada_w, ada_b)

    rope_lat = _rope_tables(n_lat)
    rope_ctx = _identity_rope(n_ctx)
    tabs_lat = _dft_tables(n_lat)
    tabs_ctx = _small_dft_tables(n_ctx)
    fg = final_norm_g[None, :]

    for l in range(DEPTH):
        last = l == DEPTH - 1
        w = _layer_weights(l, w_in, mla_q_norm, mla_w_uq, mla_kv_norm, mla_w_ukv, w_branch, w_out,
                           moe_w_group, moe_b_group, moe_w_expert, moe_b_expert,
                           moe_w_gate, moe_w_up, moe_w_down)
        ml = mod[l, :b].reshape(b, 1, 6, D_MODEL)
        mc = jnp.broadcast_to(mod[l, b].reshape(1, 1, 6, D_MODEL), (b, 1, 6, D_MODEL))
        sh1, sc1, gt1, sh2, sc2, gt2 = [ml[:, :, i] for i in range(6)]
        sh1c, sc1c, gt1c, sh2c, sc2c, gt2c = [mc[:, :, i] for i in range(6)]
        g1 = norm1_g[l][None, :]
        g2 = norm2_g[l][None, :]
        fw = dict(
            w1=jnp.concatenate([hy_w1[l], jnp.zeros((HY_EMB_PAD - HY_EMB, HY_HIDDEN), F32)], axis=0).astype(BF16),
            b1=hy_b1[l][None, :], f1=hy_freq1[l][None, :], w2=hy_w2[l].astype(BF16), b2=hy_b2[l][None, :],
            f2=hy_freq2[l][None, :],
            w3=jnp.transpose(hy_w3[l].reshape(HY_HIDDEN, 2, HY_ORDER * HY_WIDTH), (1, 0, 2)).astype(BF16))

        n_keys = n_lat + n_ctx
        kv_bufs = (jnp.zeros((b, MLA_HEADS, n_keys, MLA_HEAD_PAD), BF16),
                   jnp.zeros((b, MLA_HEADS, MLA_VT_ROWS, n_keys), PV_DTYPE))
        q, kall, vtall, hy, fn, naq, nak, nav = _inproj(x, g1, sc1, sh1, w, rope_lat, kv_bufs, 0, False)
        qc, kc, vc, kall, vtall, hyc, fnc, naqc, nakc, navc = _inproj(
            ctx, g1, sc1c, sh1c, w, rope_ctx, (kall, vtall), n_lat, True)

        oa = _flash_t(q, kall, vtall, MLA_V)
        od = _na(naq, nak, nav, nakc, navc, _na_bias_tables(na_rpb[l]))
        ob = _hyena_long(hy, hy_short_w[l], fw, hy_bias[l], tabs_lat)
        oc = _fnet_long(fn, tabs_lat)
        x_new = _merge(x, g1, sc1, sh1, gt1, oa, ob, oc, od, w)

        if not last:
            oa_c = _flash(qc, kc, vc, True)
            heads = lambda t: jnp.transpose(t.reshape(b, n_ctx, NA_HEADS, NA_HEAD_DIM), (0, 2, 1, 3))
            od_c = _flash(heads(naqc), heads(nakc), heads(navc), False)
            ob_c = _hyena_short(hyc, hy_short_w[l], fw, hy_bias[l], tabs_ctx)
            ob_c = jnp.transpose(ob_c.reshape(n_ctx, b, HY_WIDTH), (1, 0, 2))
            oc_c = _fnet_short(fnc, tabs_ctx)
            ctx = _merge(ctx, g1, sc1c, sh1c, gt1c, oa_c, ob_c, oc_c, od_c, w)
            ctx = _moe(ctx, g2, sc2c, sh2c, gt2c, w, fg, False)
        x = _moe(x_new, g2, sc2, sh2, gt2, w, fg, last)
    return x
```

```python
import functools
import math

import numpy as np
import jax
import jax.numpy as jnp
from jax import lax
from jax.experimental import pallas as pl
from jax.experimental.pallas import tpu as pltpu

F32 = jnp.float32
BF16 = jnp.bfloat16

D_MODEL = 1024
DEPTH = 2
GRID_W = 64
NORM_EPS = 1e-6
MASK_VALUE = -1e30
BRANCH_WIDTH = D_MODEL // 4
N_BRANCH = 4

MLA_HEADS = 4
MLA_Q_RANK = D_MODEL // 4
MLA_KV_RANK = D_MODEL // 8
MLA_NOPE = 64
MLA_ROPE = 32
MLA_V = BRANCH_WIDTH // MLA_HEADS
MLA_SCALE = 1.0 / math.sqrt(MLA_NOPE + MLA_ROPE)
MLA_HEAD_PAD = 128
MLA_VT_ROWS = MLA_V + 16
PV_DTYPE = BF16
ROPE_BASE = 10000.0
LOG2E = 1.4426950408889634

HY_WIDTH = BRANCH_WIDTH
HY_ORDER = 2
HY_BANDS = 16
HY_EMB = 1 + 2 * HY_BANDS
HY_EMB_PAD = 128
HY_HIDDEN = 64
HY_DECAY_TARGET = 1e-2
HY_DECAY_SHORT_PCT = 0.3
HY_DECAY_LONG_PCT = 1.5
HY_IN = (HY_ORDER + 1) * HY_WIDTH

FN_GROUPS = 4
FN_GROUP_DIM = BRANCH_WIDTH // FN_GROUPS

NA_HEADS = 4
NA_HEAD_DIM = BRANCH_WIDTH // NA_HEADS
NA_KH = 8
NA_KW = 16
NA_SCALE = 1.0 / math.sqrt(NA_HEAD_DIM)
NA_QROWS = 8
NA_KROWS = 16

MOE_GROUPS = 4
MOE_PER_GROUP = 4
MOE_EXPERTS = MOE_GROUPS * MOE_PER_GROUP
MOE_HIDDEN = 256
ROUTER_LANES = 128

MLA_IN = MLA_Q_RANK + MLA_KV_RANK + MLA_ROPE
MIX_IN = MLA_IN + HY_IN + BRANCH_WIDTH + 3 * BRANCH_WIDTH
MIX_PAD = MLA_Q_RANK + MLA_KV_RANK + MLA_HEAD_PAD + HY_IN + BRANCH_WIDTH + 3 * BRANCH_WIDTH

DFT_N2 = 256
FN_N2 = 128
BMM_GROUPS_PER_STEP = 4
FLASH_K_TILE = 3328
FLASH_Q_TILE = 1024
FLASH_Q_SUB = 512


def _params(sem, vmem_mb=None):
    kw = dict(dimension_semantics=sem)
    if vmem_mb is not None:
        kw["vmem_limit_bytes"] = vmem_mb << 20
    return pltpu.CompilerParams(**kw)


def _pick_tile(n, cap, mult):
    best = None
    for t in range(mult, min(n, cap) + 1, mult):
        if n % t == 0:
            best = t
    return best if best is not None else n


def _rms(x):
    return x * lax.rsqrt(jnp.mean(x * x, axis=-1, keepdims=True) + NORM_EPS)


def _normmod(x, g, sc, sh):
    return (_rms(x) * g) * (1.0 + sc) + sh


def _dot(a, b):
    return jnp.dot(a, b, preferred_element_type=F32)


def _dot_nt(a, b):
    return lax.dot_general(a, b, (((1,), (1,)), ((), ())), preferred_element_type=F32)


def _mod_kernel(c_ref, w_ref, b_ref, o_ref):
    c = c_ref[...]
    s = c * jax.nn.sigmoid(c)
    o_ref[0] = _dot(s.astype(BF16), w_ref[0].astype(BF16)) + b_ref[0]


def _mod_vectors(cvec, ada_w, ada_b):
    n6 = ada_w.shape[-1]
    tn = 1024
    return pl.pallas_call(
        _mod_kernel,
        grid=(DEPTH, n6 // tn),
        in_specs=[pl.BlockSpec((8, D_MODEL), lambda l, j: (0, 0)),
                  pl.BlockSpec((1, D_MODEL, tn), lambda l, j: (l, 0, j)),
                  pl.BlockSpec((1, 1, tn), lambda l, j: (l, 0, j))],
        out_specs=pl.BlockSpec((1, 8, tn), lambda l, j: (l, 0, j)),
        out_shape=jax.ShapeDtypeStruct((DEPTH, 8, n6), F32),
        compiler_params=_params(("parallel", "parallel")),
        name="adaln_vectors",
    )(cvec, ada_w, ada_b.reshape(DEPTH, 1, n6))


def _inproj_kernel(*refs, ctx_mode):
    (x_ref, g_ref, sc_ref, sh_ref, wa_ref, qn_ref, kvn_ref, wuq_ref, wuk_ref, wuv_ref,
     cos_ref, s1_ref, s2_ref) = refs[:13]
    if ctx_mode:
        (q_ref, k_ref, v_ref, kall_ref, vtall_ref, hy_ref, fn_ref, naq_ref, nak_ref, nav_ref) = refs[15:]
    else:
        (q_ref, kall_ref, vtall_ref, hy_ref, fn_ref, naq_ref, nak_ref, nav_ref) = refs[15:]
    h = _normmod(x_ref[0], g_ref[...], sc_ref[0], sh_ref[0]).astype(BF16)
    p = _dot(h, wa_ref[...])
    c0 = MLA_Q_RANK
    c1 = c0 + MLA_KV_RANK
    c2 = c1 + MLA_HEAD_PAD
    c3 = c2 + HY_IN
    c4 = c3 + BRANCH_WIDTH
    c5 = c4 + BRANCH_WIDTH
    c6 = c5 + BRANCH_WIDTH
    cq, ckv, krp = p[:, :c0], p[:, c0:c1], p[:, c1:c2]
    hy_ref[0] = p[:, c2:c3]
    fn_ref[0] = p[:, c3:c4]
    naq_ref[0] = p[:, c4:c5].astype(BF16)
    nak_ref[0] = p[:, c5:c6].astype(BF16)
    nav_ref[0] = p[:, c6:].astype(BF16)
    cqn = (_rms(cq) * qn_ref[...]).astype(BF16)
    ckn = (_rms(ckv) * kvn_ref[...]).astype(BF16)
    q = _dot(cqn, wuq_ref[...])
    kn = _dot(ckn, wuk_ref[...])
    v = _dot(ckn, wuv_ref[...])
    cos, s1, s2 = cos_ref[...], s1_ref[...], s2_ref[...]

    def rope(t):
        return (t * cos + pltpu.roll(t, MLA_HEAD_PAD - MLA_ROPE // 2, 1) * s1
                + pltpu.roll(t, MLA_ROPE // 2, 1) * s2)

    kr = rope(krp)
    vt = v.T
    tail_rows = lax.broadcasted_iota(jnp.int32, (MLA_VT_ROWS - MLA_V, v.shape[0]), 0)
    tail = jnp.where(tail_rows == 0, 1.0, 0.0).astype(PV_DTYPE)
    for hh in range(MLA_HEADS):
        sl = slice(hh * MLA_HEAD_PAD, (hh + 1) * MLA_HEAD_PAD)
        vs = slice(hh * MLA_V, (hh + 1) * MLA_V)
        kh = (kn[:, sl] + kr).astype(BF16)
        q_ref[0, hh] = rope(q[:, sl]).astype(BF16)
        kall_ref[0, hh] = kh
        vtall_ref[0, hh, :MLA_V] = vt[vs].astype(PV_DTYPE)
        vtall_ref[0, hh, MLA_V:] = tail
        if ctx_mode:
            k_ref[0, hh] = kh
            v_ref[0, hh] = v[:, vs].astype(BF16)


def _inproj(x, g, sc, sh, w, rope_tabs, kv_bufs, key_off, ctx_mode):
    b, n, _ = x.shape
    n_keys = kv_bufs[0].shape[2]
    tm = _pick_tile(n, 512, 128)
    assert key_off % tm == 0
    ko = key_off // tm
    cos, s1, s2 = rope_tabs
    tok = lambda bb, i: (bb, i, 0)
    row = lambda bb, i: (bb, 0, 0)
    const2 = lambda bb, i: (0, 0)
    hd = lambda bb, i: (bb, 0, i, 0)
    tab = lambda bb, i: (i, 0)
    hp = MLA_HEADS * MLA_HEAD_PAD
    in_specs = [pl.BlockSpec((1, tm, D_MODEL), tok),
                pl.BlockSpec((1, D_MODEL), const2),
                pl.BlockSpec((1, 1, D_MODEL), row),
                pl.BlockSpec((1, 1, D_MODEL), row),
                pl.BlockSpec((D_MODEL, MIX_PAD), const2),
                pl.BlockSpec((1, MLA_Q_RANK), const2),
                pl.BlockSpec((1, MLA_KV_RANK), const2),
                pl.BlockSpec((MLA_Q_RANK, hp), const2),
                pl.BlockSpec((MLA_KV_RANK, hp), const2),
                pl.BlockSpec((MLA_KV_RANK, BRANCH_WIDTH), const2),
                pl.BlockSpec((tm, MLA_HEAD_PAD), tab),
                pl.BlockSpec((tm, MLA_HEAD_PAD), tab),
                pl.BlockSpec((tm, MLA_HEAD_PAD), tab)]
    args = [x, g, sc, sh, w["wa"], w["qn"], w["kvn"], w["wuq"], w["wuk"], w["wuv"], cos, s1, s2]
    q_spec = pl.BlockSpec((1, MLA_HEADS, tm, MLA_HEAD_PAD), hd)
    q_sd = jax.ShapeDtypeStruct((b, MLA_HEADS, n, MLA_HEAD_PAD), BF16)
    kall_spec = pl.BlockSpec((1, MLA_HEADS, tm, MLA_HEAD_PAD), lambda bb, i: (bb, 0, ko + i, 0))
    vtall_spec = pl.BlockSpec((1, MLA_HEADS, MLA_VT_ROWS, tm), lambda bb, i: (bb, 0, 0, ko + i))
    kall_sd = jax.ShapeDtypeStruct((b, MLA_HEADS, n_keys, MLA_HEAD_PAD), BF16)
    vtall_sd = jax.ShapeDtypeStruct((b, MLA_HEADS, MLA_VT_ROWS, n_keys), PV_DTYPE)
    rest_specs = [pl.BlockSpec((1, tm, HY_IN), tok)] + [pl.BlockSpec((1, tm, BRANCH_WIDTH), tok)] * 4
    rest_sd = [jax.ShapeDtypeStruct((b, n, HY_IN), F32), jax.ShapeDtypeStruct((b, n, BRANCH_WIDTH), F32)] \
        + [jax.ShapeDtypeStruct((b, n, BRANCH_WIDTH), BF16)] * 3
    in_specs += [pl.BlockSpec(memory_space=pl.ANY), pl.BlockSpec(memory_space=pl.ANY)]
    args += list(kv_bufs)
    if ctx_mode:
        out_specs = [q_spec, q_spec, pl.BlockSpec((1, MLA_HEADS, tm, MLA_V), hd), kall_spec, vtall_spec]
        out_shape = [q_sd, q_sd, jax.ShapeDtypeStruct((b, MLA_HEADS, n, MLA_V), BF16), kall_sd, vtall_sd]
        aliases = {13: 3, 14: 4}
    else:
        out_specs = [q_spec, kall_spec, vtall_spec]
        out_shape = [q_sd, kall_sd, vtall_sd]
        aliases = {13: 1, 14: 2}
    return pl.pallas_call(
        functools.partial(_inproj_kernel, ctx_mode=ctx_mode),
        grid=(b, n // tm),
        in_specs=in_specs,
        out_specs=out_specs + rest_specs,
        out_shape=out_shape + rest_sd,
        input_output_aliases=aliases,
        compiler_params=_params(("parallel", "parallel"), 48),
        name="input_projection",
    )(*args)


def _flash_kernel(q_ref, k_ref, v_ref, o_ref, m_ref, l_ref, acc_ref, *, nh, use_exp2):
    j = pl.program_id(2)

    @pl.when(j == 0)
    def _():
        m_ref[...] = jnp.full(m_ref.shape, MASK_VALUE, F32)
        l_ref[...] = jnp.zeros(l_ref.shape, F32)
        acc_ref[...] = jnp.zeros(acc_ref.shape, F32)

    ex = jnp.exp2 if use_exp2 else jnp.exp
    for h in range(nh):
        s = _dot_nt(q_ref[0, h], k_ref[0, h])
        m_prev = m_ref[h]
        m_cur = jnp.maximum(m_prev, jnp.max(s, axis=1, keepdims=True))
        alpha = ex(m_prev - m_cur)
        p = ex(s - m_cur)
        l_ref[h] = alpha * l_ref[h] + jnp.sum(p, axis=1, keepdims=True)
        acc_ref[h] = alpha * acc_ref[h] + _dot(p.astype(BF16), v_ref[0, h])
        m_ref[h] = m_cur

    @pl.when(j == pl.num_programs(2) - 1)
    def _():
        outs = [acc_ref[h] / l_ref[h] for h in range(nh)]
        o_ref[0] = jnp.concatenate(outs, axis=1).astype(o_ref.dtype)


def _flash(q, k, v, use_exp2):
    b, nh, lq, dk = q.shape
    lk, dv = k.shape[2], v.shape[3]
    tq = _pick_tile(lq, 512, 8)
    tk = _pick_tile(lk, 1280, 128)
    return pl.pallas_call(
        functools.partial(_flash_kernel, nh=nh, use_exp2=use_exp2),
        grid=(b, lq // tq, lk // tk),
        in_specs=[pl.BlockSpec((1, nh, tq, dk), lambda bb, i, j: (bb, 0, i, 0)),
                  pl.BlockSpec((1, nh, tk, dk), lambda bb, i, j: (bb, 0, j, 0)),
                  pl.BlockSpec((1, nh, tk, dv), lambda bb, i, j: (bb, 0, j, 0))],
        out_specs=pl.BlockSpec((1, tq, nh * dv), lambda bb, i, j: (bb, i, 0)),
        out_shape=jax.ShapeDtypeStruct((b, lq, nh * dv), BF16),
        scratch_shapes=[pltpu.VMEM((nh, tq, 1), F32), pltpu.VMEM((nh, tq, 1), F32),
                        pltpu.VMEM((nh, tq, dv), F32)],
        compiler_params=_params(("parallel", "parallel", "arbitrary"), 48),
        name="softmax_attention",
    )(q, k, v)


def _flash_t_kernel(q_ref, k_ref, vt_ref, o_ref, m_ref, acc_ref, st_ref, p_ref, *, nh, dv):
    j = pl.program_id(2)

    @pl.when(j == 0)
    def _():
        m_ref[...] = jnp.full(m_ref.shape, MASK_VALUE, F32)
        acc_ref[...] = jnp.zeros(acc_ref.shape, F32)

    rows = vt_ref.shape[2]

    tq = q_ref.shape[2]
    qs = min(tq, FLASH_Q_SUB)
    units = [(h, c) for h in range(nh) for c in range(tq // qs)]

    def scores(i):
        h, c = units[i]
        cols = slice(c * qs, (c + 1) * qs)
        st = _dot_nt(k_ref[0, h], q_ref[0, h, cols, :])
        st_ref[i % 2] = st
        m_prev = m_ref[h, :, cols]
        m_cur = jnp.maximum(m_prev, jnp.max(st, axis=0, keepdims=True))
        m_ref[h, :, cols] = m_cur
        return m_prev, m_cur

    def probs(i, m_cur):
        p_ref[i % 2] = jnp.exp2(st_ref[i % 2] - m_cur).astype(PV_DTYPE)

    def values(i, m_prev, m_cur):
        h, c = units[i]
        cols = slice(c * qs, (c + 1) * qs)
        alpha = jnp.exp2(m_prev - m_cur)
        acc_ref[h, :rows, cols] = alpha * acc_ref[h, :rows, cols] + _dot(vt_ref[0, h], p_ref[i % 2])

    n_units = len(units)
    stats = {0: scores(0)}
    if n_units > 1:
        stats[1] = scores(1)
    probs(0, stats[0][1])
    for i in range(n_units):
        values(i, *stats.pop(i))
        if i + 2 < n_units:
            stats[i + 2] = scores(i + 2)
        if i + 1 < n_units:
            probs(i + 1, stats[i + 1][1])

    @pl.when(j == pl.num_programs(2) - 1)
    def _():
        outs = []
        for h in range(nh):
            a = acc_ref[h]
            o = (a / a[dv:dv + 1]).T
            outs.append(o[:, :dv])
        o_ref[0] = jnp.concatenate(outs, axis=1).astype(o_ref.dtype)


def _flash_t(q, k, vt, dv):
    b, nh, lq, dk = q.shape
    lk, rows = k.shape[2], vt.shape[2]
    tq = _pick_tile(lq, FLASH_Q_TILE, 128)
    tk = _pick_tile(lk, FLASH_K_TILE, 128)
    return pl.pallas_call(
        functools.partial(_flash_t_kernel, nh=nh, dv=dv),
        grid=(b, lq // tq, lk // tk),
        in_specs=[pl.BlockSpec((1, nh, tq, dk), lambda bb, i, j: (bb, 0, i, 0)),
                  pl.BlockSpec((1, nh, tk, dk), lambda bb, i, j: (bb, 0, j, 0)),
                  pl.BlockSpec((1, nh, rows, tk), lambda bb, i, j: (bb, 0, 0, j))],
        out_specs=pl.BlockSpec((1, tq, nh * dv), lambda bb, i, j: (bb, i, 0)),
        out_shape=jax.ShapeDtypeStruct((b, lq, nh * dv), BF16),
        scratch_shapes=[pltpu.VMEM((nh, 1, tq), F32), pltpu.VMEM((nh, 128, tq), F32),
                        pltpu.VMEM((2, tk, min(tq, FLASH_Q_SUB)), F32),
                        pltpu.VMEM((2, tk, min(tq, FLASH_Q_SUB)), PV_DTYPE)],
        compiler_params=_params(("parallel", "parallel", "arbitrary"), 56),
        name="softmax_attention_t",
    )(q, k, vt)


def _na_kernel(q_ref, k0, k1, k2, k3, v0, v1, v2, v3, kc_ref, vc_ref, b_ref, o_ref):
    q = q_ref[0]
    kcat = jnp.concatenate([k0[0], k1[0], k2[0], k3[0]], axis=0)
    vcat = jnp.concatenate([v0[0], v1[0], v2[0], v3[0]], axis=0)
    kc, vc = kc_ref[0], vc_ref[0]
    outs = []
    for h in range(NA_HEADS):
        sl = slice(h * NA_HEAD_DIM, (h + 1) * NA_HEAD_DIM)
        qh = q[:, sl]
        s = _dot_nt(qh, kcat[:, sl]) + b_ref[0, h]
        sc = _dot_nt(qh, kc[:, sl])
        m = jnp.maximum(jnp.max(s, axis=1, keepdims=True), jnp.max(sc, axis=1, keepdims=True))
        p = jnp.exp(s - m)
        pc = jnp.exp(sc - m)
        l = jnp.sum(p, axis=1, keepdims=True) + jnp.sum(pc, axis=1, keepdims=True)
        o = _dot(p.astype(BF16), vcat[:, sl]) + _dot(pc.astype(BF16), vc[:, sl])
        outs.append(o / l)
    o_ref[0] = jnp.concatenate(outs, axis=1).astype(o_ref.dtype)


def _na_bias_tables(rpb):
    j = np.arange(NA_QROWS)[:, None]
    m = np.arange(NA_KROWS)[None, :]
    valid = np.zeros((3, NA_QROWS, NA_KROWS), bool)
    st0 = np.maximum(j - NA_KH // 2, 0)
    valid[0] = (m - 4 >= st0) & (m - 4 < st0 + NA_KH)
    valid[1] = (m - j >= 0) & (m - j < NA_KH)
    st2 = 4 + np.minimum(j - NA_KH // 2, 0)
    valid[2] = (m < 12) & (m >= st2) & (m < st2 + NA_KH)
    dr_idx = np.clip(m - 4 - j + NA_KH - 1, 0, 2 * NA_KH - 2)
    qc = np.arange(GRID_W)[:, None]
    kc = np.arange(GRID_W)[None, :]
    dc_idx = np.clip(kc - qc, -(NA_KW - 1), NA_KW - 1) + NA_KW - 1
    cs = np.clip(qc - NA_KW // 2, 0, GRID_W - NA_KW)
    col_ok = (kc >= cs) & (kc < cs + NA_KW)
    t = rpb.astype(F32)[:, dr_idx]
    t = t[..., dc_idx]
    t = jnp.transpose(t, (0, 1, 3, 2, 4))
    ok = valid[:, None, :, None, :, None] & col_ok[None, None, None, :, None, :]
    out = jnp.where(jnp.asarray(ok), t[None], MASK_VALUE)
    return out.reshape(3, NA_HEADS, NA_QROWS * GRID_W, NA_KROWS * GRID_W)


def _na(q, k, v, kc, vc, bias):
    b, n, w = q.shape
    c = kc.shape[1]
    tq = NA_QROWS * GRID_W
    tkb = 4 * GRID_W
    nb = n // tq
    nkb = n // tkb
    kmaps = [lambda bb, i: (bb, jnp.maximum(2 * i - 1, 0), 0),
             lambda bb, i: (bb, 2 * i, 0),
             lambda bb, i: (bb, 2 * i + 1, 0),
             lambda bb, i: (bb, jnp.minimum(2 * i + 2, nkb - 1), 0)]
    kspecs = [pl.BlockSpec((1, tkb, w), mp) for mp in kmaps]
    return pl.pallas_call(
        _na_kernel,
        grid=(b, nb),
        in_specs=[pl.BlockSpec((1, tq, w), lambda bb, i: (bb, i, 0))] + kspecs + kspecs
                 + [pl.BlockSpec((1, c, w), lambda bb, i: (bb, 0, 0)),
                    pl.BlockSpec((1, c, w), lambda bb, i: (bb, 0, 0)),
                    pl.BlockSpec((1, NA_HEADS, tq, NA_KROWS * GRID_W),
                                 lambda bb, i: (jnp.where(i == 0, 0, jnp.where(i == nb - 1, 2, 1)), 0, 0, 0))],
        out_specs=pl.BlockSpec((1, tq, w), lambda bb, i: (bb, i, 0)),
        out_shape=jax.ShapeDtypeStruct((b, n, w), BF16),
        compiler_params=_params(("parallel", "arbitrary"), 56),
        name="neighbourhood_attention",
    )(q, k, k, k, k, v, v, v, v, kc, vc, bias)


def _sconv_kernel(u_ref, pv_ref, nx_ref, w_ref, v_ref, x1_ref, x2_ref, *, nt):
    i = pl.program_id(1)
    u = u_ref[0]
    tl = u.shape[0]
    prev = jnp.where(i > 0, pv_ref[0, 7:8, :], 0.0)
    nxt = jnp.where(i < nt - 1, nx_ref[0, 0:1, :], 0.0)
    rows = lax.broadcasted_iota(jnp.int32, u.shape, 0)
    um = jnp.where(rows == 0, prev, pltpu.roll(u, 1, 0))
    up = jnp.where(rows == tl - 1, nxt, pltpu.roll(u, tl - 1, 0))
    w = w_ref[...]
    y = um * w[0:1] + u * w[1:2] + up * w[2:3]
    v_ref[...] = y[:, :HY_WIDTH]
    x1_ref[...] = y[:, HY_WIDTH:2 * HY_WIDTH]
    x2_ref[...] = y[:, 2 * HY_WIDTH:]


def _sconv(u, w, seq_major):
    b, n, _ = u.shape
    tl = _pick_tile(n, 512, 8)
    nt = n // tl
    r8 = tl // 8
    if seq_major:
        osd = jax.ShapeDtypeStruct((n, b * HY_WIDTH), F32)
        ospec = pl.BlockSpec((tl, HY_WIDTH), lambda bb, i: (i, bb))
    else:
        osd = jax.ShapeDtypeStruct((b * n, HY_WIDTH), F32)
        ospec = pl.BlockSpec((tl, HY_WIDTH), lambda bb, i: (bb * nt + i, 0))
    return pl.pallas_call(
        functools.partial(_sconv_kernel, nt=nt),
        grid=(b, nt),
        in_specs=[pl.BlockSpec((1, tl, HY_IN), lambda bb, i: (bb, i, 0)),
                  pl.BlockSpec((1, 8, HY_IN), lambda bb, i: (bb, jnp.maximum(i * r8 - 1, 0), 0)),
                  pl.BlockSpec((1, 8, HY_IN), lambda bb, i: (bb, jnp.minimum((i + 1) * r8, n // 8 - 1), 0)),
                  pl.BlockSpec((3, HY_IN), lambda bb, i: (0, 0))],
        out_specs=[ospec, ospec, ospec],
        out_shape=[osd, osd, osd],
        compiler_params=_params(("parallel", "parallel")),
        name="hyena_short_conv",
    )(u, u, u, w)


def _filt_kernel(z_ref, w1_ref, b1_ref, f1_ref, w2_ref, b2_ref, f2_ref, w3_ref, win_ref, g_ref, ss_ref):
    d = pl.program_id(0)
    i = pl.program_id(1)
    h = jnp.sin(f1_ref[...] * (_dot(z_ref[0].astype(BF16), w1_ref[...]) + b1_ref[...]))
    h = jnp.sin(f2_ref[...] * (_dot(h.astype(BF16), w2_ref[...]) + b2_ref[...]))
    h = _dot(h.astype(BF16), w3_ref[0])
    h = h * jnp.concatenate([win_ref[0]] * HY_ORDER, axis=1)
    rows = lax.broadcasted_iota(jnp.int32, h.shape, 0)
    h = jnp.where((rows == 0) & (i == 0) & (d == 1), 0.0, h)
    g_ref[0] = h
    ss = jnp.sum(h * h, axis=0, keepdims=True)
    first = (i == 0) & (d == 0)

    @pl.when(first)
    def _():
        ss_ref[...] = ss

    @pl.when(jnp.logical_not(first))
    def _():
        ss_ref[...] += ss


def _padded_filter(n, fw):
    r = jnp.arange(n, dtype=jnp.int32)
    idx = jnp.stack([r, jnp.mod(n - r, n)], axis=0).astype(F32)[:, :, None]
    t = idx / (n - 1)
    bands = jnp.linspace(1e-4, HY_BANDS - 1, HY_BANDS, dtype=F32)
    ang = (2.0 * math.pi / n) * idx * bands
    z = jnp.concatenate([t, jnp.cos(ang), -jnp.sin(ang), jnp.zeros((2, n, HY_EMB_PAD - HY_EMB), F32)], axis=-1)
    deltas = jnp.linspace(math.log(HY_DECAY_TARGET) / HY_DECAY_LONG_PCT,
                          math.log(HY_DECAY_TARGET) / HY_DECAY_SHORT_PCT, HY_WIDTH, dtype=F32)
    window = jnp.exp(-t * jnp.abs(deltas))
    tl = _pick_tile(n, 512, 8)
    half = HY_ORDER * HY_WIDTH
    c2 = lambda d, i: (0, 0)
    g, ss = pl.pallas_call(
        _filt_kernel,
        grid=(2, n // tl),
        in_specs=[pl.BlockSpec((1, tl, HY_EMB_PAD), lambda d, i: (d, i, 0)),
                  pl.BlockSpec((HY_EMB_PAD, HY_HIDDEN), c2), pl.BlockSpec((1, HY_HIDDEN), c2),
                  pl.BlockSpec((1, HY_HIDDEN), c2),
                  pl.BlockSpec((HY_HIDDEN, HY_HIDDEN), c2), pl.BlockSpec((1, HY_HIDDEN), c2),
                  pl.BlockSpec((1, HY_HIDDEN), c2),
                  pl.BlockSpec((1, HY_HIDDEN, half), lambda d, i: (d, 0, 0)),
                  pl.BlockSpec((1, tl, HY_WIDTH), lambda d, i: (d, i, 0))],
        out_specs=[pl.BlockSpec((1, tl, half), lambda d, i: (d, i, 0)), pl.BlockSpec((1, half), c2)],
        out_shape=[jax.ShapeDtypeStruct((2, n, half), F32), jax.ShapeDtypeStruct((1, half), F32)],
        compiler_params=_params(("arbitrary", "arbitrary")),
        name="hyena_filter_mlp",
    )(z, fw["w1"], fw["b1"], fw["f1"], fw["w2"], fw["b2"], fw["f2"], fw["w3"], window)
    return g.reshape(2 * n, half), lax.rsqrt(ss + NORM_EPS)


def _colmm_kernel(*refs, epi):
    w_ref, x_ref = refs[0], refs[1]
    o_ref = refs[-1]
    acc = _dot(w_ref[...], x_ref[...].astype(BF16))
    if epi == "gate":
        xg_ref, z_ref, b_ref = refs[2:5]
        acc = xg_ref[...] * (acc + z_ref[...] * b_ref[...])
    elif epi == "scale":
        acc = acc * refs[2][...]
    elif epi == "cmul":
        g = refs[2][...]
        mh = acc.shape[0] // 2
        ar, ai, gr, gi = acc[:mh], acc[mh:], g[:mh], g[mh:]
        acc = jnp.concatenate([ar * gr - ai * gi, ar * gi + ai * gr], axis=0)
    o_ref[...] = acc.astype(o_ref.dtype)


def _colmm(w, x, out_dtype, epi="none", extra=(), tn_cap=4096, g_col=None):
    m, k = w.shape
    nc = x.shape[1]
    tn = _pick_tile(nc, tn_cap, 128)
    col = lambda j: (0, j)
    specs = [pl.BlockSpec((m, k), lambda j: (0, 0)), pl.BlockSpec((k, tn), col)]
    if epi == "gate":
        specs += [pl.BlockSpec((m, tn), col), pl.BlockSpec((m, tn), col), pl.BlockSpec((1, tn), col)]
    elif epi == "scale":
        specs += [pl.BlockSpec((1, tn), col)]
    elif epi == "cmul":
        specs += [pl.BlockSpec((m, tn), lambda j: (0, g_col))]
    return pl.pallas_call(
        functools.partial(_colmm_kernel, epi=epi),
        grid=(nc // tn,),
        in_specs=specs,
        out_specs=pl.BlockSpec((m, tn), col),
        out_shape=jax.ShapeDtypeStruct((m, nc), out_dtype),
        compiler_params=_params(("parallel",), 48),
        name="dft_stage_" + epi,
    )(w, x, *extra)


def _bmm_kernel(m_ref, x_ref, o_ref, *, gs):
    for s in range(gs):
        x = x_ref[:, s]
        x = x.reshape(x.shape[0] * x.shape[1], x.shape[2])
        acc = _dot(m_ref[s], x)
        c = acc.shape[1]
        o_ref[:, s * c:(s + 1) * c] = acc.astype(o_ref.dtype)


def _bmm(mats, x, out_dtype):
    ng, p, qq = mats.shape
    c = x.shape[3]
    gs = _pick_tile(ng, BMM_GROUPS_PER_STEP, 1)
    return pl.pallas_call(
        functools.partial(_bmm_kernel, gs=gs),
        grid=(ng // gs,),
        in_specs=[pl.BlockSpec((gs, p, qq), lambda g: (g, 0, 0)),
                  pl.BlockSpec((2, gs, qq // 2, c), lambda g: (0, g, 0, 0))],
        out_specs=pl.BlockSpec((p, gs * c), lambda g: (0, g)),
        out_shape=jax.ShapeDtypeStruct((p, ng * c), out_dtype),
        compiler_params=_params(("parallel",), 48),
        name="dft_batched_real",
    )(mats, x)


def _tbmm_kernel(*refs, mode, gs):
    f_ref, twr_ref, twi_ref, x_ref = refs[:4]
    o_ref = refs[-1]
    for s in range(gs):
        xr, xi = x_ref[0, s].astype(F32), x_ref[1, s].astype(F32)
        reps = xr.shape[1] // twr_ref.shape[2]
        tr = jnp.concatenate([twr_ref[s]] * reps, axis=1)
        ti = jnp.concatenate([twi_ref[s]] * reps, axis=1)
        x = jnp.concatenate([xr * tr - xi * ti, xr * ti + xi * tr], axis=0).astype(BF16)
        acc = _dot(f_ref[...], x)
        ph = acc.shape[0] // 2
        ar, ai = acc[:ph], acc[ph:]
        if mode == "scale":
            sc = refs[4][...]
            o_ref[0, s] = ar * sc
            o_ref[1, s] = ai * sc
        else:
            g_ref, fi_ref = refs[4], refs[5]
            gr, gi = g_ref[0, s], g_ref[1, s]
            y = jnp.concatenate([ar * gr - ai * gi, ar * gi + ai * gr], axis=0).astype(BF16)
            back = _dot(fi_ref[...], y)
            br, bi = back[:ph], back[ph:]
            o_ref[0, s] = (br * tr + bi * ti).astype(o_ref.dtype)
            o_ref[1, s] = (bi * tr - br * ti).astype(o_ref.dtype)


def _tbmm(tabs, x, mode, extra, out_dtype, g_col=0):
    _, ng, n2, c = x.shape
    gs = _pick_tile(ng, BMM_GROUPS_PER_STEP, 1)
    p = 2 * n2
    blk = lambda g: (0, g, 0, 0)
    specs = [pl.BlockSpec((p, p), lambda g: (0, 0)),
             pl.BlockSpec((gs, n2, 128), lambda g: (g, 0, 0)),
             pl.BlockSpec((gs, n2, 128), lambda g: (g, 0, 0)),
             pl.BlockSpec((2, gs, n2, c), blk)]
    args = [tabs["hy_fwd"], tabs["hy_twr"], tabs["hy_twi"], x]
    if mode == "scale":
        specs.append(pl.BlockSpec((1, c), lambda g: (0, 0)))
        args.append(extra)
    else:
        specs += [pl.BlockSpec((2, gs, n2, c), lambda g: (0, g, 0, g_col)),
                  pl.BlockSpec((p, p), lambda g: (0, 0))]
        args += [extra, tabs["hy_inv"]]
    return pl.pallas_call(
        functools.partial(_tbmm_kernel, mode=mode, gs=gs),
        grid=(ng // gs,),
        in_specs=specs,
        out_specs=pl.BlockSpec((2, gs, n2, c), blk),
        out_shape=jax.ShapeDtypeStruct(x.shape, out_dtype),
        compiler_params=_params(("parallel",), 48),
        name="dft_inner_" + mode,
    )(*args)


def _angle(num, den):
    return (2.0 * math.pi / den) * jnp.mod(num, den).astype(F32)


def _dft_tables(n_lat):
    nn = 2 * n_lat
    n2 = DFT_N2
    n1 = nn // n2
    i1 = jnp.arange(n1, dtype=jnp.int32)
    a1 = _angle(i1[:, None] * i1[None, :], n1)
    c1, s1 = jnp.cos(a1), jnp.sin(a1)
    t = {}
    t["hy_f1"] = jnp.concatenate([c1, -s1], axis=0).astype(BF16)
    ch, sh = c1[:, :n1 // 2], s1[:, :n1 // 2]
    t["hy_f1c"] = jnp.concatenate([jnp.concatenate([ch, sh], axis=1),
                                   jnp.concatenate([-sh, ch], axis=1)], axis=0).astype(BF16)
    t["hy_i2c"] = jnp.concatenate([jnp.concatenate([ch.T, -sh.T], axis=1),
                                   jnp.concatenate([sh.T, ch.T], axis=1)], axis=0).astype(BF16)
    i2 = jnp.arange(n2, dtype=jnp.int32)
    a2 = _angle(i2[:, None] * i2[None, :], n2)
    c2, s2 = jnp.cos(a2), jnp.sin(a2)
    fwd = jnp.concatenate([jnp.concatenate([c2, s2], axis=1), jnp.concatenate([-s2, c2], axis=1)], axis=0)
    t["hy_fwd"] = fwd.astype(BF16)
    t["hy_inv"] = fwd.T.astype(BF16)
    at = _angle(i1[:, None] * i2[None, :], nn)
    lanes = (n1, n2, 128)
    t["hy_twr"] = jnp.broadcast_to(jnp.cos(at)[:, :, None], lanes)
    t["hy_twi"] = jnp.broadcast_to(-jnp.sin(at)[:, :, None], lanes)
    m2 = FN_N2
    m1 = n_lat // m2
    j1 = jnp.arange(m1, dtype=jnp.int32)
    b1 = _angle(j1[:, None] * j1[None, :], m1)
    cb, sb = jnp.cos(b1), jnp.sin(b1)
    t["fn_s1"] = jnp.concatenate([jnp.concatenate([cb, sb], axis=1),
                                  jnp.concatenate([-sb, cb], axis=1)], axis=0).astype(BF16)
    j2 = jnp.arange(m2, dtype=jnp.int32)
    kf = j1[:, None, None] + m1 * j2[None, :, None]
    b2 = _angle(kf * j2[None, None, :], n_lat)
    norm = 1.0 / math.sqrt(n_lat * FN_GROUP_DIM)
    t["fn_s3"] = (norm * jnp.concatenate([jnp.cos(b2), jnp.sin(b2)], axis=2)).astype(BF16)
    return t


def _small_dft_tables(n):
    nn = 2 * n
    k = jnp.arange(nn, dtype=jnp.int32)
    a = _angle(k[:, None] * k[None, :], nn)
    c, s = jnp.cos(a), jnp.sin(a)
    t = {}
    t["f_full"] = jnp.concatenate([c, -s], axis=0).astype(BF16)
    t["f_half"] = jnp.concatenate([c[:, :n], -s[:, :n]], axis=0).astype(BF16)
    t["i_half"] = jnp.concatenate([c[:n], -s[:n]], axis=1).astype(BF16)
    j = jnp.arange(n, dtype=jnp.int32)
    b = _angle(j[:, None] * j[None, :], n)
    norm = 1.0 / math.sqrt(n * FN_GROUP_DIM)
    t["fn"] = (norm * jnp.concatenate([jnp.cos(b), jnp.sin(b)], axis=1)).astype(BF16)
    return t


def _hyena_long(hy, short_w, fw, bias, tabs):
    b, n, _ = hy.shape
    assert b == 2
    cw = HY_WIDTH
    n2 = DFT_N2
    n1 = 2 * n // n2
    v, x1, x2 = _sconv(hy, short_w, False)
    g, gscale = _padded_filter(n, fw)
    half = HY_ORDER * HY_WIDTH
    ga = _colmm(tabs["hy_f1"], g.reshape(n1, n2 * half), BF16)
    gspec = _tbmm(tabs, ga.reshape(2, n1, n2, half), "scale", gscale * (1.0 / (2 * n)), F32)
    z = v.reshape(n1, n2 * cw)
    for o, xg in enumerate((x1, x2)):
        a = _colmm(tabs["hy_f1c"], z, BF16)
        bm = _tbmm(tabs, a.reshape(2, n1, n2, cw), "conv", gspec, BF16, g_col=o)
        brow = jnp.tile(bias[o], n2)[None, :]
        z = _colmm(tabs["hy_i2c"], bm.reshape(2 * n1, n2 * cw), F32, epi="gate",
                   extra=(xg.reshape(n1, n2 * cw), z, brow))
    return z.reshape(b, n, cw)


def _hyena_short(hy, short_w, fw, bias, tabs):
    b, n, _ = hy.shape
    cw = b * HY_WIDTH
    v, x1, x2 = _sconv(hy, short_w, True)
    g, gscale = _padded_filter(n, fw)
    gspec = _colmm(tabs["f_full"], g, F32, epi="scale", extra=(gscale * (1.0 / (2 * n)),), tn_cap=256)
    z = v
    for o, xg in enumerate((x1, x2)):
        y = _colmm(tabs["f_half"], z, BF16, epi="cmul", extra=(gspec,), tn_cap=HY_WIDTH, g_col=o)
        brow = jnp.tile(bias[o], b)[None, :]
        z = _colmm(tabs["i_half"], y, F32, epi="gate", extra=(xg, z, brow), tn_cap=256)
    return z


def _fn0_kernel(x_ref, w_ref, o_ref):
    acc = _dot(x_ref[0].astype(BF16), w_ref[...])
    o_ref[0] = acc[:, :BRANCH_WIDTH].astype(o_ref.dtype)
    o_ref[1] = acc[:, BRANCH_WIDTH:].astype(o_ref.dtype)


def _fnet_channel_dft(u):
    b, n, w = u.shape
    j = np.arange(FN_GROUP_DIM)
    ang = 2.0 * np.pi * ((j[:, None] * j[None, :]) % FN_GROUP_DIM) / FN_GROUP_DIM
    eye = np.eye(FN_GROUPS)
    wc = np.concatenate([np.kron(eye, np.cos(ang)), -np.kron(eye, np.sin(ang))], axis=1)
    wc = jnp.asarray(wc, F32).astype(BF16)
    tm = _pick_tile(n, 1024, 8)
    return pl.pallas_call(
        _fn0_kernel,
        grid=(b, n // tm),
        in_specs=[pl.BlockSpec((1, tm, w), lambda bb, i: (bb, i, 0)),
                  pl.BlockSpec((w, 2 * w), lambda bb, i: (0, 0))],
        out_specs=pl.BlockSpec((2, tm, w), lambda bb, i: (0, i, bb)),
        out_shape=jax.ShapeDtypeStruct((2, n, b * w), BF16),
        compiler_params=_params(("parallel", "parallel")),
        name="fnet_channel_dft",
    )(u, wc)


def _fnet_long(u, tabs):
    b, n, w = u.shape
    cw = b * w
    m2 = FN_N2
    m1 = n // m2
    wri = _fnet_channel_dft(u)
    a = _colmm(tabs["fn_s1"], wri.reshape(2 * m1, m2 * cw), BF16)
    y = _bmm(tabs["fn_s3"], a.reshape(2, m1, m2, cw), BF16)
    return y.reshape(n, cw)


def _fnet_short(u, tabs):
    b, n, w = u.shape
    wri = _fnet_channel_dft(u)
    return _colmm(tabs["fn"], wri.reshape(2 * n, b * w), BF16, tn_cap=256)


def _merge_kernel(x_ref, g_ref, sc_ref, sh_ref, gt_ref, oa_ref, ob_ref, oc_ref, od_ref,
                  wg_ref, wb_ref, wo_ref, o_ref):
    x = x_ref[0]
    hb = _normmod(x, g_ref[...], sc_ref[0], sh_ref[0]).astype(BF16)
    branches = (oa_ref[0], ob_ref[0], oc_ref[...], od_ref[0])
    acc = jnp.zeros(x.shape, F32)
    for nbr, o in enumerate(branches):
        gate = _dot(hb, wg_ref[:, nbr * D_MODEL:(nbr + 1) * D_MODEL])
        proj = _dot(o.astype(BF16), wb_ref[nbr])
        acc = acc + jax.nn.sigmoid(gate) * proj
    m = _dot(acc.astype(BF16), wo_ref[...])
    o_ref[0] = x + gt_ref[0] * m


def _merge(x, g, sc, sh, gt, oa, ob, oc, od, w):
    b, n, _ = x.shape
    tm = _pick_tile(n, 512, 8)
    tok = lambda bb, i: (bb, i, 0)
    row = lambda bb, i: (bb, 0, 0)
    seq = lambda bb, i: (i, bb)
    c2 = lambda bb, i: (0, 0)
    return pl.pallas_call(
        _merge_kernel,
        grid=(b, n // tm),
        in_specs=[pl.BlockSpec((1, tm, D_MODEL), tok),
                  pl.BlockSpec((1, D_MODEL), c2),
                  pl.BlockSpec((1, 1, D_MODEL), row), pl.BlockSpec((1, 1, D_MODEL), row),
                  pl.BlockSpec((1, 1, D_MODEL), row),
                  pl.BlockSpec((1, tm, BRANCH_WIDTH), tok),
                  pl.BlockSpec((1, tm, BRANCH_WIDTH), tok),
                  pl.BlockSpec((tm, BRANCH_WIDTH), seq),
                  pl.BlockSpec((1, tm, BRANCH_WIDTH), tok),
                  pl.BlockSpec((D_MODEL, N_BRANCH * D_MODEL), c2),
                  pl.BlockSpec((N_BRANCH, BRANCH_WIDTH, D_MODEL), lambda bb, i: (0, 0, 0)),
                  pl.BlockSpec((D_MODEL, D_MODEL), c2)],
        out_specs=pl.BlockSpec((1, tm, D_MODEL), tok),
        out_shape=jax.ShapeDtypeStruct(x.shape, F32),
        compiler_params=_params(("parallel", "parallel"), 56),
        name="branch_merge",
    )(x, g, sc, sh, gt, oa, ob, oc, od, w["wgate"], w["wbranch"], w["wout"])


def _moe_kernel(x_ref, g_ref, sc_ref, sh_ref, gt_ref, wr_ref, br_ref, wg_ref, wu_ref, wd_ref, fg_ref,
                o_ref, h_s, wt_s, acc_s, *, final):
    e = pl.program_id(2)

    @pl.when(e == 0)
    def _():
        hb = _normmod(x_ref[0], g_ref[...], sc_ref[0], sh_ref[0]).astype(BF16)
        h_s[...] = hb
        r = _dot(hb, wr_ref[...]) + br_ref[...]
        lane = lax.broadcasted_iota(jnp.int32, r.shape, 1)
        neg = jnp.float32(-jnp.inf)
        isg = lane < MOE_GROUPS
        gmax = jnp.max(jnp.where(isg, r, neg), axis=1, keepdims=True)
        gsum = jnp.sum(jnp.where(isg, jnp.exp(r - gmax), 0.0), axis=1, keepdims=True)
        g_p = 1.0 / gsum
        gidx = jnp.min(jnp.where(isg & (r == gmax), lane, ROUTER_LANES), axis=1, keepdims=True)
        ise = ((lane >= MOE_GROUPS) & (lane < MOE_GROUPS + MOE_EXPERTS)
               & (jnp.right_shift(lane - MOE_GROUPS, 2) == gidx))
        el = jnp.where(ise, r, neg)
        e1 = jnp.max(el, axis=1, keepdims=True)
        i1 = jnp.min(jnp.where(ise & (r == e1), lane, ROUTER_LANES), axis=1, keepdims=True)
        el2 = jnp.where(lane == i1, neg, el)
        e2 = jnp.max(el2, axis=1, keepdims=True)
        i2 = jnp.min(jnp.where(el2 == e2, lane, ROUTER_LANES), axis=1, keepdims=True)
        d = jnp.exp(e2 - e1)
        w1 = 1.0 / (1.0 + d)
        w2 = d / (1.0 + d)
        wt_s[...] = jnp.where(lane == i1, g_p * w1, jnp.where(lane == i2, g_p * w2, 0.0))
        acc_s[...] = jnp.zeros(acc_s.shape, F32)

    hb = h_s[...]
    wt = wt_s[...]
    lane = lax.broadcasted_iota(jnp.int32, wt.shape, 1)
    acts = []
    for k in range(MOE_PER_GROUP):
        a = _dot(hb, wg_ref[k])
        u = _dot(hb, wu_ref[k])
        col = MOE_GROUPS + e * MOE_PER_GROUP + k
        wcol = jnp.sum(jnp.where(lane == col, wt, 0.0), axis=1, keepdims=True)
        acts.append(((a * jax.nn.sigmoid(a)) * u * wcol).astype(BF16))
    act = jnp.concatenate(acts, axis=1)
    wd = wd_ref[...].reshape(MOE_PER_GROUP * MOE_HIDDEN, D_MODEL)
    acc_s[...] += _dot(act, wd)

    @pl.when(e == MOE_GROUPS - 1)
    def _():
        y = x_ref[0] + gt_ref[0] * acc_s[...]
        if final:
            y = _rms(y) * fg_ref[...]
        o_ref[0] = y


def _moe(x, g, sc, sh, gt, w, fg, final):
    b, n, _ = x.shape
    tm = _pick_tile(n, 1024, 8)
    tok = lambda bb, i, e: (bb, i, 0)
    row = lambda bb, i, e: (bb, 0, 0)
    c2 = lambda bb, i, e: (0, 0)
    ex = lambda bb, i, e: (e, 0, 0)
    return pl.pallas_call(
        functools.partial(_moe_kernel, final=final),
        grid=(b, n // tm, MOE_GROUPS),
        in_specs=[pl.BlockSpec((1, tm, D_MODEL), tok),
                  pl.BlockSpec((1, D_MODEL), c2),
                  pl.BlockSpec((1, 1, D_MODEL), row), pl.BlockSpec((1, 1, D_MODEL), row),
                  pl.BlockSpec((1, 1, D_MODEL), row),
                  pl.BlockSpec((D_MODEL, ROUTER_LANES), c2), pl.BlockSpec((1, ROUTER_LANES), c2),
                  pl.BlockSpec((MOE_PER_GROUP, D_MODEL, MOE_HIDDEN), ex),
                  pl.BlockSpec((MOE_PER_GROUP, D_MODEL, MOE_HIDDEN), ex),
                  pl.BlockSpec((MOE_PER_GROUP, MOE_HIDDEN, D_MODEL), ex),
                  pl.BlockSpec((1, D_MODEL), c2)],
        out_specs=pl.BlockSpec((1, tm, D_MODEL), tok),
        out_shape=jax.ShapeDtypeStruct(x.shape, F32),
        scratch_shapes=[pltpu.VMEM((tm, D_MODEL), BF16), pltpu.VMEM((tm, ROUTER_LANES), F32),
                        pltpu.VMEM((tm, D_MODEL), F32)],
        compiler_params=_params(("parallel", "parallel", "arbitrary"), 56),
        name="hier_moe",
    )(x, g, sc, sh, gt, w["wr"], w["br"], w["wg"], w["wu"], w["wd"], fg)


def _layer_weights(l, w_in, mla_q_norm, mla_w_uq, mla_kv_norm, mla_w_ukv, w_branch, w_out,
                   moe_w_group, moe_b_group, moe_w_expert, moe_b_expert, moe_w_gate, moe_w_up, moe_w_down):
    wi = w_in[l]
    b0 = MLA_Q_RANK
    b1 = b0 + MLA_KV_RANK
    b2 = b1 + MLA_ROPE
    b3 = b2 + HY_IN
    b4 = b3 + BRANCH_WIDTH
    b5 = b4 + 3 * BRANCH_WIDTH
    zpad = lambda r, c: jnp.zeros((r, c), F32)
    w_kr = wi[:, b1:b2]
    krp = jnp.concatenate([zpad(D_MODEL, MLA_NOPE), w_kr[:, 0::2], w_kr[:, 1::2],
                           zpad(D_MODEL, MLA_HEAD_PAD - MLA_NOPE - MLA_ROPE)], axis=1)
    w_na = wi[:, b4:b5]
    w_na = jnp.concatenate([w_na[:, :BRANCH_WIDTH] * NA_SCALE, w_na[:, BRANCH_WIDTH:]], axis=1)
    wa = jnp.concatenate([wi[:, :b1], krp, wi[:, b2:b4], w_na], axis=1).astype(BF16)
    uq = mla_w_uq[l].reshape(MLA_Q_RANK, MLA_HEADS, MLA_NOPE + MLA_ROPE) * (MLA_SCALE * LOG2E)
    uq = jnp.concatenate([uq[..., :MLA_NOPE], uq[..., MLA_NOPE::2], uq[..., MLA_NOPE + 1::2],
                          jnp.zeros((MLA_Q_RANK, MLA_HEADS, MLA_HEAD_PAD - MLA_NOPE - MLA_ROPE), F32)], axis=-1)
    ukv = mla_w_ukv[l].reshape(MLA_KV_RANK, MLA_HEADS, MLA_NOPE + MLA_V)
    uk = jnp.concatenate([ukv[..., :MLA_NOPE],
                          jnp.zeros((MLA_KV_RANK, MLA_HEADS, MLA_HEAD_PAD - MLA_NOPE), F32)], axis=-1)
    uv = ukv[..., MLA_NOPE:]
    wr = jnp.concatenate([moe_w_group[l], moe_w_expert[l],
                          zpad(D_MODEL, ROUTER_LANES - MOE_GROUPS - MOE_EXPERTS)], axis=1)
    br = jnp.concatenate([moe_b_group[l], moe_b_expert[l],
                          jnp.zeros((ROUTER_LANES - MOE_GROUPS - MOE_EXPERTS,), F32)])[None, :]
    return dict(
        wa=wa, qn=mla_q_norm[l][None, :], kvn=mla_kv_norm[l][None, :],
        wuq=uq.reshape(MLA_Q_RANK, -1).astype(BF16), wuk=uk.reshape(MLA_KV_RANK, -1).astype(BF16),
        wuv=uv.reshape(MLA_KV_RANK, -1).astype(BF16),
        wgate=wi[:, b5:].astype(BF16), wbranch=w_branch[l].astype(BF16), wout=w_out[l].astype(BF16),
        wr=wr.astype(BF16), br=br, wg=moe_w_gate[l].astype(BF16), wu=moe_w_up[l].astype(BF16),
        wd=moe_w_down[l].astype(BF16))


def _rope_tables(n):
    t = jnp.arange(n, dtype=jnp.int32)
    row = (t // GRID_W).astype(F32)
    col = (t % GRID_W).astype(F32)
    n_freq = MLA_ROPE // 4
    inv = ROPE_BASE ** (-jnp.arange(n_freq, dtype=F32) / n_freq)
    ang = jnp.concatenate([row[:, None] * inv, col[:, None] * inv], axis=-1)
    c, s = jnp.cos(ang), jnp.sin(ang)
    one = jnp.ones((n, MLA_NOPE), F32)
    zero = jnp.zeros((n, MLA_NOPE), F32)
    hr = MLA_ROPE // 2
    tail1 = jnp.ones((n, MLA_HEAD_PAD - MLA_NOPE - MLA_ROPE), F32)
    tail0 = jnp.zeros((n, MLA_HEAD_PAD - MLA_NOPE - MLA_ROPE), F32)
    cos = jnp.concatenate([one, c, c, tail1], axis=1)
    s1 = jnp.concatenate([zero, -s, jnp.zeros((n, hr), F32), tail0], axis=1)
    s2 = jnp.concatenate([zero, jnp.zeros((n, hr), F32), s, tail0], axis=1)
    return cos, s1, s2


def _identity_rope(n):
    return (jnp.ones((n, MLA_HEAD_PAD), F32), jnp.zeros((n, MLA_HEAD_PAD), F32),
            jnp.zeros((n, MLA_HEAD_PAD), F32))


def kernel(x, c, ctx, c_ctx, ada_w, ada_b, norm1_g, norm2_g, w_in, mla_q_norm, mla_w_uq, mla_kv_norm, mla_w_ukv, hy_short_w, hy_w1, hy_b1, hy_freq1, hy_w2, hy_b2, hy_freq2, hy_w3, hy_bias, na_rpb, w_branch, w_out, moe_w_group, moe_b_group, moe_w_expert, moe_b_expert, moe_w_gate, moe_w_up, moe_w_down, final_norm_g):
    b, n_lat, _ = x.shape
    n_ctx = ctx.shape[1]
    assert b <= 4 and n_lat % (NA_QROWS * GRID_W) == 0 and n_lat // GRID_W >= 2 * NA_QROWS

    cvec = jnp.concatenate([c, c_ctx[None, :], jnp.zeros((8 - b - 1, D_MODEL), F32)], axis=0)
    mod = _mod_vectors(cvec, ada_w, ada_b)

    rope_lat = _rope_tables(n_lat)
    rope_ctx = _identity_rope(n_ctx)
    tabs_lat = _dft_tables(n_lat)
    tabs_ctx = _small_dft_tables(n_ctx)
    fg = final_norm_g[None, :]

    for l in range(DEPTH):
        last = l == DEPTH - 1
        w = _layer_weights(l, w_in, mla_q_norm, mla_w_uq, mla_kv_norm, mla_w_ukv, w_branch, w_out,
                           moe_w_group, moe_b_group, moe_w_expert, moe_b_expert,
                           moe_w_gate, moe_w_up, moe_w_down)
        ml = mod[l, :b].reshape(b, 1, 6, D_MODEL)
        mc = jnp.broadcast_to(mod[l, b].reshape(1, 1, 6, D_MODEL), (b, 1, 6, D_MODEL))
        sh1, sc1, gt1, sh2, sc2, gt2 = [ml[:, :, i] for i in range(6)]
        sh1c, sc1c, gt1c, sh2c, sc2c, gt2c = [mc[:, :, i] for i in range(6)]
        g1 = norm1_g[l][None, :]
        g2 = norm2_g[l][None, :]
        fw = dict(
            w1=jnp.concatenate([hy_w1[l], jnp.zeros((HY_EMB_PAD - HY_EMB, HY_HIDDEN), F32)], axis=0).astype(BF16),
            b1=hy_b1[l][None, :], f1=hy_freq1[l][None, :], w2=hy_w2[l].astype(BF16), b2=hy_b2[l][None, :],
            f2=hy_freq2[l][None, :],
            w3=jnp.transpose(hy_w3[l].reshape(HY_HIDDEN, 2, HY_ORDER * HY_WIDTH), (1, 0, 2)).astype(BF16))

        n_keys = n_lat + n_ctx
        kv_bufs = (jnp.zeros((b, MLA_HEADS, n_keys, MLA_HEAD_PAD), BF16),
                   jnp.zeros((b, MLA_HEADS, MLA_VT_ROWS, n_keys), PV_DTYPE))
        q, kall, vtall, hy, fn, naq, nak, nav = _inproj(x, g1, sc1, sh1, w, rope_lat, kv_bufs, 0, False)
        qc, kc, vc, kall, vtall, hyc, fnc, naqc, nakc, navc = _inproj(
            ctx, g1, sc1c, sh1c, w, rope_ctx, (kall, vtall), n_lat, True)

        oa = _flash_t(q, kall, vtall, MLA_V)
        od = _na(naq, nak, nav, nakc, navc, _na_bias_tables(na_rpb[l]))
        ob = _hyena_long(hy, hy_short_w[l], fw, hy_bias[l], tabs_lat)
        oc = _fnet_long(fn, tabs_lat)
        x_new = _merge(x, g1, sc1, sh1, gt1, oa, ob, oc, od, w)

        if not last:
            oa_c = _flash(qc, kc, vc, True)
            heads = lambda t: jnp.transpose(t.reshape(b, n_ctx, NA_HEADS, NA_HEAD_DIM), (0, 2, 1, 3))
            od_c = _flash(heads(naqc), heads(nakc), heads(navc), False)
            ob_c = _hyena_short(hyc, hy_short_w[l], fw, hy_bias[l], tabs_ctx)
            ob_c = jnp.transpose(ob_c.reshape(n_ctx, b, HY_WIDTH), (1, 0, 2))
            oc_c = _fnet_short(fnc, tabs_ctx)
            ctx = _merge(ctx, g1, sc1c, sh1c, gt1c, oa_c, ob_c, oc_c, od_c, w)
            ctx = _moe(ctx, g2, sc2c, sh2c, gt2c, w, fg, False)
        x = _moe(x_new, g2, sc2, sh2, gt2, w, fg, last)
    return x
```

```python
import functools
import math

import numpy as np
import jax
import jax.numpy as jnp
from jax import lax
from jax.experimental import pallas as pl
from jax.experimental.pallas import tpu as pltpu

F32 = jnp.float32
BF16 = jnp.bfloat16

D_MODEL = 1024
DEPTH = 2
GRID_W = 64
NORM_EPS = 1e-6
MASK_VALUE = -1e30
BRANCH_WIDTH = D_MODEL // 4
N_BRANCH = 4

MLA_HEADS = 4
MLA_Q_RANK = D_MODEL // 4
MLA_KV_RANK = D_MODEL // 8
MLA_NOPE = 64
MLA_ROPE = 32
MLA_V = BRANCH_WIDTH // MLA_HEADS
MLA_SCALE = 1.0 / math.sqrt(MLA_NOPE + MLA_ROPE)
MLA_HEAD_PAD = 128
MLA_VT_ROWS = MLA_V + 16
PV_DTYPE = BF16
ROPE_BASE = 10000.0
LOG2E = 1.4426950408889634

HY_WIDTH = BRANCH_WIDTH
HY_ORDER = 2
HY_BANDS = 16
HY_EMB = 1 + 2 * HY_BANDS
HY_EMB_PAD = 128
HY_HIDDEN = 64
HY_DECAY_TARGET = 1e-2
HY_DECAY_SHORT_PCT = 0.3
HY_DECAY_LONG_PCT = 1.5
HY_IN = (HY_ORDER + 1) * HY_WIDTH

FN_GROUPS = 4
FN_GROUP_DIM = BRANCH_WIDTH // FN_GROUPS

NA_HEADS = 4
NA_HEAD_DIM = BRANCH_WIDTH // NA_HEADS
NA_KH = 8
NA_KW = 16
NA_SCALE = 1.0 / math.sqrt(NA_HEAD_DIM)
NA_QROWS = 8
NA_KROWS = 16
NA_VT_PAD = 16

MOE_GROUPS = 4
MOE_PER_GROUP = 4
MOE_EXPERTS = MOE_GROUPS * MOE_PER_GROUP
MOE_HIDDEN = 256
ROUTER_LANES = 128

MLA_IN = MLA_Q_RANK + MLA_KV_RANK + MLA_ROPE
MIX_IN = MLA_IN + HY_IN + BRANCH_WIDTH + 3 * BRANCH_WIDTH
MIX_PAD = MLA_Q_RANK + MLA_KV_RANK + MLA_HEAD_PAD + HY_IN + BRANCH_WIDTH + 3 * BRANCH_WIDTH

DFT_N2 = 256
FN_N2 = 128
BMM_GROUPS_PER_STEP = 4
FLASH_K_TILE = 3328
FLASH_Q_TILE = 1024
FLASH_Q_SUB = 512


def _params(sem, vmem_mb=None):
    kw = dict(dimension_semantics=sem)
    if vmem_mb is not None:
        kw["vmem_limit_bytes"] = vmem_mb << 20
    return pltpu.CompilerParams(**kw)


def _pick_tile(n, cap, mult):
    best = None
    for t in range(mult, min(n, cap) + 1, mult):
        if n % t == 0:
            best = t
    return best if best is not None else n


def _rms(x):
    return x * lax.rsqrt(jnp.mean(x * x, axis=-1, keepdims=True) + NORM_EPS)


def _normmod(x, g, sc, sh):
    return (_rms(x) * g) * (1.0 + sc) + sh


def _dot(a, b):
    return jnp.dot(a, b, preferred_element_type=F32)


def _dot_nt(a, b):
    return lax.dot_general(a, b, (((1,), (1,)), ((), ())), preferred_element_type=F32)


def _mod_kernel(c_ref, w_ref, b_ref, o_ref):
    c = c_ref[...]
    s = c * jax.nn.sigmoid(c)
    o_ref[0] = _dot(s.astype(BF16), w_ref[0].astype(BF16)) + b_ref[0]


def _mod_vectors(cvec, ada_w, ada_b):
    n6 = ada_w.shape[-1]
    tn = 1024
    return pl.pallas_call(
        _mod_kernel,
        grid=(DEPTH, n6 // tn),
        in_specs=[pl.BlockSpec((8, D_MODEL), lambda l, j: (0, 0)),
                  pl.BlockSpec((1, D_MODEL, tn), lambda l, j: (l, 0, j)),
                  pl.BlockSpec((1, 1, tn), lambda l, j: (l, 0, j))],
        out_specs=pl.BlockSpec((1, 8, tn), lambda l, j: (l, 0, j)),
        out_shape=jax.ShapeDtypeStruct((DEPTH, 8, n6), F32),
        compiler_params=_params(("parallel", "parallel")),
        name="adaln_vectors",
    )(cvec, ada_w, ada_b.reshape(DEPTH, 1, n6))


def _inproj_kernel(*refs, ctx_mode):
    (x_ref, g_ref, sc_ref, sh_ref, wa_ref, qn_ref, kvn_ref, wuq_ref, wuk_ref, wuv_ref,
     cos_ref, s1_ref, s2_ref) = refs[:13]
    if ctx_mode:
        (q_ref, k_ref, v_ref, kall_ref, vtall_ref, hy_ref, fn_ref, naq_ref, nak_ref, nav_ref) = refs[15:]
    else:
        (q_ref, kall_ref, vtall_ref, hy_ref, fn_ref, naq_ref, nak_ref, nav_ref) = refs[15:]
    h = _normmod(x_ref[0], g_ref[...], sc_ref[0], sh_ref[0]).astype(BF16)
    p = _dot(h, wa_ref[...])
    c0 = MLA_Q_RANK
    c1 = c0 + MLA_KV_RANK
    c2 = c1 + MLA_HEAD_PAD
    c3 = c2 + HY_IN
    c4 = c3 + BRANCH_WIDTH
    c5 = c4 + BRANCH_WIDTH
    c6 = c5 + BRANCH_WIDTH
    cq, ckv, krp = p[:, :c0], p[:, c0:c1], p[:, c1:c2]
    hy_ref[0] = p[:, c2:c3]
    fn_ref[0] = p[:, c3:c4]
    naq_ref[0] = p[:, c4:c5].astype(BF16)
    nak_ref[0] = p[:, c5:c6].astype(BF16)
    nav_ref[0] = p[:, c6:].T.astype(BF16)
    cqn = (_rms(cq) * qn_ref[...]).astype(BF16)
    ckn = (_rms(ckv) * kvn_ref[...]).astype(BF16)
    q = _dot(cqn, wuq_ref[...])
    kn = _dot(ckn, wuk_ref[...])
    v = _dot(ckn, wuv_ref[...])
    cos, s1, s2 = cos_ref[...], s1_ref[...], s2_ref[...]

    def rope(t):
        return (t * cos + pltpu.roll(t, MLA_HEAD_PAD - MLA_ROPE // 2, 1) * s1
                + pltpu.roll(t, MLA_ROPE // 2, 1) * s2)

    kr = rope(krp)
    vt = v.T
    tail_rows = lax.broadcasted_iota(jnp.int32, (MLA_VT_ROWS - MLA_V, v.shape[0]), 0)
    tail = jnp.where(tail_rows == 0, 1.0, 0.0).astype(PV_DTYPE)
    for hh in range(MLA_HEADS):
        sl = slice(hh * MLA_HEAD_PAD, (hh + 1) * MLA_HEAD_PAD)
        vs = slice(hh * MLA_V, (hh + 1) * MLA_V)
        kh = (kn[:, sl] + kr).astype(BF16)
        q_ref[0, hh] = rope(q[:, sl]).astype(BF16)
        kall_ref[0, hh] = kh
        vtall_ref[0, hh, :MLA_V] = vt[vs].astype(PV_DTYPE)
        vtall_ref[0, hh, MLA_V:] = tail
        if ctx_mode:
            k_ref[0, hh] = kh
            v_ref[0, hh] = v[:, vs].astype(BF16)


def _inproj(x, g, sc, sh, w, rope_tabs, kv_bufs, key_off, ctx_mode):
    b, n, _ = x.shape
    n_keys = kv_bufs[0].shape[2]
    tm = _pick_tile(n, 512, 128)
    assert key_off % tm == 0
    ko = key_off // tm
    cos, s1, s2 = rope_tabs
    tok = lambda bb, i: (bb, i, 0)
    row = lambda bb, i: (bb, 0, 0)
    const2 = lambda bb, i: (0, 0)
    hd = lambda bb, i: (bb, 0, i, 0)
    tab = lambda bb, i: (i, 0)
    hp = MLA_HEADS * MLA_HEAD_PAD
    in_specs = [pl.BlockSpec((1, tm, D_MODEL), tok),
                pl.BlockSpec((1, D_MODEL), const2),
                pl.BlockSpec((1, 1, D_MODEL), row),
                pl.BlockSpec((1, 1, D_MODEL), row),
                pl.BlockSpec((D_MODEL, MIX_PAD), const2),
                pl.BlockSpec((1, MLA_Q_RANK), const2),
                pl.BlockSpec((1, MLA_KV_RANK), const2),
                pl.BlockSpec((MLA_Q_RANK, hp), const2),
                pl.BlockSpec((MLA_KV_RANK, hp), const2),
                pl.BlockSpec((MLA_KV_RANK, BRANCH_WIDTH), const2),
                pl.BlockSpec((tm, MLA_HEAD_PAD), tab),
                pl.BlockSpec((tm, MLA_HEAD_PAD), tab),
                pl.BlockSpec((tm, MLA_HEAD_PAD), tab)]
    args = [x, g, sc, sh, w["wa"], w["qn"], w["kvn"], w["wuq"], w["wuk"], w["wuv"], cos, s1, s2]
    q_spec = pl.BlockSpec((1, MLA_HEADS, tm, MLA_HEAD_PAD), hd)
    q_sd = jax.ShapeDtypeStruct((b, MLA_HEADS, n, MLA_HEAD_PAD), BF16)
    kall_spec = pl.BlockSpec((1, MLA_HEADS, tm, MLA_HEAD_PAD), lambda bb, i: (bb, 0, ko + i, 0))
    vtall_spec = pl.BlockSpec((1, MLA_HEADS, MLA_VT_ROWS, tm), lambda bb, i: (bb, 0, 0, ko + i))
    kall_sd = jax.ShapeDtypeStruct((b, MLA_HEADS, n_keys, MLA_HEAD_PAD), BF16)
    vtall_sd = jax.ShapeDtypeStruct((b, MLA_HEADS, MLA_VT_ROWS, n_keys), PV_DTYPE)
    rest_specs = [pl.BlockSpec((1, tm, HY_IN), tok)] + [pl.BlockSpec((1, tm, BRANCH_WIDTH), tok)] * 3 \
        + [pl.BlockSpec((1, BRANCH_WIDTH, tm), lambda bb, i: (bb, 0, i))]
    rest_sd = [jax.ShapeDtypeStruct((b, n, HY_IN), F32), jax.ShapeDtypeStruct((b, n, BRANCH_WIDTH), F32)] \
        + [jax.ShapeDtypeStruct((b, n, BRANCH_WIDTH), BF16)] * 2 \
        + [jax.ShapeDtypeStruct((b, BRANCH_WIDTH, n), BF16)]
    in_specs += [pl.BlockSpec(memory_space=pl.ANY), pl.BlockSpec(memory_space=pl.ANY)]
    args += list(kv_bufs)
    if ctx_mode:
        out_specs = [q_spec, q_spec, pl.BlockSpec((1, MLA_HEADS, tm, MLA_V), hd), kall_spec, vtall_spec]
        out_shape = [q_sd, q_sd, jax.ShapeDtypeStruct((b, MLA_HEADS, n, MLA_V), BF16), kall_sd, vtall_sd]
        aliases = {13: 3, 14: 4}
    else:
        out_specs = [q_spec, kall_spec, vtall_spec]
        out_shape = [q_sd, kall_sd, vtall_sd]
        aliases = {13: 1, 14: 2}
    return pl.pallas_call(
        functools.partial(_inproj_kernel, ctx_mode=ctx_mode),
        grid=(b, n // tm),
        in_specs=in_specs,
        out_specs=out_specs + rest_specs,
        out_shape=out_shape + rest_sd,
        input_output_aliases=aliases,
        compiler_params=_params(("parallel", "parallel"), 48),
        name="input_projection",
    )(*args)


def _flash_kernel(q_ref, k_ref, v_ref, o_ref, m_ref, l_ref, acc_ref, *, nh, use_exp2):
    j = pl.program_id(2)

    @pl.when(j == 0)
    def _():
        m_ref[...] = jnp.full(m_ref.shape, MASK_VALUE, F32)
        l_ref[...] = jnp.zeros(l_ref.shape, F32)
        acc_ref[...] = jnp.zeros(acc_ref.shape, F32)

    ex = jnp.exp2 if use_exp2 else jnp.exp
    for h in range(nh):
        s = _dot_nt(q_ref[0, h], k_ref[0, h])
        m_prev = m_ref[h]
        m_cur = jnp.maximum(m_prev, jnp.max(s, axis=1, keepdims=True))
        alpha = ex(m_prev - m_cur)
        p = ex(s - m_cur)
        l_ref[h] = alpha * l_ref[h] + jnp.sum(p, axis=1, keepdims=True)
        acc_ref[h] = alpha * acc_ref[h] + _dot(p.astype(BF16), v_ref[0, h])
        m_ref[h] = m_cur

    @pl.when(j == pl.num_programs(2) - 1)
    def _():
        outs = [acc_ref[h] / l_ref[h] for h in range(nh)]
        o_ref[0] = jnp.concatenate(outs, axis=1).astype(o_ref.dtype)


def _flash(q, k, v, use_exp2):
    b, nh, lq, dk = q.shape
    lk, dv = k.shape[2], v.shape[3]
    tq = _pick_tile(lq, 512, 8)
    tk = _pick_tile(lk, 1280, 128)
    return pl.pallas_call(
        functools.partial(_flash_kernel, nh=nh, use_exp2=use_exp2),
        grid=(b, lq // tq, lk // tk),
        in_specs=[pl.BlockSpec((1, nh, tq, dk), lambda bb, i, j: (bb, 0, i, 0)),
                  pl.BlockSpec((1, nh, tk, dk), lambda bb, i, j: (bb, 0, j, 0)),
                  pl.BlockSpec((1, nh, tk, dv), lambda bb, i, j: (bb, 0, j, 0))],
        out_specs=pl.BlockSpec((1, tq, nh * dv), lambda bb, i, j: (bb, i, 0)),
        out_shape=jax.ShapeDtypeStruct((b, lq, nh * dv), BF16),
        scratch_shapes=[pltpu.VMEM((nh, tq, 1), F32), pltpu.VMEM((nh, tq, 1), F32),
                        pltpu.VMEM((nh, tq, dv), F32)],
        compiler_params=_params(("parallel", "parallel", "arbitrary"), 48),
        name="softmax_attention",
    )(q, k, v)


def _flash_t_kernel(q_ref, k_ref, vt_ref, o_ref, m_ref, acc_ref, st_ref, p_ref, *, nh, dv):
    j = pl.program_id(2)

    @pl.when(j == 0)
    def _():
        m_ref[...] = jnp.full(m_ref.shape, MASK_VALUE, F32)
        acc_ref[...] = jnp.zeros(acc_ref.shape, F32)

    rows = vt_ref.shape[2]

    tq = q_ref.shape[2]
    qs = min(tq, FLASH_Q_SUB)
    units = [(h, c) for h in range(nh) for c in range(tq // qs)]

    def scores(i):
        h, c = units[i]
        cols = slice(c * qs, (c + 1) * qs)
        st = _dot_nt(k_ref[0, h], q_ref[0, h, cols, :])
        st_ref[i % 2] = st
        m_prev = m_ref[h, :, cols]
        m_cur = jnp.maximum(m_prev, jnp.max(st, axis=0, keepdims=True))
        m_ref[h, :, cols] = m_cur
        return m_prev, m_cur

    def probs(i, m_cur):
        p_ref[i % 2] = jnp.exp2(st_ref[i % 2] - m_cur).astype(PV_DTYPE)

    def values(i, m_prev, m_cur):
        h, c = units[i]
        cols = slice(c * qs, (c + 1) * qs)
        alpha = jnp.exp2(m_prev - m_cur)
        acc_ref[h, :rows, cols] = alpha * acc_ref[h, :rows, cols] + _dot(vt_ref[0, h], p_ref[i % 2])

    n_units = len(units)
    stats = {0: scores(0)}
    if n_units > 1:
        stats[1] = scores(1)
    probs(0, stats[0][1])
    for i in range(n_units):
        values(i, *stats.pop(i))
        if i + 2 < n_units:
            stats[i + 2] = scores(i + 2)
        if i + 1 < n_units:
            probs(i + 1, stats[i + 1][1])

    @pl.when(j == pl.num_programs(2) - 1)
    def _():
        outs = []
        for h in range(nh):
            a = acc_ref[h]
            o = (a / a[dv:dv + 1]).T
            outs.append(o[:, :dv])
        o_ref[0] = jnp.concatenate(outs, axis=1).astype(o_ref.dtype)


def _flash_t(q, k, vt, dv):
    b, nh, lq, dk = q.shape
    lk, rows = k.shape[2], vt.shape[2]
    tq = _pick_tile(lq, FLASH_Q_TILE, 128)
    tk = _pick_tile(lk, FLASH_K_TILE, 128)
    return pl.pallas_call(
        functools.partial(_flash_t_kernel, nh=nh, dv=dv),
        grid=(b, lq // tq, lk // tk),
        in_specs=[pl.BlockSpec((1, nh, tq, dk), lambda bb, i, j: (bb, 0, i, 0)),
                  pl.BlockSpec((1, nh, tk, dk), lambda bb, i, j: (bb, 0, j, 0)),
                  pl.BlockSpec((1, nh, rows, tk), lambda bb, i, j: (bb, 0, 0, j))],
        out_specs=pl.BlockSpec((1, tq, nh * dv), lambda bb, i, j: (bb, i, 0)),
        out_shape=jax.ShapeDtypeStruct((b, lq, nh * dv), BF16),
        scratch_shapes=[pltpu.VMEM((nh, 1, tq), F32), pltpu.VMEM((nh, 128, tq), F32),
                        pltpu.VMEM((2, tk, min(tq, FLASH_Q_SUB)), F32),
                        pltpu.VMEM((2, tk, min(tq, FLASH_Q_SUB)), PV_DTYPE)],
        compiler_params=_params(("parallel", "parallel", "arbitrary"), 56),
        name="softmax_attention_t",
    )(q, k, vt)


def _na_kernel(q_ref, k0, k1, k2, k3, v0, v1, v2, v3, kc_ref, vc_ref, b_ref, o_ref):
    q = q_ref[0]
    kcat = jnp.concatenate([k0[0], k1[0], k2[0], k3[0]], axis=0)
    vtcat = jnp.concatenate([v0[0], v1[0], v2[0], v3[0]], axis=1)
    kc, vtc = kc_ref[0], vc_ref[0]
    dh = NA_HEAD_DIM

    def with_ones(vt):
        pad = lax.broadcasted_iota(jnp.int32, (NA_VT_PAD, vt.shape[1]), 0)
        return jnp.concatenate([vt, jnp.where(pad == 0, 1.0, 0.0).astype(vt.dtype)], axis=0)

    def scores(h):
        sl = slice(h * dh, (h + 1) * dh)
        qh = q[:, sl]
        st = _dot_nt(kcat[:, sl], qh) + b_ref[0, h]
        stc = _dot_nt(kc[:, sl], qh)
        m = jnp.maximum(jnp.max(st, axis=0, keepdims=True), jnp.max(stc, axis=0, keepdims=True))
        return st, stc, m

    def values(h, st, stc, m):
        sl = slice(h * dh, (h + 1) * dh)
        p = jnp.exp(st - m).astype(BF16)
        pc = jnp.exp(stc - m).astype(BF16)
        acc = _dot(with_ones(vtcat[sl]), p) + _dot(with_ones(vtc[sl]), pc)
        o = jnp.concatenate([acc / acc[dh:dh + 1],
                             jnp.zeros((128 - dh - NA_VT_PAD, acc.shape[1]), F32)], axis=0)
        return o.T[:, :dh]

    outs = []
    pending = scores(0)
    for h in range(NA_HEADS):
        nxt = scores(h + 1) if h + 1 < NA_HEADS else None
        outs.append(values(h, *pending))
        pending = nxt
    o_ref[0] = jnp.concatenate(outs, axis=1).astype(o_ref.dtype)


def _na_bias_tables(rpb):
    j = np.arange(NA_QROWS)[:, None]
    m = np.arange(NA_KROWS)[None, :]
    valid = np.zeros((3, NA_QROWS, NA_KROWS), bool)
    st0 = np.maximum(j - NA_KH // 2, 0)
    valid[0] = (m - 4 >= st0) & (m - 4 < st0 + NA_KH)
    valid[1] = (m - j >= 0) & (m - j < NA_KH)
    st2 = 4 + np.minimum(j - NA_KH // 2, 0)
    valid[2] = (m < 12) & (m >= st2) & (m < st2 + NA_KH)
    dr_idx = np.clip(m - 4 - j + NA_KH - 1, 0, 2 * NA_KH - 2)
    qc = np.arange(GRID_W)[:, None]
    kc = np.arange(GRID_W)[None, :]
    dc_idx = np.clip(kc - qc, -(NA_KW - 1), NA_KW - 1) + NA_KW - 1
    cs = np.clip(qc - NA_KW // 2, 0, GRID_W - NA_KW)
    col_ok = (kc >= cs) & (kc < cs + NA_KW)
    t = rpb.astype(F32)[:, dr_idx]
    t = t[..., dc_idx]
    t = jnp.transpose(t, (0, 2, 4, 1, 3))
    ok = (valid.transpose(0, 2, 1)[:, None, :, None, :, None]
          & col_ok.T[None, None, None, :, None, :])
    out = jnp.where(jnp.asarray(ok), t[None], MASK_VALUE)
    return out.reshape(3, NA_HEADS, NA_KROWS * GRID_W, NA_QROWS * GRID_W)


def _na(q, k, vt, kc, vtc, bias):
    b, n, w = q.shape
    c = kc.shape[1]
    tq = NA_QROWS * GRID_W
    tkb = 4 * GRID_W
    nb = n // tq
    nkb = n // tkb
    blocks = [lambda i: jnp.maximum(2 * i - 1, 0), lambda i: 2 * i, lambda i: 2 * i + 1,
              lambda i: jnp.minimum(2 * i + 2, nkb - 1)]
    kspecs = [pl.BlockSpec((1, tkb, w), lambda bb, i, f=f: (bb, f(i), 0)) for f in blocks]
    vspecs = [pl.BlockSpec((1, w, tkb), lambda bb, i, f=f: (bb, 0, f(i))) for f in blocks]
    return pl.pallas_call(
        _na_kernel,
        grid=(b, nb),
        in_specs=[pl.BlockSpec((1, tq, w), lambda bb, i: (bb, i, 0))] + kspecs + vspecs
                 + [pl.BlockSpec((1, c, w), lambda bb, i: (bb, 0, 0)),
                    pl.BlockSpec((1, w, c), lambda bb, i: (bb, 0, 0)),
                    pl.BlockSpec((1, NA_HEADS, NA_KROWS * GRID_W, tq),
                                 lambda bb, i: (jnp.where(i == 0, 0, jnp.where(i == nb - 1, 2, 1)), 0, 0, 0))],
        out_specs=pl.BlockSpec((1, tq, w), lambda bb, i: (bb, i, 0)),
        out_shape=jax.ShapeDtypeStruct((b, n, w), BF16),
        compiler_params=_params(("parallel", "arbitrary"), 56),
        name="neighbourhood_attention",
    )(q, k, k, k, k, vt, vt, vt, vt, kc, vtc, bias)


def _sconv_kernel(u_ref, pv_ref, nx_ref, w_ref, v_ref, x1_ref, x2_ref, *, nt):
    i = pl.program_id(1)
    u = u_ref[0]
    tl = u.shape[0]
    prev = jnp.where(i > 0, pv_ref[0, 7:8, :], 0.0)
    nxt = jnp.where(i < nt - 1, nx_ref[0, 0:1, :], 0.0)
    rows = lax.broadcasted_iota(jnp.int32, u.shape, 0)
    um = jnp.where(rows == 0, prev, pltpu.roll(u, 1, 0))
    up = jnp.where(rows == tl - 1, nxt, pltpu.roll(u, tl - 1, 0))
    w = w_ref[...]
    y = um * w[0:1] + u * w[1:2] + up * w[2:3]
    v_ref[...] = y[:, :HY_WIDTH]
    x1_ref[...] = y[:, HY_WIDTH:2 * HY_WIDTH]
    x2_ref[...] = y[:, 2 * HY_WIDTH:]


def _sconv(u, w, seq_major):
    b, n, _ = u.shape
    tl = _pick_tile(n, 512, 8)
    nt = n // tl
    r8 = tl // 8
    if seq_major:
        osd = jax.ShapeDtypeStruct((n, b * HY_WIDTH), F32)
        ospec = pl.BlockSpec((tl, HY_WIDTH), lambda bb, i: (i, bb))
    else:
        osd = jax.ShapeDtypeStruct((b * n, HY_WIDTH), F32)
        ospec = pl.BlockSpec((tl, HY_WIDTH), lambda bb, i: (bb * nt + i, 0))
    return pl.pallas_call(
        functools.partial(_sconv_kernel, nt=nt),
        grid=(b, nt),
        in_specs=[pl.BlockSpec((1, tl, HY_IN), lambda bb, i: (bb, i, 0)),
                  pl.BlockSpec((1, 8, HY_IN), lambda bb, i: (bb, jnp.maximum(i * r8 - 1, 0), 0)),
                  pl.BlockSpec((1, 8, HY_IN), lambda bb, i: (bb, jnp.minimum((i + 1) * r8, n // 8 - 1), 0)),
                  pl.BlockSpec((3, HY_IN), lambda bb, i: (0, 0))],
        out_specs=[ospec, ospec, ospec],
        out_shape=[osd, osd, osd],
        compiler_params=_params(("parallel", "parallel")),
        name="hyena_short_conv",
    )(u, u, u, w)


def _filt_kernel(z_ref, w1_ref, b1_ref, f1_ref, w2_ref, b2_ref, f2_ref, w3_ref, win_ref, g_ref, ss_ref):
    d = pl.program_id(0)
    i = pl.program_id(1)
    h = jnp.sin(f1_ref[...] * (_dot(w1_ref[...], z_ref[0].astype(BF16)) + b1_ref[...]))
    h = jnp.sin(f2_ref[...] * (_dot(w2_ref[...], h.astype(BF16)) + b2_ref[...]))
    h = _dot(w3_ref[0], h.astype(BF16)).T
    h = h * jnp.concatenate([win_ref[0]] * HY_ORDER, axis=1)
    rows = lax.broadcasted_iota(jnp.int32, h.shape, 0)
    h = jnp.where((rows == 0) & (i == 0) & (d == 1), 0.0, h)
    g_ref[0] = h
    ss = jnp.sum(h * h, axis=0, keepdims=True)
    first = (i == 0) & (d == 0)

    @pl.when(first)
    def _():
        ss_ref[...] = ss

    @pl.when(jnp.logical_not(first))
    def _():
        ss_ref[...] += ss


def _padded_filter(n, fw):
    r = jnp.arange(n, dtype=jnp.int32)
    idx = jnp.stack([r, jnp.mod(n - r, n)], axis=0).astype(F32)[:, :, None]
    t = idx / (n - 1)
    bands = jnp.linspace(1e-4, HY_BANDS - 1, HY_BANDS, dtype=F32)
    ang = (2.0 * math.pi / n) * idx * bands
    z = jnp.concatenate([t, jnp.cos(ang), -jnp.sin(ang), jnp.zeros((2, n, HY_EMB_PAD - HY_EMB), F32)], axis=-1)
    z = jnp.swapaxes(z, 1, 2)
    deltas = jnp.linspace(math.log(HY_DECAY_TARGET) / HY_DECAY_LONG_PCT,
                          math.log(HY_DECAY_TARGET) / HY_DECAY_SHORT_PCT, HY_WIDTH, dtype=F32)
    window = jnp.exp(-t * jnp.abs(deltas))
    tl = _pick_tile(n, 512, 128)
    half = HY_ORDER * HY_WIDTH
    c2 = lambda d, i: (0, 0)
    g, ss = pl.pallas_call(
        _filt_kernel,
        grid=(2, n // tl),
        in_specs=[pl.BlockSpec((1, HY_EMB_PAD, tl), lambda d, i: (d, 0, i)),
                  pl.BlockSpec((HY_HIDDEN, HY_EMB_PAD), c2), pl.BlockSpec((HY_HIDDEN, 1), c2),
                  pl.BlockSpec((HY_HIDDEN, 1), c2),
                  pl.BlockSpec((HY_HIDDEN, HY_HIDDEN), c2), pl.BlockSpec((HY_HIDDEN, 1), c2),
                  pl.BlockSpec((HY_HIDDEN, 1), c2),
                  pl.BlockSpec((1, half, HY_HIDDEN), lambda d, i: (d, 0, 0)),
                  pl.BlockSpec((1, tl, HY_WIDTH), lambda d, i: (d, i, 0))],
        out_specs=[pl.BlockSpec((1, tl, half), lambda d, i: (d, i, 0)), pl.BlockSpec((1, half), c2)],
        out_shape=[jax.ShapeDtypeStruct((2, n, half), F32), jax.ShapeDtypeStruct((1, half), F32)],
        compiler_params=_params(("arbitrary", "arbitrary")),
        name="hyena_filter_mlp",
    )(z, fw["w1"], fw["b1"], fw["f1"], fw["w2"], fw["b2"], fw["f2"], fw["w3"], window)
    return g.reshape(2 * n, half), lax.rsqrt(ss + NORM_EPS)


def _colmm_kernel(*refs, epi):
    w_ref, x_ref = refs[0], refs[1]
    o_ref = refs[-1]
    acc = _dot(w_ref[...], x_ref[...].astype(BF16))
    if epi == "gate":
        xg_ref, z_ref, b_ref = refs[2:5]
        acc = xg_ref[...] * (acc + z_ref[...] * b_ref[...])
    elif epi == "scale":
        acc = acc * refs[2][...]
    elif epi == "cmul":
        g = refs[2][...]
        mh = acc.shape[0] // 2
        ar, ai, gr, gi = acc[:mh], acc[mh:], g[:mh], g[mh:]
        acc = jnp.concatenate([ar * gr - ai * gi, ar * gi + ai * gr], axis=0)
    o_ref[...] = acc.astype(o_ref.dtype)


def _colmm(w, x, out_dtype, epi="none", extra=(), tn_cap=4096, g_col=None):
    m, k = w.shape
    nc = x.shape[1]
    tn = _pick_tile(nc, tn_cap, 128)
    col = lambda j: (0, j)
    specs = [pl.BlockSpec((m, k), lambda j: (0, 0)), pl.BlockSpec((k, tn), col)]
    if epi == "gate":
        specs += [pl.BlockSpec((m, tn), col), pl.BlockSpec((m, tn), col), pl.BlockSpec((1, tn), col)]
    elif epi == "scale":
        specs += [pl.BlockSpec((1, tn), col)]
    elif epi == "cmul":
        specs += [pl.BlockSpec((m, tn), lambda j: (0, g_col))]
    return pl.pallas_call(
        functools.partial(_colmm_kernel, epi=epi),
        grid=(nc // tn,),
        in_specs=specs,
        out_specs=pl.BlockSpec((m, tn), col),
        out_shape=jax.ShapeDtypeStruct((m, nc), out_dtype),
        compiler_params=_params(("parallel",), 48),
        name="dft_stage_" + epi,
    )(w, x, *extra)


def _bmm_kernel(m_ref, x_ref, o_ref, *, gs):
    for s in range(gs):
        x = x_ref[:, s]
        x = x.reshape(x.shape[0] * x.shape[1], x.shape[2])
        acc = _dot(m_ref[s], x)
        c = acc.shape[1]
        o_ref[:, s * c:(s + 1) * c] = acc.astype(o_ref.dtype)


def _bmm(mats, x, out_dtype):
    ng, p, qq = mats.shape
    c = x.shape[3]
    gs = _pick_tile(ng, BMM_GROUPS_PER_STEP, 1)
    return pl.pallas_call(
        functools.partial(_bmm_kernel, gs=gs),
        grid=(ng // gs,),
        in_specs=[pl.BlockSpec((gs, p, qq), lambda g: (g, 0, 0)),
                  pl.BlockSpec((2, gs, qq // 2, c), lambda g: (0, g, 0, 0))],
        out_specs=pl.BlockSpec((p, gs * c), lambda g: (0, g)),
        out_shape=jax.ShapeDtypeStruct((p, ng * c), out_dtype),
        compiler_params=_params(("parallel",), 48),
        name="dft_batched_real",
    )(mats, x)


def _tbmm_kernel(*refs, mode, gs):
    f_ref, twr_ref, twi_ref, x_ref = refs[:4]
    o_ref = refs[-1]
    for s in range(gs):
        xr, xi = x_ref[0, s].astype(F32), x_ref[1, s].astype(F32)
        reps = xr.shape[1] // twr_ref.shape[2]
        tr = jnp.concatenate([twr_ref[s]] * reps, axis=1)
        ti = jnp.concatenate([twi_ref[s]] * reps, axis=1)
        x = jnp.concatenate([xr * tr - xi * ti, xr * ti + xi * tr], axis=0).astype(BF16)
        acc = _dot(f_ref[...], x)
        ph = acc.shape[0] // 2
        ar, ai = acc[:ph], acc[ph:]
        if mode == "scale":
            sc = refs[4][...]
            o_ref[0, s] = ar * sc
            o_ref[1, s] = ai * sc
        else:
            g_ref, fi_ref = refs[4], refs[5]
            gr, gi = g_ref[0, s], g_ref[1, s]
            y = jnp.concatenate([ar * gr - ai * gi, ar * gi + ai * gr], axis=0).astype(BF16)
            back = _dot(fi_ref[...], y)
            br, bi = back[:ph], back[ph:]
            o_ref[0, s] = (br * tr + bi * ti).astype(o_ref.dtype)
            o_ref[1, s] = (bi * tr - br * ti).astype(o_ref.dtype)


def _tbmm(tabs, x, mode, extra, out_dtype, g_col=0):
    _, ng, n2, c = x.shape
    gs = _pick_tile(ng, BMM_GROUPS_PER_STEP, 1)
    p = 2 * n2
    blk = lambda g: (0, g, 0, 0)
    specs = [pl.BlockSpec((p, p), lambda g: (0, 0)),
             pl.BlockSpec((gs, n2, 128), lambda g: (g, 0, 0)),
             pl.BlockSpec((gs, n2, 128), lambda g: (g, 0, 0)),
             pl.BlockSpec((2, gs, n2, c), blk)]
    args = [tabs["hy_fwd"], tabs["hy_twr"], tabs["hy_twi"], x]
    if mode == "scale":
        specs.append(pl.BlockSpec((1, c), lambda g: (0, 0)))
        args.append(extra)
    else:
        specs += [pl.BlockSpec((2, gs, n2, c), lambda g: (0, g, 0, g_col)),
                  pl.BlockSpec((p, p), lambda g: (0, 0))]
        args += [extra, tabs["hy_inv"]]
    return pl.pallas_call(
        functools.partial(_tbmm_kernel, mode=mode, gs=gs),
        grid=(ng // gs,),
        in_specs=specs,
        out_specs=pl.BlockSpec((2, gs, n2, c), blk),
        out_shape=jax.ShapeDtypeStruct(x.shape, out_dtype),
        compiler_params=_params(("parallel",), 48),
        name="dft_inner_" + mode,
    )(*args)


def _angle(num, den):
    return (2.0 * math.pi / den) * jnp.mod(num, den).astype(F32)


def _dft_tables(n_lat):
    nn = 2 * n_lat
    n2 = DFT_N2
    n1 = nn // n2
    i1 = jnp.arange(n1, dtype=jnp.int32)
    a1 = _angle(i1[:, None] * i1[None, :], n1)
    c1, s1 = jnp.cos(a1), jnp.sin(a1)
    t = {}
    t["hy_f1"] = jnp.concatenate([c1, -s1], axis=0).astype(BF16)
    ch, sh = c1[:, :n1 // 2], s1[:, :n1 // 2]
    t["hy_f1c"] = jnp.concatenate([jnp.concatenate([ch, sh], axis=1),
                                   jnp.concatenate([-sh, ch], axis=1)], axis=0).astype(BF16)
    t["hy_i2c"] = jnp.concatenate([jnp.concatenate([ch.T, -sh.T], axis=1),
                                   jnp.concatenate([sh.T, ch.T], axis=1)], axis=0).astype(BF16)
    i2 = jnp.arange(n2, dtype=jnp.int32)
    a2 = _angle(i2[:, None] * i2[None, :], n2)
    c2, s2 = jnp.cos(a2), jnp.sin(a2)
    fwd = jnp.concatenate([jnp.concatenate([c2, s2], axis=1), jnp.concatenate([-s2, c2], axis=1)], axis=0)
    t["hy_fwd"] = fwd.astype(BF16)
    t["hy_inv"] = fwd.T.astype(BF16)
    at = _angle(i1[:, None] * i2[None, :], nn)
    lanes = (n1, n2, 128)
    t["hy_twr"] = jnp.broadcast_to(jnp.cos(at)[:, :, None], lanes)
    t["hy_twi"] = jnp.broadcast_to(-jnp.sin(at)[:, :, None], lanes)
    m2 = FN_N2
    m1 = n_lat // m2
    j1 = jnp.arange(m1, dtype=jnp.int32)
    b1 = _angle(j1[:, None] * j1[None, :], m1)
    cb, sb = jnp.cos(b1), jnp.sin(b1)
    t["fn_s1"] = jnp.concatenate([jnp.concatenate([cb, sb], axis=1),
                                  jnp.concatenate([-sb, cb], axis=1)], axis=0).astype(BF16)
    j2 = jnp.arange(m2, dtype=jnp.int32)
    kf = j1[:, None, None] + m1 * j2[None, :, None]
    b2 = _angle(kf * j2[None, None, :], n_lat)
    norm = 1.0 / math.sqrt(n_lat * FN_GROUP_DIM)
    t["fn_s3"] = (norm * jnp.concatenate([jnp.cos(b2), jnp.sin(b2)], axis=2)).astype(BF16)
    return t


def _small_dft_tables(n):
    nn = 2 * n
    k = jnp.arange(nn, dtype=jnp.int32)
    a = _angle(k[:, None] * k[None, :], nn)
    c, s = jnp.cos(a), jnp.sin(a)
    t = {}
    t["f_full"] = jnp.concatenate([c, -s], axis=0).astype(BF16)
    t["f_half"] = jnp.concatenate([c[:, :n], -s[:, :n]], axis=0).astype(BF16)
    t["i_half"] = jnp.concatenate([c[:n], -s[:n]], axis=1).astype(BF16)
    j = jnp.arange(n, dtype=jnp.int32)
    b = _angle(j[:, None] * j[None, :], n)
    norm = 1.0 / math.sqrt(n * FN_GROUP_DIM)
    t["fn"] = (norm * jnp.concatenate([jnp.cos(b), jnp.sin(b)], axis=1)).astype(BF16)
    return t


def _hyena_long(hy, short_w, fw, bias, tabs):
    b, n, _ = hy.shape
    assert b == 2
    cw = HY_WIDTH
    n2 = DFT_N2
    n1 = 2 * n // n2
    v, x1, x2 = _sconv(hy, short_w, False)
    g, gscale = _padded_filter(n, fw)
    half = HY_ORDER * HY_WIDTH
    ga = _colmm(tabs["hy_f1"], g.reshape(n1, n2 * half), BF16)
    gspec = _tbmm(tabs, ga.reshape(2, n1, n2, half), "scale", gscale * (1.0 / (2 * n)), F32)
    z = v.reshape(n1, n2 * cw)
    for o, xg in enumerate((x1, x2)):
        a = _colmm(tabs["hy_f1c"], z, BF16)
        bm = _tbmm(tabs, a.reshape(2, n1, n2, cw), "conv", gspec, BF16, g_col=o)
        brow = jnp.tile(bias[o], n2)[None, :]
        z = _colmm(tabs["hy_i2c"], bm.reshape(2 * n1, n2 * cw), F32, epi="gate",
                   extra=(xg.reshape(n1, n2 * cw), z, brow))
    return z.reshape(b, n, cw)


def _hyena_short(hy, short_w, fw, bias, tabs):
    b, n, _ = hy.shape
    cw = b * HY_WIDTH
    v, x1, x2 = _sconv(hy, short_w, True)
    g, gscale = _padded_filter(n, fw)
    gspec = _colmm(tabs["f_full"], g, F32, epi="scale", extra=(gscale * (1.0 / (2 * n)),), tn_cap=256)
    z = v
    for o, xg in enumerate((x1, x2)):
        y = _colmm(tabs["f_half"], z, BF16, epi="cmul", extra=(gspec,), tn_cap=HY_WIDTH, g_col=o)
        brow = jnp.tile(bias[o], b)[None, :]
        z = _colmm(tabs["i_half"], y, F32, epi="gate", extra=(xg, z, brow), tn_cap=256)
    return z


def _fn0_kernel(x_ref, w_ref, o_ref):
    acc = _dot(x_ref[0].astype(BF16), w_ref[...])
    o_ref[0] = acc[:, :BRANCH_WIDTH].astype(o_ref.dtype)
    o_ref[1] = acc[:, BRANCH_WIDTH:].astype(o_ref.dtype)


def _fnet_channel_dft(u):
    b, n, w = u.shape
    j = np.arange(FN_GROUP_DIM)
    ang = 2.0 * np.pi * ((j[:, None] * j[None, :]) % FN_GROUP_DIM) / FN_GROUP_DIM
    eye = np.eye(FN_GROUPS)
    wc = np.concatenate([np.kron(eye, np.cos(ang)), -np.kron(eye, np.sin(ang))], axis=1)
    wc = jnp.asarray(wc, F32).astype(BF16)
    tm = _pick_tile(n, 1024, 8)
    return pl.pallas_call(
        _fn0_kernel,
        grid=(b, n // tm),
        in_specs=[pl.BlockSpec((1, tm, w), lambda bb, i: (bb, i, 0)),
                  pl.BlockSpec((w, 2 * w), lambda bb, i: (0, 0))],
        out_specs=pl.BlockSpec((2, tm, w), lambda bb, i: (0, i, bb)),
        out_shape=jax.ShapeDtypeStruct((2, n, b * w), BF16),
        compiler_params=_params(("parallel", "parallel")),
        name="fnet_channel_dft",
    )(u, wc)


def _fnet_long(u, tabs):
    b, n, w = u.shape
    cw = b * w
    m2 = FN_N2
    m1 = n // m2
    wri = _fnet_channel_dft(u)
    a = _colmm(tabs["fn_s1"], wri.reshape(2 * m1, m2 * cw), BF16)
    y = _bmm(tabs["fn_s3"], a.reshape(2, m1, m2, cw), BF16)
    return y.reshape(n, cw)


def _fnet_short(u, tabs):
    b, n, w = u.shape
    wri = _fnet_channel_dft(u)
    return _colmm(tabs["fn"], wri.reshape(2 * n, b * w), BF16, tn_cap=256)


def _merge_kernel(x_ref, g_ref, sc_ref, sh_ref, gt_ref, oa_ref, ob_ref, oc_ref, od_ref,
                  wg_ref, wb_ref, wo_ref, o_ref):
    x = x_ref[0]
    hb = _normmod(x, g_ref[...], sc_ref[0], sh_ref[0]).astype(BF16)
    branches = (oa_ref[0], ob_ref[0], oc_ref[...], od_ref[0])
    acc = jnp.zeros(x.shape, F32)
    for nbr, o in enumerate(branches):
        gate = _dot(hb, wg_ref[:, nbr * D_MODEL:(nbr + 1) * D_MODEL])
        proj = _dot(o.astype(BF16), wb_ref[nbr])
        acc = acc + jax.nn.sigmoid(gate) * proj
    m = _dot(acc.astype(BF16), wo_ref[...])
    o_ref[0] = x + gt_ref[0] * m


def _merge(x, g, sc, sh, gt, oa, ob, oc, od, w):
    b, n, _ = x.shape
    tm = _pick_tile(n, 512, 8)
    tok = lambda bb, i: (bb, i, 0)
    row = lambda bb, i: (bb, 0, 0)
    seq = lambda bb, i: (i, bb)
    c2 = lambda bb, i: (0, 0)
    return pl.pallas_call(
        _merge_kernel,
        grid=(b, n // tm),
        in_specs=[pl.BlockSpec((1, tm, D_MODEL), tok),
                  pl.BlockSpec((1, D_MODEL), c2),
                  pl.BlockSpec((1, 1, D_MODEL), row), pl.BlockSpec((1, 1, D_MODEL), row),
                  pl.BlockSpec((1, 1, D_MODEL), row),
                  pl.BlockSpec((1, tm, BRANCH_WIDTH), tok),
                  pl.BlockSpec((1, tm, BRANCH_WIDTH), tok),
                  pl.BlockSpec((tm, BRANCH_WIDTH), seq),
                  pl.BlockSpec((1, tm, BRANCH_WIDTH), tok),
                  pl.BlockSpec((D_MODEL, N_BRANCH * D_MODEL), c2),
                  pl.BlockSpec((N_BRANCH, BRANCH_WIDTH, D_MODEL), lambda bb, i: (0, 0, 0)),
                  pl.BlockSpec((D_MODEL, D_MODEL), c2)],
        out_specs=pl.BlockSpec((1, tm, D_MODEL), tok),
        out_shape=jax.ShapeDtypeStruct(x.shape, F32),
        compiler_params=_params(("parallel", "parallel"), 56),
        name="branch_merge",
    )(x, g, sc, sh, gt, oa, ob, oc, od, w["wgate"], w["wbranch"], w["wout"])


def _moe_kernel(x_ref, g_ref, sc_ref, sh_ref, gt_ref, wr_ref, br_ref, wg_ref, wu_ref, wd_ref, fg_ref,
                o_ref, h_s, wt_s, acc_s, *, final):
    e = pl.program_id(2)

    @pl.when(e == 0)
    def _():
        hb = _normmod(x_ref[0], g_ref[...], sc_ref[0], sh_ref[0]).astype(BF16)
        h_s[...] = hb
        r = _dot(hb, wr_ref[...]) + br_ref[...]
        lane = lax.broadcasted_iota(jnp.int32, r.shape, 1)
        neg = jnp.float32(-jnp.inf)
        isg = lane < MOE_GROUPS
        gmax = jnp.max(jnp.where(isg, r, neg), axis=1, keepdims=True)
        gsum = jnp.sum(jnp.where(isg, jnp.exp(r - gmax), 0.0), axis=1, keepdims=True)
        g_p = 1.0 / gsum
        gidx = jnp.min(jnp.where(isg & (r == gmax), lane, ROUTER_LANES), axis=1, keepdims=True)
        ise = ((lane >= MOE_GROUPS) & (lane < MOE_GROUPS + MOE_EXPERTS)
               & (jnp.right_shift(lane - MOE_GROUPS, 2) == gidx))
        el = jnp.where(ise, r, neg)
        e1 = jnp.max(el, axis=1, keepdims=True)
        i1 = jnp.min(jnp.where(ise & (r == e1), lane, ROUTER_LANES), axis=1, keepdims=True)
        el2 = jnp.where(lane == i1, neg, el)
        e2 = jnp.max(el2, axis=1, keepdims=True)
        i2 = jnp.min(jnp.where(el2 == e2, lane, ROUTER_LANES), axis=1, keepdims=True)
        d = jnp.exp(e2 - e1)
        w1 = 1.0 / (1.0 + d)
        w2 = d / (1.0 + d)
        wt_s[...] = jnp.where(lane == i1, g_p * w1, jnp.where(lane == i2, g_p * w2, 0.0))
        acc_s[...] = jnp.zeros(acc_s.shape, F32)

    hb = h_s[...]
    wt = wt_s[...]
    lane = lax.broadcasted_iota(jnp.int32, wt.shape, 1)
    acts = []
    for k in range(MOE_PER_GROUP):
        a = _dot(hb, wg_ref[k])
        u = _dot(hb, wu_ref[k])
        col = MOE_GROUPS + e * MOE_PER_GROUP + k
        wcol = jnp.sum(jnp.where(lane == col, wt, 0.0), axis=1, keepdims=True)
        acts.append(((a * jax.nn.sigmoid(a)) * u * wcol).astype(BF16))
    act = jnp.concatenate(acts, axis=1)
    wd = wd_ref[...].reshape(MOE_PER_GROUP * MOE_HIDDEN, D_MODEL)
    acc_s[...] += _dot(act, wd)

    @pl.when(e == MOE_GROUPS - 1)
    def _():
        y = x_ref[0] + gt_ref[0] * acc_s[...]
        if final:
            y = _rms(y) * fg_ref[...]
        o_ref[0] = y


def _moe(x, g, sc, sh, gt, w, fg, final):
    b, n, _ = x.shape
    tm = _pick_tile(n, 1024, 8)
    tok = lambda bb, i, e: (bb, i, 0)
    row = lambda bb, i, e: (bb, 0, 0)
    c2 = lambda bb, i, e: (0, 0)
    ex = lambda bb, i, e: (e, 0, 0)
    return pl.pallas_call(
        functools.partial(_moe_kernel, final=final),
        grid=(b, n // tm, MOE_GROUPS),
        in_specs=[pl.BlockSpec((1, tm, D_MODEL), tok),
                  pl.BlockSpec((1, D_MODEL), c2),
                  pl.BlockSpec((1, 1, D_MODEL), row), pl.BlockSpec((1, 1, D_MODEL), row),
                  pl.BlockSpec((1, 1, D_MODEL), row),
                  pl.BlockSpec((D_MODEL, ROUTER_LANES), c2), pl.BlockSpec((1, ROUTER_LANES), c2),
                  pl.BlockSpec((MOE_PER_GROUP, D_MODEL, MOE_HIDDEN), ex),
                  pl.BlockSpec((MOE_PER_GROUP, D_MODEL, MOE_HIDDEN), ex),
                  pl.BlockSpec((MOE_PER_GROUP, MOE_HIDDEN, D_MODEL), ex),
                  pl.BlockSpec((1, D_MODEL), c2)],
        out_specs=pl.BlockSpec((1, tm, D_MODEL), tok),
        out_shape=jax.ShapeDtypeStruct(x.shape, F32),
        scratch_shapes=[pltpu.VMEM((tm, D_MODEL), BF16), pltpu.VMEM((tm, ROUTER_LANES), F32),
                        pltpu.VMEM((tm, D_MODEL), F32)],
        compiler_params=_params(("parallel", "parallel", "arbitrary"), 56),
        name="hier_moe",
    )(x, g, sc, sh, gt, w["wr"], w["br"], w["wg"], w["wu"], w["wd"], fg)


def _layer_weights(l, w_in, mla_q_norm, mla_w_uq, mla_kv_norm, mla_w_ukv, w_branch, w_out,
                   moe_w_group, moe_b_group, moe_w_expert, moe_b_expert, moe_w_gate, moe_w_up, moe_w_down):
    wi = w_in[l]
    b0 = MLA_Q_RANK
    b1 = b0 + MLA_KV_RANK
    b2 = b1 + MLA_ROPE
    b3 = b2 + HY_IN
    b4 = b3 + BRANCH_WIDTH
    b5 = b4 + 3 * BRANCH_WIDTH
    zpad = lambda r, c: jnp.zeros((r, c), F32)
    w_kr = wi[:, b1:b2]
    krp = jnp.concatenate([zpad(D_MODEL, MLA_NOPE), w_kr[:, 0::2], w_kr[:, 1::2],
                           zpad(D_MODEL, MLA_HEAD_PAD - MLA_NOPE - MLA_ROPE)], axis=1)
    w_na = wi[:, b4:b5]
    w_na = jnp.concatenate([w_na[:, :BRANCH_WIDTH] * NA_SCALE, w_na[:, BRANCH_WIDTH:]], axis=1)
    wa = jnp.concatenate([wi[:, :b1], krp, wi[:, b2:b4], w_na], axis=1).astype(BF16)
    uq = mla_w_uq[l].reshape(MLA_Q_RANK, MLA_HEADS, MLA_NOPE + MLA_ROPE) * (MLA_SCALE * LOG2E)
    uq = jnp.concatenate([uq[..., :MLA_NOPE], uq[..., MLA_NOPE::2], uq[..., MLA_NOPE + 1::2],
                          jnp.zeros((MLA_Q_RANK, MLA_HEADS, MLA_HEAD_PAD - MLA_NOPE - MLA_ROPE), F32)], axis=-1)
    ukv = mla_w_ukv[l].reshape(MLA_KV_RANK, MLA_HEADS, MLA_NOPE + MLA_V)
    uk = jnp.concatenate([ukv[..., :MLA_NOPE],
                          jnp.zeros((MLA_KV_RANK, MLA_HEADS, MLA_HEAD_PAD - MLA_NOPE), F32)], axis=-1)
    uv = ukv[..., MLA_NOPE:]
    wr = jnp.concatenate([moe_w_group[l], moe_w_expert[l],
                          zpad(D_MODEL, ROUTER_LANES - MOE_GROUPS - MOE_EXPERTS)], axis=1)
    br = jnp.concatenate([moe_b_group[l], moe_b_expert[l],
                          jnp.zeros((ROUTER_LANES - MOE_GROUPS - MOE_EXPERTS,), F32)])[None, :]
    return dict(
        wa=wa, qn=mla_q_norm[l][None, :], kvn=mla_kv_norm[l][None, :],
        wuq=uq.reshape(MLA_Q_RANK, -1).astype(BF16), wuk=uk.reshape(MLA_KV_RANK, -1).astype(BF16),
        wuv=uv.reshape(MLA_KV_RANK, -1).astype(BF16),
        wgate=wi[:, b5:].astype(BF16), wbranch=w_branch[l].astype(BF16), wout=w_out[l].astype(BF16),
        wr=wr.astype(BF16), br=br, wg=moe_w_gate[l].astype(BF16), wu=moe_w_up[l].astype(BF16),
        wd=moe_w_down[l].astype(BF16))


def _rope_tables(n):
    t = jnp.arange(n, dtype=jnp.int32)
    row = (t // GRID_W).astype(F32)
    col = (t % GRID_W).astype(F32)
    n_freq = MLA_ROPE // 4
    inv = ROPE_BASE ** (-jnp.arange(n_freq, dtype=F32) / n_freq)
    ang = jnp.concatenate([row[:, None] * inv, col[:, None] * inv], axis=-1)
    c, s = jnp.cos(ang), jnp.sin(ang)
    one = jnp.ones((n, MLA_NOPE), F32)
    zero = jnp.zeros((n, MLA_NOPE), F32)
    hr = MLA_ROPE // 2
    tail1 = jnp.ones((n, MLA_HEAD_PAD - MLA_NOPE - MLA_ROPE), F32)
    tail0 = jnp.zeros((n, MLA_HEAD_PAD - MLA_NOPE - MLA_ROPE), F32)
    cos = jnp.concatenate([one, c, c, tail1], axis=1)
    s1 = jnp.concatenate([zero, -s, jnp.zeros((n, hr), F32), tail0], axis=1)
    s2 = jnp.concatenate([zero, jnp.zeros((n, hr), F32), s, tail0], axis=1)
    return cos, s1, s2


def _identity_rope(n):
    return (jnp.ones((n, MLA_HEAD_PAD), F32), jnp.zeros((n, MLA_HEAD_PAD), F32),
            jnp.zeros((n, MLA_HEAD_PAD), F32))


def kernel(x, c, ctx, c_ctx, ada_w, ada_b, norm1_g, norm2_g, w_in, mla_q_norm, mla_w_uq, mla_kv_norm, mla_w_ukv, hy_short_w, hy_w1, hy_b1, hy_freq1, hy_w2, hy_b2, hy_freq2, hy_w3, hy_bias, na_rpb, w_branch, w_out, moe_w_group, moe_b_group, moe_w_expert, moe_b_expert, moe_w_gate, moe_w_up, moe_w_down, final_norm_g):
    b, n_lat, _ = x.shape
    n_ctx = ctx.shape[1]
    assert b <= 4 and n_lat % (NA_QROWS * GRID_W) == 0 and n_lat // GRID_W >= 2 * NA_QROWS

    cvec = jnp.concatenate([c, c_ctx[None, :], jnp.zeros((8 - b - 1, D_MODEL), F32)], axis=0)
    mod = _mod_vectors(cvec, ada_w, ada_b)

    rope_lat = _rope_tables(n_lat)
    rope_ctx = _identity_rope(n_ctx)
    tabs_lat = _dft_tables(n_lat)
    tabs_ctx = _small_dft_tables(n_ctx)
    fg = final_norm_g[None, :]

    for l in range(DEPTH):
        last = l == DEPTH - 1
        w = _layer_weights(l, w_in, mla_q_norm, mla_w_uq, mla_kv_norm, mla_w_ukv, w_branch, w_out,
                           moe_w_group, moe_b_group, moe_w_expert, moe_b_expert,
                           moe_w_gate, moe_w_up, moe_w_down)
        ml = mod[l, :b].reshape(b, 1, 6, D_MODEL)
        mc = jnp.broadcast_to(mod[l, b].reshape(1, 1, 6, D_MODEL), (b, 1, 6, D_MODEL))
        sh1, sc1, gt1, sh2, sc2, gt2 = [ml[:, :, i] for i in range(6)]
        sh1c, sc1c, gt1c, sh2c, sc2c, gt2c = [mc[:, :, i] for i in range(6)]
        g1 = norm1_g[l][None, :]
        g2 = norm2_g[l][None, :]
        fw = dict(
            w1=jnp.concatenate([hy_w1[l], jnp.zeros((HY_EMB_PAD - HY_EMB, HY_HIDDEN), F32)], axis=0).T.astype(BF16),
            b1=hy_b1[l][:, None], f1=hy_freq1[l][:, None], w2=hy_w2[l].T.astype(BF16), b2=hy_b2[l][:, None],
            f2=hy_freq2[l][:, None],
            w3=jnp.transpose(hy_w3[l].reshape(HY_HIDDEN, 2, HY_ORDER * HY_WIDTH), (1, 2, 0)).astype(BF16))

        n_keys = n_lat + n_ctx
        kv_bufs = (jnp.zeros((b, MLA_HEADS, n_keys, MLA_HEAD_PAD), BF16),
                   jnp.zeros((b, MLA_HEADS, MLA_VT_ROWS, n_keys), PV_DTYPE))
        q, kall, vtall, hy, fn, naq, nak, nav = _inproj(x, g1, sc1, sh1, w, rope_lat, kv_bufs, 0, False)
        qc, kc, vc, kall, vtall, hyc, fnc, naqc, nakc, navc = _inproj(
            ctx, g1, sc1c, sh1c, w, rope_ctx, (kall, vtall), n_lat, True)

        oa = _flash_t(q, kall, vtall, MLA_V)
        od = _na(naq, nak, nav, nakc, navc, _na_bias_tables(na_rpb[l]))
        ob = _hyena_long(hy, hy_short_w[l], fw, hy_bias[l], tabs_lat)
        oc = _fnet_long(fn, tabs_lat)
        x_new = _merge(x, g1, sc1, sh1, gt1, oa, ob, oc, od, w)

        if not last:
            oa_c = _flash(qc, kc, vc, True)
            heads = lambda t: jnp.transpose(t.reshape(b, n_ctx, NA_HEADS, NA_HEAD_DIM), (0, 2, 1, 3))
            od_c = _flash(heads(naqc), heads(nakc), heads(jnp.swapaxes(navc, 1, 2)), False)
            ob_c = _hyena_short(hyc, hy_short_w[l], fw, hy_bias[l], tabs_ctx)
            ob_c = jnp.transpose(ob_c.reshape(n_ctx, b, HY_WIDTH), (1, 0, 2))
            oc_c = _fnet_short(fnc, tabs_ctx)
            ctx = _merge(ctx, g1, sc1c, sh1c, gt1c, oa_c, ob_c, oc_c, od_c, w)
            ctx = _moe(ctx, g2, sc2c, sh2c, gt2c, w, fg, False)
        x = _moe(x_new, g2, sc2, sh2, gt2, w, fg, last)
    return x
```

```python
import functools
import math

import numpy as np
import jax
import jax.numpy as jnp
from jax import lax
from jax.experimental import pallas as pl
from jax.experimental.pallas import tpu as pltpu

F32 = jnp.float32
BF16 = jnp.bfloat16

D_MODEL = 1024
DEPTH = 2
GRID_W = 64
NORM_EPS = 1e-6
MASK_VALUE = -1e30
BRANCH_WIDTH = D_MODEL // 4
N_BRANCH = 4

MLA_HEADS = 4
MLA_Q_RANK = D_MODEL // 4
MLA_KV_RANK = D_MODEL // 8
MLA_NOPE = 64
MLA_ROPE = 32
MLA_V = BRANCH_WIDTH // MLA_HEADS
MLA_SCALE = 1.0 / math.sqrt(MLA_NOPE + MLA_ROPE)
MLA_HEAD_PAD = 128
MLA_VT_ROWS = MLA_V + 16
PV_DTYPE = BF16
ROPE_BASE = 10000.0
LOG2E = 1.4426950408889634

HY_WIDTH = BRANCH_WIDTH
HY_ORDER = 2
HY_BANDS = 16
HY_EMB = 1 + 2 * HY_BANDS
HY_EMB_PAD = 128
HY_HIDDEN = 64
HY_DECAY_TARGET = 1e-2
HY_DECAY_SHORT_PCT = 0.3
HY_DECAY_LONG_PCT = 1.5
HY_IN = (HY_ORDER + 1) * HY_WIDTH

FN_GROUPS = 4
FN_GROUP_DIM = BRANCH_WIDTH // FN_GROUPS

NA_HEADS = 4
NA_HEAD_DIM = BRANCH_WIDTH // NA_HEADS
NA_KH = 8
NA_KW = 16
NA_SCALE = 1.0 / math.sqrt(NA_HEAD_DIM)
NA_QROWS = 8
NA_KROWS = 16
NA_VT_PAD = 16

MOE_GROUPS = 4
MOE_PER_GROUP = 4
MOE_EXPERTS = MOE_GROUPS * MOE_PER_GROUP
MOE_HIDDEN = 256
ROUTER_LANES = 128

MLA_IN = MLA_Q_RANK + MLA_KV_RANK + MLA_ROPE
MIX_IN = MLA_IN + HY_IN + BRANCH_WIDTH + 3 * BRANCH_WIDTH
MIX_PAD = MLA_Q_RANK + MLA_KV_RANK + MLA_HEAD_PAD + HY_IN + BRANCH_WIDTH + 3 * BRANCH_WIDTH

DFT_N2 = 256
FN_N2 = 128
BMM_GROUPS_PER_STEP = 4
FLASH_K_TILE = 3328
FLASH_CHUNKS = 3
FLASH_Q_TILE = 1024
FLASH_Q_SUB = 512


def _params(sem, vmem_mb=None):
    kw = dict(dimension_semantics=sem)
    if vmem_mb is not None:
        kw["vmem_limit_bytes"] = vmem_mb << 20
    return pltpu.CompilerParams(**kw)


def _pick_tile(n, cap, mult):
    best = None
    for t in range(mult, min(n, cap) + 1, mult):
        if n % t == 0:
            best = t
    return best if best is not None else n


def _rms(x):
    return x * lax.rsqrt(jnp.mean(x * x, axis=-1, keepdims=True) + NORM_EPS)


def _normmod(x, g, sc, sh):
    return (_rms(x) * g) * (1.0 + sc) + sh


def _dot(a, b):
    return jnp.dot(a, b, preferred_element_type=F32)


def _dot_nt(a, b):
    return lax.dot_general(a, b, (((1,), (1,)), ((), ())), preferred_element_type=F32)


def _mod_kernel(c_ref, w_ref, b_ref, o_ref):
    c = c_ref[...]
    s = c * jax.nn.sigmoid(c)
    o_ref[0] = _dot(s.astype(BF16), w_ref[0].astype(BF16)) + b_ref[0]


def _mod_vectors(cvec, ada_w, ada_b):
    n6 = ada_w.shape[-1]
    tn = 1024
    return pl.pallas_call(
        _mod_kernel,
        grid=(DEPTH, n6 // tn),
        in_specs=[pl.BlockSpec((8, D_MODEL), lambda l, j: (0, 0)),
                  pl.BlockSpec((1, D_MODEL, tn), lambda l, j: (l, 0, j)),
                  pl.BlockSpec((1, 1, tn), lambda l, j: (l, 0, j))],
        out_specs=pl.BlockSpec((1, 8, tn), lambda l, j: (l, 0, j)),
        out_shape=jax.ShapeDtypeStruct((DEPTH, 8, n6), F32),
        compiler_params=_params(("parallel", "parallel")),
        name="adaln_vectors",
    )(cvec, ada_w, ada_b.reshape(DEPTH, 1, n6))


def _inproj_kernel(*refs, ctx_mode):
    (x_ref, g_ref, sc_ref, sh_ref, wa_ref, qn_ref, kvn_ref, wuq_ref, wuk_ref, wuv_ref,
     cos_ref, s1_ref, s2_ref) = refs[:13]
    if ctx_mode:
        (q_ref, k_ref, v_ref, kall_ref, vtall_ref, hy_ref, fn_ref, naq_ref, nak_ref, nav_ref) = refs[15:]
    else:
        (q_ref, kall_ref, vtall_ref, hy_ref, fn_ref, naq_ref, nak_ref, nav_ref) = refs[15:]
    h = _normmod(x_ref[0], g_ref[...], sc_ref[0], sh_ref[0]).astype(BF16)
    p = _dot(h, wa_ref[...])
    c0 = MLA_Q_RANK
    c1 = c0 + MLA_KV_RANK
    c2 = c1 + MLA_HEAD_PAD
    c3 = c2 + HY_IN
    c4 = c3 + BRANCH_WIDTH
    c5 = c4 + BRANCH_WIDTH
    c6 = c5 + BRANCH_WIDTH
    cq, ckv, krp = p[:, :c0], p[:, c0:c1], p[:, c1:c2]
    hy_ref[0] = p[:, c2:c3]
    fn_ref[0] = p[:, c3:c4]
    naq_ref[0] = p[:, c4:c5].astype(BF16)
    nak_ref[0] = p[:, c5:c6].astype(BF16)
    nav_ref[0] = p[:, c6:].T.astype(BF16)
    cqn = (_rms(cq) * qn_ref[...]).astype(BF16)
    ckn = (_rms(ckv) * kvn_ref[...]).astype(BF16)
    q = _dot(cqn, wuq_ref[...])
    kn = _dot(ckn, wuk_ref[...])
    v = _dot(ckn, wuv_ref[...])
    cos, s1, s2 = cos_ref[...], s1_ref[...], s2_ref[...]

    def rope(t):
        return (t * cos + pltpu.roll(t, MLA_HEAD_PAD - MLA_ROPE // 2, 1) * s1
                + pltpu.roll(t, MLA_ROPE // 2, 1) * s2)

    kr = rope(krp)
    vt = v.T
    tail_rows = lax.broadcasted_iota(jnp.int32, (MLA_VT_ROWS - MLA_V, v.shape[0]), 0)
    tail = jnp.where(tail_rows == 0, 1.0, 0.0).astype(PV_DTYPE)
    for hh in range(MLA_HEADS):
        sl = slice(hh * MLA_HEAD_PAD, (hh + 1) * MLA_HEAD_PAD)
        vs = slice(hh * MLA_V, (hh + 1) * MLA_V)
        kh = (kn[:, sl] + kr).astype(BF16)
        q_ref[0, hh] = rope(q[:, sl]).astype(BF16)
        kall_ref[0, hh] = kh
        vtall_ref[0, hh, :MLA_V] = vt[vs].astype(PV_DTYPE)
        vtall_ref[0, hh, MLA_V:] = tail
        if ctx_mode:
            k_ref[0, hh] = kh
            v_ref[0, hh] = v[:, vs].astype(BF16)


def _inproj(x, g, sc, sh, w, rope_tabs, kv_bufs, key_off, ctx_mode):
    b, n, _ = x.shape
    n_keys = kv_bufs[0].shape[2]
    tm = _pick_tile(n, 512, 128)
    assert key_off % tm == 0
    ko = key_off // tm
    cos, s1, s2 = rope_tabs
    tok = lambda bb, i: (bb, i, 0)
    row = lambda bb, i: (bb, 0, 0)
    const2 = lambda bb, i: (0, 0)
    hd = lambda bb, i: (bb, 0, i, 0)
    tab = lambda bb, i: (i, 0)
    hp = MLA_HEADS * MLA_HEAD_PAD
    in_specs = [pl.BlockSpec((1, tm, D_MODEL), tok),
                pl.BlockSpec((1, D_MODEL), const2),
                pl.BlockSpec((1, 1, D_MODEL), row),
                pl.BlockSpec((1, 1, D_MODEL), row),
                pl.BlockSpec((D_MODEL, MIX_PAD), const2),
                pl.BlockSpec((1, MLA_Q_RANK), const2),
                pl.BlockSpec((1, MLA_KV_RANK), const2),
                pl.BlockSpec((MLA_Q_RANK, hp), const2),
                pl.BlockSpec((MLA_KV_RANK, hp), const2),
                pl.BlockSpec((MLA_KV_RANK, BRANCH_WIDTH), const2),
                pl.BlockSpec((tm, MLA_HEAD_PAD), tab),
                pl.BlockSpec((tm, MLA_HEAD_PAD), tab),
                pl.BlockSpec((tm, MLA_HEAD_PAD), tab)]
    args = [x, g, sc, sh, w["wa"], w["qn"], w["kvn"], w["wuq"], w["wuk"], w["wuv"], cos, s1, s2]
    q_spec = pl.BlockSpec((1, MLA_HEADS, tm, MLA_HEAD_PAD), hd)
    q_sd = jax.ShapeDtypeStruct((b, MLA_HEADS, n, MLA_HEAD_PAD), BF16)
    kall_spec = pl.BlockSpec((1, MLA_HEADS, tm, MLA_HEAD_PAD), lambda bb, i: (bb, 0, ko + i, 0))
    vtall_spec = pl.BlockSpec((1, MLA_HEADS, MLA_VT_ROWS, tm), lambda bb, i: (bb, 0, 0, ko + i))
    kall_sd = jax.ShapeDtypeStruct((b, MLA_HEADS, n_keys, MLA_HEAD_PAD), BF16)
    vtall_sd = jax.ShapeDtypeStruct((b, MLA_HEADS, MLA_VT_ROWS, n_keys), PV_DTYPE)
    rest_specs = [pl.BlockSpec((1, tm, HY_IN), tok)] + [pl.BlockSpec((1, tm, BRANCH_WIDTH), tok)] * 3 \
        + [pl.BlockSpec((1, BRANCH_WIDTH, tm), lambda bb, i: (bb, 0, i))]
    rest_sd = [jax.ShapeDtypeStruct((b, n, HY_IN), F32), jax.ShapeDtypeStruct((b, n, BRANCH_WIDTH), F32)] \
        + [jax.ShapeDtypeStruct((b, n, BRANCH_WIDTH), BF16)] * 2 \
        + [jax.ShapeDtypeStruct((b, BRANCH_WIDTH, n), BF16)]
    in_specs += [pl.BlockSpec(memory_space=pl.ANY), pl.BlockSpec(memory_space=pl.ANY)]
    args += list(kv_bufs)
    if ctx_mode:
        out_specs = [q_spec, q_spec, pl.BlockSpec((1, MLA_HEADS, tm, MLA_V), hd), kall_spec, vtall_spec]
        out_shape = [q_sd, q_sd, jax.ShapeDtypeStruct((b, MLA_HEADS, n, MLA_V), BF16), kall_sd, vtall_sd]
        aliases = {13: 3, 14: 4}
    else:
        out_specs = [q_spec, kall_spec, vtall_spec]
        out_shape = [q_sd, kall_sd, vtall_sd]
        aliases = {13: 1, 14: 2}
    return pl.pallas_call(
        functools.partial(_inproj_kernel, ctx_mode=ctx_mode),
        grid=(b, n // tm),
        in_specs=in_specs,
        out_specs=out_specs + rest_specs,
        out_shape=out_shape + rest_sd,
        input_output_aliases=aliases,
        compiler_params=_params(("parallel", "parallel"), 48),
        name="input_projection",
    )(*args)


def _flash_kernel(q_ref, k_ref, v_ref, o_ref, m_ref, l_ref, acc_ref, *, nh, use_exp2):
    j = pl.program_id(2)

    @pl.when(j == 0)
    def _():
        m_ref[...] = jnp.full(m_ref.shape, MASK_VALUE, F32)
        l_ref[...] = jnp.zeros(l_ref.shape, F32)
        acc_ref[...] = jnp.zeros(acc_ref.shape, F32)

    ex = jnp.exp2 if use_exp2 else jnp.exp
    for h in range(nh):
        s = _dot_nt(q_ref[0, h], k_ref[0, h])
        m_prev = m_ref[h]
        m_cur = jnp.maximum(m_prev, jnp.max(s, axis=1, keepdims=True))
        alpha = ex(m_prev - m_cur)
        p = ex(s - m_cur)
        l_ref[h] = alpha * l_ref[h] + jnp.sum(p, axis=1, keepdims=True)
        acc_ref[h] = alpha * acc_ref[h] + _dot(p.astype(BF16), v_ref[0, h])
        m_ref[h] = m_cur

    @pl.when(j == pl.num_programs(2) - 1)
    def _():
        outs = [acc_ref[h] / l_ref[h] for h in range(nh)]
        o_ref[0] = jnp.concatenate(outs, axis=1).astype(o_ref.dtype)


def _flash(q, k, v, use_exp2):
    b, nh, lq, dk = q.shape
    lk, dv = k.shape[2], v.shape[3]
    tq = _pick_tile(lq, 512, 8)
    tk = _pick_tile(lk, 1280, 128)
    return pl.pallas_call(
        functools.partial(_flash_kernel, nh=nh, use_exp2=use_exp2),
        grid=(b, lq // tq, lk // tk),
        in_specs=[pl.BlockSpec((1, nh, tq, dk), lambda bb, i, j: (bb, 0, i, 0)),
                  pl.BlockSpec((1, nh, tk, dk), lambda bb, i, j: (bb, 0, j, 0)),
                  pl.BlockSpec((1, nh, tk, dv), lambda bb, i, j: (bb, 0, j, 0))],
        out_specs=pl.BlockSpec((1, tq, nh * dv), lambda bb, i, j: (bb, i, 0)),
        out_shape=jax.ShapeDtypeStruct((b, lq, nh * dv), BF16),
        scratch_shapes=[pltpu.VMEM((nh, tq, 1), F32), pltpu.VMEM((nh, tq, 1), F32),
                        pltpu.VMEM((nh, tq, dv), F32)],
        compiler_params=_params(("parallel", "parallel", "arbitrary"), 48),
        name="softmax_attention",
    )(q, k, v)


def _flash_t_kernel(q_ref, k_ref, vt_ref, o_ref, m_ref, acc_ref, st_ref, p_ref, *, nh, dv):
    j = pl.program_id(2)

    @pl.when(j == 0)
    def _():
        m_ref[...] = jnp.full(m_ref.shape, MASK_VALUE, F32)
        acc_ref[...] = jnp.zeros(acc_ref.shape, F32)

    rows = vt_ref.shape[2]

    tq = q_ref.shape[2]
    qs = min(tq, FLASH_Q_SUB)
    units = [(h, c) for h in range(nh) for c in range(tq // qs)]

    def scores(i):
        h, c = units[i]
        cols = slice(c * qs, (c + 1) * qs)
        st = _dot_nt(k_ref[0, h], q_ref[0, h, cols, :])
        st_ref[i % 2] = st
        m_prev = m_ref[h, :, cols]
        m_cur = jnp.maximum(m_prev, jnp.max(st, axis=0, keepdims=True))
        m_ref[h, :, cols] = m_cur
        return m_prev, m_cur

    def probs(i, m_cur):
        p_ref[i % 2] = jnp.exp2(st_ref[i % 2] - m_cur).astype(PV_DTYPE)

    def values(i, m_prev, m_cur):
        h, c = units[i]
        cols = slice(c * qs, (c + 1) * qs)
        alpha = jnp.exp2(m_prev - m_cur)
        acc_ref[h, :rows, cols] = alpha * acc_ref[h, :rows, cols] + _dot(vt_ref[0, h], p_ref[i % 2])

    n_units = len(units)
    tk = k_ref.shape[2]
    kt = tk // 256
    n_chunks = min(FLASH_CHUNKS, kt) if tk % 256 == 0 else 1
    v_edges = [256 * ((kt * c) // n_chunks) for c in range(n_chunks + 1)] if n_chunks > 1 else [0, tk]
    s_edges = [16 * (((tk // 16) * c) // n_chunks) for c in range(n_chunks + 1)]

    def scores_chunk(i, c, m_run):
        h, cc = units[i]
        cols = slice(cc * qs, (cc + 1) * qs)
        r0, r1 = s_edges[c], s_edges[c + 1]
        st = _dot_nt(k_ref[0, h, r0:r1, :], q_ref[0, h, cols, :])
        st_ref[i % 2, r0:r1, :] = st
        return jnp.maximum(m_run, jnp.max(st, axis=0, keepdims=True))

    def values_chunk(i, c):
        h, _ = units[i]
        r0, r1 = v_edges[c], v_edges[c + 1]
        return _dot(vt_ref[0, h, :, r0:r1], p_ref[i % 2, r0:r1, :])

    stats = {0: scores(0)}
    if n_units > 1:
        stats[1] = scores(1)
    probs(0, stats[0][1])
    for i in range(n_units):
        h, cc = units[i]
        cols = slice(cc * qs, (cc + 1) * qs)
        m_prev_i, m_cur_i = stats.pop(i)
        nxt = i + 2 < n_units
        if nxt:
            h2, c2 = units[i + 2]
            cols2 = slice(c2 * qs, (c2 + 1) * qs)
            m_prev2 = m_ref[h2, :, cols2]
            m_run = m_prev2
        pv = None
        for c in range(n_chunks):
            part = values_chunk(i, c)
            pv = part if pv is None else pv + part
            if nxt:
                m_run = scores_chunk(i + 2, c, m_run)
        alpha = jnp.exp2(m_prev_i - m_cur_i)
        acc_ref[h, :rows, cols] = alpha * acc_ref[h, :rows, cols] + pv
        if nxt:
            m_ref[h2, :, cols2] = m_run
            stats[i + 2] = (m_prev2, m_run)
        if i + 1 < n_units:
            probs(i + 1, stats[i + 1][1])

    @pl.when(j == pl.num_programs(2) - 1)
    def _():
        outs = []
        for h in range(nh):
            a = acc_ref[h]
            o = (a / a[dv:dv + 1]).T
            outs.append(o[:, :dv])
        o_ref[0] = jnp.concatenate(outs, axis=1).astype(o_ref.dtype)


def _flash_t(q, k, vt, dv):
    b, nh, lq, dk = q.shape
    lk, rows = k.shape[2], vt.shape[2]
    tq = _pick_tile(lq, FLASH_Q_TILE, 128)
    tk = _pick_tile(lk, FLASH_K_TILE, 128)
    return pl.pallas_call(
        functools.partial(_flash_t_kernel, nh=nh, dv=dv),
        grid=(b, lq // tq, lk // tk),
        in_specs=[pl.BlockSpec((1, nh, tq, dk), lambda bb, i, j: (bb, 0, i, 0)),
                  pl.BlockSpec((1, nh, tk, dk), lambda bb, i, j: (bb, 0, j, 0)),
                  pl.BlockSpec((1, nh, rows, tk), lambda bb, i, j: (bb, 0, 0, j))],
        out_specs=pl.BlockSpec((1, tq, nh * dv), lambda bb, i, j: (bb, i, 0)),
        out_shape=jax.ShapeDtypeStruct((b, lq, nh * dv), BF16),
        scratch_shapes=[pltpu.VMEM((nh, 1, tq), F32), pltpu.VMEM((nh, 128, tq), F32),
                        pltpu.VMEM((2, tk, min(tq, FLASH_Q_SUB)), F32),
                        pltpu.VMEM((2, tk, min(tq, FLASH_Q_SUB)), PV_DTYPE)],
        compiler_params=_params(("parallel", "parallel", "arbitrary"), 56),
        name="softmax_attention_t",
    )(q, k, vt)


def _na_kernel(q_ref, k0, k1, k2, k3, v0, v1, v2, v3, kc_ref, vc_ref, b_ref, o_ref):
    q = q_ref[0]
    kcat = jnp.concatenate([k0[0], k1[0], k2[0], k3[0]], axis=0)
    vtcat = jnp.concatenate([v0[0], v1[0], v2[0], v3[0]], axis=1)
    kc, vtc = kc_ref[0], vc_ref[0]
    dh = NA_HEAD_DIM

    def with_ones(vt):
        pad = lax.broadcasted_iota(jnp.int32, (NA_VT_PAD, vt.shape[1]), 0)
        return jnp.concatenate([vt, jnp.where(pad == 0, 1.0, 0.0).astype(vt.dtype)], axis=0)

    def scores(h):
        sl = slice(h * dh, (h + 1) * dh)
        qh = q[:, sl]
        st = _dot_nt(kcat[:, sl], qh) + b_ref[0, h]
        stc = _dot_nt(kc[:, sl], qh)
        m = jnp.maximum(jnp.max(st, axis=0, keepdims=True), jnp.max(stc, axis=0, keepdims=True))
        return st, stc, m

    def values(h, st, stc, m):
        sl = slice(h * dh, (h + 1) * dh)
        p = jnp.exp(st - m).astype(BF16)
        pc = jnp.exp(stc - m).astype(BF16)
        acc = _dot(with_ones(vtcat[sl]), p) + _dot(with_ones(vtc[sl]), pc)
        o = jnp.concatenate([acc / acc[dh:dh + 1],
                             jnp.zeros((128 - dh - NA_VT_PAD, acc.shape[1]), F32)], axis=0)
        return o.T[:, :dh]

    outs = []
    pending = scores(0)
    for h in range(NA_HEADS):
        nxt = scores(h + 1) if h + 1 < NA_HEADS else None
        outs.append(values(h, *pending))
        pending = nxt
    o_ref[0] = jnp.concatenate(outs, axis=1).astype(o_ref.dtype)


def _na_bias_tables(rpb):
    j = np.arange(NA_QROWS)[:, None]
    m = np.arange(NA_KROWS)[None, :]
    valid = np.zeros((3, NA_QROWS, NA_KROWS), bool)
    st0 = np.maximum(j - NA_KH // 2, 0)
    valid[0] = (m - 4 >= st0) & (m - 4 < st0 + NA_KH)
    valid[1] = (m - j >= 0) & (m - j < NA_KH)
    st2 = 4 + np.minimum(j - NA_KH // 2, 0)
    valid[2] = (m < 12) & (m >= st2) & (m < st2 + NA_KH)
    dr_idx = np.clip(m - 4 - j + NA_KH - 1, 0, 2 * NA_KH - 2)
    qc = np.arange(GRID_W)[:, None]
    kc = np.arange(GRID_W)[None, :]
    dc_idx = np.clip(kc - qc, -(NA_KW - 1), NA_KW - 1) + NA_KW - 1
    cs = np.clip(qc - NA_KW // 2, 0, GRID_W - NA_KW)
    col_ok = (kc >= cs) & (kc < cs + NA_KW)
    t = rpb.astype(F32)[:, dr_idx]
    t = t[..., dc_idx]
    t = jnp.transpose(t, (0, 2, 4, 1, 3))
    ok = (valid.transpose(0, 2, 1)[:, None, :, None, :, None]
          & col_ok.T[None, None, None, :, None, :])
    out = jnp.where(jnp.asarray(ok), t[None], MASK_VALUE)
    return out.reshape(3, NA_HEADS, NA_KROWS * GRID_W, NA_QROWS * GRID_W)


def _na(q, k, vt, kc, vtc, bias):
    b, n, w = q.shape
    c = kc.shape[1]
    tq = NA_QROWS * GRID_W
    tkb = 4 * GRID_W
    nb = n // tq
    nkb = n // tkb
    blocks = [lambda i: jnp.maximum(2 * i - 1, 0), lambda i: 2 * i, lambda i: 2 * i + 1,
              lambda i: jnp.minimum(2 * i + 2, nkb - 1)]
    kspecs = [pl.BlockSpec((1, tkb, w), lambda bb, i, f=f: (bb, f(i), 0)) for f in blocks]
    vspecs = [pl.BlockSpec((1, w, tkb), lambda bb, i, f=f: (bb, 0, f(i))) for f in blocks]
    return pl.pallas_call(
        _na_kernel,
        grid=(b, nb),
        in_specs=[pl.BlockSpec((1, tq, w), lambda bb, i: (bb, i, 0))] + kspecs + vspecs
                 + [pl.BlockSpec((1, c, w), lambda bb, i: (bb, 0, 0)),
                    pl.BlockSpec((1, w, c), lambda bb, i: (bb, 0, 0)),
                    pl.BlockSpec((1, NA_HEADS, NA_KROWS * GRID_W, tq),
                                 lambda bb, i: (jnp.where(i == 0, 0, jnp.where(i == nb - 1, 2, 1)), 0, 0, 0))],
        out_specs=pl.BlockSpec((1, tq, w), lambda bb, i: (bb, i, 0)),
        out_shape=jax.ShapeDtypeStruct((b, n, w), BF16),
        compiler_params=_params(("parallel", "arbitrary"), 56),
        name="neighbourhood_attention",
    )(q, k, k, k, k, vt, vt, vt, vt, kc, vtc, bias)


def _sconv_kernel(u_ref, pv_ref, nx_ref, w_ref, v_ref, x1_ref, x2_ref, *, nt):
    i = pl.program_id(1)
    u = u_ref[0]
    tl = u.shape[0]
    prev = jnp.where(i > 0, pv_ref[0, 7:8, :], 0.0)
    nxt = jnp.where(i < nt - 1, nx_ref[0, 0:1, :], 0.0)
    rows = lax.broadcasted_iota(jnp.int32, u.shape, 0)
    um = jnp.where(rows == 0, prev, pltpu.roll(u, 1, 0))
    up = jnp.where(rows == tl - 1, nxt, pltpu.roll(u, tl - 1, 0))
    w = w_ref[...]
    y = um * w[0:1] + u * w[1:2] + up * w[2:3]
    v_ref[...] = y[:, :HY_WIDTH]
    x1_ref[...] = y[:, HY_WIDTH:2 * HY_WIDTH]
    x2_ref[...] = y[:, 2 * HY_WIDTH:]


def _sconv(u, w, seq_major):
    b, n, _ = u.shape
    tl = _pick_tile(n, 512, 8)
    nt = n // tl
    r8 = tl // 8
    if seq_major:
        osd = jax.ShapeDtypeStruct((n, b * HY_WIDTH), F32)
        ospec = pl.BlockSpec((tl, HY_WIDTH), lambda bb, i: (i, bb))
    else:
        osd = jax.ShapeDtypeStruct((b * n, HY_WIDTH), F32)
        ospec = pl.BlockSpec((tl, HY_WIDTH), lambda bb, i: (bb * nt + i, 0))
    return pl.pallas_call(
        functools.partial(_sconv_kernel, nt=nt),
        grid=(b, nt),
        in_specs=[pl.BlockSpec((1, tl, HY_IN), lambda bb, i: (bb, i, 0)),
                  pl.BlockSpec((1, 8, HY_IN), lambda bb, i: (bb, jnp.maximum(i * r8 - 1, 0), 0)),
                  pl.BlockSpec((1, 8, HY_IN), lambda bb, i: (bb, jnp.minimum((i + 1) * r8, n // 8 - 1), 0)),
                  pl.BlockSpec((3, HY_IN), lambda bb, i: (0, 0))],
        out_specs=[ospec, ospec, ospec],
        out_shape=[osd, osd, osd],
        compiler_params=_params(("parallel", "parallel")),
        name="hyena_short_conv",
    )(u, u, u, w)


def _filt_kernel(z_ref, w1_ref, b1_ref, f1_ref, w2_ref, b2_ref, f2_ref, w3_ref, win_ref, g_ref, ss_ref):
    d = pl.program_id(0)
    i = pl.program_id(1)
    h = jnp.sin(f1_ref[...] * (_dot(w1_ref[...], z_ref[0].astype(BF16)) + b1_ref[...]))
    h = jnp.sin(f2_ref[...] * (_dot(w2_ref[...], h.astype(BF16)) + b2_ref[...]))
    h = _dot(w3_ref[0], h.astype(BF16)).T
    h = h * jnp.concatenate([win_ref[0]] * HY_ORDER, axis=1)
    rows = lax.broadcasted_iota(jnp.int32, h.shape, 0)
    h = jnp.where((rows == 0) & (i == 0) & (d == 1), 0.0, h)
    g_ref[0] = h
    ss = jnp.sum(h * h, axis=0, keepdims=True)
    first = (i == 0) & (d == 0)

    @pl.when(first)
    def _():
        ss_ref[...] = ss

    @pl.when(jnp.logical_not(first))
    def _():
        ss_ref[...] += ss


def _padded_filter(n, fw):
    r = jnp.arange(n, dtype=jnp.int32)
    idx = jnp.stack([r, jnp.mod(n - r, n)], axis=0).astype(F32)[:, :, None]
    t = idx / (n - 1)
    bands = jnp.linspace(1e-4, HY_BANDS - 1, HY_BANDS, dtype=F32)
    ang = (2.0 * math.pi / n) * idx * bands
    z = jnp.concatenate([t, jnp.cos(ang), -jnp.sin(ang), jnp.zeros((2, n, HY_EMB_PAD - HY_EMB), F32)], axis=-1)
    z = jnp.swapaxes(z, 1, 2)
    deltas = jnp.linspace(math.log(HY_DECAY_TARGET) / HY_DECAY_LONG_PCT,
                          math.log(HY_DECAY_TARGET) / HY_DECAY_SHORT_PCT, HY_WIDTH, dtype=F32)
    window = jnp.exp(-t * jnp.abs(deltas))
    tl = _pick_tile(n, 512, 128)
    half = HY_ORDER * HY_WIDTH
    c2 = lambda d, i: (0, 0)
    g, ss = pl.pallas_call(
        _filt_kernel,
        grid=(2, n // tl),
        in_specs=[pl.BlockSpec((1, HY_EMB_PAD, tl), lambda d, i: (d, 0, i)),
                  pl.BlockSpec((HY_HIDDEN, HY_EMB_PAD), c2), pl.BlockSpec((HY_HIDDEN, 1), c2),
                  pl.BlockSpec((HY_HIDDEN, 1), c2),
                  pl.BlockSpec((HY_HIDDEN, HY_HIDDEN), c2), pl.BlockSpec((HY_HIDDEN, 1), c2),
                  pl.BlockSpec((HY_HIDDEN, 1), c2),
                  pl.BlockSpec((1, half, HY_HIDDEN), lambda d, i: (d, 0, 0)),
                  pl.BlockSpec((1, tl, HY_WIDTH), lambda d, i: (d, i, 0))],
        out_specs=[pl.BlockSpec((1, tl, half), lambda d, i: (d, i, 0)), pl.BlockSpec((1, half), c2)],
        out_shape=[jax.ShapeDtypeStruct((2, n, half), F32), jax.ShapeDtypeStruct((1, half), F32)],
        compiler_params=_params(("arbitrary", "arbitrary")),
        name="hyena_filter_mlp",
    )(z, fw["w1"], fw["b1"], fw["f1"], fw["w2"], fw["b2"], fw["f2"], fw["w3"], window)
    return g.reshape(2 * n, half), lax.rsqrt(ss + NORM_EPS)


def _colmm_kernel(*refs, epi):
    w_ref, x_ref = refs[0], refs[1]
    o_ref = refs[-1]
    acc = _dot(w_ref[...], x_ref[...].astype(BF16))
    if epi == "gate":
        xg_ref, z_ref, b_ref = refs[2:5]
        acc = xg_ref[...] * (acc + z_ref[...] * b_ref[...])
    elif epi == "scale":
        acc = acc * refs[2][...]
    elif epi == "cmul":
        g = refs[2][...]
        mh = acc.shape[0] // 2
        ar, ai, gr, gi = acc[:mh], acc[mh:], g[:mh], g[mh:]
        acc = jnp.concatenate([ar * gr - ai * gi, ar * gi + ai * gr], axis=0)
    o_ref[...] = acc.astype(o_ref.dtype)


def _colmm(w, x, out_dtype, epi="none", extra=(), tn_cap=4096, g_col=None):
    m, k = w.shape
    nc = x.shape[1]
    tn = _pick_tile(nc, tn_cap, 128)
    col = lambda j: (0, j)
    specs = [pl.BlockSpec((m, k), lambda j: (0, 0)), pl.BlockSpec((k, tn), col)]
    if epi == "gate":
        specs += [pl.BlockSpec((m, tn), col), pl.BlockSpec((m, tn), col), pl.BlockSpec((1, tn), col)]
    elif epi == "scale":
        specs += [pl.BlockSpec((1, tn), col)]
    elif epi == "cmul":
        specs += [pl.BlockSpec((m, tn), lambda j: (0, g_col))]
    return pl.pallas_call(
        functools.partial(_colmm_kernel, epi=epi),
        grid=(nc // tn,),
        in_specs=specs,
        out_specs=pl.BlockSpec((m, tn), col),
        out_shape=jax.ShapeDtypeStruct((m, nc), out_dtype),
        compiler_params=_params(("parallel",), 48),
        name="dft_stage_" + epi,
    )(w, x, *extra)


def _bmm_kernel(m_ref, x_ref, o_ref, *, gs):
    for s in range(gs):
        x = x_ref[:, s]
        x = x.reshape(x.shape[0] * x.shape[1], x.shape[2])
        acc = _dot(m_ref[s], x)
        c = acc.shape[1]
        o_ref[:, s * c:(s + 1) * c] = acc.astype(o_ref.dtype)


def _bmm(mats, x, out_dtype):
    ng, p, qq = mats.shape
    c = x.shape[3]
    gs = _pick_tile(ng, BMM_GROUPS_PER_STEP, 1)
    return pl.pallas_call(
        functools.partial(_bmm_kernel, gs=gs),
        grid=(ng // gs,),
        in_specs=[pl.BlockSpec((gs, p, qq), lambda g: (g, 0, 0)),
                  pl.BlockSpec((2, gs, qq // 2, c), lambda g: (0, g, 0, 0))],
        out_specs=pl.BlockSpec((p, gs * c), lambda g: (0, g)),
        out_shape=jax.ShapeDtypeStruct((p, ng * c), out_dtype),
        compiler_params=_params(("parallel",), 48),
        name="dft_batched_real",
    )(mats, x)


def _tbmm_kernel(*refs, mode, gs):
    f_ref, twr_ref, twi_ref, x_ref = refs[:4]
    o_ref = refs[-1]
    for s in range(gs):
        xr, xi = x_ref[0, s].astype(F32), x_ref[1, s].astype(F32)
        reps = xr.shape[1] // twr_ref.shape[2]
        tr = jnp.concatenate([twr_ref[s]] * reps, axis=1)
        ti = jnp.concatenate([twi_ref[s]] * reps, axis=1)
        x = jnp.concatenate([xr * tr - xi * ti, xr * ti + xi * tr], axis=0).astype(BF16)
        acc = _dot(f_ref[...], x)
        ph = acc.shape[0] // 2
        ar, ai = acc[:ph], acc[ph:]
        if mode == "scale":
            sc = refs[4][...]
            o_ref[0, s] = ar * sc
            o_ref[1, s] = ai * sc
        else:
            g_ref, fi_ref = refs[4], refs[5]
            gr, gi = g_ref[0, s], g_ref[1, s]
            y = jnp.concatenate([ar * gr - ai * gi, ar * gi + ai * gr], axis=0).astype(BF16)
            back = _dot(fi_ref[...], y)
            br, bi = back[:ph], back[ph:]
            o_ref[0, s] = (br * tr + bi * ti).astype(o_ref.dtype)
            o_ref[1, s] = (bi * tr - br * ti).astype(o_ref.dtype)


def _tbmm(tabs, x, mode, extra, out_dtype, g_col=0):
    _, ng, n2, c = x.shape
    gs = _pick_tile(ng, BMM_GROUPS_PER_STEP, 1)
    p = 2 * n2
    blk = lambda g: (0, g, 0, 0)
    specs = [pl.BlockSpec((p, p), lambda g: (0, 0)),
             pl.BlockSpec((gs, n2, 128), lambda g: (g, 0, 0)),
             pl.BlockSpec((gs, n2, 128), lambda g: (g, 0, 0)),
             pl.BlockSpec((2, gs, n2, c), blk)]
    args = [tabs["hy_fwd"], tabs["hy_twr"], tabs["hy_twi"], x]
    if mode == "scale":
        specs.append(pl.BlockSpec((1, c), lambda g: (0, 0)))
        args.append(extra)
    else:
        specs += [pl.BlockSpec((2, gs, n2, c), lambda g: (0, g, 0, g_col)),
                  pl.BlockSpec((p, p), lambda g: (0, 0))]
        args += [extra, tabs["hy_inv"]]
    return pl.pallas_call(
        functools.partial(_tbmm_kernel, mode=mode, gs=gs),
        grid=(ng // gs,),
        in_specs=specs,
        out_specs=pl.BlockSpec((2, gs, n2, c), blk),
        out_shape=jax.ShapeDtypeStruct(x.shape, out_dtype),
        compiler_params=_params(("parallel",), 48),
        name="dft_inner_" + mode,
    )(*args)


def _angle(num, den):
    return (2.0 * math.pi / den) * jnp.mod(num, den).astype(F32)


def _dft_tables(n_lat):
    nn = 2 * n_lat
    n2 = DFT_N2
    n1 = nn // n2
    i1 = jnp.arange(n1, dtype=jnp.int32)
    a1 = _angle(i1[:, None] * i1[None, :], n1)
    c1, s1 = jnp.cos(a1), jnp.sin(a1)
    t = {}
    t["hy_f1"] = jnp.concatenate([c1, -s1], axis=0).astype(BF16)
    ch, sh = c1[:, :n1 // 2], s1[:, :n1 // 2]
    t["hy_f1c"] = jnp.concatenate([jnp.concatenate([ch, sh], axis=1),
                                   jnp.concatenate([-sh, ch], axis=1)], axis=0).astype(BF16)
    t["hy_i2c"] = jnp.concatenate([jnp.concatenate([ch.T, -sh.T], axis=1),
                                   jnp.concatenate([sh.T, ch.T], axis=1)], axis=0).astype(BF16)
    i2 = jnp.arange(n2, dtype=jnp.int32)
    a2 = _angle(i2[:, None] * i2[None, :], n2)
    c2, s2 = jnp.cos(a2), jnp.sin(a2)
    fwd = jnp.concatenate([jnp.concatenate([c2, s2], axis=1), jnp.concatenate([-s2, c2], axis=1)], axis=0)
    t["hy_fwd"] = fwd.astype(BF16)
    t["hy_inv"] = fwd.T.astype(BF16)
    at = _angle(i1[:, None] * i2[None, :], nn)
    lanes = (n1, n2, 128)
    t["hy_twr"] = jnp.broadcast_to(jnp.cos(at)[:, :, None], lanes)
    t["hy_twi"] = jnp.broadcast_to(-jnp.sin(at)[:, :, None], lanes)
    m2 = FN_N2
    m1 = n_lat // m2
    j1 = jnp.arange(m1, dtype=jnp.int32)
    b1 = _angle(j1[:, None] * j1[None, :], m1)
    cb, sb = jnp.cos(b1), jnp.sin(b1)
    t["fn_s1"] = jnp.concatenate([jnp.concatenate([cb, sb], axis=1),
                                  jnp.concatenate([-sb, cb], axis=1)], axis=0).astype(BF16)
    j2 = jnp.arange(m2, dtype=jnp.int32)
    kf = j1[:, None, None] + m1 * j2[None, :, None]
    b2 = _angle(kf * j2[None, None, :], n_lat)
    norm = 1.0 / math.sqrt(n_lat * FN_GROUP_DIM)
    t["fn_s3"] = (norm * jnp.concatenate([jnp.cos(b2), jnp.sin(b2)], axis=2)).astype(BF16)
    return t


def _small_dft_tables(n):
    nn = 2 * n
    k = jnp.arange(nn, dtype=jnp.int32)
    a = _angle(k[:, None] * k[None, :], nn)
    c, s = jnp.cos(a), jnp.sin(a)
    t = {}
    t["f_full"] = jnp.concatenate([c, -s], axis=0).astype(BF16)
    t["f_half"] = jnp.concatenate([c[:, :n], -s[:, :n]], axis=0).astype(BF16)
    t["i_half"] = jnp.concatenate([c[:n], -s[:n]], axis=1).astype(BF16)
    j = jnp.arange(n, dtype=jnp.int32)
    b = _angle(j[:, None] * j[None, :], n)
    norm = 1.0 / math.sqrt(n * FN_GROUP_DIM)
    t["fn"] = (norm * jnp.concatenate([jnp.cos(b), jnp.sin(b)], axis=1)).astype(BF16)
    return t


def _hyena_long(hy, short_w, fw, bias, tabs):
    b, n, _ = hy.shape
    assert b == 2
    cw = HY_WIDTH
    n2 = DFT_N2
    n1 = 2 * n // n2
    v, x1, x2 = _sconv(hy, short_w, False)
    g, gscale = _padded_filter(n, fw)
    half = HY_ORDER * HY_WIDTH
    ga = _colmm(tabs["hy_f1"], g.reshape(n1, n2 * half), BF16)
    gspec = _tbmm(tabs, ga.reshape(2, n1, n2, half), "scale", gscale * (1.0 / (2 * n)), F32)
    z = v.reshape(n1, n2 * cw)
    for o, xg in enumerate((x1, x2)):
        a = _colmm(tabs["hy_f1c"], z, BF16)
        bm = _tbmm(tabs, a.reshape(2, n1, n2, cw), "conv", gspec, BF16, g_col=o)
        brow = jnp.tile(bias[o], n2)[None, :]
        z = _colmm(tabs["hy_i2c"], bm.reshape(2 * n1, n2 * cw), F32, epi="gate",
                   extra=(xg.reshape(n1, n2 * cw), z, brow))
    return z.reshape(b, n, cw)


def _hyena_short(hy, short_w, fw, bias, tabs):
    b, n, _ = hy.shape
    cw = b * HY_WIDTH
    v, x1, x2 = _sconv(hy, short_w, True)
    g, gscale = _padded_filter(n, fw)
    gspec = _colmm(tabs["f_full"], g, F32, epi="scale", extra=(gscale * (1.0 / (2 * n)),), tn_cap=256)
    z = v
    for o, xg in enumerate((x1, x2)):
        y = _colmm(tabs["f_half"], z, BF16, epi="cmul", extra=(gspec,), tn_cap=HY_WIDTH, g_col=o)
        brow = jnp.tile(bias[o], b)[None, :]
        z = _colmm(tabs["i_half"], y, F32, epi="gate", extra=(xg, z, brow), tn_cap=256)
    return z


def _fn0_kernel(x_ref, w_ref, o_ref):
    acc = _dot(x_ref[0].astype(BF16), w_ref[...])
    o_ref[0] = acc[:, :BRANCH_WIDTH].astype(o_ref.dtype)
    o_ref[1] = acc[:, BRANCH_WIDTH:].astype(o_ref.dtype)


def _fnet_channel_dft(u):
    b, n, w = u.shape
    j = np.arange(FN_GROUP_DIM)
    ang = 2.0 * np.pi * ((j[:, None] * j[None, :]) % FN_GROUP_DIM) / FN_GROUP_DIM
    eye = np.eye(FN_GROUPS)
    wc = np.concatenate([np.kron(eye, np.cos(ang)), -np.kron(eye, np.sin(ang))], axis=1)
    wc = jnp.asarray(wc, F32).astype(BF16)
    tm = _pick_tile(n, 1024, 8)
    return pl.pallas_call(
        _fn0_kernel,
        grid=(b, n // tm),
        in_specs=[pl.BlockSpec((1, tm, w), lambda bb, i: (bb, i, 0)),
                  pl.BlockSpec((w, 2 * w), lambda bb, i: (0, 0))],
        out_specs=pl.BlockSpec((2, tm, w), lambda bb, i: (0, i, bb)),
        out_shape=jax.ShapeDtypeStruct((2, n, b * w), BF16),
        compiler_params=_params(("parallel", "parallel")),
        name="fnet_channel_dft",
    )(u, wc)


def _fnet_long(u, tabs):
    b, n, w = u.shape
    cw = b * w
    m2 = FN_N2
    m1 = n // m2
    wri = _fnet_channel_dft(u)
    a = _colmm(tabs["fn_s1"], wri.reshape(2 * m1, m2 * cw), BF16)
    y = _bmm(tabs["fn_s3"], a.reshape(2, m1, m2, cw), BF16)
    return y.reshape(n, cw)


def _fnet_short(u, tabs):
    b, n, w = u.shape
    wri = _fnet_channel_dft(u)
    return _colmm(tabs["fn"], wri.reshape(2 * n, b * w), BF16, tn_cap=256)


def _merge_kernel(x_ref, g_ref, sc_ref, sh_ref, gt_ref, oa_ref, ob_ref, oc_ref, od_ref,
                  wg_ref, wb_ref, wo_ref, o_ref):
    x = x_ref[0]
    hb = _normmod(x, g_ref[...], sc_ref[0], sh_ref[0]).astype(BF16)
    branches = (oa_ref[0], ob_ref[0], oc_ref[...], od_ref[0])
    acc = jnp.zeros(x.shape, F32)
    for nbr, o in enumerate(branches):
        gate = _dot(hb, wg_ref[:, nbr * D_MODEL:(nbr + 1) * D_MODEL])
        proj = _dot(o.astype(BF16), wb_ref[nbr])
        acc = acc + jax.nn.sigmoid(gate) * proj
    m = _dot(acc.astype(BF16), wo_ref[...])
    o_ref[0] = x + gt_ref[0] * m


def _merge(x, g, sc, sh, gt, oa, ob, oc, od, w):
    b, n, _ = x.shape
    tm = _pick_tile(n, 512, 8)
    tok = lambda bb, i: (bb, i, 0)
    row = lambda bb, i: (bb, 0, 0)
    seq = lambda bb, i: (i, bb)
    c2 = lambda bb, i: (0, 0)
    return pl.pallas_call(
        _merge_kernel,
        grid=(b, n // tm),
        in_specs=[pl.BlockSpec((1, tm, D_MODEL), tok),
                  pl.BlockSpec((1, D_MODEL), c2),
                  pl.BlockSpec((1, 1, D_MODEL), row), pl.BlockSpec((1, 1, D_MODEL), row),
                  pl.BlockSpec((1, 1, D_MODEL), row),
                  pl.BlockSpec((1, tm, BRANCH_WIDTH), tok),
                  pl.BlockSpec((1, tm, BRANCH_WIDTH), tok),
                  pl.BlockSpec((tm, BRANCH_WIDTH), seq),
                  pl.BlockSpec((1, tm, BRANCH_WIDTH), tok),
                  pl.BlockSpec((D_MODEL, N_BRANCH * D_MODEL), c2),
                  pl.BlockSpec((N_BRANCH, BRANCH_WIDTH, D_MODEL), lambda bb, i: (0, 0, 0)),
                  pl.BlockSpec((D_MODEL, D_MODEL), c2)],
        out_specs=pl.BlockSpec((1, tm, D_MODEL), tok),
        out_shape=jax.ShapeDtypeStruct(x.shape, F32),
        compiler_params=_params(("parallel", "parallel"), 56),
        name="branch_merge",
    )(x, g, sc, sh, gt, oa, ob, oc, od, w["wgate"], w["wbranch"], w["wout"])


def _moe_kernel(x_ref, g_ref, sc_ref, sh_ref, gt_ref, wr_ref, br_ref, wg_ref, wu_ref, wd_ref, fg_ref,
                o_ref, h_s, wt_s, acc_s, *, final):
    e = pl.program_id(2)

    @pl.when(e == 0)
    def _():
        hb = _normmod(x_ref[0], g_ref[...], sc_ref[0], sh_ref[0]).astype(BF16)
        h_s[...] = hb
        r = _dot(hb, wr_ref[...]) + br_ref[...]
        lane = lax.broadcasted_iota(jnp.int32, r.shape, 1)
        neg = jnp.float32(-jnp.inf)
        isg = lane < MOE_GROUPS
        gmax = jnp.max(jnp.where(isg, r, neg), axis=1, keepdims=True)
        gsum = jnp.sum(jnp.where(isg, jnp.exp(r - gmax), 0.0), axis=1, keepdims=True)
        g_p = 1.0 / gsum
        gidx = jnp.min(jnp.where(isg & (r == gmax), lane, ROUTER_LANES), axis=1, keepdims=True)
        ise = ((lane >= MOE_GROUPS) & (lane < MOE_GROUPS + MOE_EXPERTS)
               & (jnp.right_shift(lane - MOE_GROUPS, 2) == gidx))
        el = jnp.where(ise, r, neg)
        e1 = jnp.max(el, axis=1, keepdims=True)
        i1 = jnp.min(jnp.where(ise & (r == e1), lane, ROUTER_LANES), axis=1, keepdims=True)
        el2 = jnp.where(lane == i1, neg, el)
        e2 = jnp.max(el2, axis=1, keepdims=True)
        i2 = jnp.min(jnp.where(el2 == e2, lane, ROUTER_LANES), axis=1, keepdims=True)
        d = jnp.exp(e2 - e1)
        w1 = 1.0 / (1.0 + d)
        w2 = d / (1.0 + d)
        wt_s[...] = jnp.where(lane == i1, g_p * w1, jnp.where(lane == i2, g_p * w2, 0.0))
        acc_s[...] = jnp.zeros(acc_s.shape, F32)

    hb = h_s[...]
    wt = wt_s[...]
    lane = lax.broadcasted_iota(jnp.int32, wt.shape, 1)
    acts = []
    for k in range(MOE_PER_GROUP):
        a = _dot(hb, wg_ref[k])
        u = _dot(hb, wu_ref[k])
        col = MOE_GROUPS + e * MOE_PER_GROUP + k
        wcol = jnp.sum(jnp.where(lane == col, wt, 0.0), axis=1, keepdims=True)
        acts.append(((a * jax.nn.sigmoid(a)) * u * wcol).astype(BF16))
    act = jnp.concatenate(acts, axis=1)
    wd = wd_ref[...].reshape(MOE_PER_GROUP * MOE_HIDDEN, D_MODEL)
    acc_s[...] += _dot(act, wd)

    @pl.when(e == MOE_GROUPS - 1)
    def _():
        y = x_ref[0] + gt_ref[0] * acc_s[...]
        if final:
            y = _rms(y) * fg_ref[...]
        o_ref[0] = y


def _moe(x, g, sc, sh, gt, w, fg, final):
    b, n, _ = x.shape
    tm = _pick_tile(n, 1024, 8)
    tok = lambda bb, i, e: (bb, i, 0)
    row = lambda bb, i, e: (bb, 0, 0)
    c2 = lambda bb, i, e: (0, 0)
    ex = lambda bb, i, e: (e, 0, 0)
    return pl.pallas_call(
        functools.partial(_moe_kernel, final=final),
        grid=(b, n // tm, MOE_GROUPS),
        in_specs=[pl.BlockSpec((1, tm, D_MODEL), tok),
                  pl.BlockSpec((1, D_MODEL), c2),
                  pl.BlockSpec((1, 1, D_MODEL), row), pl.BlockSpec((1, 1, D_MODEL), row),
                  pl.BlockSpec((1, 1, D_MODEL), row),
                  pl.BlockSpec((D_MODEL, ROUTER_LANES), c2), pl.BlockSpec((1, ROUTER_LANES), c2),
                  pl.BlockSpec((MOE_PER_GROUP, D_MODEL, MOE_HIDDEN), ex),
                  pl.BlockSpec((MOE_PER_GROUP, D_MODEL, MOE_HIDDEN), ex),
                  pl.BlockSpec((MOE_PER_GROUP, MOE_HIDDEN, D_MODEL), ex),
                  pl.BlockSpec((1, D_MODEL), c2)],
        out_specs=pl.BlockSpec((1, tm, D_MODEL), tok),
        out_shape=jax.ShapeDtypeStruct(x.shape, F32),
        scratch_shapes=[pltpu.VMEM((tm, D_MODEL), BF16), pltpu.VMEM((tm, ROUTER_LANES), F32),
                        pltpu.VMEM((tm, D_MODEL), F32)],
        compiler_params=_params(("parallel", "parallel", "arbitrary"), 56),
        name="hier_moe",
    )(x, g, sc, sh, gt, w["wr"], w["br"], w["wg"], w["wu"], w["wd"], fg)


def _layer_weights(l, w_in, mla_q_norm, mla_w_uq, mla_kv_norm, mla_w_ukv, w_branch, w_out,
                   moe_w_group, moe_b_group, moe_w_expert, moe_b_expert, moe_w_gate, moe_w_up, moe_w_down):
    wi = w_in[l]
    b0 = MLA_Q_RANK
    b1 = b0 + MLA_KV_RANK
    b2 = b1 + MLA_ROPE
    b3 = b2 + HY_IN
    b4 = b3 + BRANCH_WIDTH
    b5 = b4 + 3 * BRANCH_WIDTH
    zpad = lambda r, c: jnp.zeros((r, c), F32)
    w_kr = wi[:, b1:b2]
    krp = jnp.concatenate([zpad(D_MODEL, MLA_NOPE), w_kr[:, 0::2], w_kr[:, 1::2],
                           zpad(D_MODEL, MLA_HEAD_PAD - MLA_NOPE - MLA_ROPE)], axis=1)
    w_na = wi[:, b4:b5]
    w_na = jnp.concatenate([w_na[:, :BRANCH_WIDTH] * NA_SCALE, w_na[:, BRANCH_WIDTH:]], axis=1)
    wa = jnp.concatenate([wi[:, :b1], krp, wi[:, b2:b4], w_na], axis=1).astype(BF16)
    uq = mla_w_uq[l].reshape(MLA_Q_RANK, MLA_HEADS, MLA_NOPE + MLA_ROPE) * (MLA_SCALE * LOG2E)
    uq = jnp.concatenate([uq[..., :MLA_NOPE], uq[..., MLA_NOPE::2], uq[..., MLA_NOPE + 1::2],
                          jnp.zeros((MLA_Q_RANK, MLA_HEADS, MLA_HEAD_PAD - MLA_NOPE - MLA_ROPE), F32)], axis=-1)
    ukv = mla_w_ukv[l].reshape(MLA_KV_RANK, MLA_HEADS, MLA_NOPE + MLA_V)
    uk = jnp.concatenate([ukv[..., :MLA_NOPE],
                          jnp.zeros((MLA_KV_RANK, MLA_HEADS, MLA_HEAD_PAD - MLA_NOPE), F32)], axis=-1)
    uv = ukv[..., MLA_NOPE:]
    wr = jnp.concatenate([moe_w_group[l], moe_w_expert[l],
                          zpad(D_MODEL, ROUTER_LANES - MOE_GROUPS - MOE_EXPERTS)], axis=1)
    br = jnp.concatenate([moe_b_group[l], moe_b_expert[l],
                          jnp.zeros((ROUTER_LANES - MOE_GROUPS - MOE_EXPERTS,), F32)])[None, :]
    return dict(
        wa=wa, qn=mla_q_norm[l][None, :], kvn=mla_kv_norm[l][None, :],
        wuq=uq.reshape(MLA_Q_RANK, -1).astype(BF16), wuk=uk.reshape(MLA_KV_RANK, -1).astype(BF16),
        wuv=uv.reshape(MLA_KV_RANK, -1).astype(BF16),
        wgate=wi[:, b5:].astype(BF16), wbranch=w_branch[l].astype(BF16), wout=w_out[l].astype(BF16),
        wr=wr.astype(BF16), br=br, wg=moe_w_gate[l].astype(BF16), wu=moe_w_up[l].astype(BF16),
        wd=moe_w_down[l].astype(BF16))


def _rope_tables(n):
    t = jnp.arange(n, dtype=jnp.int32)
    row = (t // GRID_W).astype(F32)
    col = (t % GRID_W).astype(F32)
    n_freq = MLA_ROPE // 4
    inv = ROPE_BASE ** (-jnp.arange(n_freq, dtype=F32) / n_freq)
    ang = jnp.concatenate([row[:, None] * inv, col[:, None] * inv], axis=-1)
    c, s = jnp.cos(ang), jnp.sin(ang)
    one = jnp.ones((n, MLA_NOPE), F32)
    zero = jnp.zeros((n, MLA_NOPE), F32)
    hr = MLA_ROPE // 2
    tail1 = jnp.ones((n, MLA_HEAD_PAD - MLA_NOPE - MLA_ROPE), F32)
    tail0 = jnp.zeros((n, MLA_HEAD_PAD - MLA_NOPE - MLA_ROPE), F32)
    cos = jnp.concatenate([one, c, c, tail1], axis=1)
    s1 = jnp.concatenate([zero, -s, jnp.zeros((n, hr), F32), tail0], axis=1)
    s2 = jnp.concatenate([zero, jnp.zeros((n, hr), F32), s, tail0], axis=1)
    return cos, s1, s2


def _identity_rope(n):
    return (jnp.ones((n, MLA_HEAD_PAD), F32), jnp.zeros((n, MLA_HEAD_PAD), F32),
            jnp.zeros((n, MLA_HEAD_PAD), F32))


def kernel(x, c, ctx, c_ctx, ada_w, ada_b, norm1_g, norm2_g, w_in, mla_q_norm, mla_w_uq, mla_kv_norm, mla_w_ukv, hy_short_w, hy_w1, hy_b1, hy_freq1, hy_w2, hy_b2, hy_freq2, hy_w3, hy_bias, na_rpb, w_branch, w_out, moe_w_group, moe_b_group, moe_w_expert, moe_b_expert, moe_w_gate, moe_w_up, moe_w_down, final_norm_g):
    b, n_lat, _ = x.shape
    n_ctx = ctx.shape[1]
    assert b <= 4 and n_lat % (NA_QROWS * GRID_W) == 0 and n_lat // GRID_W >= 2 * NA_QROWS

    cvec = jnp.concatenate([c, c_ctx[None, :], jnp.zeros((8 - b - 1, D_MODEL), F32)], axis=0)
    mod = _mod_vectors(cvec, ada_w, ada_b)

    rope_lat = _rope_tables(n_lat)
    rope_ctx = _identity_rope(n_ctx)
    tabs_lat = _dft_tables(n_lat)
    tabs_ctx = _small_dft_tables(n_ctx)
    fg = final_norm_g[None, :]

    for l in range(DEPTH):
        last = l == DEPTH - 1
        w = _layer_weights(l, w_in, mla_q_norm, mla_w_uq, mla_kv_norm, mla_w_ukv, w_branch, w_out,
                           moe_w_group, moe_b_group, moe_w_expert, moe_b_expert,
                           moe_w_gate, moe_w_up, moe_w_down)
        ml = mod[l, :b].reshape(b, 1, 6, D_MODEL)
        mc = jnp.broadcast_to(mod[l, b].reshape(1, 1, 6, D_MODEL), (b, 1, 6, D_MODEL))
        sh1, sc1, gt1, sh2, sc2, gt2 = [ml[:, :, i] for i in range(6)]
        sh1c, sc1c, gt1c, sh2c, sc2c, gt2c = [mc[:, :, i] for i in range(6)]
        g1 = norm1_g[l][None, :]
        g2 = norm2_g[l][None, :]
        fw = dict(
            w1=jnp.concatenate([hy_w1[l], jnp.zeros((HY_EMB_PAD - HY_EMB, HY_HIDDEN), F32)], axis=0).T.astype(BF16),
            b1=hy_b1[l][:, None], f1=hy_freq1[l][:, None], w2=hy_w2[l].T.astype(BF16), b2=hy_b2[l][:, None],
            f2=hy_freq2[l][:, None],
            w3=jnp.transpose(hy_w3[l].reshape(HY_HIDDEN, 2, HY_ORDER * HY_WIDTH), (1, 2, 0)).astype(BF16))

        n_keys = n_lat + n_ctx
        kv_bufs = (jnp.zeros((b, MLA_HEADS, n_keys, MLA_HEAD_PAD), BF16),
                   jnp.zeros((b, MLA_HEADS, MLA_VT_ROWS, n_keys), PV_DTYPE))
        q, kall, vtall, hy, fn, naq, nak, nav = _inproj(x, g1, sc1, sh1, w, rope_lat, kv_bufs, 0, False)
        qc, kc, vc, kall, vtall, hyc, fnc, naqc, nakc, navc = _inproj(
            ctx, g1, sc1c, sh1c, w, rope_ctx, (kall, vtall), n_lat, True)

        oa = _flash_t(q, kall, vtall, MLA_V)
        od = _na(naq, nak, nav, nakc, navc, _na_bias_tables(na_rpb[l]))
        ob = _hyena_long(hy, hy_short_w[l], fw, hy_bias[l], tabs_lat)
        oc = _fnet_long(fn, tabs_lat)
        x_new = _merge(x, g1, sc1, sh1, gt1, oa, ob, oc, od, w)

        if not last:
            oa_c = _flash(qc, kc, vc, True)
            heads = lambda t: jnp.transpose(t.reshape(b, n_ctx, NA_HEADS, NA_HEAD_DIM), (0, 2, 1, 3))
            od_c = _flash(heads(naqc), heads(nakc), heads(jnp.swapaxes(navc, 1, 2)), False)
            ob_c = _hyena_short(hyc, hy_short_w[l], fw, hy_bias[l], tabs_ctx)
            ob_c = jnp.transpose(ob_c.reshape(n_ctx, b, HY_WIDTH), (1, 0, 2))
            oc_c = _fnet_short(fnc, tabs_ctx)
            ctx = _merge(ctx, g1, sc1c, sh1c, gt1c, oa_c, ob_c, oc_c, od_c, w)
            ctx = _moe(ctx, g2, sc2c, sh2c, gt2c, w, fg, False)
        x = _moe(x_new, g2, sc2, sh2, gt2, w, fg, last)
    return x
```

```python
import functools
import math

import numpy as np
import jax
import jax.numpy as jnp
from jax import lax
from jax.experimental import pallas as pl
from jax.experimental.pallas import tpu as pltpu

F32 = jnp.float32
BF16 = jnp.bfloat16

D_MODEL = 1024
DEPTH = 2
GRID_W = 64
NORM_EPS = 1e-6
MASK_VALUE = -1e30
BRANCH_WIDTH = D_MODEL // 4
N_BRANCH = 4

MLA_HEADS = 4
MLA_Q_RANK = D_MODEL // 4
MLA_KV_RANK = D_MODEL // 8
MLA_NOPE = 64
MLA_ROPE = 32
MLA_V = BRANCH_WIDTH // MLA_HEADS
MLA_SCALE = 1.0 / math.sqrt(MLA_NOPE + MLA_ROPE)
MLA_HEAD_PAD = 128
MLA_VT_ROWS = MLA_V + 16
PV_DTYPE = BF16
ROPE_BASE = 10000.0
LOG2E = 1.4426950408889634

HY_WIDTH = BRANCH_WIDTH
HY_ORDER = 2
HY_BANDS = 16
HY_EMB = 1 + 2 * HY_BANDS
HY_EMB_PAD = 128
HY_HIDDEN = 64
HY_DECAY_TARGET = 1e-2
HY_DECAY_SHORT_PCT = 0.3
HY_DECAY_LONG_PCT = 1.5
HY_IN = (HY_ORDER + 1) * HY_WIDTH

FN_GROUPS = 4
FN_GROUP_DIM = BRANCH_WIDTH // FN_GROUPS

NA_HEADS = 4
NA_HEAD_DIM = BRANCH_WIDTH // NA_HEADS
NA_KH = 8
NA_KW = 16
NA_SCALE = 1.0 / math.sqrt(NA_HEAD_DIM)
NA_QROWS = 8
NA_KROWS = 16
NA_VT_PAD = 16

MOE_GROUPS = 4
MOE_PER_GROUP = 4
MOE_EXPERTS = MOE_GROUPS * MOE_PER_GROUP
MOE_HIDDEN = 256
ROUTER_LANES = 128

MLA_IN = MLA_Q_RANK + MLA_KV_RANK + MLA_ROPE
MIX_IN = MLA_IN + HY_IN + BRANCH_WIDTH + 3 * BRANCH_WIDTH
MIX_PAD = MLA_Q_RANK + MLA_KV_RANK + MLA_HEAD_PAD + HY_IN + BRANCH_WIDTH + 3 * BRANCH_WIDTH

DFT_N2 = 256
FN_N2 = 128
BMM_GROUPS_PER_STEP = 4
FLASH_K_TILE = 3328
FLASH_CHUNKS = 3
FLASH_Q_TILE = 1024
FLASH_Q_SUB = 512


def _params(sem, vmem_mb=None):
    kw = dict(dimension_semantics=sem)
    if vmem_mb is not None:
        kw["vmem_limit_bytes"] = vmem_mb << 20
    return pltpu.CompilerParams(**kw)


def _pick_tile(n, cap, mult):
    best = None
    for t in range(mult, min(n, cap) + 1, mult):
        if n % t == 0:
            best = t
    return best if best is not None else n


def _rms(x):
    return x * lax.rsqrt(jnp.mean(x * x, axis=-1, keepdims=True) + NORM_EPS)


def _normmod(x, g, sc, sh):
    return (_rms(x) * g) * (1.0 + sc) + sh


def _dot(a, b):
    return jnp.dot(a, b, preferred_element_type=F32)


def _dot_nt(a, b):
    return lax.dot_general(a, b, (((1,), (1,)), ((), ())), preferred_element_type=F32)


def _mod_kernel(c_ref, w_ref, b_ref, o_ref):
    c = c_ref[...]
    s = c * jax.nn.sigmoid(c)
    o_ref[0] = _dot(s.astype(BF16), w_ref[0].astype(BF16)) + b_ref[0]


def _mod_vectors(cvec, ada_w, ada_b):
    n6 = ada_w.shape[-1]
    tn = 1024
    return pl.pallas_call(
        _mod_kernel,
        grid=(DEPTH, n6 // tn),
        in_specs=[pl.BlockSpec((8, D_MODEL), lambda l, j: (0, 0)),
                  pl.BlockSpec((1, D_MODEL, tn), lambda l, j: (l, 0, j)),
                  pl.BlockSpec((1, 1, tn), lambda l, j: (l, 0, j))],
        out_specs=pl.BlockSpec((1, 8, tn), lambda l, j: (l, 0, j)),
        out_shape=jax.ShapeDtypeStruct((DEPTH, 8, n6), F32),
        compiler_params=_params(("parallel", "parallel")),
        name="adaln_vectors",
    )(cvec, ada_w, ada_b.reshape(DEPTH, 1, n6))


def _inproj_kernel(*refs, ctx_mode):
    (x_ref, g_ref, sc_ref, sh_ref, wa_ref, qn_ref, kvn_ref, wuq_ref, wuk_ref, wuv_ref,
     cos_ref, s1_ref, s2_ref) = refs[:13]
    if ctx_mode:
        (q_ref, k_ref, v_ref, kall_ref, vtall_ref, hy_ref, fn_ref, naq_ref, nak_ref, nav_ref) = refs[15:]
    else:
        (q_ref, kall_ref, vtall_ref, hy_ref, fn_ref, naq_ref, nak_ref, nav_ref) = refs[15:]
    h = _normmod(x_ref[0], g_ref[...], sc_ref[0], sh_ref[0]).astype(BF16)
    p = _dot(h, wa_ref[...])
    c0 = MLA_Q_RANK
    c1 = c0 + MLA_KV_RANK
    c2 = c1 + MLA_HEAD_PAD
    c3 = c2 + HY_IN
    c4 = c3 + BRANCH_WIDTH
    c5 = c4 + BRANCH_WIDTH
    c6 = c5 + BRANCH_WIDTH
    cq, ckv, krp = p[:, :c0], p[:, c0:c1], p[:, c1:c2]
    hy_ref[0] = p[:, c2:c3]
    fn_ref[0] = p[:, c3:c4]
    naq_ref[0] = p[:, c4:c5].astype(BF16)
    nak_ref[0] = p[:, c5:c6].astype(BF16)
    nav_ref[0] = p[:, c6:].T.astype(BF16)
    cqn = (_rms(cq) * qn_ref[...]).astype(BF16)
    ckn = (_rms(ckv) * kvn_ref[...]).astype(BF16)
    q = _dot(cqn, wuq_ref[...])
    kn = _dot(ckn, wuk_ref[...])
    v = _dot(ckn, wuv_ref[...])
    cos, s1, s2 = cos_ref[...], s1_ref[...], s2_ref[...]

    def rope(t):
        return (t * cos + pltpu.roll(t, MLA_HEAD_PAD - MLA_ROPE // 2, 1) * s1
                + pltpu.roll(t, MLA_ROPE // 2, 1) * s2)

    kr = rope(krp)
    vt = v.T
    tail_rows = lax.broadcasted_iota(jnp.int32, (MLA_VT_ROWS - MLA_V, v.shape[0]), 0)
    tail = jnp.where(tail_rows == 0, 1.0, 0.0).astype(PV_DTYPE)
    for hh in range(MLA_HEADS):
        sl = slice(hh * MLA_HEAD_PAD, (hh + 1) * MLA_HEAD_PAD)
        vs = slice(hh * MLA_V, (hh + 1) * MLA_V)
        kh = (kn[:, sl] + kr).astype(BF16)
        qh = rope(q[:, sl])
        q_ref[0, hh] = qh.astype(BF16) if ctx_mode else qh.T.astype(BF16)
        kall_ref[0, hh] = kh
        vtall_ref[0, hh, :MLA_V] = vt[vs].astype(PV_DTYPE)
        vtall_ref[0, hh, MLA_V:] = tail
        if ctx_mode:
            k_ref[0, hh] = kh
            v_ref[0, hh] = v[:, vs].astype(BF16)


def _inproj(x, g, sc, sh, w, rope_tabs, kv_bufs, key_off, ctx_mode):
    b, n, _ = x.shape
    n_keys = kv_bufs[0].shape[2]
    tm = _pick_tile(n, 512, 128)
    assert key_off % tm == 0
    ko = key_off // tm
    cos, s1, s2 = rope_tabs
    tok = lambda bb, i: (bb, i, 0)
    row = lambda bb, i: (bb, 0, 0)
    const2 = lambda bb, i: (0, 0)
    hd = lambda bb, i: (bb, 0, i, 0)
    tab = lambda bb, i: (i, 0)
    hp = MLA_HEADS * MLA_HEAD_PAD
    in_specs = [pl.BlockSpec((1, tm, D_MODEL), tok),
                pl.BlockSpec((1, D_MODEL), const2),
                pl.BlockSpec((1, 1, D_MODEL), row),
                pl.BlockSpec((1, 1, D_MODEL), row),
                pl.BlockSpec((D_MODEL, MIX_PAD), const2),
                pl.BlockSpec((1, MLA_Q_RANK), const2),
                pl.BlockSpec((1, MLA_KV_RANK), const2),
                pl.BlockSpec((MLA_Q_RANK, hp), const2),
                pl.BlockSpec((MLA_KV_RANK, hp), const2),
                pl.BlockSpec((MLA_KV_RANK, BRANCH_WIDTH), const2),
                pl.BlockSpec((tm, MLA_HEAD_PAD), tab),
                pl.BlockSpec((tm, MLA_HEAD_PAD), tab),
                pl.BlockSpec((tm, MLA_HEAD_PAD), tab)]
    args = [x, g, sc, sh, w["wa"], w["qn"], w["kvn"], w["wuq"], w["wuk"], w["wuv"], cos, s1, s2]
    q_spec = pl.BlockSpec((1, MLA_HEADS, tm, MLA_HEAD_PAD), hd)
    q_sd = jax.ShapeDtypeStruct((b, MLA_HEADS, n, MLA_HEAD_PAD), BF16)
    kall_spec = pl.BlockSpec((1, MLA_HEADS, tm, MLA_HEAD_PAD), lambda bb, i: (bb, 0, ko + i, 0))
    vtall_spec = pl.BlockSpec((1, MLA_HEADS, MLA_VT_ROWS, tm), lambda bb, i: (bb, 0, 0, ko + i))
    kall_sd = jax.ShapeDtypeStruct((b, MLA_HEADS, n_keys, MLA_HEAD_PAD), BF16)
    vtall_sd = jax.ShapeDtypeStruct((b, MLA_HEADS, MLA_VT_ROWS, n_keys), PV_DTYPE)
    rest_specs = [pl.BlockSpec((1, tm, HY_IN), tok)] + [pl.BlockSpec((1, tm, BRANCH_WIDTH), tok)] * 3 \
        + [pl.BlockSpec((1, BRANCH_WIDTH, tm), lambda bb, i: (bb, 0, i))]
    rest_sd = [jax.ShapeDtypeStruct((b, n, HY_IN), F32), jax.ShapeDtypeStruct((b, n, BRANCH_WIDTH), F32)] \
        + [jax.ShapeDtypeStruct((b, n, BRANCH_WIDTH), BF16)] * 2 \
        + [jax.ShapeDtypeStruct((b, BRANCH_WIDTH, n), BF16)]
    in_specs += [pl.BlockSpec(memory_space=pl.ANY), pl.BlockSpec(memory_space=pl.ANY)]
    args += list(kv_bufs)
    if ctx_mode:
        out_specs = [q_spec, q_spec, pl.BlockSpec((1, MLA_HEADS, tm, MLA_V), hd), kall_spec, vtall_spec]
        out_shape = [q_sd, q_sd, jax.ShapeDtypeStruct((b, MLA_HEADS, n, MLA_V), BF16), kall_sd, vtall_sd]
        aliases = {13: 3, 14: 4}
    else:
        qt_spec = pl.BlockSpec((1, MLA_HEADS, MLA_HEAD_PAD, tm), lambda bb, i: (bb, 0, 0, i))
        qt_sd = jax.ShapeDtypeStruct((b, MLA_HEADS, MLA_HEAD_PAD, n), BF16)
        out_specs = [qt_spec, kall_spec, vtall_spec]
        out_shape = [qt_sd, kall_sd, vtall_sd]
        aliases = {13: 1, 14: 2}
    return pl.pallas_call(
        functools.partial(_inproj_kernel, ctx_mode=ctx_mode),
        grid=(b, n // tm),
        in_specs=in_specs,
        out_specs=out_specs + rest_specs,
        out_shape=out_shape + rest_sd,
        input_output_aliases=aliases,
        compiler_params=_params(("parallel", "parallel"), 48),
        name="input_projection",
    )(*args)


def _flash_kernel(q_ref, k_ref, v_ref, o_ref, m_ref, l_ref, acc_ref, *, nh, use_exp2):
    j = pl.program_id(2)

    @pl.when(j == 0)
    def _():
        m_ref[...] = jnp.full(m_ref.shape, MASK_VALUE, F32)
        l_ref[...] = jnp.zeros(l_ref.shape, F32)
        acc_ref[...] = jnp.zeros(acc_ref.shape, F32)

    ex = jnp.exp2 if use_exp2 else jnp.exp
    for h in range(nh):
        s = _dot_nt(q_ref[0, h], k_ref[0, h])
        m_prev = m_ref[h]
        m_cur = jnp.maximum(m_prev, jnp.max(s, axis=1, keepdims=True))
        alpha = ex(m_prev - m_cur)
        p = ex(s - m_cur)
        l_ref[h] = alpha * l_ref[h] + jnp.sum(p, axis=1, keepdims=True)
        acc_ref[h] = alpha * acc_ref[h] + _dot(p.astype(BF16), v_ref[0, h])
        m_ref[h] = m_cur

    @pl.when(j == pl.num_programs(2) - 1)
    def _():
        outs = [acc_ref[h] / l_ref[h] for h in range(nh)]
        o_ref[0] = jnp.concatenate(outs, axis=1).astype(o_ref.dtype)


def _flash(q, k, v, use_exp2):
    b, nh, lq, dk = q.shape
    lk, dv = k.shape[2], v.shape[3]
    tq = _pick_tile(lq, 512, 8)
    tk = _pick_tile(lk, 1280, 128)
    return pl.pallas_call(
        functools.partial(_flash_kernel, nh=nh, use_exp2=use_exp2),
        grid=(b, lq // tq, lk // tk),
        in_specs=[pl.BlockSpec((1, nh, tq, dk), lambda bb, i, j: (bb, 0, i, 0)),
                  pl.BlockSpec((1, nh, tk, dk), lambda bb, i, j: (bb, 0, j, 0)),
                  pl.BlockSpec((1, nh, tk, dv), lambda bb, i, j: (bb, 0, j, 0))],
        out_specs=pl.BlockSpec((1, tq, nh * dv), lambda bb, i, j: (bb, i, 0)),
        out_shape=jax.ShapeDtypeStruct((b, lq, nh * dv), BF16),
        scratch_shapes=[pltpu.VMEM((nh, tq, 1), F32), pltpu.VMEM((nh, tq, 1), F32),
                        pltpu.VMEM((nh, tq, dv), F32)],
        compiler_params=_params(("parallel", "parallel", "arbitrary"), 48),
        name="softmax_attention",
    )(q, k, v)


def _flash_t_kernel(q_ref, k_ref, vt_ref, o_ref, m_ref, acc_ref, st_ref, p_ref, *, nh, dv):
    j = pl.program_id(2)

    @pl.when(j == 0)
    def _():
        m_ref[...] = jnp.full(m_ref.shape, MASK_VALUE, F32)
        acc_ref[...] = jnp.zeros(acc_ref.shape, F32)

    rows = vt_ref.shape[2]

    tq = q_ref.shape[3]
    qs = min(tq, FLASH_Q_SUB)
    units = [(h, c) for h in range(nh) for c in range(tq // qs)]

    def scores(i):
        h, c = units[i]
        cols = slice(c * qs, (c + 1) * qs)
        st = _dot(k_ref[0, h], q_ref[0, h, :, cols])
        st_ref[i % 2] = st
        m_prev = m_ref[h, :, cols]
        m_cur = jnp.maximum(m_prev, jnp.max(st, axis=0, keepdims=True))
        m_ref[h, :, cols] = m_cur
        return m_prev, m_cur

    def probs(i, m_cur):
        p_ref[i % 2] = jnp.exp2(st_ref[i % 2] - m_cur).astype(PV_DTYPE)

    def values(i, m_prev, m_cur):
        h, c = units[i]
        cols = slice(c * qs, (c + 1) * qs)
        alpha = jnp.exp2(m_prev - m_cur)
        acc_ref[h, :rows, cols] = alpha * acc_ref[h, :rows, cols] + _dot(vt_ref[0, h], p_ref[i % 2])

    n_units = len(units)
    tk = k_ref.shape[2]
    kt = tk // 256
    n_chunks = min(FLASH_CHUNKS, kt) if tk % 256 == 0 else 1
    v_edges = [256 * ((kt * c) // n_chunks) for c in range(n_chunks + 1)] if n_chunks > 1 else [0, tk]
    s_edges = [16 * (((tk // 16) * c) // n_chunks) for c in range(n_chunks + 1)]

    def scores_chunk(i, c, m_run):
        h, cc = units[i]
        cols = slice(cc * qs, (cc + 1) * qs)
        r0, r1 = s_edges[c], s_edges[c + 1]
        st = _dot(k_ref[0, h, r0:r1, :], q_ref[0, h, :, cols])
        st_ref[i % 2, r0:r1, :] = st
        return jnp.maximum(m_run, jnp.max(st, axis=0, keepdims=True))

    def values_chunk(i, c):
        h, _ = units[i]
        r0, r1 = v_edges[c], v_edges[c + 1]
        return _dot(vt_ref[0, h, :, r0:r1], p_ref[i % 2, r0:r1, :])

    stats = {0: scores(0)}
    if n_units > 1:
        stats[1] = scores(1)
    probs(0, stats[0][1])
    for i in range(n_units):
        h, cc = units[i]
        cols = slice(cc * qs, (cc + 1) * qs)
        m_prev_i, m_cur_i = stats.pop(i)
        nxt = i + 2 < n_units
        if nxt:
            h2, c2 = units[i + 2]
            cols2 = slice(c2 * qs, (c2 + 1) * qs)
            m_prev2 = m_ref[h2, :, cols2]
            m_run = m_prev2
        pv = None
        for c in range(n_chunks):
            part = values_chunk(i, c)
            pv = part if pv is None else pv + part
            if nxt:
                m_run = scores_chunk(i + 2, c, m_run)
        alpha = jnp.exp2(m_prev_i - m_cur_i)
        acc_ref[h, :rows, cols] = alpha * acc_ref[h, :rows, cols] + pv
        if nxt:
            m_ref[h2, :, cols2] = m_run
            stats[i + 2] = (m_prev2, m_run)
        if i + 1 < n_units:
            probs(i + 1, stats[i + 1][1])

    @pl.when(j == pl.num_programs(2) - 1)
    def _():
        outs = []
        for h in range(nh):
            a = acc_ref[h]
            o = (a / a[dv:dv + 1]).T
            outs.append(o[:, :dv])
        o_ref[0] = jnp.concatenate(outs, axis=1).astype(o_ref.dtype)


def _flash_t(q, k, vt, dv):
    b, nh, dk, lq = q.shape
    lk, rows = k.shape[2], vt.shape[2]
    tq = _pick_tile(lq, FLASH_Q_TILE, 128)
    tk = _pick_tile(lk, FLASH_K_TILE, 128)
    return pl.pallas_call(
        functools.partial(_flash_t_kernel, nh=nh, dv=dv),
        grid=(b, lq // tq, lk // tk),
        in_specs=[pl.BlockSpec((1, nh, dk, tq), lambda bb, i, j: (bb, 0, 0, i)),
                  pl.BlockSpec((1, nh, tk, dk), lambda bb, i, j: (bb, 0, j, 0)),
                  pl.BlockSpec((1, nh, rows, tk), lambda bb, i, j: (bb, 0, 0, j))],
        out_specs=pl.BlockSpec((1, tq, nh * dv), lambda bb, i, j: (bb, i, 0)),
        out_shape=jax.ShapeDtypeStruct((b, lq, nh * dv), BF16),
        scratch_shapes=[pltpu.VMEM((nh, 1, tq), F32), pltpu.VMEM((nh, 128, tq), F32),
                        pltpu.VMEM((2, tk, min(tq, FLASH_Q_SUB)), F32),
                        pltpu.VMEM((2, tk, min(tq, FLASH_Q_SUB)), PV_DTYPE)],
        compiler_params=_params(("parallel", "parallel", "arbitrary"), 56),
        name="softmax_attention_t",
    )(q, k, vt)


def _na_kernel(q_ref, k0, k1, k2, k3, v0, v1, v2, v3, kc_ref, vc_ref, b_ref, o_ref):
    q = q_ref[0]
    kcat = jnp.concatenate([k0[0], k1[0], k2[0], k3[0]], axis=0)
    vtcat = jnp.concatenate([v0[0], v1[0], v2[0], v3[0]], axis=1)
    kc, vtc = kc_ref[0], vc_ref[0]
    dh = NA_HEAD_DIM

    def with_ones(vt):
        pad = lax.broadcasted_iota(jnp.int32, (NA_VT_PAD, vt.shape[1]), 0)
        return jnp.concatenate([vt, jnp.where(pad == 0, 1.0, 0.0).astype(vt.dtype)], axis=0)

    def scores(h):
        sl = slice(h * dh, (h + 1) * dh)
        qh = q[:, sl]
        st = _dot_nt(kcat[:, sl], qh) + b_ref[0, h]
        stc = _dot_nt(kc[:, sl], qh)
        m = jnp.maximum(jnp.max(st, axis=0, keepdims=True), jnp.max(stc, axis=0, keepdims=True))
        return st, stc, m

    def values(h, st, stc, m):
        sl = slice(h * dh, (h + 1) * dh)
        p = jnp.exp(st - m).astype(BF16)
        pc = jnp.exp(stc - m).astype(BF16)
        acc = _dot(with_ones(vtcat[sl]), p) + _dot(with_ones(vtc[sl]), pc)
        o = jnp.concatenate([acc / acc[dh:dh + 1],
                             jnp.zeros((128 - dh - NA_VT_PAD, acc.shape[1]), F32)], axis=0)
        return o.T[:, :dh]

    outs = []
    pending = scores(0)
    for h in range(NA_HEADS):
        nxt = scores(h + 1) if h + 1 < NA_HEADS else None
        outs.append(values(h, *pending))
        pending = nxt
    o_ref[0] = jnp.concatenate(outs, axis=1).astype(o_ref.dtype)


def _na_bias_tables(rpb):
    j = np.arange(NA_QROWS)[:, None]
    m = np.arange(NA_KROWS)[None, :]
    valid = np.zeros((3, NA_QROWS, NA_KROWS), bool)
    st0 = np.maximum(j - NA_KH // 2, 0)
    valid[0] = (m - 4 >= st0) & (m - 4 < st0 + NA_KH)
    valid[1] = (m - j >= 0) & (m - j < NA_KH)
    st2 = 4 + np.minimum(j - NA_KH // 2, 0)
    valid[2] = (m < 12) & (m >= st2) & (m < st2 + NA_KH)
    dr_idx = np.clip(m - 4 - j + NA_KH - 1, 0, 2 * NA_KH - 2)
    qc = np.arange(GRID_W)[:, None]
    kc = np.arange(GRID_W)[None, :]
    dc_idx = np.clip(kc - qc, -(NA_KW - 1), NA_KW - 1) + NA_KW - 1
    cs = np.clip(qc - NA_KW // 2, 0, GRID_W - NA_KW)
    col_ok = (kc >= cs) & (kc < cs + NA_KW)
    t = rpb.astype(F32)[:, dr_idx]
    t = t[..., dc_idx]
    t = jnp.transpose(t, (0, 2, 4, 1, 3))
    ok = (valid.transpose(0, 2, 1)[:, None, :, None, :, None]
          & col_ok.T[None, None, None, :, None, :])
    out = jnp.where(jnp.asarray(ok), t[None], MASK_VALUE)
    return out.reshape(3, NA_HEADS, NA_KROWS * GRID_W, NA_QROWS * GRID_W)


def _na(q, k, vt, kc, vtc, bias):
    b, n, w = q.shape
    c = kc.shape[1]
    tq = NA_QROWS * GRID_W
    tkb = 4 * GRID_W
    nb = n // tq
    nkb = n // tkb
    blocks = [lambda i: jnp.maximum(2 * i - 1, 0), lambda i: 2 * i, lambda i: 2 * i + 1,
              lambda i: jnp.minimum(2 * i + 2, nkb - 1)]
    kspecs = [pl.BlockSpec((1, tkb, w), lambda bb, i, f=f: (bb, f(i), 0)) for f in blocks]
    vspecs = [pl.BlockSpec((1, w, tkb), lambda bb, i, f=f: (bb, 0, f(i))) for f in blocks]
    return pl.pallas_call(
        _na_kernel,
        grid=(b, nb),
        in_specs=[pl.BlockSpec((1, tq, w), lambda bb, i: (bb, i, 0))] + kspecs + vspecs
                 + [pl.BlockSpec((1, c, w), lambda bb, i: (bb, 0, 0)),
                    pl.BlockSpec((1, w, c), lambda bb, i: (bb, 0, 0)),
                    pl.BlockSpec((1, NA_HEADS, NA_KROWS * GRID_W, tq),
                                 lambda bb, i: (jnp.where(i == 0, 0, jnp.where(i == nb - 1, 2, 1)), 0, 0, 0))],
        out_specs=pl.BlockSpec((1, tq, w), lambda bb, i: (bb, i, 0)),
        out_shape=jax.ShapeDtypeStruct((b, n, w), BF16),
        compiler_params=_params(("parallel", "arbitrary"), 56),
        name="neighbourhood_attention",
    )(q, k, k, k, k, vt, vt, vt, vt, kc, vtc, bias)


def _sconv_kernel(u_ref, pv_ref, nx_ref, w_ref, v_ref, x1_ref, x2_ref, *, nt):
    i = pl.program_id(1)
    u = u_ref[0]
    tl = u.shape[0]
    prev = jnp.where(i > 0, pv_ref[0, 7:8, :], 0.0)
    nxt = jnp.where(i < nt - 1, nx_ref[0, 0:1, :], 0.0)
    rows = lax.broadcasted_iota(jnp.int32, u.shape, 0)
    um = jnp.where(rows == 0, prev, pltpu.roll(u, 1, 0))
    up = jnp.where(rows == tl - 1, nxt, pltpu.roll(u, tl - 1, 0))
    w = w_ref[...]
    y = um * w[0:1] + u * w[1:2] + up * w[2:3]
    v_ref[...] = y[:, :HY_WIDTH]
    x1_ref[...] = y[:, HY_WIDTH:2 * HY_WIDTH]
    x2_ref[...] = y[:, 2 * HY_WIDTH:]


def _sconv(u, w, seq_major):
    b, n, _ = u.shape
    tl = _pick_tile(n, 512, 8)
    nt = n // tl
    r8 = tl // 8
    if seq_major:
        osd = jax.ShapeDtypeStruct((n, b * HY_WIDTH), F32)
        ospec = pl.BlockSpec((tl, HY_WIDTH), lambda bb, i: (i, bb))
    else:
        osd = jax.ShapeDtypeStruct((b * n, HY_WIDTH), F32)
        ospec = pl.BlockSpec((tl, HY_WIDTH), lambda bb, i: (bb * nt + i, 0))
    return pl.pallas_call(
        functools.partial(_sconv_kernel, nt=nt),
        grid=(b, nt),
        in_specs=[pl.BlockSpec((1, tl, HY_IN), lambda bb, i: (bb, i, 0)),
                  pl.BlockSpec((1, 8, HY_IN), lambda bb, i: (bb, jnp.maximum(i * r8 - 1, 0), 0)),
                  pl.BlockSpec((1, 8, HY_IN), lambda bb, i: (bb, jnp.minimum((i + 1) * r8, n // 8 - 1), 0)),
                  pl.BlockSpec((3, HY_IN), lambda bb, i: (0, 0))],
        out_specs=[ospec, ospec, ospec],
        out_shape=[osd, osd, osd],
        compiler_params=_params(("parallel", "parallel")),
        name="hyena_short_conv",
    )(u, u, u, w)


def _filt_kernel(z_ref, w1_ref, b1_ref, f1_ref, w2_ref, b2_ref, f2_ref, w3_ref, win_ref, g_ref, ss_ref):
    d = pl.program_id(0)
    i = pl.program_id(1)
    h = jnp.sin(f1_ref[...] * (_dot(w1_ref[...], z_ref[0].astype(BF16)) + b1_ref[...]))
    h = jnp.sin(f2_ref[...] * (_dot(w2_ref[...], h.astype(BF16)) + b2_ref[...]))
    h = _dot(w3_ref[0], h.astype(BF16)).T
    h = h * jnp.concatenate([win_ref[0]] * HY_ORDER, axis=1)
    rows = lax.broadcasted_iota(jnp.int32, h.shape, 0)
    h = jnp.where((rows == 0) & (i == 0) & (d == 1), 0.0, h)
    g_ref[0] = h
    ss = jnp.sum(h * h, axis=0, keepdims=True)
    first = (i == 0) & (d == 0)

    @pl.when(first)
    def _():
        ss_ref[...] = ss

    @pl.when(jnp.logical_not(first))
    def _():
        ss_ref[...] += ss


def _padded_filter(n, fw):
    r = jnp.arange(n, dtype=jnp.int32)
    idx = jnp.stack([r, jnp.mod(n - r, n)], axis=0).astype(F32)[:, :, None]
    t = idx / (n - 1)
    bands = jnp.linspace(1e-4, HY_BANDS - 1, HY_BANDS, dtype=F32)
    ang = (2.0 * math.pi / n) * idx * bands
    z = jnp.concatenate([t, jnp.cos(ang), -jnp.sin(ang), jnp.zeros((2, n, HY_EMB_PAD - HY_EMB), F32)], axis=-1)
    z = jnp.swapaxes(z, 1, 2)
    deltas = jnp.linspace(math.log(HY_DECAY_TARGET) / HY_DECAY_LONG_PCT,
                          math.log(HY_DECAY_TARGET) / HY_DECAY_SHORT_PCT, HY_WIDTH, dtype=F32)
    window = jnp.exp(-t * jnp.abs(deltas))
    tl = _pick_tile(n, 512, 128)
    half = HY_ORDER * HY_WIDTH
    c2 = lambda d, i: (0, 0)
    g, ss = pl.pallas_call(
        _filt_kernel,
        grid=(2, n // tl),
        in_specs=[pl.BlockSpec((1, HY_EMB_PAD, tl), lambda d, i: (d, 0, i)),
                  pl.BlockSpec((HY_HIDDEN, HY_EMB_PAD), c2), pl.BlockSpec((HY_HIDDEN, 1), c2),
                  pl.BlockSpec((HY_HIDDEN, 1), c2),
                  pl.BlockSpec((HY_HIDDEN, HY_HIDDEN), c2), pl.BlockSpec((HY_HIDDEN, 1), c2),
                  pl.BlockSpec((HY_HIDDEN, 1), c2),
                  pl.BlockSpec((1, half, HY_HIDDEN), lambda d, i: (d, 0, 0)),
                  pl.BlockSpec((1, tl, HY_WIDTH), lambda d, i: (d, i, 0))],
        out_specs=[pl.BlockSpec((1, tl, half), lambda d, i: (d, i, 0)), pl.BlockSpec((1, half), c2)],
        out_shape=[jax.ShapeDtypeStruct((2, n, half), F32), jax.ShapeDtypeStruct((1, half), F32)],
        compiler_params=_params(("arbitrary", "arbitrary")),
        name="hyena_filter_mlp",
    )(z, fw["w1"], fw["b1"], fw["f1"], fw["w2"], fw["b2"], fw["f2"], fw["w3"], window)
    return g.reshape(2 * n, half), lax.rsqrt(ss + NORM_EPS)


def _colmm_kernel(*refs, epi):
    w_ref, x_ref = refs[0], refs[1]
    o_ref = refs[-1]
    acc = _dot(w_ref[...], x_ref[...].astype(BF16))
    if epi == "gate":
        xg_ref, z_ref, b_ref = refs[2:5]
        acc = xg_ref[...] * (acc + z_ref[...] * b_ref[...])
    elif epi == "scale":
        acc = acc * refs[2][...]
    elif epi == "cmul":
        g = refs[2][...]
        mh = acc.shape[0] // 2
        ar, ai, gr, gi = acc[:mh], acc[mh:], g[:mh], g[mh:]
        acc = jnp.concatenate([ar * gr - ai * gi, ar * gi + ai * gr], axis=0)
    o_ref[...] = acc.astype(o_ref.dtype)


def _colmm(w, x, out_dtype, epi="none", extra=(), tn_cap=4096, g_col=None):
    m, k = w.shape
    nc = x.shape[1]
    tn = _pick_tile(nc, tn_cap, 128)
    col = lambda j: (0, j)
    specs = [pl.BlockSpec((m, k), lambda j: (0, 0)), pl.BlockSpec((k, tn), col)]
    if epi == "gate":
        specs += [pl.BlockSpec((m, tn), col), pl.BlockSpec((m, tn), col), pl.BlockSpec((1, tn), col)]
    elif epi == "scale":
        specs += [pl.BlockSpec((1, tn), col)]
    elif epi == "cmul":
        specs += [pl.BlockSpec((m, tn), lambda j: (0, g_col))]
    return pl.pallas_call(
        functools.partial(_colmm_kernel, epi=epi),
        grid=(nc // tn,),
        in_specs=specs,
        out_specs=pl.BlockSpec((m, tn), col),
        out_shape=jax.ShapeDtypeStruct((m, nc), out_dtype),
        compiler_params=_params(("parallel",), 48),
        name="dft_stage_" + epi,
    )(w, x, *extra)


def _bmm_kernel(m_ref, x_ref, o_ref, *, gs):
    for s in range(gs):
        x = x_ref[:, s]
        x = x.reshape(x.shape[0] * x.shape[1], x.shape[2])
        acc = _dot(m_ref[s], x)
        c = acc.shape[1]
        o_ref[:, s * c:(s + 1) * c] = acc.astype(o_ref.dtype)


def _bmm(mats, x, out_dtype):
    ng, p, qq = mats.shape
    c = x.shape[3]
    gs = _pick_tile(ng, BMM_GROUPS_PER_STEP, 1)
    return pl.pallas_call(
        functools.partial(_bmm_kernel, gs=gs),
        grid=(ng // gs,),
        in_specs=[pl.BlockSpec((gs, p, qq), lambda g: (g, 0, 0)),
                  pl.BlockSpec((2, gs, qq // 2, c), lambda g: (0, g, 0, 0))],
        out_specs=pl.BlockSpec((p, gs * c), lambda g: (0, g)),
        out_shape=jax.ShapeDtypeStruct((p, ng * c), out_dtype),
        compiler_params=_params(("parallel",), 48),
        name="dft_batched_real",
    )(mats, x)


def _tbmm_kernel(*refs, mode, gs):
    f_ref, twr_ref, twi_ref, x_ref = refs[:4]
    o_ref = refs[-1]
    for s in range(gs):
        xr, xi = x_ref[0, s].astype(F32), x_ref[1, s].astype(F32)
        reps = xr.shape[1] // twr_ref.shape[2]
        tr = jnp.concatenate([twr_ref[s]] * reps, axis=1)
        ti = jnp.concatenate([twi_ref[s]] * reps, axis=1)
        x = jnp.concatenate([xr * tr - xi * ti, xr * ti + xi * tr], axis=0).astype(BF16)
        acc = _dot(f_ref[...], x)
        ph = acc.shape[0] // 2
        ar, ai = acc[:ph], acc[ph:]
        if mode == "scale":
            sc = refs[4][...]
            o_ref[0, s] = ar * sc
            o_ref[1, s] = ai * sc
        else:
            g_ref, fi_ref = refs[4], refs[5]
            gr, gi = g_ref[0, s], g_ref[1, s]
            y = jnp.concatenate([ar * gr - ai * gi, ar * gi + ai * gr], axis=0).astype(BF16)
            back = _dot(fi_ref[...], y)
            br, bi = back[:ph], back[ph:]
            o_ref[0, s] = (br * tr + bi * ti).astype(o_ref.dtype)
            o_ref[1, s] = (bi * tr - br * ti).astype(o_ref.dtype)


def _tbmm(tabs, x, mode, extra, out_dtype, g_col=0):
    _, ng, n2, c = x.shape
    gs = _pick_tile(ng, BMM_GROUPS_PER_STEP, 1)
    p = 2 * n2
    blk = lambda g: (0, g, 0, 0)
    specs = [pl.BlockSpec((p, p), lambda g: (0, 0)),
             pl.BlockSpec((gs, n2, 128), lambda g: (g, 0, 0)),
             pl.BlockSpec((gs, n2, 128), lambda g: (g, 0, 0)),
             pl.BlockSpec((2, gs, n2, c), blk)]
    args = [tabs["hy_fwd"], tabs["hy_twr"], tabs["hy_twi"], x]
    if mode == "scale":
        specs.append(pl.BlockSpec((1, c), lambda g: (0, 0)))
        args.append(extra)
    else:
        specs += [pl.BlockSpec((2, gs, n2, c), lambda g: (0, g, 0, g_col)),
                  pl.BlockSpec((p, p), lambda g: (0, 0))]
        args += [extra, tabs["hy_inv"]]
    return pl.pallas_call(
        functools.partial(_tbmm_kernel, mode=mode, gs=gs),
        grid=(ng // gs,),
        in_specs=specs,
        out_specs=pl.BlockSpec((2, gs, n2, c), blk),
        out_shape=jax.ShapeDtypeStruct(x.shape, out_dtype),
        compiler_params=_params(("parallel",), 48),
        name="dft_inner_" + mode,
    )(*args)


def _angle(num, den):
    return (2.0 * math.pi / den) * jnp.mod(num, den).astype(F32)


def _dft_tables(n_lat):
    nn = 2 * n_lat
    n2 = DFT_N2
    n1 = nn // n2
    i1 = jnp.arange(n1, dtype=jnp.int32)
    a1 = _angle(i1[:, None] * i1[None, :], n1)
    c1, s1 = jnp.cos(a1), jnp.sin(a1)
    t = {}
    t["hy_f1"] = jnp.concatenate([c1, -s1], axis=0).astype(BF16)
    ch, sh = c1[:, :n1 // 2], s1[:, :n1 // 2]
    t["hy_f1c"] = jnp.concatenate([jnp.concatenate([ch, sh], axis=1),
                                   jnp.concatenate([-sh, ch], axis=1)], axis=0).astype(BF16)
    t["hy_i2c"] = jnp.concatenate([jnp.concatenate([ch.T, -sh.T], axis=1),
                                   jnp.concatenate([sh.T, ch.T], axis=1)], axis=0).astype(BF16)
    i2 = jnp.arange(n2, dtype=jnp.int32)
    a2 = _angle(i2[:, None] * i2[None, :], n2)
    c2, s2 = jnp.cos(a2), jnp.sin(a2)
    fwd = jnp.concatenate([jnp.concatenate([c2, s2], axis=1), jnp.concatenate([-s2, c2], axis=1)], axis=0)
    t["hy_fwd"] = fwd.astype(BF16)
    t["hy_inv"] = fwd.T.astype(BF16)
    at = _angle(i1[:, None] * i2[None, :], nn)
    lanes = (n1, n2, 128)
    t["hy_twr"] = jnp.broadcast_to(jnp.cos(at)[:, :, None], lanes)
    t["hy_twi"] = jnp.broadcast_to(-jnp.sin(at)[:, :, None], lanes)
    m2 = FN_N2
    m1 = n_lat // m2
    j1 = jnp.arange(m1, dtype=jnp.int32)
    b1 = _angle(j1[:, None] * j1[None, :], m1)
    cb, sb = jnp.cos(b1), jnp.sin(b1)
    t["fn_s1"] = jnp.concatenate([jnp.concatenate([cb, sb], axis=1),
                                  jnp.concatenate([-sb, cb], axis=1)], axis=0).astype(BF16)
    j2 = jnp.arange(m2, dtype=jnp.int32)
    kf = j1[:, None, None] + m1 * j2[None, :, None]
    b2 = _angle(kf * j2[None, None, :], n_lat)
    norm = 1.0 / math.sqrt(n_lat * FN_GROUP_DIM)
    t["fn_s3"] = (norm * jnp.concatenate([jnp.cos(b2), jnp.sin(b2)], axis=2)).astype(BF16)
    return t


def _small_dft_tables(n):
    nn = 2 * n
    k = jnp.arange(nn, dtype=jnp.int32)
    a = _angle(k[:, None] * k[None, :], nn)
    c, s = jnp.cos(a), jnp.sin(a)
    t = {}
    t["f_full"] = jnp.concatenate([c, -s], axis=0).astype(BF16)
    t["f_half"] = jnp.concatenate([c[:, :n], -s[:, :n]], axis=0).astype(BF16)
    t["i_half"] = jnp.concatenate([c[:n], -s[:n]], axis=1).astype(BF16)
    j = jnp.arange(n, dtype=jnp.int32)
    b = _angle(j[:, None] * j[None, :], n)
    norm = 1.0 / math.sqrt(n * FN_GROUP_DIM)
    t["fn"] = (norm * jnp.concatenate([jnp.cos(b), jnp.sin(b)], axis=1)).astype(BF16)
    return t


def _hyena_long(hy, short_w, fw, bias, tabs):
    b, n, _ = hy.shape
    assert b == 2
    cw = HY_WIDTH
    n2 = DFT_N2
    n1 = 2 * n // n2
    v, x1, x2 = _sconv(hy, short_w, False)
    g, gscale = _padded_filter(n, fw)
    half = HY_ORDER * HY_WIDTH
    ga = _colmm(tabs["hy_f1"], g.reshape(n1, n2 * half), BF16)
    gspec = _tbmm(tabs, ga.reshape(2, n1, n2, half), "scale", gscale * (1.0 / (2 * n)), F32)
    z = v.reshape(n1, n2 * cw)
    for o, xg in enumerate((x1, x2)):
        a = _colmm(tabs["hy_f1c"], z, BF16)
        bm = _tbmm(tabs, a.reshape(2, n1, n2, cw), "conv", gspec, BF16, g_col=o)
        brow = jnp.tile(bias[o], n2)[None, :]
        z = _colmm(tabs["hy_i2c"], bm.reshape(2 * n1, n2 * cw), F32, epi="gate",
                   extra=(xg.reshape(n1, n2 * cw), z, brow))
    return z.reshape(b, n, cw)


def _hyena_short(hy, short_w, fw, bias, tabs):
    b, n, _ = hy.shape
    cw = b * HY_WIDTH
    v, x1, x2 = _sconv(hy, short_w, True)
    g, gscale = _padded_filter(n, fw)
    gspec = _colmm(tabs["f_full"], g, F32, epi="scale", extra=(gscale * (1.0 / (2 * n)),), tn_cap=256)
    z = v
    for o, xg in enumerate((x1, x2)):
        y = _colmm(tabs["f_half"], z, BF16, epi="cmul", extra=(gspec,), tn_cap=HY_WIDTH, g_col=o)
        brow = jnp.tile(bias[o], b)[None, :]
        z = _colmm(tabs["i_half"], y, F32, epi="gate", extra=(xg, z, brow), tn_cap=256)
    return z


def _fn0_kernel(x_ref, w_ref, o_ref):
    acc = _dot(x_ref[0].astype(BF16), w_ref[...])
    o_ref[0] = acc[:, :BRANCH_WIDTH].astype(o_ref.dtype)
    o_ref[1] = acc[:, BRANCH_WIDTH:].astype(o_ref.dtype)


def _fnet_channel_dft(u):
    b, n, w = u.shape
    j = np.arange(FN_GROUP_DIM)
    ang = 2.0 * np.pi * ((j[:, None] * j[None, :]) % FN_GROUP_DIM) / FN_GROUP_DIM
    eye = np.eye(FN_GROUPS)
    wc = np.concatenate([np.kron(eye, np.cos(ang)), -np.kron(eye, np.sin(ang))], axis=1)
    wc = jnp.asarray(wc, F32).astype(BF16)
    tm = _pick_tile(n, 1024, 8)
    return pl.pallas_call(
        _fn0_kernel,
        grid=(b, n // tm),
        in_specs=[pl.BlockSpec((1, tm, w), lambda bb, i: (bb, i, 0)),
                  pl.BlockSpec((w, 2 * w), lambda bb, i: (0, 0))],
        out_specs=pl.BlockSpec((2, tm, w), lambda bb, i: (0, i, bb)),
        out_shape=jax.ShapeDtypeStruct((2, n, b * w), BF16),
        compiler_params=_params(("parallel", "parallel")),
        name="fnet_channel_dft",
    )(u, wc)


def _fnet_long(u, tabs):
    b, n, w = u.shape
    cw = b * w
    m2 = FN_N2
    m1 = n // m2
    wri = _fnet_channel_dft(u)
    a = _colmm(tabs["fn_s1"], wri.reshape(2 * m1, m2 * cw), BF16)
    y = _bmm(tabs["fn_s3"], a.reshape(2, m1, m2, cw), BF16)
    return y.reshape(n, cw)


def _fnet_short(u, tabs):
    b, n, w = u.shape
    wri = _fnet_channel_dft(u)
    return _colmm(tabs["fn"], wri.reshape(2 * n, b * w), BF16, tn_cap=256)


def _merge_kernel(x_ref, g_ref, sc_ref, sh_ref, gt_ref, oa_ref, ob_ref, oc_ref, od_ref,
                  wg_ref, wb_ref, wo_ref, o_ref):
    x = x_ref[0]
    hb = _normmod(x, g_ref[...], sc_ref[0], sh_ref[0]).astype(BF16)
    branches = (oa_ref[0], ob_ref[0], oc_ref[...], od_ref[0])
    acc = jnp.zeros(x.shape, F32)
    for nbr, o in enumerate(branches):
        gate = _dot(hb, wg_ref[:, nbr * D_MODEL:(nbr + 1) * D_MODEL])
        proj = _dot(o.astype(BF16), wb_ref[nbr])
        acc = acc + jax.nn.sigmoid(gate) * proj
    m = _dot(acc.astype(BF16), wo_ref[...])
    o_ref[0] = x + gt_ref[0] * m


def _merge(x, g, sc, sh, gt, oa, ob, oc, od, w):
    b, n, _ = x.shape
    tm = _pick_tile(n, 512, 8)
    tok = lambda bb, i: (bb, i, 0)
    row = lambda bb, i: (bb, 0, 0)
    seq = lambda bb, i: (i, bb)
    c2 = lambda bb, i: (0, 0)
    return pl.pallas_call(
        _merge_kernel,
        grid=(b, n // tm),
        in_specs=[pl.BlockSpec((1, tm, D_MODEL), tok),
                  pl.BlockSpec((1, D_MODEL), c2),
                  pl.BlockSpec((1, 1, D_MODEL), row), pl.BlockSpec((1, 1, D_MODEL), row),
                  pl.BlockSpec((1, 1, D_MODEL), row),
                  pl.BlockSpec((1, tm, BRANCH_WIDTH), tok),
                  pl.BlockSpec((1, tm, BRANCH_WIDTH), tok),
                  pl.BlockSpec((tm, BRANCH_WIDTH), seq),
                  pl.BlockSpec((1, tm, BRANCH_WIDTH), tok),
                  pl.BlockSpec((D_MODEL, N_BRANCH * D_MODEL), c2),
                  pl.BlockSpec((N_BRANCH, BRANCH_WIDTH, D_MODEL), lambda bb, i: (0, 0, 0)),
                  pl.BlockSpec((D_MODEL, D_MODEL), c2)],
        out_specs=pl.BlockSpec((1, tm, D_MODEL), tok),
        out_shape=jax.ShapeDtypeStruct(x.shape, F32),
        compiler_params=_params(("parallel", "parallel"), 56),
        name="branch_merge",
    )(x, g, sc, sh, gt, oa, ob, oc, od, w["wgate"], w["wbranch"], w["wout"])


def _moe_kernel(x_ref, g_ref, sc_ref, sh_ref, gt_ref, wr_ref, br_ref, wg_ref, wu_ref, wd_ref, fg_ref,
                o_ref, h_s, wt_s, acc_s, *, final):
    e = pl.program_id(2)

    @pl.when(e == 0)
    def _():
        hb = _normmod(x_ref[0], g_ref[...], sc_ref[0], sh_ref[0]).astype(BF16)
        h_s[...] = hb
        r = _dot(hb, wr_ref[...]) + br_ref[...]
        lane = lax.broadcasted_iota(jnp.int32, r.shape, 1)
        neg = jnp.float32(-jnp.inf)
        isg = lane < MOE_GROUPS
        gmax = jnp.max(jnp.where(isg, r, neg), axis=1, keepdims=True)
        gsum = jnp.sum(jnp.where(isg, jnp.exp(r - gmax), 0.0), axis=1, keepdims=True)
        g_p = 1.0 / gsum
        gidx = jnp.min(jnp.where(isg & (r == gmax), lane, ROUTER_LANES), axis=1, keepdims=True)
        ise = ((lane >= MOE_GROUPS) & (lane < MOE_GROUPS + MOE_EXPERTS)
               & (jnp.right_shift(lane - MOE_GROUPS, 2) == gidx))
        el = jnp.where(ise, r, neg)
        e1 = jnp.max(el, axis=1, keepdims=True)
        i1 = jnp.min(jnp.where(ise & (r == e1), lane, ROUTER_LANES), axis=1, keepdims=True)
        el2 = jnp.where(lane == i1, neg, el)
        e2 = jnp.max(el2, axis=1, keepdims=True)
        i2 = jnp.min(jnp.where(el2 == e2, lane, ROUTER_LANES), axis=1, keepdims=True)
        d = jnp.exp(e2 - e1)
        w1 = 1.0 / (1.0 + d)
        w2 = d / (1.0 + d)
        wt_s[...] = jnp.where(lane == i1, g_p * w1, jnp.where(lane == i2, g_p * w2, 0.0))
        acc_s[...] = jnp.zeros(acc_s.shape, F32)

    hb = h_s[...]
    wt = wt_s[...]
    lane = lax.broadcasted_iota(jnp.int32, wt.shape, 1)
    acts = []
    for k in range(MOE_PER_GROUP):
        a = _dot(hb, wg_ref[k])
        u = _dot(hb, wu_ref[k])
        col = MOE_GROUPS + e * MOE_PER_GROUP + k
        wcol = jnp.sum(jnp.where(lane == col, wt, 0.0), axis=1, keepdims=True)
        acts.append(((a * jax.nn.sigmoid(a)) * u * wcol).astype(BF16))
    act = jnp.concatenate(acts, axis=1)
    wd = wd_ref[...].reshape(MOE_PER_GROUP * MOE_HIDDEN, D_MODEL)
    acc_s[...] += _dot(act, wd)

    @pl.when(e == MOE_GROUPS - 1)
    def _():
        y = x_ref[0] + gt_ref[0] * acc_s[...]
        if final:
            y = _rms(y) * fg_ref[...]
        o_ref[0] = y


def _moe(x, g, sc, sh, gt, w, fg, final):
    b, n, _ = x.shape
    tm = _pick_tile(n, 1024, 8)
    tok = lambda bb, i, e: (bb, i, 0)
    row = lambda bb, i, e: (bb, 0, 0)
    c2 = lambda bb, i, e: (0, 0)
    ex = lambda bb, i, e: (e, 0, 0)
    return pl.pallas_call(
        functools.partial(_moe_kernel, final=final),
        grid=(b, n // tm, MOE_GROUPS),
        in_specs=[pl.BlockSpec((1, tm, D_MODEL), tok),
                  pl.BlockSpec((1, D_MODEL), c2),
                  pl.BlockSpec((1, 1, D_MODEL), row), pl.BlockSpec((1, 1, D_MODEL), row),
                  pl.BlockSpec((1, 1, D_MODEL), row),
                  pl.BlockSpec((D_MODEL, ROUTER_LANES), c2), pl.BlockSpec((1, ROUTER_LANES), c2),
                  pl.BlockSpec((MOE_PER_GROUP, D_MODEL, MOE_HIDDEN), ex),
                  pl.BlockSpec((MOE_PER_GROUP, D_MODEL, MOE_HIDDEN), ex),
                  pl.BlockSpec((MOE_PER_GROUP, MOE_HIDDEN, D_MODEL), ex),
                  pl.BlockSpec((1, D_MODEL), c2)],
        out_specs=pl.BlockSpec((1, tm, D_MODEL), tok),
        out_shape=jax.ShapeDtypeStruct(x.shape, F32),
        scratch_shapes=[pltpu.VMEM((tm, D_MODEL), BF16), pltpu.VMEM((tm, ROUTER_LANES), F32),
                        pltpu.VMEM((tm, D_MODEL), F32)],
        compiler_params=_params(("parallel", "parallel", "arbitrary"), 56),
        name="hier_moe",
    )(x, g, sc, sh, gt, w["wr"], w["br"], w["wg"], w["wu"], w["wd"], fg)


def _layer_weights(l, w_in, mla_q_norm, mla_w_uq, mla_kv_norm, mla_w_ukv, w_branch, w_out,
                   moe_w_group, moe_b_group, moe_w_expert, moe_b_expert, moe_w_gate, moe_w_up, moe_w_down):
    wi = w_in[l]
    b0 = MLA_Q_RANK
    b1 = b0 + MLA_KV_RANK
    b2 = b1 + MLA_ROPE
    b3 = b2 + HY_IN
    b4 = b3 + BRANCH_WIDTH
    b5 = b4 + 3 * BRANCH_WIDTH
    zpad = lambda r, c: jnp.zeros((r, c), F32)
    w_kr = wi[:, b1:b2]
    krp = jnp.concatenate([zpad(D_MODEL, MLA_NOPE), w_kr[:, 0::2], w_kr[:, 1::2],
                           zpad(D_MODEL, MLA_HEAD_PAD - MLA_NOPE - MLA_ROPE)], axis=1)
    w_na = wi[:, b4:b5]
    w_na = jnp.concatenate([w_na[:, :BRANCH_WIDTH] * NA_SCALE, w_na[:, BRANCH_WIDTH:]], axis=1)
    wa = jnp.concatenate([wi[:, :b1], krp, wi[:, b2:b4], w_na], axis=1).astype(BF16)
    uq = mla_w_uq[l].reshape(MLA_Q_RANK, MLA_HEADS, MLA_NOPE + MLA_ROPE) * (MLA_SCALE * LOG2E)
    uq = jnp.concatenate([uq[..., :MLA_NOPE], uq[..., MLA_NOPE::2], uq[..., MLA_NOPE + 1::2],
                          jnp.zeros((MLA_Q_RANK, MLA_HEADS, MLA_HEAD_PAD - MLA_NOPE - MLA_ROPE), F32)], axis=-1)
    ukv = mla_w_ukv[l].reshape(MLA_KV_RANK, MLA_HEADS, MLA_NOPE + MLA_V)
    uk = jnp.concatenate([ukv[..., :MLA_NOPE],
                          jnp.zeros((MLA_KV_RANK, MLA_HEADS, MLA_HEAD_PAD - MLA_NOPE), F32)], axis=-1)
    uv = ukv[..., MLA_NOPE:]
    wr = jnp.concatenate([moe_w_group[l], moe_w_expert[l],
                          zpad(D_MODEL, ROUTER_LANES - MOE_GROUPS - MOE_EXPERTS)], axis=1)
    br = jnp.concatenate([moe_b_group[l], moe_b_expert[l],
                          jnp.zeros((ROUTER_LANES - MOE_GROUPS - MOE_EXPERTS,), F32)])[None, :]
    return dict(
        wa=wa, qn=mla_q_norm[l][None, :], kvn=mla_kv_norm[l][None, :],
        wuq=uq.reshape(MLA_Q_RANK, -1).astype(BF16), wuk=uk.reshape(MLA_KV_RANK, -1).astype(BF16),
        wuv=uv.reshape(MLA_KV_RANK, -1).astype(BF16),
        wgate=wi[:, b5:].astype(BF16), wbranch=w_branch[l].astype(BF16), wout=w_out[l].astype(BF16),
        wr=wr.astype(BF16), br=br, wg=moe_w_gate[l].astype(BF16), wu=moe_w_up[l].astype(BF16),
        wd=moe_w_down[l].astype(BF16))


def _rope_tables(n):
    t = jnp.arange(n, dtype=jnp.int32)
    row = (t // GRID_W).astype(F32)
    col = (t % GRID_W).astype(F32)
    n_freq = MLA_ROPE // 4
    inv = ROPE_BASE ** (-jnp.arange(n_freq, dtype=F32) / n_freq)
    ang = jnp.concatenate([row[:, None] * inv, col[:, None] * inv], axis=-1)
    c, s = jnp.cos(ang), jnp.sin(ang)
    one = jnp.ones((n, MLA_NOPE), F32)
    zero = jnp.zeros((n, MLA_NOPE), F32)
    hr = MLA_ROPE // 2
    tail1 = jnp.ones((n, MLA_HEAD_PAD - MLA_NOPE - MLA_ROPE), F32)
    tail0 = jnp.zeros((n, MLA_HEAD_PAD - MLA_NOPE - MLA_ROPE), F32)
    cos = jnp.concatenate([one, c, c, tail1], axis=1)
    s1 = jnp.concatenate([zero, -s, jnp.zeros((n, hr), F32), tail0], axis=1)
    s2 = jnp.concatenate([zero, jnp.zeros((n, hr), F32), s, tail0], axis=1)
    return cos, s1, s2


def _identity_rope(n):
    return (jnp.ones((n, MLA_HEAD_PAD), F32), jnp.zeros((n, MLA_HEAD_PAD), F32),
            jnp.zeros((n, MLA_HEAD_PAD), F32))


def kernel(x, c, ctx, c_ctx, ada_w, ada_b, norm1_g, norm2_g, w_in, mla_q_norm, mla_w_uq, mla_kv_norm, mla_w_ukv, hy_short_w, hy_w1, hy_b1, hy_freq1, hy_w2, hy_b2, hy_freq2, hy_w3, hy_bias, na_rpb, w_branch, w_out, moe_w_group, moe_b_group, moe_w_expert, moe_b_expert, moe_w_gate, moe_w_up, moe_w_down, final_norm_g):
    b, n_lat, _ = x.shape
    n_ctx = ctx.shape[1]
    assert b <= 4 and n_lat % (NA_QROWS * GRID_W) == 0 and n_lat // GRID_W >= 2 * NA_QROWS

    cvec = jnp.concatenate([c, c_ctx[None, :], jnp.zeros((8 - b - 1, D_MODEL), F32)], axis=0)
    mod = _mod_vectors(cvec, ada_w, ada_b)

    rope_lat = _rope_tables(n_lat)
    rope_ctx = _identity_rope(n_ctx)
    tabs_lat = _dft_tables(n_lat)
    tabs_ctx = _small_dft_tables(n_ctx)
    fg = final_norm_g[None, :]

    for l in range(DEPTH):
        last = l == DEPTH - 1
        w = _layer_weights(l, w_in, mla_q_norm, mla_w_uq, mla_kv_norm, mla_w_ukv, w_branch, w_out,
                           moe_w_group, moe_b_group, moe_w_expert, moe_b_expert,
                           moe_w_gate, moe_w_up, moe_w_down)
        ml = mod[l, :b].reshape(b, 1, 6, D_MODEL)
        mc = jnp.broadcast_to(mod[l, b].reshape(1, 1, 6, D_MODEL), (b, 1, 6, D_MODEL))
        sh1, sc1, gt1, sh2, sc2, gt2 = [ml[:, :, i] for i in range(6)]
        sh1c, sc1c, gt1c, sh2c, sc2c, gt2c = [mc[:, :, i] for i in range(6)]
        g1 = norm1_g[l][None, :]
        g2 = norm2_g[l][None, :]
        fw = dict(
            w1=jnp.concatenate([hy_w1[l], jnp.zeros((HY_EMB_PAD - HY_EMB, HY_HIDDEN), F32)], axis=0).T.astype(BF16),
            b1=hy_b1[l][:, None], f1=hy_freq1[l][:, None], w2=hy_w2[l].T.astype(BF16), b2=hy_b2[l][:, None],
            f2=hy_freq2[l][:, None],
            w3=jnp.transpose(hy_w3[l].reshape(HY_HIDDEN, 2, HY_ORDER * HY_WIDTH), (1, 2, 0)).astype(BF16))

        n_keys = n_lat + n_ctx
        kv_bufs = (jnp.zeros((b, MLA_HEADS, n_keys, MLA_HEAD_PAD), BF16),
                   jnp.zeros((b, MLA_HEADS, MLA_VT_ROWS, n_keys), PV_DTYPE))
        q, kall, vtall, hy, fn, naq, nak, nav = _inproj(x, g1, sc1, sh1, w, rope_lat, kv_bufs, 0, False)
        qc, kc, vc, kall, vtall, hyc, fnc, naqc, nakc, navc = _inproj(
            ctx, g1, sc1c, sh1c, w, rope_ctx, (kall, vtall), n_lat, True)

        oa = _flash_t(q, kall, vtall, MLA_V)
        od = _na(naq, nak, nav, nakc, navc, _na_bias_tables(na_rpb[l]))
        ob = _hyena_long(hy, hy_short_w[l], fw, hy_bias[l], tabs_lat)
        oc = _fnet_long(fn, tabs_lat)
        x_new = _merge(x, g1, sc1, sh1, gt1, oa, ob, oc, od, w)

        if not last:
            oa_c = _flash(qc, kc, vc, True)
            heads = lambda t: jnp.transpose(t.reshape(b, n_ctx, NA_HEADS, NA_HEAD_DIM), (0, 2, 1, 3))
            od_c = _flash(heads(naqc), heads(nakc), heads(jnp.swapaxes(navc, 1, 2)), False)
            ob_c = _hyena_short(hyc, hy_short_w[l], fw, hy_bias[l], tabs_ctx)
            ob_c = jnp.transpose(ob_c.reshape(n_ctx, b, HY_WIDTH), (1, 0, 2))
            oc_c = _fnet_short(fnc, tabs_ctx)
            ctx = _merge(ctx, g1, sc1c, sh1c, gt1c, oa_c, ob_c, oc_c, od_c, w)
            ctx = _moe(ctx, g2, sc2c, sh2c, gt2c, w, fg, False)
        x = _moe(x_new, g2, sc2, sh2, gt2, w, fg, last)
    return x
```

```python
import functools
import math

import numpy as np
import jax
import jax.numpy as jnp
from jax import lax
from jax.experimental import pallas as pl
from jax.experimental.pallas import tpu as pltpu

F32 = jnp.float32
BF16 = jnp.bfloat16

D_MODEL = 1024
DEPTH = 2
GRID_W = 64
NORM_EPS = 1e-6
MASK_VALUE = -1e30
BRANCH_WIDTH = D_MODEL // 4
N_BRANCH = 4

MLA_HEADS = 4
MLA_Q_RANK = D_MODEL // 4
MLA_KV_RANK = D_MODEL // 8
MLA_NOPE = 64
MLA_ROPE = 32
MLA_V = BRANCH_WIDTH // MLA_HEADS
MLA_SCALE = 1.0 / math.sqrt(MLA_NOPE + MLA_ROPE)
MLA_HEAD_PAD = 128
MLA_VT_ROWS = MLA_V + 16
PV_DTYPE = BF16
ROPE_BASE = 10000.0
LOG2E = 1.4426950408889634

HY_WIDTH = BRANCH_WIDTH
HY_ORDER = 2
HY_BANDS = 16
HY_EMB = 1 + 2 * HY_BANDS
HY_EMB_PAD = 128
HY_HIDDEN = 64
HY_DECAY_TARGET = 1e-2
HY_DECAY_SHORT_PCT = 0.3
HY_DECAY_LONG_PCT = 1.5
HY_IN = (HY_ORDER + 1) * HY_WIDTH

FN_GROUPS = 4
FN_GROUP_DIM = BRANCH_WIDTH // FN_GROUPS

NA_HEADS = 4
NA_HEAD_DIM = BRANCH_WIDTH // NA_HEADS
NA_KH = 8
NA_KW = 16
NA_SCALE = 1.0 / math.sqrt(NA_HEAD_DIM)
NA_QROWS = 8
NA_KROWS = 16
NA_VT_PAD = 16

MOE_GROUPS = 4
MOE_PER_GROUP = 4
MOE_EXPERTS = MOE_GROUPS * MOE_PER_GROUP
MOE_HIDDEN = 256
ROUTER_LANES = 128

MLA_IN = MLA_Q_RANK + MLA_KV_RANK + MLA_ROPE
MIX_IN = MLA_IN + HY_IN + BRANCH_WIDTH + 3 * BRANCH_WIDTH
MIX_PAD = MLA_Q_RANK + MLA_KV_RANK + MLA_HEAD_PAD + HY_IN + BRANCH_WIDTH + 3 * BRANCH_WIDTH

DFT_N2 = 256
FN_N2 = 128
BMM_GROUPS_PER_STEP = 4
FLASH_K_TILE = 3328
FLASH_CHUNKS = 3
FLASH_Q_TILE = 2048
FLASH_Q_SUB = 512


def _params(sem, vmem_mb=None):
    kw = dict(dimension_semantics=sem)
    if vmem_mb is not None:
        kw["vmem_limit_bytes"] = vmem_mb << 20
    return pltpu.CompilerParams(**kw)


def _pick_tile(n, cap, mult):
    best = None
    for t in range(mult, min(n, cap) + 1, mult):
        if n % t == 0:
            best = t
    return best if best is not None else n


def _rms(x):
    return x * lax.rsqrt(jnp.mean(x * x, axis=-1, keepdims=True) + NORM_EPS)


def _normmod(x, g, sc, sh):
    return (_rms(x) * g) * (1.0 + sc) + sh


def _dot(a, b):
    return jnp.dot(a, b, preferred_element_type=F32)


def _dot_nt(a, b):
    return lax.dot_general(a, b, (((1,), (1,)), ((), ())), preferred_element_type=F32)


def _mod_kernel(c_ref, w_ref, b_ref, o_ref):
    c = c_ref[...]
    s = c * jax.nn.sigmoid(c)
    o_ref[0] = _dot(s.astype(BF16), w_ref[0].astype(BF16)) + b_ref[0]


def _mod_vectors(cvec, ada_w, ada_b):
    n6 = ada_w.shape[-1]
    tn = 1024
    return pl.pallas_call(
        _mod_kernel,
        grid=(DEPTH, n6 // tn),
        in_specs=[pl.BlockSpec((8, D_MODEL), lambda l, j: (0, 0)),
                  pl.BlockSpec((1, D_MODEL, tn), lambda l, j: (l, 0, j)),
                  pl.BlockSpec((1, 1, tn), lambda l, j: (l, 0, j))],
        out_specs=pl.BlockSpec((1, 8, tn), lambda l, j: (l, 0, j)),
        out_shape=jax.ShapeDtypeStruct((DEPTH, 8, n6), F32),
        compiler_params=_params(("parallel", "parallel")),
        name="adaln_vectors",
    )(cvec, ada_w, ada_b.reshape(DEPTH, 1, n6))


def _inproj_kernel(*refs, ctx_mode):
    (x_ref, g_ref, sc_ref, sh_ref, wa_ref, qn_ref, kvn_ref, wuq_ref, wuk_ref, wuv_ref,
     cos_ref, s1_ref, s2_ref) = refs[:13]
    if ctx_mode:
        (q_ref, k_ref, v_ref, kall_ref, vtall_ref, hy_ref, fn_ref, naq_ref, nak_ref, nav_ref) = refs[15:]
    else:
        (q_ref, kall_ref, vtall_ref, hy_ref, fn_ref, naq_ref, nak_ref, nav_ref) = refs[15:]
    h = _normmod(x_ref[0], g_ref[...], sc_ref[0], sh_ref[0]).astype(BF16)
    p = _dot(h, wa_ref[...])
    c0 = MLA_Q_RANK
    c1 = c0 + MLA_KV_RANK
    c2 = c1 + MLA_HEAD_PAD
    c3 = c2 + HY_IN
    c4 = c3 + BRANCH_WIDTH
    c5 = c4 + BRANCH_WIDTH
    c6 = c5 + BRANCH_WIDTH
    cq, ckv, krp = p[:, :c0], p[:, c0:c1], p[:, c1:c2]
    hy_ref[0] = p[:, c2:c3]
    fn_ref[0] = p[:, c3:c4]
    naq_ref[0] = p[:, c4:c5].astype(BF16)
    nak_ref[0] = p[:, c5:c6].astype(BF16)
    nav_ref[0] = p[:, c6:].T.astype(BF16)
    cqn = (_rms(cq) * qn_ref[...]).astype(BF16)
    ckn = (_rms(ckv) * kvn_ref[...]).astype(BF16)
    q = _dot(cqn, wuq_ref[...])
    kn = _dot(ckn, wuk_ref[...])
    v = _dot(ckn, wuv_ref[...])
    cos, s1, s2 = cos_ref[...], s1_ref[...], s2_ref[...]

    def rope(t):
        return (t * cos + pltpu.roll(t, MLA_HEAD_PAD - MLA_ROPE // 2, 1) * s1
                + pltpu.roll(t, MLA_ROPE // 2, 1) * s2)

    kr = rope(krp)
    vt = v.T
    tail_rows = lax.broadcasted_iota(jnp.int32, (MLA_VT_ROWS - MLA_V, v.shape[0]), 0)
    tail = jnp.where(tail_rows == 0, 1.0, 0.0).astype(PV_DTYPE)
    for hh in range(MLA_HEADS):
        sl = slice(hh * MLA_HEAD_PAD, (hh + 1) * MLA_HEAD_PAD)
        vs = slice(hh * MLA_V, (hh + 1) * MLA_V)
        kh = (kn[:, sl] + kr).astype(BF16)
        qh = rope(q[:, sl])
        q_ref[0, hh] = qh.astype(BF16) if ctx_mode else qh.T.astype(BF16)
        kall_ref[0, hh] = kh
        vtall_ref[0, hh, :MLA_V] = vt[vs].astype(PV_DTYPE)
        vtall_ref[0, hh, MLA_V:] = tail
        if ctx_mode:
            k_ref[0, hh] = kh
            v_ref[0, hh] = v[:, vs].astype(BF16)


def _inproj(x, g, sc, sh, w, rope_tabs, kv_bufs, key_off, ctx_mode):
    b, n, _ = x.shape
    n_keys = kv_bufs[0].shape[2]
    tm = _pick_tile(n, 512, 128)
    assert key_off % tm == 0
    ko = key_off // tm
    cos, s1, s2 = rope_tabs
    tok = lambda bb, i: (bb, i, 0)
    row = lambda bb, i: (bb, 0, 0)
    const2 = lambda bb, i: (0, 0)
    hd = lambda bb, i: (bb, 0, i, 0)
    tab = lambda bb, i: (i, 0)
    hp = MLA_HEADS * MLA_HEAD_PAD
    in_specs = [pl.BlockSpec((1, tm, D_MODEL), tok),
                pl.BlockSpec((1, D_MODEL), const2),
                pl.BlockSpec((1, 1, D_MODEL), row),
                pl.BlockSpec((1, 1, D_MODEL), row),
                pl.BlockSpec((D_MODEL, MIX_PAD), const2),
                pl.BlockSpec((1, MLA_Q_RANK), const2),
                pl.BlockSpec((1, MLA_KV_RANK), const2),
                pl.BlockSpec((MLA_Q_RANK, hp), const2),
                pl.BlockSpec((MLA_KV_RANK, hp), const2),
                pl.BlockSpec((MLA_KV_RANK, BRANCH_WIDTH), const2),
                pl.BlockSpec((tm, MLA_HEAD_PAD), tab),
                pl.BlockSpec((tm, MLA_HEAD_PAD), tab),
                pl.BlockSpec((tm, MLA_HEAD_PAD), tab)]
    args = [x, g, sc, sh, w["wa"], w["qn"], w["kvn"], w["wuq"], w["wuk"], w["wuv"], cos, s1, s2]
    q_spec = pl.BlockSpec((1, MLA_HEADS, tm, MLA_HEAD_PAD), hd)
    q_sd = jax.ShapeDtypeStruct((b, MLA_HEADS, n, MLA_HEAD_PAD), BF16)
    kall_spec = pl.BlockSpec((1, MLA_HEADS, tm, MLA_HEAD_PAD), lambda bb, i: (bb, 0, ko + i, 0))
    vtall_spec = pl.BlockSpec((1, MLA_HEADS, MLA_VT_ROWS, tm), lambda bb, i: (bb, 0, 0, ko + i))
    kall_sd = jax.ShapeDtypeStruct((b, MLA_HEADS, n_keys, MLA_HEAD_PAD), BF16)
    vtall_sd = jax.ShapeDtypeStruct((b, MLA_HEADS, MLA_VT_ROWS, n_keys), PV_DTYPE)
    rest_specs = [pl.BlockSpec((1, tm, HY_IN), tok)] + [pl.BlockSpec((1, tm, BRANCH_WIDTH), tok)] * 3 \
        + [pl.BlockSpec((1, BRANCH_WIDTH, tm), lambda bb, i: (bb, 0, i))]
    rest_sd = [jax.ShapeDtypeStruct((b, n, HY_IN), F32), jax.ShapeDtypeStruct((b, n, BRANCH_WIDTH), F32)] \
        + [jax.ShapeDtypeStruct((b, n, BRANCH_WIDTH), BF16)] * 2 \
        + [jax.ShapeDtypeStruct((b, BRANCH_WIDTH, n), BF16)]
    in_specs += [pl.BlockSpec(memory_space=pl.ANY), pl.BlockSpec(memory_space=pl.ANY)]
    args += list(kv_bufs)
    if ctx_mode:
        out_specs = [q_spec, q_spec, pl.BlockSpec((1, MLA_HEADS, tm, MLA_V), hd), kall_spec, vtall_spec]
        out_shape = [q_sd, q_sd, jax.ShapeDtypeStruct((b, MLA_HEADS, n, MLA_V), BF16), kall_sd, vtall_sd]
        aliases = {13: 3, 14: 4}
    else:
        qt_spec = pl.BlockSpec((1, MLA_HEADS, MLA_HEAD_PAD, tm), lambda bb, i: (bb, 0, 0, i))
        qt_sd = jax.ShapeDtypeStruct((b, MLA_HEADS, MLA_HEAD_PAD, n), BF16)
        out_specs = [qt_spec, kall_spec, vtall_spec]
        out_shape = [qt_sd, kall_sd, vtall_sd]
        aliases = {13: 1, 14: 2}
    return pl.pallas_call(
        functools.partial(_inproj_kernel, ctx_mode=ctx_mode),
        grid=(b, n // tm),
        in_specs=in_specs,
        out_specs=out_specs + rest_specs,
        out_shape=out_shape + rest_sd,
        input_output_aliases=aliases,
        compiler_params=_params(("parallel", "parallel"), 48),
        name="input_projection",
    )(*args)


def _flash_kernel(q_ref, k_ref, v_ref, o_ref, m_ref, l_ref, acc_ref, *, nh, use_exp2):
    j = pl.program_id(2)

    @pl.when(j == 0)
    def _():
        m_ref[...] = jnp.full(m_ref.shape, MASK_VALUE, F32)
        l_ref[...] = jnp.zeros(l_ref.shape, F32)
        acc_ref[...] = jnp.zeros(acc_ref.shape, F32)

    ex = jnp.exp2 if use_exp2 else jnp.exp
    for h in range(nh):
        s = _dot_nt(q_ref[0, h], k_ref[0, h])
        m_prev = m_ref[h]
        m_cur = jnp.maximum(m_prev, jnp.max(s, axis=1, keepdims=True))
        alpha = ex(m_prev - m_cur)
        p = ex(s - m_cur)
        l_ref[h] = alpha * l_ref[h] + jnp.sum(p, axis=1, keepdims=True)
        acc_ref[h] = alpha * acc_ref[h] + _dot(p.astype(BF16), v_ref[0, h])
        m_ref[h] = m_cur

    @pl.when(j == pl.num_programs(2) - 1)
    def _():
        outs = [acc_ref[h] / l_ref[h] for h in range(nh)]
        o_ref[0] = jnp.concatenate(outs, axis=1).astype(o_ref.dtype)


def _flash(q, k, v, use_exp2):
    b, nh, lq, dk = q.shape
    lk, dv = k.shape[2], v.shape[3]
    tq = _pick_tile(lq, 512, 8)
    tk = _pick_tile(lk, 1280, 128)
    return pl.pallas_call(
        functools.partial(_flash_kernel, nh=nh, use_exp2=use_exp2),
        grid=(b, lq // tq, lk // tk),
        in_specs=[pl.BlockSpec((1, nh, tq, dk), lambda bb, i, j: (bb, 0, i, 0)),
                  pl.BlockSpec((1, nh, tk, dk), lambda bb, i, j: (bb, 0, j, 0)),
                  pl.BlockSpec((1, nh, tk, dv), lambda bb, i, j: (bb, 0, j, 0))],
        out_specs=pl.BlockSpec((1, tq, nh * dv), lambda bb, i, j: (bb, i, 0)),
        out_shape=jax.ShapeDtypeStruct((b, lq, nh * dv), BF16),
        scratch_shapes=[pltpu.VMEM((nh, tq, 1), F32), pltpu.VMEM((nh, tq, 1), F32),
                        pltpu.VMEM((nh, tq, dv), F32)],
        compiler_params=_params(("parallel", "parallel", "arbitrary"), 48),
        name="softmax_attention",
    )(q, k, v)


def _flash_t_kernel(q_ref, k_ref, vt_ref, o_ref, m_ref, acc_ref, st_ref, p_ref, *, nh, dv):
    j = pl.program_id(2)

    @pl.when(j == 0)
    def _():
        m_ref[...] = jnp.full(m_ref.shape, MASK_VALUE, F32)
        acc_ref[...] = jnp.zeros(acc_ref.shape, F32)

    rows = vt_ref.shape[2]

    tq = q_ref.shape[3]
    qs = min(tq, FLASH_Q_SUB)
    units = [(h, c) for h in range(nh) for c in range(tq // qs)]

    def scores(i):
        h, c = units[i]
        cols = slice(c * qs, (c + 1) * qs)
        st = _dot(k_ref[0, h], q_ref[0, h, :, cols])
        st_ref[i % 2] = st
        m_prev = m_ref[h, :, cols]
        m_cur = jnp.maximum(m_prev, jnp.max(st, axis=0, keepdims=True))
        m_ref[h, :, cols] = m_cur
        return m_prev, m_cur

    def probs(i, m_cur):
        p_ref[i % 2] = jnp.exp2(st_ref[i % 2] - m_cur).astype(PV_DTYPE)

    def values(i, m_prev, m_cur):
        h, c = units[i]
        cols = slice(c * qs, (c + 1) * qs)
        alpha = jnp.exp2(m_prev - m_cur)
        acc_ref[h, :rows, cols] = alpha * acc_ref[h, :rows, cols] + _dot(vt_ref[0, h], p_ref[i % 2])

    n_units = len(units)
    tk = k_ref.shape[2]
    kt = tk // 256
    n_chunks = min(FLASH_CHUNKS, kt) if tk % 256 == 0 else 1
    v_edges = [256 * ((kt * c) // n_chunks) for c in range(n_chunks + 1)] if n_chunks > 1 else [0, tk]
    s_edges = [16 * (((tk // 16) * c) // n_chunks) for c in range(n_chunks + 1)]

    def scores_chunk(i, c, m_run):
        h, cc = units[i]
        cols = slice(cc * qs, (cc + 1) * qs)
        r0, r1 = s_edges[c], s_edges[c + 1]
        st = _dot(k_ref[0, h, r0:r1, :], q_ref[0, h, :, cols])
        st_ref[i % 2, r0:r1, :] = st
        return jnp.maximum(m_run, jnp.max(st, axis=0, keepdims=True))

    def values_chunk(i, c):
        h, _ = units[i]
        r0, r1 = v_edges[c], v_edges[c + 1]
        return _dot(vt_ref[0, h, :, r0:r1], p_ref[i % 2, r0:r1, :])

    stats = {0: scores(0)}
    if n_units > 1:
        stats[1] = scores(1)
    probs(0, stats[0][1])
    for i in range(n_units):
        h, cc = units[i]
        cols = slice(cc * qs, (cc + 1) * qs)
        m_prev_i, m_cur_i = stats.pop(i)
        nxt = i + 2 < n_units
        if nxt:
            h2, c2 = units[i + 2]
            cols2 = slice(c2 * qs, (c2 + 1) * qs)
            m_prev2 = m_ref[h2, :, cols2]
            m_run = m_prev2
        pv = None
        for c in range(n_chunks):
            part = values_chunk(i, c)
            pv = part if pv is None else pv + part
            if nxt:
                m_run = scores_chunk(i + 2, c, m_run)
        alpha = jnp.exp2(m_prev_i - m_cur_i)
        acc_ref[h, :rows, cols] = alpha * acc_ref[h, :rows, cols] + pv
        if nxt:
            m_ref[h2, :, cols2] = m_run
            stats[i + 2] = (m_prev2, m_run)
        if i + 1 < n_units:
            probs(i + 1, stats[i + 1][1])

    @pl.when(j == pl.num_programs(2) - 1)
    def _():
        outs = []
        for h in range(nh):
            a = acc_ref[h]
            o = (a / a[dv:dv + 1]).T
            outs.append(o[:, :dv])
        o_ref[0] = jnp.concatenate(outs, axis=1).astype(o_ref.dtype)


def _flash_t(q, k, vt, dv):
    b, nh, dk, lq = q.shape
    lk, rows = k.shape[2], vt.shape[2]
    tq = _pick_tile(lq, FLASH_Q_TILE, 128)
    tk = _pick_tile(lk, FLASH_K_TILE, 128)
    return pl.pallas_call(
        functools.partial(_flash_t_kernel, nh=nh, dv=dv),
        grid=(b, lq // tq, lk // tk),
        in_specs=[pl.BlockSpec((1, nh, dk, tq), lambda bb, i, j: (bb, 0, 0, i)),
                  pl.BlockSpec((1, nh, tk, dk), lambda bb, i, j: (bb, 0, j, 0)),
                  pl.BlockSpec((1, nh, rows, tk), lambda bb, i, j: (bb, 0, 0, j))],
        out_specs=pl.BlockSpec((1, tq, nh * dv), lambda bb, i, j: (bb, i, 0)),
        out_shape=jax.ShapeDtypeStruct((b, lq, nh * dv), BF16),
        scratch_shapes=[pltpu.VMEM((nh, 1, tq), F32), pltpu.VMEM((nh, 128, tq), F32),
                        pltpu.VMEM((2, tk, min(tq, FLASH_Q_SUB)), F32),
                        pltpu.VMEM((2, tk, min(tq, FLASH_Q_SUB)), PV_DTYPE)],
        compiler_params=_params(("parallel", "parallel", "arbitrary"), 56),
        name="softmax_attention_t",
    )(q, k, vt)


def _na_kernel(q_ref, k0, k1, k2, k3, v0, v1, v2, v3, kc_ref, vc_ref, b_ref, o_ref):
    q = q_ref[0]
    kcat = jnp.concatenate([k0[0], k1[0], k2[0], k3[0]], axis=0)
    vtcat = jnp.concatenate([v0[0], v1[0], v2[0], v3[0]], axis=1)
    kc, vtc = kc_ref[0], vc_ref[0]
    dh = NA_HEAD_DIM

    def with_ones(vt):
        pad = lax.broadcasted_iota(jnp.int32, (NA_VT_PAD, vt.shape[1]), 0)
        return jnp.concatenate([vt, jnp.where(pad == 0, 1.0, 0.0).astype(vt.dtype)], axis=0)

    def scores(h):
        sl = slice(h * dh, (h + 1) * dh)
        qh = q[:, sl]
        st = _dot_nt(kcat[:, sl], qh) + b_ref[0, h]
        stc = _dot_nt(kc[:, sl], qh)
        m = jnp.maximum(jnp.max(st, axis=0, keepdims=True), jnp.max(stc, axis=0, keepdims=True))
        return st, stc, m

    def values(h, st, stc, m):
        sl = slice(h * dh, (h + 1) * dh)
        p = jnp.exp(st - m).astype(BF16)
        pc = jnp.exp(stc - m).astype(BF16)
        acc = _dot(with_ones(vtcat[sl]), p) + _dot(with_ones(vtc[sl]), pc)
        o = jnp.concatenate([acc / acc[dh:dh + 1],
                             jnp.zeros((128 - dh - NA_VT_PAD, acc.shape[1]), F32)], axis=0)
        return o.T[:, :dh]

    outs = []
    pending = scores(0)
    for h in range(NA_HEADS):
        nxt = scores(h + 1) if h + 1 < NA_HEADS else None
        outs.append(values(h, *pending))
        pending = nxt
    o_ref[0] = jnp.concatenate(outs, axis=1).astype(o_ref.dtype)


def _na_bias_tables(rpb):
    j = np.arange(NA_QROWS)[:, None]
    m = np.arange(NA_KROWS)[None, :]
    valid = np.zeros((3, NA_QROWS, NA_KROWS), bool)
    st0 = np.maximum(j - NA_KH // 2, 0)
    valid[0] = (m - 4 >= st0) & (m - 4 < st0 + NA_KH)
    valid[1] = (m - j >= 0) & (m - j < NA_KH)
    st2 = 4 + np.minimum(j - NA_KH // 2, 0)
    valid[2] = (m < 12) & (m >= st2) & (m < st2 + NA_KH)
    dr_idx = np.clip(m - 4 - j + NA_KH - 1, 0, 2 * NA_KH - 2)
    qc = np.arange(GRID_W)[:, None]
    kc = np.arange(GRID_W)[None, :]
    dc_idx = np.clip(kc - qc, -(NA_KW - 1), NA_KW - 1) + NA_KW - 1
    cs = np.clip(qc - NA_KW // 2, 0, GRID_W - NA_KW)
    col_ok = (kc >= cs) & (kc < cs + NA_KW)
    t = rpb.astype(F32)[:, dr_idx]
    t = t[..., dc_idx]
    t = jnp.transpose(t, (0, 2, 4, 1, 3))
    ok = (valid.transpose(0, 2, 1)[:, None, :, None, :, None]
          & col_ok.T[None, None, None, :, None, :])
    out = jnp.where(jnp.asarray(ok), t[None], MASK_VALUE)
    return out.reshape(3, NA_HEADS, NA_KROWS * GRID_W, NA_QROWS * GRID_W)


def _na(q, k, vt, kc, vtc, bias):
    b, n, w = q.shape
    c = kc.shape[1]
    tq = NA_QROWS * GRID_W
    tkb = 4 * GRID_W
    nb = n // tq
    nkb = n // tkb
    blocks = [lambda i: jnp.maximum(2 * i - 1, 0), lambda i: 2 * i, lambda i: 2 * i + 1,
              lambda i: jnp.minimum(2 * i + 2, nkb - 1)]
    kspecs = [pl.BlockSpec((1, tkb, w), lambda bb, i, f=f: (bb, f(i), 0)) for f in blocks]
    vspecs = [pl.BlockSpec((1, w, tkb), lambda bb, i, f=f: (bb, 0, f(i))) for f in blocks]
    return pl.pallas_call(
        _na_kernel,
        grid=(b, nb),
        in_specs=[pl.BlockSpec((1, tq, w), lambda bb, i: (bb, i, 0))] + kspecs + vspecs
                 + [pl.BlockSpec((1, c, w), lambda bb, i: (bb, 0, 0)),
                    pl.BlockSpec((1, w, c), lambda bb, i: (bb, 0, 0)),
                    pl.BlockSpec((1, NA_HEADS, NA_KROWS * GRID_W, tq),
                                 lambda bb, i: (jnp.where(i == 0, 0, jnp.where(i == nb - 1, 2, 1)), 0, 0, 0))],
        out_specs=pl.BlockSpec((1, tq, w), lambda bb, i: (bb, i, 0)),
        out_shape=jax.ShapeDtypeStruct((b, n, w), BF16),
        compiler_params=_params(("parallel", "arbitrary"), 56),
        name="neighbourhood_attention",
    )(q, k, k, k, k, vt, vt, vt, vt, kc, vtc, bias)


def _sconv_kernel(u_ref, pv_ref, nx_ref, w_ref, v_ref, x1_ref, x2_ref, *, nt):
    i = pl.program_id(1)
    u = u_ref[0]
    tl = u.shape[0]
    prev = jnp.where(i > 0, pv_ref[0, 7:8, :], 0.0)
    nxt = jnp.where(i < nt - 1, nx_ref[0, 0:1, :], 0.0)
    rows = lax.broadcasted_iota(jnp.int32, u.shape, 0)
    um = jnp.where(rows == 0, prev, pltpu.roll(u, 1, 0))
    up = jnp.where(rows == tl - 1, nxt, pltpu.roll(u, tl - 1, 0))
    w = w_ref[...]
    y = um * w[0:1] + u * w[1:2] + up * w[2:3]
    v_ref[...] = y[:, :HY_WIDTH]
    x1_ref[...] = y[:, HY_WIDTH:2 * HY_WIDTH]
    x2_ref[...] = y[:, 2 * HY_WIDTH:]


def _sconv(u, w, seq_major):
    b, n, _ = u.shape
    tl = _pick_tile(n, 512, 8)
    nt = n // tl
    r8 = tl // 8
    if seq_major:
        osd = jax.ShapeDtypeStruct((n, b * HY_WIDTH), F32)
        ospec = pl.BlockSpec((tl, HY_WIDTH), lambda bb, i: (i, bb))
    else:
        osd = jax.ShapeDtypeStruct((b * n, HY_WIDTH), F32)
        ospec = pl.BlockSpec((tl, HY_WIDTH), lambda bb, i: (bb * nt + i, 0))
    return pl.pallas_call(
        functools.partial(_sconv_kernel, nt=nt),
        grid=(b, nt),
        in_specs=[pl.BlockSpec((1, tl, HY_IN), lambda bb, i: (bb, i, 0)),
                  pl.BlockSpec((1, 8, HY_IN), lambda bb, i: (bb, jnp.maximum(i * r8 - 1, 0), 0)),
                  pl.BlockSpec((1, 8, HY_IN), lambda bb, i: (bb, jnp.minimum((i + 1) * r8, n // 8 - 1), 0)),
                  pl.BlockSpec((3, HY_IN), lambda bb, i: (0, 0))],
        out_specs=[ospec, ospec, ospec],
        out_shape=[osd, osd, osd],
        compiler_params=_params(("parallel", "parallel")),
        name="hyena_short_conv",
    )(u, u, u, w)


def _filt_kernel(z_ref, w1_ref, b1_ref, f1_ref, w2_ref, b2_ref, f2_ref, w3_ref, win_ref, g_ref, ss_ref):
    d = pl.program_id(0)
    i = pl.program_id(1)
    h = jnp.sin(f1_ref[...] * (_dot(w1_ref[...], z_ref[0].astype(BF16)) + b1_ref[...]))
    h = jnp.sin(f2_ref[...] * (_dot(w2_ref[...], h.astype(BF16)) + b2_ref[...]))
    h = _dot(w3_ref[0], h.astype(BF16)).T
    h = h * jnp.concatenate([win_ref[0]] * HY_ORDER, axis=1)
    rows = lax.broadcasted_iota(jnp.int32, h.shape, 0)
    h = jnp.where((rows == 0) & (i == 0) & (d == 1), 0.0, h)
    g_ref[0] = h
    ss = jnp.sum(h * h, axis=0, keepdims=True)
    first = (i == 0) & (d == 0)

    @pl.when(first)
    def _():
        ss_ref[...] = ss

    @pl.when(jnp.logical_not(first))
    def _():
        ss_ref[...] += ss


def _padded_filter(n, fw):
    r = jnp.arange(n, dtype=jnp.int32)
    idx = jnp.stack([r, jnp.mod(n - r, n)], axis=0).astype(F32)[:, :, None]
    t = idx / (n - 1)
    bands = jnp.linspace(1e-4, HY_BANDS - 1, HY_BANDS, dtype=F32)
    ang = (2.0 * math.pi / n) * idx * bands
    z = jnp.concatenate([t, jnp.cos(ang), -jnp.sin(ang), jnp.zeros((2, n, HY_EMB_PAD - HY_EMB), F32)], axis=-1)
    z = jnp.swapaxes(z, 1, 2)
    deltas = jnp.linspace(math.log(HY_DECAY_TARGET) / HY_DECAY_LONG_PCT,
                          math.log(HY_DECAY_TARGET) / HY_DECAY_SHORT_PCT, HY_WIDTH, dtype=F32)
    window = jnp.exp(-t * jnp.abs(deltas))
    tl = _pick_tile(n, 512, 128)
    half = HY_ORDER * HY_WIDTH
    c2 = lambda d, i: (0, 0)
    g, ss = pl.pallas_call(
        _filt_kernel,
        grid=(2, n // tl),
        in_specs=[pl.BlockSpec((1, HY_EMB_PAD, tl), lambda d, i: (d, 0, i)),
                  pl.BlockSpec((HY_HIDDEN, HY_EMB_PAD), c2), pl.BlockSpec((HY_HIDDEN, 1), c2),
                  pl.BlockSpec((HY_HIDDEN, 1), c2),
                  pl.BlockSpec((HY_HIDDEN, HY_HIDDEN), c2), pl.BlockSpec((HY_HIDDEN, 1), c2),
                  pl.BlockSpec((HY_HIDDEN, 1), c2),
                  pl.BlockSpec((1, half, HY_HIDDEN), lambda d, i: (d, 0, 0)),
                  pl.BlockSpec((1, tl, HY_WIDTH), lambda d, i: (d, i, 0))],
        out_specs=[pl.BlockSpec((1, tl, half), lambda d, i: (d, i, 0)), pl.BlockSpec((1, half), c2)],
        out_shape=[jax.ShapeDtypeStruct((2, n, half), F32), jax.ShapeDtypeStruct((1, half), F32)],
        compiler_params=_params(("arbitrary", "arbitrary")),
        name="hyena_filter_mlp",
    )(z, fw["w1"], fw["b1"], fw["f1"], fw["w2"], fw["b2"], fw["f2"], fw["w3"], window)
    return g.reshape(2 * n, half), lax.rsqrt(ss + NORM_EPS)


def _colmm_kernel(*refs, epi):
    w_ref, x_ref = refs[0], refs[1]
    o_ref = refs[-1]
    acc = _dot(w_ref[...], x_ref[...].astype(BF16))
    if epi == "gate":
        xg_ref, z_ref, b_ref = refs[2:5]
        acc = xg_ref[...] * (acc + z_ref[...] * b_ref[...])
    elif epi == "scale":
        acc = acc * refs[2][...]
    elif epi == "cmul":
        g = refs[2][...]
        mh = acc.shape[0] // 2
        ar, ai, gr, gi = acc[:mh], acc[mh:], g[:mh], g[mh:]
        acc = jnp.concatenate([ar * gr - ai * gi, ar * gi + ai * gr], axis=0)
    o_ref[...] = acc.astype(o_ref.dtype)


def _colmm(w, x, out_dtype, epi="none", extra=(), tn_cap=4096, g_col=None):
    m, k = w.shape
    nc = x.shape[1]
    tn = _pick_tile(nc, tn_cap, 128)
    col = lambda j: (0, j)
    specs = [pl.BlockSpec((m, k), lambda j: (0, 0)), pl.BlockSpec((k, tn), col)]
    if epi == "gate":
        specs += [pl.BlockSpec((m, tn), col), pl.BlockSpec((m, tn), col), pl.BlockSpec((1, tn), col)]
    elif epi == "scale":
        specs += [pl.BlockSpec((1, tn), col)]
    elif epi == "cmul":
        specs += [pl.BlockSpec((m, tn), lambda j: (0, g_col))]
    return pl.pallas_call(
        functools.partial(_colmm_kernel, epi=epi),
        grid=(nc // tn,),
        in_specs=specs,
        out_specs=pl.BlockSpec((m, tn), col),
        out_shape=jax.ShapeDtypeStruct((m, nc), out_dtype),
        compiler_params=_params(("parallel",), 48),
        name="dft_stage_" + epi,
    )(w, x, *extra)


def _bmm_kernel(m_ref, x_ref, o_ref, *, gs):
    for s in range(gs):
        x = x_ref[:, s]
        x = x.reshape(x.shape[0] * x.shape[1], x.shape[2])
        acc = _dot(m_ref[s], x)
        c = acc.shape[1]
        o_ref[:, s * c:(s + 1) * c] = acc.astype(o_ref.dtype)


def _bmm(mats, x, out_dtype):
    ng, p, qq = mats.shape
    c = x.shape[3]
    gs = _pick_tile(ng, BMM_GROUPS_PER_STEP, 1)
    return pl.pallas_call(
        functools.partial(_bmm_kernel, gs=gs),
        grid=(ng // gs,),
        in_specs=[pl.BlockSpec((gs, p, qq), lambda g: (g, 0, 0)),
                  pl.BlockSpec((2, gs, qq // 2, c), lambda g: (0, g, 0, 0))],
        out_specs=pl.BlockSpec((p, gs * c), lambda g: (0, g)),
        out_shape=jax.ShapeDtypeStruct((p, ng * c), out_dtype),
        compiler_params=_params(("parallel",), 48),
        name="dft_batched_real",
    )(mats, x)


def _tbmm_kernel(*refs, mode, gs):
    f_ref, twr_ref, twi_ref, x_ref = refs[:4]
    o_ref = refs[-1]
    for s in range(gs):
        xr, xi = x_ref[0, s].astype(F32), x_ref[1, s].astype(F32)
        reps = xr.shape[1] // twr_ref.shape[2]
        tr = jnp.concatenate([twr_ref[s]] * reps, axis=1)
        ti = jnp.concatenate([twi_ref[s]] * reps, axis=1)
        x = jnp.concatenate([xr * tr - xi * ti, xr * ti + xi * tr], axis=0).astype(BF16)
        acc = _dot(f_ref[...], x)
        ph = acc.shape[0] // 2
        ar, ai = acc[:ph], acc[ph:]
        if mode == "scale":
            sc = refs[4][...]
            o_ref[0, s] = ar * sc
            o_ref[1, s] = ai * sc
        else:
            g_ref, fi_ref = refs[4], refs[5]
            gr, gi = g_ref[0, s], g_ref[1, s]
            y = jnp.concatenate([ar * gr - ai * gi, ar * gi + ai * gr], axis=0).astype(BF16)
            back = _dot(fi_ref[...], y)
            br, bi = back[:ph], back[ph:]
            o_ref[0, s] = (br * tr + bi * ti).astype(o_ref.dtype)
            o_ref[1, s] = (bi * tr - br * ti).astype(o_ref.dtype)


def _tbmm(tabs, x, mode, extra, out_dtype, g_col=0):
    _, ng, n2, c = x.shape
    gs = _pick_tile(ng, BMM_GROUPS_PER_STEP, 1)
    p = 2 * n2
    blk = lambda g: (0, g, 0, 0)
    specs = [pl.BlockSpec((p, p), lambda g: (0, 0)),
             pl.BlockSpec((gs, n2, 128), lambda g: (g, 0, 0)),
             pl.BlockSpec((gs, n2, 128), lambda g: (g, 0, 0)),
             pl.BlockSpec((2, gs, n2, c), blk)]
    args = [tabs["hy_fwd"], tabs["hy_twr"], tabs["hy_twi"], x]
    if mode == "scale":
        specs.append(pl.BlockSpec((1, c), lambda g: (0, 0)))
        args.append(extra)
    else:
        specs += [pl.BlockSpec((2, gs, n2, c), lambda g: (0, g, 0, g_col)),
                  pl.BlockSpec((p, p), lambda g: (0, 0))]
        args += [extra, tabs["hy_inv"]]
    return pl.pallas_call(
        functools.partial(_tbmm_kernel, mode=mode, gs=gs),
        grid=(ng // gs,),
        in_specs=specs,
        out_specs=pl.BlockSpec((2, gs, n2, c), blk),
        out_shape=jax.ShapeDtypeStruct(x.shape, out_dtype),
        compiler_params=_params(("parallel",), 48),
        name="dft_inner_" + mode,
    )(*args)


def _angle(num, den):
    return (2.0 * math.pi / den) * jnp.mod(num, den).astype(F32)


def _dft_tables(n_lat):
    nn = 2 * n_lat
    n2 = DFT_N2
    n1 = nn // n2
    i1 = jnp.arange(n1, dtype=jnp.int32)
    a1 = _angle(i1[:, None] * i1[None, :], n1)
    c1, s1 = jnp.cos(a1), jnp.sin(a1)
    t = {}
    t["hy_f1"] = jnp.concatenate([c1, -s1], axis=0).astype(BF16)
    ch, sh = c1[:, :n1 // 2], s1[:, :n1 // 2]
    t["hy_f1c"] = jnp.concatenate([jnp.concatenate([ch, sh], axis=1),
                                   jnp.concatenate([-sh, ch], axis=1)], axis=0).astype(BF16)
    t["hy_i2c"] = jnp.concatenate([jnp.concatenate([ch.T, -sh.T], axis=1),
                                   jnp.concatenate([sh.T, ch.T], axis=1)], axis=0).astype(BF16)
    i2 = jnp.arange(n2, dtype=jnp.int32)
    a2 = _angle(i2[:, None] * i2[None, :], n2)
    c2, s2 = jnp.cos(a2), jnp.sin(a2)
    fwd = jnp.concatenate([jnp.concatenate([c2, s2], axis=1), jnp.concatenate([-s2, c2], axis=1)], axis=0)
    t["hy_fwd"] = fwd.astype(BF16)
    t["hy_inv"] = fwd.T.astype(BF16)
    at = _angle(i1[:, None] * i2[None, :], nn)
    lanes = (n1, n2, 128)
    t["hy_twr"] = jnp.broadcast_to(jnp.cos(at)[:, :, None], lanes)
    t["hy_twi"] = jnp.broadcast_to(-jnp.sin(at)[:, :, None], lanes)
    m2 = FN_N2
    m1 = n_lat // m2
    j1 = jnp.arange(m1, dtype=jnp.int32)
    b1 = _angle(j1[:, None] * j1[None, :], m1)
    cb, sb = jnp.cos(b1), jnp.sin(b1)
    t["fn_s1"] = jnp.concatenate([jnp.concatenate([cb, sb], axis=1),
                                  jnp.concatenate([-sb, cb], axis=1)], axis=0).astype(BF16)
    j2 = jnp.arange(m2, dtype=jnp.int32)
    kf = j1[:, None, None] + m1 * j2[None, :, None]
    b2 = _angle(kf * j2[None, None, :], n_lat)
    norm = 1.0 / math.sqrt(n_lat * FN_GROUP_DIM)
    t["fn_s3"] = (norm * jnp.concatenate([jnp.cos(b2), jnp.sin(b2)], axis=2)).astype(BF16)
    return t


def _small_dft_tables(n):
    nn = 2 * n
    k = jnp.arange(nn, dtype=jnp.int32)
    a = _angle(k[:, None] * k[None, :], nn)
    c, s = jnp.cos(a), jnp.sin(a)
    t = {}
    t["f_full"] = jnp.concatenate([c, -s], axis=0).astype(BF16)
    t["f_half"] = jnp.concatenate([c[:, :n], -s[:, :n]], axis=0).astype(BF16)
    t["i_half"] = jnp.concatenate([c[:n], -s[:n]], axis=1).astype(BF16)
    j = jnp.arange(n, dtype=jnp.int32)
    b = _angle(j[:, None] * j[None, :], n)
    norm = 1.0 / math.sqrt(n * FN_GROUP_DIM)
    t["fn"] = (norm * jnp.concatenate([jnp.cos(b), jnp.sin(b)], axis=1)).astype(BF16)
    return t


def _hyena_long(hy, short_w, fw, bias, tabs):
    b, n, _ = hy.shape
    assert b == 2
    cw = HY_WIDTH
    n2 = DFT_N2
    n1 = 2 * n // n2
    v, x1, x2 = _sconv(hy, short_w, False)
    g, gscale = _padded_filter(n, fw)
    half = HY_ORDER * HY_WIDTH
    ga = _colmm(tabs["hy_f1"], g.reshape(n1, n2 * half), BF16)
    gspec = _tbmm(tabs, ga.reshape(2, n1, n2, half), "scale", gscale * (1.0 / (2 * n)), F32)
    z = v.reshape(n1, n2 * cw)
    for o, xg in enumerate((x1, x2)):
        a = _colmm(tabs["hy_f1c"], z, BF16)
        bm = _tbmm(tabs, a.reshape(2, n1, n2, cw), "conv", gspec, BF16, g_col=o)
        brow = jnp.tile(bias[o], n2)[None, :]
        z = _colmm(tabs["hy_i2c"], bm.reshape(2 * n1, n2 * cw), F32, epi="gate",
                   extra=(xg.reshape(n1, n2 * cw), z, brow))
    return z.reshape(b, n, cw)


def _hyena_short(hy, short_w, fw, bias, tabs):
    b, n, _ = hy.shape
    cw = b * HY_WIDTH
    v, x1, x2 = _sconv(hy, short_w, True)
    g, gscale = _padded_filter(n, fw)
    gspec = _colmm(tabs["f_full"], g, F32, epi="scale", extra=(gscale * (1.0 / (2 * n)),), tn_cap=256)
    z = v
    for o, xg in enumerate((x1, x2)):
        y = _colmm(tabs["f_half"], z, BF16, epi="cmul", extra=(gspec,), tn_cap=HY_WIDTH, g_col=o)
        brow = jnp.tile(bias[o], b)[None, :]
        z = _colmm(tabs["i_half"], y, F32, epi="gate", extra=(xg, z, brow), tn_cap=256)
    return z


def _fn0_kernel(x_ref, w_ref, o_ref):
    acc = _dot(x_ref[0].astype(BF16), w_ref[...])
    o_ref[0] = acc[:, :BRANCH_WIDTH].astype(o_ref.dtype)
    o_ref[1] = acc[:, BRANCH_WIDTH:].astype(o_ref.dtype)


def _fnet_channel_dft(u):
    b, n, w = u.shape
    j = np.arange(FN_GROUP_DIM)
    ang = 2.0 * np.pi * ((j[:, None] * j[None, :]) % FN_GROUP_DIM) / FN_GROUP_DIM
    eye = np.eye(FN_GROUPS)
    wc = np.concatenate([np.kron(eye, np.cos(ang)), -np.kron(eye, np.sin(ang))], axis=1)
    wc = jnp.asarray(wc, F32).astype(BF16)
    tm = _pick_tile(n, 1024, 8)
    return pl.pallas_call(
        _fn0_kernel,
        grid=(b, n // tm),
        in_specs=[pl.BlockSpec((1, tm, w), lambda bb, i: (bb, i, 0)),
                  pl.BlockSpec((w, 2 * w), lambda bb, i: (0, 0))],
        out_specs=pl.BlockSpec((2, tm, w), lambda bb, i: (0, i, bb)),
        out_shape=jax.ShapeDtypeStruct((2, n, b * w), BF16),
        compiler_params=_params(("parallel", "parallel")),
        name="fnet_channel_dft",
    )(u, wc)


def _fnet_long(u, tabs):
    b, n, w = u.shape
    cw = b * w
    m2 = FN_N2
    m1 = n // m2
    wri = _fnet_channel_dft(u)
    a = _colmm(tabs["fn_s1"], wri.reshape(2 * m1, m2 * cw), BF16)
    y = _bmm(tabs["fn_s3"], a.reshape(2, m1, m2, cw), BF16)
    return y.reshape(n, cw)


def _fnet_short(u, tabs):
    b, n, w = u.shape
    wri = _fnet_channel_dft(u)
    return _colmm(tabs["fn"], wri.reshape(2 * n, b * w), BF16, tn_cap=256)


def _merge_kernel(x_ref, g_ref, sc_ref, sh_ref, gt_ref, oa_ref, ob_ref, oc_ref, od_ref,
                  wg_ref, wb_ref, wo_ref, o_ref):
    x = x_ref[0]
    hb = _normmod(x, g_ref[...], sc_ref[0], sh_ref[0]).astype(BF16)
    branches = (oa_ref[0], ob_ref[0], oc_ref[...], od_ref[0])
    acc = jnp.zeros(x.shape, F32)
    for nbr, o in enumerate(branches):
        gate = _dot(hb, wg_ref[:, nbr * D_MODEL:(nbr + 1) * D_MODEL])
        proj = _dot(o.astype(BF16), wb_ref[nbr])
        acc = acc + jax.nn.sigmoid(gate) * proj
    m = _dot(acc.astype(BF16), wo_ref[...])
    o_ref[0] = x + gt_ref[0] * m


def _merge(x, g, sc, sh, gt, oa, ob, oc, od, w):
    b, n, _ = x.shape
    tm = _pick_tile(n, 512, 8)
    tok = lambda bb, i: (bb, i, 0)
    row = lambda bb, i: (bb, 0, 0)
    seq = lambda bb, i: (i, bb)
    c2 = lambda bb, i: (0, 0)
    return pl.pallas_call(
        _merge_kernel,
        grid=(b, n // tm),
        in_specs=[pl.BlockSpec((1, tm, D_MODEL), tok),
                  pl.BlockSpec((1, D_MODEL), c2),
                  pl.BlockSpec((1, 1, D_MODEL), row), pl.BlockSpec((1, 1, D_MODEL), row),
                  pl.BlockSpec((1, 1, D_MODEL), row),
                  pl.BlockSpec((1, tm, BRANCH_WIDTH), tok),
                  pl.BlockSpec((1, tm, BRANCH_WIDTH), tok),
                  pl.BlockSpec((tm, BRANCH_WIDTH), seq),
                  pl.BlockSpec((1, tm, BRANCH_WIDTH), tok),
                  pl.BlockSpec((D_MODEL, N_BRANCH * D_MODEL), c2),
                  pl.BlockSpec((N_BRANCH, BRANCH_WIDTH, D_MODEL), lambda bb, i: (0, 0, 0)),
                  pl.BlockSpec((D_MODEL, D_MODEL), c2)],
        out_specs=pl.BlockSpec((1, tm, D_MODEL), tok),
        out_shape=jax.ShapeDtypeStruct(x.shape, F32),
        compiler_params=_params(("parallel", "parallel"), 56),
        name="branch_merge",
    )(x, g, sc, sh, gt, oa, ob, oc, od, w["wgate"], w["wbranch"], w["wout"])


def _moe_kernel(x_ref, g_ref, sc_ref, sh_ref, gt_ref, wr_ref, br_ref, wg_ref, wu_ref, wd_ref, fg_ref,
                o_ref, h_s, wt_s, acc_s, *, final):
    e = pl.program_id(2)

    @pl.when(e == 0)
    def _():
        hb = _normmod(x_ref[0], g_ref[...], sc_ref[0], sh_ref[0]).astype(BF16)
        h_s[...] = hb
        r = _dot(hb, wr_ref[...]) + br_ref[...]
        lane = lax.broadcasted_iota(jnp.int32, r.shape, 1)
        neg = jnp.float32(-jnp.inf)
        isg = lane < MOE_GROUPS
        gmax = jnp.max(jnp.where(isg, r, neg), axis=1, keepdims=True)
        gsum = jnp.sum(jnp.where(isg, jnp.exp(r - gmax), 0.0), axis=1, keepdims=True)
        g_p = 1.0 / gsum
        gidx = jnp.min(jnp.where(isg & (r == gmax), lane, ROUTER_LANES), axis=1, keepdims=True)
        ise = ((lane >= MOE_GROUPS) & (lane < MOE_GROUPS + MOE_EXPERTS)
               & (jnp.right_shift(lane - MOE_GROUPS, 2) == gidx))
        el = jnp.where(ise, r, neg)
        e1 = jnp.max(el, axis=1, keepdims=True)
        i1 = jnp.min(jnp.where(ise & (r == e1), lane, ROUTER_LANES), axis=1, keepdims=True)
        el2 = jnp.where(lane == i1, neg, el)
        e2 = jnp.max(el2, axis=1, keepdims=True)
        i2 = jnp.min(jnp.where(el2 == e2, lane, ROUTER_LANES), axis=1, keepdims=True)
        d = jnp.exp(e2 - e1)
        w1 = 1.0 / (1.0 + d)
        w2 = d / (1.0 + d)
        wt_s[...] = jnp.where(lane == i1, g_p * w1, jnp.where(lane == i2, g_p * w2, 0.0))
        acc_s[...] = jnp.zeros(acc_s.shape, F32)

    hb = h_s[...]
    wt = wt_s[...]
    lane = lax.broadcasted_iota(jnp.int32, wt.shape, 1)
    acts = []
    for k in range(MOE_PER_GROUP):
        a = _dot(hb, wg_ref[k])
        u = _dot(hb, wu_ref[k])
        col = MOE_GROUPS + e * MOE_PER_GROUP + k
        wcol = jnp.sum(jnp.where(lane == col, wt, 0.0), axis=1, keepdims=True)
        acts.append(((a * jax.nn.sigmoid(a)) * u * wcol).astype(BF16))
    act = jnp.concatenate(acts, axis=1)
    wd = wd_ref[...].reshape(MOE_PER_GROUP * MOE_HIDDEN, D_MODEL)
    acc_s[...] += _dot(act, wd)

    @pl.when(e == MOE_GROUPS - 1)
    def _():
        y = x_ref[0] + gt_ref[0] * acc_s[...]
        if final:
            y = _rms(y) * fg_ref[...]
        o_ref[0] = y


def _moe(x, g, sc, sh, gt, w, fg, final):
    b, n, _ = x.shape
    tm = _pick_tile(n, 1024, 8)
    tok = lambda bb, i, e: (bb, i, 0)
    row = lambda bb, i, e: (bb, 0, 0)
    c2 = lambda bb, i, e: (0, 0)
    ex = lambda bb, i, e: (e, 0, 0)
    return pl.pallas_call(
        functools.partial(_moe_kernel, final=final),
        grid=(b, n // tm, MOE_GROUPS),
        in_specs=[pl.BlockSpec((1, tm, D_MODEL), tok),
                  pl.BlockSpec((1, D_MODEL), c2),
                  pl.BlockSpec((1, 1, D_MODEL), row), pl.BlockSpec((1, 1, D_MODEL), row),
                  pl.BlockSpec((1, 1, D_MODEL), row),
                  pl.BlockSpec((D_MODEL, ROUTER_LANES), c2), pl.BlockSpec((1, ROUTER_LANES), c2),
                  pl.BlockSpec((MOE_PER_GROUP, D_MODEL, MOE_HIDDEN), ex),
                  pl.BlockSpec((MOE_PER_GROUP, D_MODEL, MOE_HIDDEN), ex),
                  pl.BlockSpec((MOE_PER_GROUP, MOE_HIDDEN, D_MODEL), ex),
                  pl.BlockSpec((1, D_MODEL), c2)],
        out_specs=pl.BlockSpec((1, tm, D_MODEL), tok),
        out_shape=jax.ShapeDtypeStruct(x.shape, F32),
        scratch_shapes=[pltpu.VMEM((tm, D_MODEL), BF16), pltpu.VMEM((tm, ROUTER_LANES), F32),
                        pltpu.VMEM((tm, D_MODEL), F32)],
        compiler_params=_params(("parallel", "parallel", "arbitrary"), 56),
        name="hier_moe",
    )(x, g, sc, sh, gt, w["wr"], w["br"], w["wg"], w["wu"], w["wd"], fg)


def _layer_weights(l, w_in, mla_q_norm, mla_w_uq, mla_kv_norm, mla_w_ukv, w_branch, w_out,
                   moe_w_group, moe_b_group, moe_w_expert, moe_b_expert, moe_w_gate, moe_w_up, moe_w_down):
    wi = w_in[l]
    b0 = MLA_Q_RANK
    b1 = b0 + MLA_KV_RANK
    b2 = b1 + MLA_ROPE
    b3 = b2 + HY_IN
    b4 = b3 + BRANCH_WIDTH
    b5 = b4 + 3 * BRANCH_WIDTH
    zpad = lambda r, c: jnp.zeros((r, c), F32)
    w_kr = wi[:, b1:b2]
    krp = jnp.concatenate([zpad(D_MODEL, MLA_NOPE), w_kr[:, 0::2], w_kr[:, 1::2],
                           zpad(D_MODEL, MLA_HEAD_PAD - MLA_NOPE - MLA_ROPE)], axis=1)
    w_na = wi[:, b4:b5]
    w_na = jnp.concatenate([w_na[:, :BRANCH_WIDTH] * NA_SCALE, w_na[:, BRANCH_WIDTH:]], axis=1)
    wa = jnp.concatenate([wi[:, :b1], krp, wi[:, b2:b4], w_na], axis=1).astype(BF16)
    uq = mla_w_uq[l].reshape(MLA_Q_RANK, MLA_HEADS, MLA_NOPE + MLA_ROPE) * (MLA_SCALE * LOG2E)
    uq = jnp.concatenate([uq[..., :MLA_NOPE], uq[..., MLA_NOPE::2], uq[..., MLA_NOPE + 1::2],
                          jnp.zeros((MLA_Q_RANK, MLA_HEADS, MLA_HEAD_PAD - MLA_NOPE - MLA_ROPE), F32)], axis=-1)
    ukv = mla_w_ukv[l].reshape(MLA_KV_RANK, MLA_HEADS, MLA_NOPE + MLA_V)
    uk = jnp.concatenate([ukv[..., :MLA_NOPE],
                          jnp.zeros((MLA_KV_RANK, MLA_HEADS, MLA_HEAD_PAD - MLA_NOPE), F32)], axis=-1)
    uv = ukv[..., MLA_NOPE:]
    wr = jnp.concatenate([moe_w_group[l], moe_w_expert[l],
                          zpad(D_MODEL, ROUTER_LANES - MOE_GROUPS - MOE_EXPERTS)], axis=1)
    br = jnp.concatenate([moe_b_group[l], moe_b_expert[l],
                          jnp.zeros((ROUTER_LANES - MOE_GROUPS - MOE_EXPERTS,), F32)])[None, :]
    return dict(
        wa=wa, qn=mla_q_norm[l][None, :], kvn=mla_kv_norm[l][None, :],
        wuq=uq.reshape(MLA_Q_RANK, -1).astype(BF16), wuk=uk.reshape(MLA_KV_RANK, -1).astype(BF16),
        wuv=uv.reshape(MLA_KV_RANK, -1).astype(BF16),
        wgate=wi[:, b5:].astype(BF16), wbranch=w_branch[l].astype(BF16), wout=w_out[l].astype(BF16),
        wr=wr.astype(BF16), br=br, wg=moe_w_gate[l].astype(BF16), wu=moe_w_up[l].astype(BF16),
        wd=moe_w_down[l].astype(BF16))


def _rope_tables(n):
    t = jnp.arange(n, dtype=jnp.int32)
    row = (t // GRID_W).astype(F32)
    col = (t % GRID_W).astype(F32)
    n_freq = MLA_ROPE // 4
    inv = ROPE_BASE ** (-jnp.arange(n_freq, dtype=F32) / n_freq)
    ang = jnp.concatenate([row[:, None] * inv, col[:, None] * inv], axis=-1)
    c, s = jnp.cos(ang), jnp.sin(ang)
    one = jnp.ones((n, MLA_NOPE), F32)
    zero = jnp.zeros((n, MLA_NOPE), F32)
    hr = MLA_ROPE // 2
    tail1 = jnp.ones((n, MLA_HEAD_PAD - MLA_NOPE - MLA_ROPE), F32)
    tail0 = jnp.zeros((n, MLA_HEAD_PAD - MLA_NOPE - MLA_ROPE), F32)
    cos = jnp.concatenate([one, c, c, tail1], axis=1)
    s1 = jnp.concatenate([zero, -s, jnp.zeros((n, hr), F32), tail0], axis=1)
    s2 = jnp.concatenate([zero, jnp.zeros((n, hr), F32), s, tail0], axis=1)
    return cos, s1, s2


def _identity_rope(n):
    return (jnp.ones((n, MLA_HEAD_PAD), F32), jnp.zeros((n, MLA_HEAD_PAD), F32),
            jnp.zeros((n, MLA_HEAD_PAD), F32))


def kernel(x, c, ctx, c_ctx, ada_w, ada_b, norm1_g, norm2_g, w_in, mla_q_norm, mla_w_uq, mla_kv_norm, mla_w_ukv, hy_short_w, hy_w1, hy_b1, hy_freq1, hy_w2, hy_b2, hy_freq2, hy_w3, hy_bias, na_rpb, w_branch, w_out, moe_w_group, moe_b_group, moe_w_expert, moe_b_expert, moe_w_gate, moe_w_up, moe_w_down, final_norm_g):
    b, n_lat, _ = x.shape
    n_ctx = ctx.shape[1]
    assert b <= 4 and n_lat % (NA_QROWS * GRID_W) == 0 and n_lat // GRID_W >= 2 * NA_QROWS

    cvec = jnp.concatenate([c, c_ctx[None, :], jnp.zeros((8 - b - 1, D_MODEL), F32)], axis=0)
    mod = _mod_vectors(cvec, ada_w, ada_b)

    rope_lat = _rope_tables(n_lat)
    rope_ctx = _identity_rope(n_ctx)
    tabs_lat = _dft_tables(n_lat)
    tabs_ctx = _small_dft_tables(n_ctx)
    fg = final_norm_g[None, :]

    for l in range(DEPTH):
        last = l == DEPTH - 1
        w = _layer_weights(l, w_in, mla_q_norm, mla_w_uq, mla_kv_norm, mla_w_ukv, w_branch, w_out,
                           moe_w_group, moe_b_group, moe_w_expert, moe_b_expert,
                           moe_w_gate, moe_w_up, moe_w_down)
        ml = mod[l, :b].reshape(b, 1, 6, D_MODEL)
        mc = jnp.broadcast_to(mod[l, b].reshape(1, 1, 6, D_MODEL), (b, 1, 6, D_MODEL))
        sh1, sc1, gt1, sh2, sc2, gt2 = [ml[:, :, i] for i in range(6)]
        sh1c, sc1c, gt1c, sh2c, sc2c, gt2c = [mc[:, :, i] for i in range(6)]
        g1 = norm1_g[l][None, :]
        g2 = norm2_g[l][None, :]
        fw = dict(
            w1=jnp.concatenate([hy_w1[l], jnp.zeros((HY_EMB_PAD - HY_EMB, HY_HIDDEN), F32)], axis=0).T.astype(BF16),
            b1=hy_b1[l][:, None], f1=hy_freq1[l][:, None], w2=hy_w2[l].T.astype(BF16), b2=hy_b2[l][:, None],
            f2=hy_freq2[l][:, None],
            w3=jnp.transpose(hy_w3[l].reshape(HY_HIDDEN, 2, HY_ORDER * HY_WIDTH), (1, 2, 0)).astype(BF16))

        n_keys = n_lat + n_ctx
        kv_bufs = (jnp.zeros((b, MLA_HEADS, n_keys, MLA_HEAD_PAD), BF16),
                   jnp.zeros((b, MLA_HEADS, MLA_VT_ROWS, n_keys), PV_DTYPE))
        q, kall, vtall, hy, fn, naq, nak, nav = _inproj(x, g1, sc1, sh1, w, rope_lat, kv_bufs, 0, False)
        qc, kc, vc, kall, vtall, hyc, fnc, naqc, nakc, navc = _inproj(
            ctx, g1, sc1c, sh1c, w, rope_ctx, (kall, vtall), n_lat, True)

        oa = _flash_t(q, kall, vtall, MLA_V)
        od = _na(naq, nak, nav, nakc, navc, _na_bias_tables(na_rpb[l]))
        ob = _hyena_long(hy, hy_short_w[l], fw, hy_bias[l], tabs_lat)
        oc = _fnet_long(fn, tabs_lat)
        x_new = _merge(x, g1, sc1, sh1, gt1, oa, ob, oc, od, w)

        if not last:
            oa_c = _flash(qc, kc, vc, True)
            heads = lambda t: jnp.transpose(t.reshape(b, n_ctx, NA_HEADS, NA_HEAD_DIM), (0, 2, 1, 3))
            od_c = _flash(heads(naqc), heads(nakc), heads(jnp.swapaxes(navc, 1, 2)), False)
            ob_c = _hyena_short(hyc, hy_short_w[l], fw, hy_bias[l], tabs_ctx)
            ob_c = jnp.transpose(ob_c.reshape(n_ctx, b, HY_WIDTH), (1, 0, 2))
            oc_c = _fnet_short(fnc, tabs_ctx)
            ctx = _merge(ctx, g1, sc1c, sh1c, gt1c, oa_c, ob_c, oc_c, od_c, w)
            ctx = _moe(ctx, g2, sc2c, sh2c, gt2c, w, fg, False)
        x = _moe(x_new, g2, sc2, sh2, gt2, w, fg, last)
    return x
```

```python
import functools
import math

import numpy as np
import jax
import jax.numpy as jnp
from jax import lax
from jax.experimental import pallas as pl
from jax.experimental.pallas import tpu as pltpu

F32 = jnp.float32
BF16 = jnp.bfloat16

D_MODEL = 1024
DEPTH = 2
GRID_W = 64
NORM_EPS = 1e-6
MASK_VALUE = -1e30
BRANCH_WIDTH = D_MODEL // 4
N_BRANCH = 4

MLA_HEADS = 4
MLA_Q_RANK = D_MODEL // 4
MLA_KV_RANK = D_MODEL // 8
MLA_NOPE = 64
MLA_ROPE = 32
MLA_V = BRANCH_WIDTH // MLA_HEADS
MLA_SCALE = 1.0 / math.sqrt(MLA_NOPE + MLA_ROPE)
MLA_HEAD_PAD = 128
MLA_VT_ROWS = MLA_V + 16
PV_DTYPE = BF16
ROPE_BASE = 10000.0
LOG2E = 1.4426950408889634

HY_WIDTH = BRANCH_WIDTH
HY_ORDER = 2
HY_BANDS = 16
HY_EMB = 1 + 2 * HY_BANDS
HY_EMB_PAD = 128
HY_HIDDEN = 64
HY_DECAY_TARGET = 1e-2
HY_DECAY_SHORT_PCT = 0.3
HY_DECAY_LONG_PCT = 1.5
HY_IN = (HY_ORDER + 1) * HY_WIDTH

FN_GROUPS = 4
FN_GROUP_DIM = BRANCH_WIDTH // FN_GROUPS

NA_HEADS = 4
NA_HEAD_DIM = BRANCH_WIDTH // NA_HEADS
NA_KH = 8
NA_KW = 16
NA_SCALE = 1.0 / math.sqrt(NA_HEAD_DIM)
NA_QROWS = 8
NA_KROWS = 16
NA_VT_PAD = 16

MOE_GROUPS = 4
MOE_PER_GROUP = 4
MOE_EXPERTS = MOE_GROUPS * MOE_PER_GROUP
MOE_HIDDEN = 256
ROUTER_LANES = 128

MLA_IN = MLA_Q_RANK + MLA_KV_RANK + MLA_ROPE
MIX_IN = MLA_IN + HY_IN + BRANCH_WIDTH + 3 * BRANCH_WIDTH
MIX_PAD = MLA_Q_RANK + MLA_KV_RANK + MLA_HEAD_PAD + HY_IN + BRANCH_WIDTH + 3 * BRANCH_WIDTH

DFT_N2 = 256
FN_N2 = 128
BMM_GROUPS_PER_STEP = 4
FLASH_K_TILE = 3328
FLASH_CHUNKS = 4
FLASH_Q_TILE = 2048
FLASH_Q_SUB = 512


def _params(sem, vmem_mb=None):
    kw = dict(dimension_semantics=sem)
    if vmem_mb is not None:
        kw["vmem_limit_bytes"] = vmem_mb << 20
    return pltpu.CompilerParams(**kw)


def _pick_tile(n, cap, mult):
    best = None
    for t in range(mult, min(n, cap) + 1, mult):
        if n % t == 0:
            best = t
    return best if best is not None else n


def _rms(x):
    return x * lax.rsqrt(jnp.mean(x * x, axis=-1, keepdims=True) + NORM_EPS)


def _normmod(x, g, sc, sh):
    return (_rms(x) * g) * (1.0 + sc) + sh


def _dot(a, b):
    return jnp.dot(a, b, preferred_element_type=F32)


def _dot_nt(a, b):
    return lax.dot_general(a, b, (((1,), (1,)), ((), ())), preferred_element_type=F32)


def _mod_kernel(c_ref, w_ref, b_ref, o_ref):
    c = c_ref[...]
    s = c * jax.nn.sigmoid(c)
    o_ref[0] = _dot(s.astype(BF16), w_ref[0].astype(BF16)) + b_ref[0]


def _mod_vectors(cvec, ada_w, ada_b):
    n6 = ada_w.shape[-1]
    tn = 1024
    return pl.pallas_call(
        _mod_kernel,
        grid=(DEPTH, n6 // tn),
        in_specs=[pl.BlockSpec((8, D_MODEL), lambda l, j: (0, 0)),
                  pl.BlockSpec((1, D_MODEL, tn), lambda l, j: (l, 0, j)),
                  pl.BlockSpec((1, 1, tn), lambda l, j: (l, 0, j))],
        out_specs=pl.BlockSpec((1, 8, tn), lambda l, j: (l, 0, j)),
        out_shape=jax.ShapeDtypeStruct((DEPTH, 8, n6), F32),
        compiler_params=_params(("parallel", "parallel")),
        name="adaln_vectors",
    )(cvec, ada_w, ada_b.reshape(DEPTH, 1, n6))


def _inproj_kernel(*refs, ctx_mode):
    (x_ref, g_ref, sc_ref, sh_ref, wa_ref, qn_ref, kvn_ref, wuq_ref, wuk_ref, wuv_ref,
     cos_ref, s1_ref, s2_ref) = refs[:13]
    if ctx_mode:
        (q_ref, k_ref, v_ref, kall_ref, vtall_ref, hy_ref, fn_ref, naq_ref, nak_ref, nav_ref) = refs[15:]
    else:
        (q_ref, kall_ref, vtall_ref, hy_ref, fn_ref, naq_ref, nak_ref, nav_ref) = refs[15:]
    h = _normmod(x_ref[0], g_ref[...], sc_ref[0], sh_ref[0]).astype(BF16)
    p = _dot(h, wa_ref[...])
    c0 = MLA_Q_RANK
    c1 = c0 + MLA_KV_RANK
    c2 = c1 + MLA_HEAD_PAD
    c3 = c2 + HY_IN
    c4 = c3 + BRANCH_WIDTH
    c5 = c4 + BRANCH_WIDTH
    c6 = c5 + BRANCH_WIDTH
    cq, ckv, krp = p[:, :c0], p[:, c0:c1], p[:, c1:c2]
    hy_ref[0] = p[:, c2:c3]
    fn_ref[0] = p[:, c3:c4]
    naq_ref[0] = p[:, c4:c5].astype(BF16)
    nak_ref[0] = p[:, c5:c6].astype(BF16)
    nav_ref[0] = p[:, c6:].T.astype(BF16)
    cqn = (_rms(cq) * qn_ref[...]).astype(BF16)
    ckn = (_rms(ckv) * kvn_ref[...]).astype(BF16)
    q = _dot(cqn, wuq_ref[...])
    kn = _dot(ckn, wuk_ref[...])
    v = _dot(ckn, wuv_ref[...])
    cos, s1, s2 = cos_ref[...], s1_ref[...], s2_ref[...]

    def rope(t):
        return (t * cos + pltpu.roll(t, MLA_HEAD_PAD - MLA_ROPE // 2, 1) * s1
                + pltpu.roll(t, MLA_ROPE // 2, 1) * s2)

    kr = rope(krp)
    vt = v.T
    tail_rows = lax.broadcasted_iota(jnp.int32, (MLA_VT_ROWS - MLA_V, v.shape[0]), 0)
    tail = jnp.where(tail_rows == 0, 1.0, 0.0).astype(PV_DTYPE)
    for hh in range(MLA_HEADS):
        sl = slice(hh * MLA_HEAD_PAD, (hh + 1) * MLA_HEAD_PAD)
        vs = slice(hh * MLA_V, (hh + 1) * MLA_V)
        kh = (kn[:, sl] + kr).astype(BF16)
        qh = rope(q[:, sl])
        q_ref[0, hh] = qh.astype(BF16) if ctx_mode else qh.T.astype(BF16)
        kall_ref[0, hh] = kh
        vtall_ref[0, hh, :MLA_V] = vt[vs].astype(PV_DTYPE)
        vtall_ref[0, hh, MLA_V:] = tail
        if ctx_mode:
            k_ref[0, hh] = kh
            v_ref[0, hh] = v[:, vs].astype(BF16)


def _inproj(x, g, sc, sh, w, rope_tabs, kv_bufs, key_off, ctx_mode):
    b, n, _ = x.shape
    n_keys = kv_bufs[0].shape[2]
    tm = _pick_tile(n, 512, 128)
    assert key_off % tm == 0
    ko = key_off // tm
    cos, s1, s2 = rope_tabs
    tok = lambda bb, i: (bb, i, 0)
    row = lambda bb, i: (bb, 0, 0)
    const2 = lambda bb, i: (0, 0)
    hd = lambda bb, i: (bb, 0, i, 0)
    tab = lambda bb, i: (i, 0)
    hp = MLA_HEADS * MLA_HEAD_PAD
    in_specs = [pl.BlockSpec((1, tm, D_MODEL), tok),
                pl.BlockSpec((1, D_MODEL), const2),
                pl.BlockSpec((1, 1, D_MODEL), row),
                pl.BlockSpec((1, 1, D_MODEL), row),
                pl.BlockSpec((D_MODEL, MIX_PAD), const2),
                pl.BlockSpec((1, MLA_Q_RANK), const2),
                pl.BlockSpec((1, MLA_KV_RANK), const2),
                pl.BlockSpec((MLA_Q_RANK, hp), const2),
                pl.BlockSpec((MLA_KV_RANK, hp), const2),
                pl.BlockSpec((MLA_KV_RANK, BRANCH_WIDTH), const2),
                pl.BlockSpec((tm, MLA_HEAD_PAD), tab),
                pl.BlockSpec((tm, MLA_HEAD_PAD), tab),
                pl.BlockSpec((tm, MLA_HEAD_PAD), tab)]
    args = [x, g, sc, sh, w["wa"], w["qn"], w["kvn"], w["wuq"], w["wuk"], w["wuv"], cos, s1, s2]
    q_spec = pl.BlockSpec((1, MLA_HEADS, tm, MLA_HEAD_PAD), hd)
    q_sd = jax.ShapeDtypeStruct((b, MLA_HEADS, n, MLA_HEAD_PAD), BF16)
    kall_spec = pl.BlockSpec((1, MLA_HEADS, tm, MLA_HEAD_PAD), lambda bb, i: (bb, 0, ko + i, 0))
    vtall_spec = pl.BlockSpec((1, MLA_HEADS, MLA_VT_ROWS, tm), lambda bb, i: (bb, 0, 0, ko + i))
    kall_sd = jax.ShapeDtypeStruct((b, MLA_HEADS, n_keys, MLA_HEAD_PAD), BF16)
    vtall_sd = jax.ShapeDtypeStruct((b, MLA_HEADS, MLA_VT_ROWS, n_keys), PV_DTYPE)
    rest_specs = [pl.BlockSpec((1, tm, HY_IN), tok)] + [pl.BlockSpec((1, tm, BRANCH_WIDTH), tok)] * 3 \
        + [pl.BlockSpec((1, BRANCH_WIDTH, tm), lambda bb, i: (bb, 0, i))]
    rest_sd = [jax.ShapeDtypeStruct((b, n, HY_IN), F32), jax.ShapeDtypeStruct((b, n, BRANCH_WIDTH), F32)] \
        + [jax.ShapeDtypeStruct((b, n, BRANCH_WIDTH), BF16)] * 2 \
        + [jax.ShapeDtypeStruct((b, BRANCH_WIDTH, n), BF16)]
    in_specs += [pl.BlockSpec(memory_space=pl.ANY), pl.BlockSpec(memory_space=pl.ANY)]
    args += list(kv_bufs)
    if ctx_mode:
        out_specs = [q_spec, q_spec, pl.BlockSpec((1, MLA_HEADS, tm, MLA_V), hd), kall_spec, vtall_spec]
        out_shape = [q_sd, q_sd, jax.ShapeDtypeStruct((b, MLA_HEADS, n, MLA_V), BF16), kall_sd, vtall_sd]
        aliases = {13: 3, 14: 4}
    else:
        qt_spec = pl.BlockSpec((1, MLA_HEADS, MLA_HEAD_PAD, tm), lambda bb, i: (bb, 0, 0, i))
        qt_sd = jax.ShapeDtypeStruct((b, MLA_HEADS, MLA_HEAD_PAD, n), BF16)
        out_specs = [qt_spec, kall_spec, vtall_spec]
        out_shape = [qt_sd, kall_sd, vtall_sd]
        aliases = {13: 1, 14: 2}
    return pl.pallas_call(
        functools.partial(_inproj_kernel, ctx_mode=ctx_mode),
        grid=(b, n // tm),
        in_specs=in_specs,
        out_specs=out_specs + rest_specs,
        out_shape=out_shape + rest_sd,
        input_output_aliases=aliases,
        compiler_params=_params(("parallel", "parallel"), 48),
        name="input_projection",
    )(*args)


def _flash_kernel(q_ref, k_ref, v_ref, o_ref, m_ref, l_ref, acc_ref, *, nh, use_exp2):
    j = pl.program_id(2)

    @pl.when(j == 0)
    def _():
        m_ref[...] = jnp.full(m_ref.shape, MASK_VALUE, F32)
        l_ref[...] = jnp.zeros(l_ref.shape, F32)
        acc_ref[...] = jnp.zeros(acc_ref.shape, F32)

    ex = jnp.exp2 if use_exp2 else jnp.exp
    for h in range(nh):
        s = _dot_nt(q_ref[0, h], k_ref[0, h])
        m_prev = m_ref[h]
        m_cur = jnp.maximum(m_prev, jnp.max(s, axis=1, keepdims=True))
        alpha = ex(m_prev - m_cur)
        p = ex(s - m_cur)
        l_ref[h] = alpha * l_ref[h] + jnp.sum(p, axis=1, keepdims=True)
        acc_ref[h] = alpha * acc_ref[h] + _dot(p.astype(BF16), v_ref[0, h])
        m_ref[h] = m_cur

    @pl.when(j == pl.num_programs(2) - 1)
    def _():
        outs = [acc_ref[h] / l_ref[h] for h in range(nh)]
        o_ref[0] = jnp.concatenate(outs, axis=1).astype(o_ref.dtype)


def _flash(q, k, v, use_exp2):
    b, nh, lq, dk = q.shape
    lk, dv = k.shape[2], v.shape[3]
    tq = _pick_tile(lq, 512, 8)
    tk = _pick_tile(lk, 1280, 128)
    return pl.pallas_call(
        functools.partial(_flash_kernel, nh=nh, use_exp2=use_exp2),
        grid=(b, lq // tq, lk // tk),
        in_specs=[pl.BlockSpec((1, nh, tq, dk), lambda bb, i, j: (bb, 0, i, 0)),
                  pl.BlockSpec((1, nh, tk, dk), lambda bb, i, j: (bb, 0, j, 0)),
                  pl.BlockSpec((1, nh, tk, dv), lambda bb, i, j: (bb, 0, j, 0))],
        out_specs=pl.BlockSpec((1, tq, nh * dv), lambda bb, i, j: (bb, i, 0)),
        out_shape=jax.ShapeDtypeStruct((b, lq, nh * dv), BF16),
        scratch_shapes=[pltpu.VMEM((nh, tq, 1), F32), pltpu.VMEM((nh, tq, 1), F32),
                        pltpu.VMEM((nh, tq, dv), F32)],
        compiler_params=_params(("parallel", "parallel", "arbitrary"), 48),
        name="softmax_attention",
    )(q, k, v)


def _flash_t_kernel(q_ref, k_ref, vt_ref, o_ref, m_ref, acc_ref, st_ref, p_ref, *, nh, dv):
    j = pl.program_id(2)

    @pl.when(j == 0)
    def _():
        m_ref[...] = jnp.full(m_ref.shape, MASK_VALUE, F32)
        acc_ref[...] = jnp.zeros(acc_ref.shape, F32)

    rows = vt_ref.shape[2]

    tq = q_ref.shape[3]
    qs = min(tq, FLASH_Q_SUB)
    units = [(h, c) for h in range(nh) for c in range(tq // qs)]

    def scores(i):
        h, c = units[i]
        cols = slice(c * qs, (c + 1) * qs)
        st = _dot(k_ref[0, h], q_ref[0, h, :, cols])
        st_ref[i % 2] = st
        m_prev = m_ref[h, :, cols]
        m_cur = jnp.maximum(m_prev, jnp.max(st, axis=0, keepdims=True))
        m_ref[h, :, cols] = m_cur
        return m_prev, m_cur

    def probs(i, m_cur):
        p_ref[i % 2] = jnp.exp2(st_ref[i % 2] - m_cur).astype(PV_DTYPE)

    def values(i, m_prev, m_cur):
        h, c = units[i]
        cols = slice(c * qs, (c + 1) * qs)
        alpha = jnp.exp2(m_prev - m_cur)
        acc_ref[h, :rows, cols] = alpha * acc_ref[h, :rows, cols] + _dot(vt_ref[0, h], p_ref[i % 2])

    n_units = len(units)
    tk = k_ref.shape[2]
    kt = tk // 256
    n_chunks = min(FLASH_CHUNKS, kt) if tk % 256 == 0 else 1
    v_edges = [256 * ((kt * c) // n_chunks) for c in range(n_chunks + 1)] if n_chunks > 1 else [0, tk]
    s_edges = [16 * (((tk // 16) * c) // n_chunks) for c in range(n_chunks + 1)]

    def scores_chunk(i, c, m_run):
        h, cc = units[i]
        cols = slice(cc * qs, (cc + 1) * qs)
        r0, r1 = s_edges[c], s_edges[c + 1]
        st = _dot(k_ref[0, h, r0:r1, :], q_ref[0, h, :, cols])
        st_ref[i % 2, r0:r1, :] = st
        return jnp.maximum(m_run, jnp.max(st, axis=0, keepdims=True))

    def values_chunk(i, c):
        h, _ = units[i]
        r0, r1 = v_edges[c], v_edges[c + 1]
        return _dot(vt_ref[0, h, :, r0:r1], p_ref[i % 2, r0:r1, :])

    stats = {0: scores(0)}
    if n_units > 1:
        stats[1] = scores(1)
    probs(0, stats[0][1])
    for i in range(n_units):
        h, cc = units[i]
        cols = slice(cc * qs, (cc + 1) * qs)
        m_prev_i, m_cur_i = stats.pop(i)
        nxt = i + 2 < n_units
        if nxt:
            h2, c2 = units[i + 2]
            cols2 = slice(c2 * qs, (c2 + 1) * qs)
            m_prev2 = m_ref[h2, :, cols2]
            m_run = m_prev2
        pv = None
        for c in range(n_chunks):
            part = values_chunk(i, c)
            pv = part if pv is None else pv + part
            if nxt:
                m_run = scores_chunk(i + 2, c, m_run)
        alpha = jnp.exp2(m_prev_i - m_cur_i)
        acc_ref[h, :rows, cols] = alpha * acc_ref[h, :rows, cols] + pv
        if nxt:
            m_ref[h2, :, cols2] = m_run
            stats[i + 2] = (m_prev2, m_run)
        if i + 1 < n_units:
            probs(i + 1, stats[i + 1][1])

    @pl.when(j == pl.num_programs(2) - 1)
    def _():
        outs = []
        for h in range(nh):
            a = acc_ref[h]
            o = (a / a[dv:dv + 1]).T
            outs.append(o[:, :dv])
        o_ref[0] = jnp.concatenate(outs, axis=1).astype(o_ref.dtype)


def _flash_t(q, k, vt, dv):
    b, nh, dk, lq = q.shape
    lk, rows = k.shape[2], vt.shape[2]
    tq = _pick_tile(lq, FLASH_Q_TILE, 128)
    tk = _pick_tile(lk, FLASH_K_TILE, 128)
    return pl.pallas_call(
        functools.partial(_flash_t_kernel, nh=nh, dv=dv),
        grid=(b, lq // tq, lk // tk),
        in_specs=[pl.BlockSpec((1, nh, dk, tq), lambda bb, i, j: (bb, 0, 0, i)),
                  pl.BlockSpec((1, nh, tk, dk), lambda bb, i, j: (bb, 0, j, 0)),
                  pl.BlockSpec((1, nh, rows, tk), lambda bb, i, j: (bb, 0, 0, j))],
        out_specs=pl.BlockSpec((1, tq, nh * dv), lambda bb, i, j: (bb, i, 0)),
        out_shape=jax.ShapeDtypeStruct((b, lq, nh * dv), BF16),
        scratch_shapes=[pltpu.VMEM((nh, 1, tq), F32), pltpu.VMEM((nh, 128, tq), F32),
                        pltpu.VMEM((2, tk, min(tq, FLASH_Q_SUB)), F32),
                        pltpu.VMEM((2, tk, min(tq, FLASH_Q_SUB)), PV_DTYPE)],
        compiler_params=_params(("parallel", "parallel", "arbitrary"), 56),
        name="softmax_attention_t",
    )(q, k, vt)


def _na_kernel(q_ref, k0, k1, k2, k3, v0, v1, v2, v3, kc_ref, vc_ref, b_ref, o_ref):
    q = q_ref[0]
    kcat = jnp.concatenate([k0[0], k1[0], k2[0], k3[0]], axis=0)
    vtcat = jnp.concatenate([v0[0], v1[0], v2[0], v3[0]], axis=1)
    kc, vtc = kc_ref[0], vc_ref[0]
    dh = NA_HEAD_DIM

    def with_ones(vt):
        pad = lax.broadcasted_iota(jnp.int32, (NA_VT_PAD, vt.shape[1]), 0)
        return jnp.concatenate([vt, jnp.where(pad == 0, 1.0, 0.0).astype(vt.dtype)], axis=0)

    def scores(h):
        sl = slice(h * dh, (h + 1) * dh)
        qh = q[:, sl]
        st = _dot_nt(kcat[:, sl], qh) + b_ref[0, h]
        stc = _dot_nt(kc[:, sl], qh)
        m = jnp.maximum(jnp.max(st, axis=0, keepdims=True), jnp.max(stc, axis=0, keepdims=True))
        return st, stc, m

    def values(h, st, stc, m):
        sl = slice(h * dh, (h + 1) * dh)
        p = jnp.exp(st - m).astype(BF16)
        pc = jnp.exp(stc - m).astype(BF16)
        acc = _dot(with_ones(vtcat[sl]), p) + _dot(with_ones(vtc[sl]), pc)
        o = jnp.concatenate([acc / acc[dh:dh + 1],
                             jnp.zeros((128 - dh - NA_VT_PAD, acc.shape[1]), F32)], axis=0)
        return o.T[:, :dh]

    outs = []
    pending = scores(0)
    for h in range(NA_HEADS):
        nxt = scores(h + 1) if h + 1 < NA_HEADS else None
        outs.append(values(h, *pending))
        pending = nxt
    o_ref[0] = jnp.concatenate(outs, axis=1).astype(o_ref.dtype)


def _na_bias_tables(rpb):
    j = np.arange(NA_QROWS)[:, None]
    m = np.arange(NA_KROWS)[None, :]
    valid = np.zeros((3, NA_QROWS, NA_KROWS), bool)
    st0 = np.maximum(j - NA_KH // 2, 0)
    valid[0] = (m - 4 >= st0) & (m - 4 < st0 + NA_KH)
    valid[1] = (m - j >= 0) & (m - j < NA_KH)
    st2 = 4 + np.minimum(j - NA_KH // 2, 0)
    valid[2] = (m < 12) & (m >= st2) & (m < st2 + NA_KH)
    dr_idx = np.clip(m - 4 - j + NA_KH - 1, 0, 2 * NA_KH - 2)
    qc = np.arange(GRID_W)[:, None]
    kc = np.arange(GRID_W)[None, :]
    dc_idx = np.clip(kc - qc, -(NA_KW - 1), NA_KW - 1) + NA_KW - 1
    cs = np.clip(qc - NA_KW // 2, 0, GRID_W - NA_KW)
    col_ok = (kc >= cs) & (kc < cs + NA_KW)
    t = rpb.astype(F32)[:, dr_idx]
    t = t[..., dc_idx]
    t = jnp.transpose(t, (0, 2, 4, 1, 3))
    ok = (valid.transpose(0, 2, 1)[:, None, :, None, :, None]
          & col_ok.T[None, None, None, :, None, :])
    out = jnp.where(jnp.asarray(ok), t[None], MASK_VALUE)
    return out.reshape(3, NA_HEADS, NA_KROWS * GRID_W, NA_QROWS * GRID_W)


def _na(q, k, vt, kc, vtc, bias):
    b, n, w = q.shape
    c = kc.shape[1]
    tq = NA_QROWS * GRID_W
    tkb = 4 * GRID_W
    nb = n // tq
    nkb = n // tkb
    blocks = [lambda i: jnp.maximum(2 * i - 1, 0), lambda i: 2 * i, lambda i: 2 * i + 1,
              lambda i: jnp.minimum(2 * i + 2, nkb - 1)]
    kspecs = [pl.BlockSpec((1, tkb, w), lambda bb, i, f=f: (bb, f(i), 0)) for f in blocks]
    vspecs = [pl.BlockSpec((1, w, tkb), lambda bb, i, f=f: (bb, 0, f(i))) for f in blocks]
    return pl.pallas_call(
        _na_kernel,
        grid=(b, nb),
        in_specs=[pl.BlockSpec((1, tq, w), lambda bb, i: (bb, i, 0))] + kspecs + vspecs
                 + [pl.BlockSpec((1, c, w), lambda bb, i: (bb, 0, 0)),
                    pl.BlockSpec((1, w, c), lambda bb, i: (bb, 0, 0)),
                    pl.BlockSpec((1, NA_HEADS, NA_KROWS * GRID_W, tq),
                                 lambda bb, i: (jnp.where(i == 0, 0, jnp.where(i == nb - 1, 2, 1)), 0, 0, 0))],
        out_specs=pl.BlockSpec((1, tq, w), lambda bb, i: (bb, i, 0)),
        out_shape=jax.ShapeDtypeStruct((b, n, w), BF16),
        compiler_params=_params(("parallel", "arbitrary"), 56),
        name="neighbourhood_attention",
    )(q, k, k, k, k, vt, vt, vt, vt, kc, vtc, bias)


def _sconv_kernel(u_ref, pv_ref, nx_ref, w_ref, v_ref, x1_ref, x2_ref, *, nt):
    i = pl.program_id(1)
    u = u_ref[0]
    tl = u.shape[0]
    prev = jnp.where(i > 0, pv_ref[0, 7:8, :], 0.0)
    nxt = jnp.where(i < nt - 1, nx_ref[0, 0:1, :], 0.0)
    rows = lax.broadcasted_iota(jnp.int32, u.shape, 0)
    um = jnp.where(rows == 0, prev, pltpu.roll(u, 1, 0))
    up = jnp.where(rows == tl - 1, nxt, pltpu.roll(u, tl - 1, 0))
    w = w_ref[...]
    y = um * w[0:1] + u * w[1:2] + up * w[2:3]
    v_ref[...] = y[:, :HY_WIDTH]
    x1_ref[...] = y[:, HY_WIDTH:2 * HY_WIDTH]
    x2_ref[...] = y[:, 2 * HY_WIDTH:]


def _sconv(u, w, seq_major):
    b, n, _ = u.shape
    tl = _pick_tile(n, 512, 8)
    nt = n // tl
    r8 = tl // 8
    if seq_major:
        osd = jax.ShapeDtypeStruct((n, b * HY_WIDTH), F32)
        ospec = pl.BlockSpec((tl, HY_WIDTH), lambda bb, i: (i, bb))
    else:
        osd = jax.ShapeDtypeStruct((b * n, HY_WIDTH), F32)
        ospec = pl.BlockSpec((tl, HY_WIDTH), lambda bb, i: (bb * nt + i, 0))
    return pl.pallas_call(
        functools.partial(_sconv_kernel, nt=nt),
        grid=(b, nt),
        in_specs=[pl.BlockSpec((1, tl, HY_IN), lambda bb, i: (bb, i, 0)),
                  pl.BlockSpec((1, 8, HY_IN), lambda bb, i: (bb, jnp.maximum(i * r8 - 1, 0), 0)),
                  pl.BlockSpec((1, 8, HY_IN), lambda bb, i: (bb, jnp.minimum((i + 1) * r8, n // 8 - 1), 0)),
                  pl.BlockSpec((3, HY_IN), lambda bb, i: (0, 0))],
        out_specs=[ospec, ospec, ospec],
        out_shape=[osd, osd, osd],
        compiler_params=_params(("parallel", "parallel")),
        name="hyena_short_conv",
    )(u, u, u, w)


def _filt_kernel(z_ref, w1_ref, b1_ref, f1_ref, w2_ref, b2_ref, f2_ref, w3_ref, win_ref, g_ref, ss_ref):
    d = pl.program_id(0)
    i = pl.program_id(1)
    h = jnp.sin(f1_ref[...] * (_dot(w1_ref[...], z_ref[0].astype(BF16)) + b1_ref[...]))
    h = jnp.sin(f2_ref[...] * (_dot(w2_ref[...], h.astype(BF16)) + b2_ref[...]))
    h = _dot(w3_ref[0], h.astype(BF16)).T
    h = h * jnp.concatenate([win_ref[0]] * HY_ORDER, axis=1)
    rows = lax.broadcasted_iota(jnp.int32, h.shape, 0)
    h = jnp.where((rows == 0) & (i == 0) & (d == 1), 0.0, h)
    g_ref[0] = h
    ss = jnp.sum(h * h, axis=0, keepdims=True)
    first = (i == 0) & (d == 0)

    @pl.when(first)
    def _():
        ss_ref[...] = ss

    @pl.when(jnp.logical_not(first))
    def _():
        ss_ref[...] += ss


def _padded_filter(n, fw):
    r = jnp.arange(n, dtype=jnp.int32)
    idx = jnp.stack([r, jnp.mod(n - r, n)], axis=0).astype(F32)[:, :, None]
    t = idx / (n - 1)
    bands = jnp.linspace(1e-4, HY_BANDS - 1, HY_BANDS, dtype=F32)
    ang = (2.0 * math.pi / n) * idx * bands
    z = jnp.concatenate([t, jnp.cos(ang), -jnp.sin(ang), jnp.zeros((2, n, HY_EMB_PAD - HY_EMB), F32)], axis=-1)
    z = jnp.swapaxes(z, 1, 2)
    deltas = jnp.linspace(math.log(HY_DECAY_TARGET) / HY_DECAY_LONG_PCT,
                          math.log(HY_DECAY_TARGET) / HY_DECAY_SHORT_PCT, HY_WIDTH, dtype=F32)
    window = jnp.exp(-t * jnp.abs(deltas))
    tl = _pick_tile(n, 512, 128)
    half = HY_ORDER * HY_WIDTH
    c2 = lambda d, i: (0, 0)
    g, ss = pl.pallas_call(
        _filt_kernel,
        grid=(2, n // tl),
        in_specs=[pl.BlockSpec((1, HY_EMB_PAD, tl), lambda d, i: (d, 0, i)),
                  pl.BlockSpec((HY_HIDDEN, HY_EMB_PAD), c2), pl.BlockSpec((HY_HIDDEN, 1), c2),
                  pl.BlockSpec((HY_HIDDEN, 1), c2),
                  pl.BlockSpec((HY_HIDDEN, HY_HIDDEN), c2), pl.BlockSpec((HY_HIDDEN, 1), c2),
                  pl.BlockSpec((HY_HIDDEN, 1), c2),
                  pl.BlockSpec((1, half, HY_HIDDEN), lambda d, i: (d, 0, 0)),
                  pl.BlockSpec((1, tl, HY_WIDTH), lambda d, i: (d, i, 0))],
        out_specs=[pl.BlockSpec((1, tl, half), lambda d, i: (d, i, 0)), pl.BlockSpec((1, half), c2)],
        out_shape=[jax.ShapeDtypeStruct((2, n, half), F32), jax.ShapeDtypeStruct((1, half), F32)],
        compiler_params=_params(("arbitrary", "arbitrary")),
        name="hyena_filter_mlp",
    )(z, fw["w1"], fw["b1"], fw["f1"], fw["w2"], fw["b2"], fw["f2"], fw["w3"], window)
    return g.reshape(2 * n, half), lax.rsqrt(ss + NORM_EPS)


def _colmm_kernel(*refs, epi):
    w_ref, x_ref = refs[0], refs[1]
    o_ref = refs[-1]
    acc = _dot(w_ref[...], x_ref[...].astype(BF16))
    if epi == "gate":
        xg_ref, z_ref, b_ref = refs[2:5]
        acc = xg_ref[...] * (acc + z_ref[...] * b_ref[...])
    elif epi == "scale":
        acc = acc * refs[2][...]
    elif epi == "cmul":
        g = refs[2][...]
        mh = acc.shape[0] // 2
        ar, ai, gr, gi = acc[:mh], acc[mh:], g[:mh], g[mh:]
        acc = jnp.concatenate([ar * gr - ai * gi, ar * gi + ai * gr], axis=0)
    o_ref[...] = acc.astype(o_ref.dtype)


def _colmm(w, x, out_dtype, epi="none", extra=(), tn_cap=4096, g_col=None):
    m, k = w.shape
    nc = x.shape[1]
    tn = _pick_tile(nc, tn_cap, 128)
    col = lambda j: (0, j)
    specs = [pl.BlockSpec((m, k), lambda j: (0, 0)), pl.BlockSpec((k, tn), col)]
    if epi == "gate":
        specs += [pl.BlockSpec((m, tn), col), pl.BlockSpec((m, tn), col), pl.BlockSpec((1, tn), col)]
    elif epi == "scale":
        specs += [pl.BlockSpec((1, tn), col)]
    elif epi == "cmul":
        specs += [pl.BlockSpec((m, tn), lambda j: (0, g_col))]
    return pl.pallas_call(
        functools.partial(_colmm_kernel, epi=epi),
        grid=(nc // tn,),
        in_specs=specs,
        out_specs=pl.BlockSpec((m, tn), col),
        out_shape=jax.ShapeDtypeStruct((m, nc), out_dtype),
        compiler_params=_params(("parallel",), 48),
        name="dft_stage_" + epi,
    )(w, x, *extra)


def _bmm_kernel(m_ref, x_ref, o_ref, *, gs):
    for s in range(gs):
        x = x_ref[:, s]
        x = x.reshape(x.shape[0] * x.shape[1], x.shape[2])
        acc = _dot(m_ref[s], x)
        c = acc.shape[1]
        o_ref[:, s * c:(s + 1) * c] = acc.astype(o_ref.dtype)


def _bmm(mats, x, out_dtype):
    ng, p, qq = mats.shape
    c = x.shape[3]
    gs = _pick_tile(ng, BMM_GROUPS_PER_STEP, 1)
    return pl.pallas_call(
        functools.partial(_bmm_kernel, gs=gs),
        grid=(ng // gs,),
        in_specs=[pl.BlockSpec((gs, p, qq), lambda g: (g, 0, 0)),
                  pl.BlockSpec((2, gs, qq // 2, c), lambda g: (0, g, 0, 0))],
        out_specs=pl.BlockSpec((p, gs * c), lambda g: (0, g)),
        out_shape=jax.ShapeDtypeStruct((p, ng * c), out_dtype),
        compiler_params=_params(("parallel",), 48),
        name="dft_batched_real",
    )(mats, x)


def _tbmm_kernel(*refs, mode, gs):
    f_ref, twr_ref, twi_ref, x_ref = refs[:4]
    o_ref = refs[-1]
    for s in range(gs):
        xr, xi = x_ref[0, s].astype(F32), x_ref[1, s].astype(F32)
        reps = xr.shape[1] // twr_ref.shape[2]
        tr = jnp.concatenate([twr_ref[s]] * reps, axis=1)
        ti = jnp.concatenate([twi_ref[s]] * reps, axis=1)
        x = jnp.concatenate([xr * tr - xi * ti, xr * ti + xi * tr], axis=0).astype(BF16)
        acc = _dot(f_ref[...], x)
        ph = acc.shape[0] // 2
        ar, ai = acc[:ph], acc[ph:]
        if mode == "scale":
            sc = refs[4][...]
            o_ref[0, s] = ar * sc
            o_ref[1, s] = ai * sc
        else:
            g_ref, fi_ref = refs[4], refs[5]
            gr, gi = g_ref[0, s], g_ref[1, s]
            y = jnp.concatenate([ar * gr - ai * gi, ar * gi + ai * gr], axis=0).astype(BF16)
            back = _dot(fi_ref[...], y)
            br, bi = back[:ph], back[ph:]
            o_ref[0, s] = (br * tr + bi * ti).astype(o_ref.dtype)
            o_ref[1, s] = (bi * tr - br * ti).astype(o_ref.dtype)


def _tbmm(tabs, x, mode, extra, out_dtype, g_col=0):
    _, ng, n2, c = x.shape
    gs = _pick_tile(ng, BMM_GROUPS_PER_STEP, 1)
    p = 2 * n2
    blk = lambda g: (0, g, 0, 0)
    specs = [pl.BlockSpec((p, p), lambda g: (0, 0)),
             pl.BlockSpec((gs, n2, 128), lambda g: (g, 0, 0)),
             pl.BlockSpec((gs, n2, 128), lambda g: (g, 0, 0)),
             pl.BlockSpec((2, gs, n2, c), blk)]
    args = [tabs["hy_fwd"], tabs["hy_twr"], tabs["hy_twi"], x]
    if mode == "scale":
        specs.append(pl.BlockSpec((1, c), lambda g: (0, 0)))
        args.append(extra)
    else:
        specs += [pl.BlockSpec((2, gs, n2, c), lambda g: (0, g, 0, g_col)),
                  pl.BlockSpec((p, p), lambda g: (0, 0))]
        args += [extra, tabs["hy_inv"]]
    return pl.pallas_call(
        functools.partial(_tbmm_kernel, mode=mode, gs=gs),
        grid=(ng // gs,),
        in_specs=specs,
        out_specs=pl.BlockSpec((2, gs, n2, c), blk),
        out_shape=jax.ShapeDtypeStruct(x.shape, out_dtype),
        compiler_params=_params(("parallel",), 48),
        name="dft_inner_" + mode,
    )(*args)


def _angle(num, den):
    return (2.0 * math.pi / den) * jnp.mod(num, den).astype(F32)


def _dft_tables(n_lat):
    nn = 2 * n_lat
    n2 = DFT_N2
    n1 = nn // n2
    i1 = jnp.arange(n1, dtype=jnp.int32)
    a1 = _angle(i1[:, None] * i1[None, :], n1)
    c1, s1 = jnp.cos(a1), jnp.sin(a1)
    t = {}
    t["hy_f1"] = jnp.concatenate([c1, -s1], axis=0).astype(BF16)
    ch, sh = c1[:, :n1 // 2], s1[:, :n1 // 2]
    t["hy_f1c"] = jnp.concatenate([jnp.concatenate([ch, sh], axis=1),
                                   jnp.concatenate([-sh, ch], axis=1)], axis=0).astype(BF16)
    t["hy_i2c"] = jnp.concatenate([jnp.concatenate([ch.T, -sh.T], axis=1),
                                   jnp.concatenate([sh.T, ch.T], axis=1)], axis=0).astype(BF16)
    i2 = jnp.arange(n2, dtype=jnp.int32)
    a2 = _angle(i2[:, None] * i2[None, :], n2)
    c2, s2 = jnp.cos(a2), jnp.sin(a2)
    fwd = jnp.concatenate([jnp.concatenate([c2, s2], axis=1), jnp.concatenate([-s2, c2], axis=1)], axis=0)
    t["hy_fwd"] = fwd.astype(BF16)
    t["hy_inv"] = fwd.T.astype(BF16)
    at = _angle(i1[:, None] * i2[None, :], nn)
    lanes = (n1, n2, 128)
    t["hy_twr"] = jnp.broadcast_to(jnp.cos(at)[:, :, None], lanes)
    t["hy_twi"] = jnp.broadcast_to(-jnp.sin(at)[:, :, None], lanes)
    m2 = FN_N2
    m1 = n_lat // m2
    j1 = jnp.arange(m1, dtype=jnp.int32)
    b1 = _angle(j1[:, None] * j1[None, :], m1)
    cb, sb = jnp.cos(b1), jnp.sin(b1)
    t["fn_s1"] = jnp.concatenate([jnp.concatenate([cb, sb], axis=1),
                                  jnp.concatenate([-sb, cb], axis=1)], axis=0).astype(BF16)
    j2 = jnp.arange(m2, dtype=jnp.int32)
    kf = j1[:, None, None] + m1 * j2[None, :, None]
    b2 = _angle(kf * j2[None, None, :], n_lat)
    norm = 1.0 / math.sqrt(n_lat * FN_GROUP_DIM)
    t["fn_s3"] = (norm * jnp.concatenate([jnp.cos(b2), jnp.sin(b2)], axis=2)).astype(BF16)
    return t


def _small_dft_tables(n):
    nn = 2 * n
    k = jnp.arange(nn, dtype=jnp.int32)
    a = _angle(k[:, None] * k[None, :], nn)
    c, s = jnp.cos(a), jnp.sin(a)
    t = {}
    t["f_full"] = jnp.concatenate([c, -s], axis=0).astype(BF16)
    t["f_half"] = jnp.concatenate([c[:, :n], -s[:, :n]], axis=0).astype(BF16)
    t["i_half"] = jnp.concatenate([c[:n], -s[:n]], axis=1).astype(BF16)
    j = jnp.arange(n, dtype=jnp.int32)
    b = _angle(j[:, None] * j[None, :], n)
    norm = 1.0 / math.sqrt(n * FN_GROUP_DIM)
    t["fn"] = (norm * jnp.concatenate([jnp.cos(b), jnp.sin(b)], axis=1)).astype(BF16)
    return t


def _hyena_long(hy, short_w, fw, bias, tabs):
    b, n, _ = hy.shape
    assert b == 2
    cw = HY_WIDTH
    n2 = DFT_N2
    n1 = 2 * n // n2
    v, x1, x2 = _sconv(hy, short_w, False)
    g, gscale = _padded_filter(n, fw)
    half = HY_ORDER * HY_WIDTH
    ga = _colmm(tabs["hy_f1"], g.reshape(n1, n2 * half), BF16)
    gspec = _tbmm(tabs, ga.reshape(2, n1, n2, half), "scale", gscale * (1.0 / (2 * n)), F32)
    z = v.reshape(n1, n2 * cw)
    for o, xg in enumerate((x1, x2)):
        a = _colmm(tabs["hy_f1c"], z, BF16)
        bm = _tbmm(tabs, a.reshape(2, n1, n2, cw), "conv", gspec, BF16, g_col=o)
        brow = jnp.tile(bias[o], n2)[None, :]
        z = _colmm(tabs["hy_i2c"], bm.reshape(2 * n1, n2 * cw), F32, epi="gate",
                   extra=(xg.reshape(n1, n2 * cw), z, brow))
    return z.reshape(b, n, cw)


def _hyena_short(hy, short_w, fw, bias, tabs):
    b, n, _ = hy.shape
    cw = b * HY_WIDTH
    v, x1, x2 = _sconv(hy, short_w, True)
    g, gscale = _padded_filter(n, fw)
    gspec = _colmm(tabs["f_full"], g, F32, epi="scale", extra=(gscale * (1.0 / (2 * n)),), tn_cap=256)
    z = v
    for o, xg in enumerate((x1, x2)):
        y = _colmm(tabs["f_half"], z, BF16, epi="cmul", extra=(gspec,), tn_cap=HY_WIDTH, g_col=o)
        brow = jnp.tile(bias[o], b)[None, :]
        z = _colmm(tabs["i_half"], y, F32, epi="gate", extra=(xg, z, brow), tn_cap=256)
    return z


def _fn0_kernel(x_ref, w_ref, o_ref):
    acc = _dot(x_ref[0].astype(BF16), w_ref[...])
    o_ref[0] = acc[:, :BRANCH_WIDTH].astype(o_ref.dtype)
    o_ref[1] = acc[:, BRANCH_WIDTH:].astype(o_ref.dtype)


def _fnet_channel_dft(u):
    b, n, w = u.shape
    j = np.arange(FN_GROUP_DIM)
    ang = 2.0 * np.pi * ((j[:, None] * j[None, :]) % FN_GROUP_DIM) / FN_GROUP_DIM
    eye = np.eye(FN_GROUPS)
    wc = np.concatenate([np.kron(eye, np.cos(ang)), -np.kron(eye, np.sin(ang))], axis=1)
    wc = jnp.asarray(wc, F32).astype(BF16)
    tm = _pick_tile(n, 1024, 8)
    return pl.pallas_call(
        _fn0_kernel,
        grid=(b, n // tm),
        in_specs=[pl.BlockSpec((1, tm, w), lambda bb, i: (bb, i, 0)),
                  pl.BlockSpec((w, 2 * w), lambda bb, i: (0, 0))],
        out_specs=pl.BlockSpec((2, tm, w), lambda bb, i: (0, i, bb)),
        out_shape=jax.ShapeDtypeStruct((2, n, b * w), BF16),
        compiler_params=_params(("parallel", "parallel")),
        name="fnet_channel_dft",
    )(u, wc)


def _fnet_long(u, tabs):
    b, n, w = u.shape
    cw = b * w
    m2 = FN_N2
    m1 = n // m2
    wri = _fnet_channel_dft(u)
    a = _colmm(tabs["fn_s1"], wri.reshape(2 * m1, m2 * cw), BF16)
    y = _bmm(tabs["fn_s3"], a.reshape(2, m1, m2, cw), BF16)
    return y.reshape(n, cw)


def _fnet_short(u, tabs):
    b, n, w = u.shape
    wri = _fnet_channel_dft(u)
    return _colmm(tabs["fn"], wri.reshape(2 * n, b * w), BF16, tn_cap=256)


def _merge_kernel(x_ref, g_ref, sc_ref, sh_ref, gt_ref, oa_ref, ob_ref, oc_ref, od_ref,
                  wg_ref, wb_ref, wo_ref, o_ref):
    x = x_ref[0]
    hb = _normmod(x, g_ref[...], sc_ref[0], sh_ref[0]).astype(BF16)
    branches = (oa_ref[0], ob_ref[0], oc_ref[...], od_ref[0])
    acc = jnp.zeros(x.shape, F32)
    for nbr, o in enumerate(branches):
        gate = _dot(hb, wg_ref[:, nbr * D_MODEL:(nbr + 1) * D_MODEL])
        proj = _dot(o.astype(BF16), wb_ref[nbr])
        acc = acc + jax.nn.sigmoid(gate) * proj
    m = _dot(acc.astype(BF16), wo_ref[...])
    o_ref[0] = x + gt_ref[0] * m


def _merge(x, g, sc, sh, gt, oa, ob, oc, od, w):
    b, n, _ = x.shape
    tm = _pick_tile(n, 512, 8)
    tok = lambda bb, i: (bb, i, 0)
    row = lambda bb, i: (bb, 0, 0)
    seq = lambda bb, i: (i, bb)
    c2 = lambda bb, i: (0, 0)
    return pl.pallas_call(
        _merge_kernel,
        grid=(b, n // tm),
        in_specs=[pl.BlockSpec((1, tm, D_MODEL), tok),
                  pl.BlockSpec((1, D_MODEL), c2),
                  pl.BlockSpec((1, 1, D_MODEL), row), pl.BlockSpec((1, 1, D_MODEL), row),
                  pl.BlockSpec((1, 1, D_MODEL), row),
                  pl.BlockSpec((1, tm, BRANCH_WIDTH), tok),
                  pl.BlockSpec((1, tm, BRANCH_WIDTH), tok),
                  pl.BlockSpec((tm, BRANCH_WIDTH), seq),
                  pl.BlockSpec((1, tm, BRANCH_WIDTH), tok),
                  pl.BlockSpec((D_MODEL, N_BRANCH * D_MODEL), c2),
                  pl.BlockSpec((N_BRANCH, BRANCH_WIDTH, D_MODEL), lambda bb, i: (0, 0, 0)),
                  pl.BlockSpec((D_MODEL, D_MODEL), c2)],
        out_specs=pl.BlockSpec((1, tm, D_MODEL), tok),
        out_shape=jax.ShapeDtypeStruct(x.shape, F32),
        compiler_params=_params(("parallel", "parallel"), 56),
        name="branch_merge",
    )(x, g, sc, sh, gt, oa, ob, oc, od, w["wgate"], w["wbranch"], w["wout"])


def _moe_kernel(x_ref, g_ref, sc_ref, sh_ref, gt_ref, wr_ref, br_ref, wg_ref, wu_ref, wd_ref, fg_ref,
                o_ref, h_s, wt_s, acc_s, *, final):
    e = pl.program_id(2)

    @pl.when(e == 0)
    def _():
        hb = _normmod(x_ref[0], g_ref[...], sc_ref[0], sh_ref[0]).astype(BF16)
        h_s[...] = hb
        r = _dot(hb, wr_ref[...]) + br_ref[...]
        lane = lax.broadcasted_iota(jnp.int32, r.shape, 1)
        neg = jnp.float32(-jnp.inf)
        isg = lane < MOE_GROUPS
        gmax = jnp.max(jnp.where(isg, r, neg), axis=1, keepdims=True)
        gsum = jnp.sum(jnp.where(isg, jnp.exp(r - gmax), 0.0), axis=1, keepdims=True)
        g_p = 1.0 / gsum
        gidx = jnp.min(jnp.where(isg & (r == gmax), lane, ROUTER_LANES), axis=1, keepdims=True)
        ise = ((lane >= MOE_GROUPS) & (lane < MOE_GROUPS + MOE_EXPERTS)
               & (jnp.right_shift(lane - MOE_GROUPS, 2) == gidx))
        el = jnp.where(ise, r, neg)
        e1 = jnp.max(el, axis=1, keepdims=True)
        i1 = jnp.min(jnp.where(ise & (r == e1), lane, ROUTER_LANES), axis=1, keepdims=True)
        el2 = jnp.where(lane == i1, neg, el)
        e2 = jnp.max(el2, axis=1, keepdims=True)
        i2 = jnp.min(jnp.where(el2 == e2, lane, ROUTER_LANES), axis=1, keepdims=True)
        d = jnp.exp(e2 - e1)
        w1 = 1.0 / (1.0 + d)
        w2 = d / (1.0 + d)
        wt_s[...] = jnp.where(lane == i1, g_p * w1, jnp.where(lane == i2, g_p * w2, 0.0))
        acc_s[...] = jnp.zeros(acc_s.shape, F32)

    hb = h_s[...]
    wt = wt_s[...]
    lane = lax.broadcasted_iota(jnp.int32, wt.shape, 1)
    acts = []
    for k in range(MOE_PER_GROUP):
        a = _dot(hb, wg_ref[k])
        u = _dot(hb, wu_ref[k])
        col = MOE_GROUPS + e * MOE_PER_GROUP + k
        wcol = jnp.sum(jnp.where(lane == col, wt, 0.0), axis=1, keepdims=True)
        acts.append(((a * jax.nn.sigmoid(a)) * u * wcol).astype(BF16))
    act = jnp.concatenate(acts, axis=1)
    wd = wd_ref[...].reshape(MOE_PER_GROUP * MOE_HIDDEN, D_MODEL)
    acc_s[...] += _dot(act, wd)

    @pl.when(e == MOE_GROUPS - 1)
    def _():
        y = x_ref[0] + gt_ref[0] * acc_s[...]
        if final:
            y = _rms(y) * fg_ref[...]
        o_ref[0] = y


def _moe(x, g, sc, sh, gt, w, fg, final):
    b, n, _ = x.shape
    tm = _pick_tile(n, 1024, 8)
    tok = lambda bb, i, e: (bb, i, 0)
    row = lambda bb, i, e: (bb, 0, 0)
    c2 = lambda bb, i, e: (0, 0)
    ex = lambda bb, i, e: (e, 0, 0)
    return pl.pallas_call(
        functools.partial(_moe_kernel, final=final),
        grid=(b, n // tm, MOE_GROUPS),
        in_specs=[pl.BlockSpec((1, tm, D_MODEL), tok),
                  pl.BlockSpec((1, D_MODEL), c2),
                  pl.BlockSpec((1, 1, D_MODEL), row), pl.BlockSpec((1, 1, D_MODEL), row),
                  pl.BlockSpec((1, 1, D_MODEL), row),
                  pl.BlockSpec((D_MODEL, ROUTER_LANES), c2), pl.BlockSpec((1, ROUTER_LANES), c2),
                  pl.BlockSpec((MOE_PER_GROUP, D_MODEL, MOE_HIDDEN), ex),
                  pl.BlockSpec((MOE_PER_GROUP, D_MODEL, MOE_HIDDEN), ex),
                  pl.BlockSpec((MOE_PER_GROUP, MOE_HIDDEN, D_MODEL), ex),
                  pl.BlockSpec((1, D_MODEL), c2)],
        out_specs=pl.BlockSpec((1, tm, D_MODEL), tok),
        out_shape=jax.ShapeDtypeStruct(x.shape, F32),
        scratch_shapes=[pltpu.VMEM((tm, D_MODEL), BF16), pltpu.VMEM((tm, ROUTER_LANES), F32),
                        pltpu.VMEM((tm, D_MODEL), F32)],
        compiler_params=_params(("parallel", "parallel", "arbitrary"), 56),
        name="hier_moe",
    )(x, g, sc, sh, gt, w["wr"], w["br"], w["wg"], w["wu"], w["wd"], fg)


def _layer_weights(l, w_in, mla_q_norm, mla_w_uq, mla_kv_norm, mla_w_ukv, w_branch, w_out,
                   moe_w_group, moe_b_group, moe_w_expert, moe_b_expert, moe_w_gate, moe_w_up, moe_w_down):
    wi = w_in[l]
    b0 = MLA_Q_RANK
    b1 = b0 + MLA_KV_RANK
    b2 = b1 + MLA_ROPE
    b3 = b2 + HY_IN
    b4 = b3 + BRANCH_WIDTH
    b5 = b4 + 3 * BRANCH_WIDTH
    zpad = lambda r, c: jnp.zeros((r, c), F32)
    w_kr = wi[:, b1:b2]
    krp = jnp.concatenate([zpad(D_MODEL, MLA_NOPE), w_kr[:, 0::2], w_kr[:, 1::2],
                           zpad(D_MODEL, MLA_HEAD_PAD - MLA_NOPE - MLA_ROPE)], axis=1)
    w_na = wi[:, b4:b5]
    w_na = jnp.concatenate([w_na[:, :BRANCH_WIDTH] * NA_SCALE, w_na[:, BRANCH_WIDTH:]], axis=1)
    wa = jnp.concatenate([wi[:, :b1], krp, wi[:, b2:b4], w_na], axis=1).astype(BF16)
    uq = mla_w_uq[l].reshape(MLA_Q_RANK, MLA_HEADS, MLA_NOPE + MLA_ROPE) * (MLA_SCALE * LOG2E)
    uq = jnp.concatenate([uq[..., :MLA_NOPE], uq[..., MLA_NOPE::2], uq[..., MLA_NOPE + 1::2],
                          jnp.zeros((MLA_Q_RANK, MLA_HEADS, MLA_HEAD_PAD - MLA_NOPE - MLA_ROPE), F32)], axis=-1)
    ukv = mla_w_ukv[l].reshape(MLA_KV_RANK, MLA_HEADS, MLA_NOPE + MLA_V)
    uk = jnp.concatenate([ukv[..., :MLA_NOPE],
                          jnp.zeros((MLA_KV_RANK, MLA_HEADS, MLA_HEAD_PAD - MLA_NOPE), F32)], axis=-1)
    uv = ukv[..., MLA_NOPE:]
    wr = jnp.concatenate([moe_w_group[l], moe_w_expert[l],
                          zpad(D_MODEL, ROUTER_LANES - MOE_GROUPS - MOE_EXPERTS)], axis=1)
    br = jnp.concatenate([moe_b_group[l], moe_b_expert[l],
                          jnp.zeros((ROUTER_LANES - MOE_GROUPS - MOE_EXPERTS,), F32)])[None, :]
    return dict(
        wa=wa, qn=mla_q_norm[l][None, :], kvn=mla_kv_norm[l][None, :],
        wuq=uq.reshape(MLA_Q_RANK, -1).astype(BF16), wuk=uk.reshape(MLA_KV_RANK, -1).astype(BF16),
        wuv=uv.reshape(MLA_KV_RANK, -1).astype(BF16),
        wgate=wi[:, b5:].astype(BF16), wbranch=w_branch[l].astype(BF16), wout=w_out[l].astype(BF16),
        wr=wr.astype(BF16), br=br, wg=moe_w_gate[l].astype(BF16), wu=moe_w_up[l].astype(BF16),
        wd=moe_w_down[l].astype(BF16))


def _rope_tables(n):
    t = jnp.arange(n, dtype=jnp.int32)
    row = (t // GRID_W).astype(F32)
    col = (t % GRID_W).astype(F32)
    n_freq = MLA_ROPE // 4
    inv = ROPE_BASE ** (-jnp.arange(n_freq, dtype=F32) / n_freq)
    ang = jnp.concatenate([row[:, None] * inv, col[:, None] * inv], axis=-1)
    c, s = jnp.cos(ang), jnp.sin(ang)
    one = jnp.ones((n, MLA_NOPE), F32)
    zero = jnp.zeros((n, MLA_NOPE), F32)
    hr = MLA_ROPE // 2
    tail1 = jnp.ones((n, MLA_HEAD_PAD - MLA_NOPE - MLA_ROPE), F32)
    tail0 = jnp.zeros((n, MLA_HEAD_PAD - MLA_NOPE - MLA_ROPE), F32)
    cos = jnp.concatenate([one, c, c, tail1], axis=1)
    s1 = jnp.concatenate([zero, -s, jnp.zeros((n, hr), F32), tail0], axis=1)
    s2 = jnp.concatenate([zero, jnp.zeros((n, hr), F32), s, tail0], axis=1)
    return cos, s1, s2


def _identity_rope(n):
    return (jnp.ones((n, MLA_HEAD_PAD), F32), jnp.zeros((n, MLA_HEAD_PAD), F32),
            jnp.zeros((n, MLA_HEAD_PAD), F32))


def kernel(x, c, ctx, c_ctx, ada_w, ada_b, norm1_g, norm2_g, w_in, mla_q_norm, mla_w_uq, mla_kv_norm, mla_w_ukv, hy_short_w, hy_w1, hy_b1, hy_freq1, hy_w2, hy_b2, hy_freq2, hy_w3, hy_bias, na_rpb, w_branch, w_out, moe_w_group, moe_b_group, moe_w_expert, moe_b_expert, moe_w_gate, moe_w_up, moe_w_down, final_norm_g):
    b, n_lat, _ = x.shape
    n_ctx = ctx.shape[1]
    assert b <= 4 and n_lat % (NA_QROWS * GRID_W) == 0 and n_lat // GRID_W >= 2 * NA_QROWS

    cvec = jnp.concatenate([c, c_ctx[None, :], jnp.zeros((8 - b - 1, D_MODEL), F32)], axis=0)
    mod = _mod_vectors(cvec, ada_w, ada_b)

    rope_lat = _rope_tables(n_lat)
    rope_ctx = _identity_rope(n_ctx)
    tabs_lat = _dft_tables(n_lat)
    tabs_ctx = _small_dft_tables(n_ctx)
    fg = final_norm_g[None, :]

    for l in range(DEPTH):
        last = l == DEPTH - 1
        w = _layer_weights(l, w_in, mla_q_norm, mla_w_uq, mla_kv_norm, mla_w_ukv, w_branch, w_out,
                           moe_w_group, moe_b_group, moe_w_expert, moe_b_expert,
                           moe_w_gate, moe_w_up, moe_w_down)
        ml = mod[l, :b].reshape(b, 1, 6, D_MODEL)
        mc = jnp.broadcast_to(mod[l, b].reshape(1, 1, 6, D_MODEL), (b, 1, 6, D_MODEL))
        sh1, sc1, gt1, sh2, sc2, gt2 = [ml[:, :, i] for i in range(6)]
        sh1c, sc1c, gt1c, sh2c, sc2c, gt2c = [mc[:, :, i] for i in range(6)]
        g1 = norm1_g[l][None, :]
        g2 = norm2_g[l][None, :]
        fw = dict(
            w1=jnp.concatenate([hy_w1[l], jnp.zeros((HY_EMB_PAD - HY_EMB, HY_HIDDEN), F32)], axis=0).T.astype(BF16),
            b1=hy_b1[l][:, None], f1=hy_freq1[l][:, None], w2=hy_w2[l].T.astype(BF16), b2=hy_b2[l][:, None],
            f2=hy_freq2[l][:, None],
            w3=jnp.transpose(hy_w3[l].reshape(HY_HIDDEN, 2, HY_ORDER * HY_WIDTH), (1, 2, 0)).astype(BF16))

        n_keys = n_lat + n_ctx
        kv_bufs = (jnp.zeros((b, MLA_HEADS, n_keys, MLA_HEAD_PAD), BF16),
                   jnp.zeros((b, MLA_HEADS, MLA_VT_ROWS, n_keys), PV_DTYPE))
        q, kall, vtall, hy, fn, naq, nak, nav = _inproj(x, g1, sc1, sh1, w, rope_lat, kv_bufs, 0, False)
        qc, kc, vc, kall, vtall, hyc, fnc, naqc, nakc, navc = _inproj(
            ctx, g1, sc1c, sh1c, w, rope_ctx, (kall, vtall), n_lat, True)

        oa = _flash_t(q, kall, vtall, MLA_V)
        od = _na(naq, nak, nav, nakc, navc, _na_bias_tables(na_rpb[l]))
        ob = _hyena_long(hy, hy_short_w[l], fw, hy_bias[l], tabs_lat)
        oc = _fnet_long(fn, tabs_lat)
        x_new = _merge(x, g1, sc1, sh1, gt1, oa, ob, oc, od, w)

        if not last:
            oa_c = _flash(qc, kc, vc, True)
            heads = lambda t: jnp.transpose(t.reshape(b, n_ctx, NA_HEADS, NA_HEAD_DIM), (0, 2, 1, 3))
            od_c = _flash(heads(naqc), heads(nakc), heads(jnp.swapaxes(navc, 1, 2)), False)
            ob_c = _hyena_short(hyc, hy_short_w[l], fw, hy_bias[l], tabs_ctx)
            ob_c = jnp.transpose(ob_c.reshape(n_ctx, b, HY_WIDTH), (1, 0, 2))
            oc_c = _fnet_short(fnc, tabs_ctx)
            ctx = _merge(ctx, g1, sc1c, sh1c, gt1c, oa_c, ob_c, oc_c, od_c, w)
            ctx = _moe(ctx, g2, sc2c, sh2c, gt2c, w, fg, False)
        x = _moe(x_new, g2, sc2, sh2, gt2, w, fg, last)
    return x
```

```python
import functools
import math

import numpy as np
import jax
import jax.numpy as jnp
from jax import lax
from jax.experimental import pallas as pl
from jax.experimental.pallas import tpu as pltpu

F32 = jnp.float32
BF16 = jnp.bfloat16

D_MODEL = 1024
DEPTH = 2
GRID_W = 64
NORM_EPS = 1e-6
MASK_VALUE = -1e30
BRANCH_WIDTH = D_MODEL // 4
N_BRANCH = 4

MLA_HEADS = 4
MLA_Q_RANK = D_MODEL // 4
MLA_KV_RANK = D_MODEL // 8
MLA_NOPE = 64
MLA_ROPE = 32
MLA_V = BRANCH_WIDTH // MLA_HEADS
MLA_SCALE = 1.0 / math.sqrt(MLA_NOPE + MLA_ROPE)
MLA_HEAD_PAD = 128
MLA_VT_ROWS = MLA_V + 16
PV_DTYPE = BF16
ROPE_BASE = 10000.0
LOG2E = 1.4426950408889634

HY_WIDTH = BRANCH_WIDTH
HY_ORDER = 2
HY_BANDS = 16
HY_EMB = 1 + 2 * HY_BANDS
HY_EMB_PAD = 128
HY_HIDDEN = 64
HY_DECAY_TARGET = 1e-2
HY_DECAY_SHORT_PCT = 0.3
HY_DECAY_LONG_PCT = 1.5
HY_IN = (HY_ORDER + 1) * HY_WIDTH

FN_GROUPS = 4
FN_GROUP_DIM = BRANCH_WIDTH // FN_GROUPS

NA_HEADS = 4
NA_HEAD_DIM = BRANCH_WIDTH // NA_HEADS
NA_KH = 8
NA_KW = 16
NA_SCALE = 1.0 / math.sqrt(NA_HEAD_DIM)
NA_QROWS = 8
NA_KROWS = 16
NA_VT_PAD = 16

MOE_GROUPS = 4
MOE_PER_GROUP = 4
MOE_EXPERTS = MOE_GROUPS * MOE_PER_GROUP
MOE_HIDDEN = 256
ROUTER_LANES = 128

MLA_IN = MLA_Q_RANK + MLA_KV_RANK + MLA_ROPE
MIX_IN = MLA_IN + HY_IN + BRANCH_WIDTH + 3 * BRANCH_WIDTH
MIX_PAD = MLA_Q_RANK + MLA_KV_RANK + MLA_HEAD_PAD + HY_IN + BRANCH_WIDTH + 3 * BRANCH_WIDTH

DFT_N2 = 256
FN_N2 = 128
BMM_GROUPS_PER_STEP = 4
FLASH_K_TILE = 3328
FLASH_CHUNKS = 3
FLASH_Q_TILE = 2048
FLASH_Q_SUB = 512


def _params(sem, vmem_mb=None):
    kw = dict(dimension_semantics=sem)
    if vmem_mb is not None:
        kw["vmem_limit_bytes"] = vmem_mb << 20
    return pltpu.CompilerParams(**kw)


def _pick_tile(n, cap, mult):
    best = None
    for t in range(mult, min(n, cap) + 1, mult):
        if n % t == 0:
            best = t
    return best if best is not None else n


def _rms(x):
    return x * lax.rsqrt(jnp.mean(x * x, axis=-1, keepdims=True) + NORM_EPS)


def _normmod(x, g, sc, sh):
    return (_rms(x) * g) * (1.0 + sc) + sh


def _dot(a, b):
    return jnp.dot(a, b, preferred_element_type=F32)


def _dot_nt(a, b):
    return lax.dot_general(a, b, (((1,), (1,)), ((), ())), preferred_element_type=F32)


def _mod_kernel(c_ref, w_ref, b_ref, o_ref):
    c = c_ref[...]
    s = c * jax.nn.sigmoid(c)
    o_ref[0] = _dot(s.astype(BF16), w_ref[0].astype(BF16)) + b_ref[0]


def _mod_vectors(cvec, ada_w, ada_b):
    n6 = ada_w.shape[-1]
    tn = 1024
    return pl.pallas_call(
        _mod_kernel,
        grid=(DEPTH, n6 // tn),
        in_specs=[pl.BlockSpec((8, D_MODEL), lambda l, j: (0, 0)),
                  pl.BlockSpec((1, D_MODEL, tn), lambda l, j: (l, 0, j)),
                  pl.BlockSpec((1, 1, tn), lambda l, j: (l, 0, j))],
        out_specs=pl.BlockSpec((1, 8, tn), lambda l, j: (l, 0, j)),
        out_shape=jax.ShapeDtypeStruct((DEPTH, 8, n6), F32),
        compiler_params=_params(("parallel", "parallel")),
        name="adaln_vectors",
    )(cvec, ada_w, ada_b.reshape(DEPTH, 1, n6))


def _inproj_kernel(*refs, ctx_mode):
    (x_ref, g_ref, sc_ref, sh_ref, wa_ref, qn_ref, kvn_ref, wuq_ref, wuk_ref, wuv_ref,
     cos_ref, s1_ref, s2_ref) = refs[:13]
    if ctx_mode:
        (q_ref, k_ref, v_ref, kall_ref, vtall_ref, hy_ref, fn_ref, naq_ref, nak_ref, nav_ref) = refs[15:]
    else:
        (q_ref, kall_ref, vtall_ref, hy_ref, fn_ref, naq_ref, nak_ref, nav_ref) = refs[15:]
    h = _normmod(x_ref[0], g_ref[...], sc_ref[0], sh_ref[0]).astype(BF16)
    p = _dot(h, wa_ref[...])
    c0 = MLA_Q_RANK
    c1 = c0 + MLA_KV_RANK
    c2 = c1 + MLA_HEAD_PAD
    c3 = c2 + HY_IN
    c4 = c3 + BRANCH_WIDTH
    c5 = c4 + BRANCH_WIDTH
    c6 = c5 + BRANCH_WIDTH
    cq, ckv, krp = p[:, :c0], p[:, c0:c1], p[:, c1:c2]
    hy_ref[0] = p[:, c2:c3]
    fn_ref[0] = p[:, c3:c4]
    naq_ref[0] = p[:, c4:c5].astype(BF16)
    nak_ref[0] = p[:, c5:c6].astype(BF16)
    nav_ref[0] = p[:, c6:].T.astype(BF16)
    cqn = (_rms(cq) * qn_ref[...]).astype(BF16)
    ckn = (_rms(ckv) * kvn_ref[...]).astype(BF16)
    q = _dot(cqn, wuq_ref[...])
    kn = _dot(ckn, wuk_ref[...])
    v = _dot(ckn, wuv_ref[...])
    cos, s1, s2 = cos_ref[...], s1_ref[...], s2_ref[...]

    def rope(t):
        return (t * cos + pltpu.roll(t, MLA_HEAD_PAD - MLA_ROPE // 2, 1) * s1
                + pltpu.roll(t, MLA_ROPE // 2, 1) * s2)

    kr = rope(krp)
    vt = v.T
    tail_rows = lax.broadcasted_iota(jnp.int32, (MLA_VT_ROWS - MLA_V, v.shape[0]), 0)
    tail = jnp.where(tail_rows == 0, 1.0, 0.0).astype(PV_DTYPE)
    for hh in range(MLA_HEADS):
        sl = slice(hh * MLA_HEAD_PAD, (hh + 1) * MLA_HEAD_PAD)
        vs = slice(hh * MLA_V, (hh + 1) * MLA_V)
        kh = (kn[:, sl] + kr).astype(BF16)
        qh = rope(q[:, sl])
        q_ref[0, hh] = qh.astype(BF16) if ctx_mode else qh.T.astype(BF16)
        kall_ref[0, hh] = kh
        vtall_ref[0, hh, :MLA_V] = vt[vs].astype(PV_DTYPE)
        vtall_ref[0, hh, MLA_V:] = tail
        if ctx_mode:
            k_ref[0, hh] = kh
            v_ref[0, hh] = v[:, vs].astype(BF16)


def _inproj(x, g, sc, sh, w, rope_tabs, kv_bufs, key_off, ctx_mode):
    b, n, _ = x.shape
    n_keys = kv_bufs[0].shape[2]
    tm = _pick_tile(n, 512, 128)
    assert key_off % tm == 0
    ko = key_off // tm
    cos, s1, s2 = rope_tabs
    tok = lambda bb, i: (bb, i, 0)
    row = lambda bb, i: (bb, 0, 0)
    const2 = lambda bb, i: (0, 0)
    hd = lambda bb, i: (bb, 0, i, 0)
    tab = lambda bb, i: (i, 0)
    hp = MLA_HEADS * MLA_HEAD_PAD
    in_specs = [pl.BlockSpec((1, tm, D_MODEL), tok),
                pl.BlockSpec((1, D_MODEL), const2),
                pl.BlockSpec((1, 1, D_MODEL), row),
                pl.BlockSpec((1, 1, D_MODEL), row),
                pl.BlockSpec((D_MODEL, MIX_PAD), const2),
                pl.BlockSpec((1, MLA_Q_RANK), const2),
                pl.BlockSpec((1, MLA_KV_RANK), const2),
                pl.BlockSpec((MLA_Q_RANK, hp), const2),
                pl.BlockSpec((MLA_KV_RANK, hp), const2),
                pl.BlockSpec((MLA_KV_RANK, BRANCH_WIDTH), const2),
                pl.BlockSpec((tm, MLA_HEAD_PAD), tab),
                pl.BlockSpec((tm, MLA_HEAD_PAD), tab),
                pl.BlockSpec((tm, MLA_HEAD_PAD), tab)]
    args = [x, g, sc, sh, w["wa"], w["qn"], w["kvn"], w["wuq"], w["wuk"], w["wuv"], cos, s1, s2]
    q_spec = pl.BlockSpec((1, MLA_HEADS, tm, MLA_HEAD_PAD), hd)
    q_sd = jax.ShapeDtypeStruct((b, MLA_HEADS, n, MLA_HEAD_PAD), BF16)
    kall_spec = pl.BlockSpec((1, MLA_HEADS, tm, MLA_HEAD_PAD), lambda bb, i: (bb, 0, ko + i, 0))
    vtall_spec = pl.BlockSpec((1, MLA_HEADS, MLA_VT_ROWS, tm), lambda bb, i: (bb, 0, 0, ko + i))
    kall_sd = jax.ShapeDtypeStruct((b, MLA_HEADS, n_keys, MLA_HEAD_PAD), BF16)
    vtall_sd = jax.ShapeDtypeStruct((b, MLA_HEADS, MLA_VT_ROWS, n_keys), PV_DTYPE)
    rest_specs = [pl.BlockSpec((1, tm, HY_IN), tok)] + [pl.BlockSpec((1, tm, BRANCH_WIDTH), tok)] * 3 \
        + [pl.BlockSpec((1, BRANCH_WIDTH, tm), lambda bb, i: (bb, 0, i))]
    rest_sd = [jax.ShapeDtypeStruct((b, n, HY_IN), F32), jax.ShapeDtypeStruct((b, n, BRANCH_WIDTH), F32)] \
        + [jax.ShapeDtypeStruct((b, n, BRANCH_WIDTH), BF16)] * 2 \
        + [jax.ShapeDtypeStruct((b, BRANCH_WIDTH, n), BF16)]
    in_specs += [pl.BlockSpec(memory_space=pl.ANY), pl.BlockSpec(memory_space=pl.ANY)]
    args += list(kv_bufs)
    if ctx_mode:
        out_specs = [q_spec, q_spec, pl.BlockSpec((1, MLA_HEADS, tm, MLA_V), hd), kall_spec, vtall_spec]
        out_shape = [q_sd, q_sd, jax.ShapeDtypeStruct((b, MLA_HEADS, n, MLA_V), BF16), kall_sd, vtall_sd]
        aliases = {13: 3, 14: 4}
    else:
        qt_spec = pl.BlockSpec((1, MLA_HEADS, MLA_HEAD_PAD, tm), lambda bb, i: (bb, 0, 0, i))
        qt_sd = jax.ShapeDtypeStruct((b, MLA_HEADS, MLA_HEAD_PAD, n), BF16)
        out_specs = [qt_spec, kall_spec, vtall_spec]
        out_shape = [qt_sd, kall_sd, vtall_sd]
        aliases = {13: 1, 14: 2}
    return pl.pallas_call(
        functools.partial(_inproj_kernel, ctx_mode=ctx_mode),
        grid=(b, n // tm),
        in_specs=in_specs,
        out_specs=out_specs + rest_specs,
        out_shape=out_shape + rest_sd,
        input_output_aliases=aliases,
        compiler_params=_params(("parallel", "parallel"), 48),
        name="input_projection",
    )(*args)


def _flash_kernel(q_ref, k_ref, v_ref, o_ref, m_ref, l_ref, acc_ref, *, nh, use_exp2):
    j = pl.program_id(2)

    @pl.when(j == 0)
    def _():
        m_ref[...] = jnp.full(m_ref.shape, MASK_VALUE, F32)
        l_ref[...] = jnp.zeros(l_ref.shape, F32)
        acc_ref[...] = jnp.zeros(acc_ref.shape, F32)

    ex = jnp.exp2 if use_exp2 else jnp.exp
    for h in range(nh):
        s = _dot_nt(q_ref[0, h], k_ref[0, h])
        m_prev = m_ref[h]
        m_cur = jnp.maximum(m_prev, jnp.max(s, axis=1, keepdims=True))
        alpha = ex(m_prev - m_cur)
        p = ex(s - m_cur)
        l_ref[h] = alpha * l_ref[h] + jnp.sum(p, axis=1, keepdims=True)
        acc_ref[h] = alpha * acc_ref[h] + _dot(p.astype(BF16), v_ref[0, h])
        m_ref[h] = m_cur

    @pl.when(j == pl.num_programs(2) - 1)
    def _():
        outs = [acc_ref[h] / l_ref[h] for h in range(nh)]
        o_ref[0] = jnp.concatenate(outs, axis=1).astype(o_ref.dtype)


def _flash(q, k, v, use_exp2):
    b, nh, lq, dk = q.shape
    lk, dv = k.shape[2], v.shape[3]
    tq = _pick_tile(lq, 512, 8)
    tk = _pick_tile(lk, 1280, 128)
    return pl.pallas_call(
        functools.partial(_flash_kernel, nh=nh, use_exp2=use_exp2),
        grid=(b, lq // tq, lk // tk),
        in_specs=[pl.BlockSpec((1, nh, tq, dk), lambda bb, i, j: (bb, 0, i, 0)),
                  pl.BlockSpec((1, nh, tk, dk), lambda bb, i, j: (bb, 0, j, 0)),
                  pl.BlockSpec((1, nh, tk, dv), lambda bb, i, j: (bb, 0, j, 0))],
        out_specs=pl.BlockSpec((1, tq, nh * dv), lambda bb, i, j: (bb, i, 0)),
        out_shape=jax.ShapeDtypeStruct((b, lq, nh * dv), BF16),
        scratch_shapes=[pltpu.VMEM((nh, tq, 1), F32), pltpu.VMEM((nh, tq, 1), F32),
                        pltpu.VMEM((nh, tq, dv), F32)],
        compiler_params=_params(("parallel", "parallel", "arbitrary"), 48),
        name="softmax_attention",
    )(q, k, v)


def _flash_t_kernel(q_ref, k_ref, vt_ref, o_ref, m_ref, acc_ref, st_ref, p_ref, *, nh, dv):
    j = pl.program_id(2)

    @pl.when(j == 0)
    def _():
        m_ref[...] = jnp.full(m_ref.shape, MASK_VALUE, F32)
        acc_ref[...] = jnp.zeros(acc_ref.shape, F32)

    rows = vt_ref.shape[2]

    tq = q_ref.shape[3]
    qs = min(tq, FLASH_Q_SUB)
    units = [(h, c) for h in range(nh) for c in range(tq // qs)]

    def scores(i):
        h, c = units[i]
        cols = slice(c * qs, (c + 1) * qs)
        st = _dot(k_ref[0, h], q_ref[0, h, :, cols])
        st_ref[i % 2] = st
        m_prev = m_ref[h, :, cols]
        m_cur = jnp.maximum(m_prev, jnp.max(st, axis=0, keepdims=True))
        m_ref[h, :, cols] = m_cur
        return m_prev, m_cur

    def probs(i, m_cur):
        p_ref[i % 2] = jnp.exp2(st_ref[i % 2] - m_cur).astype(PV_DTYPE)

    def values(i, m_prev, m_cur):
        h, c = units[i]
        cols = slice(c * qs, (c + 1) * qs)
        alpha = jnp.exp2(m_prev - m_cur)
        acc_ref[h, :rows, cols] = alpha * acc_ref[h, :rows, cols] + _dot(vt_ref[0, h], p_ref[i % 2])

    n_units = len(units)
    tk = k_ref.shape[2]
    kt = tk // 256
    n_chunks = min(FLASH_CHUNKS, kt) if tk % 256 == 0 else 1
    v_edges = [256 * ((kt * c) // n_chunks) for c in range(n_chunks + 1)] if n_chunks > 1 else [0, tk]
    s_edges = [16 * (((tk // 16) * c) // n_chunks) for c in range(n_chunks + 1)]

    def scores_chunk(i, c, m_run):
        h, cc = units[i]
        cols = slice(cc * qs, (cc + 1) * qs)
        r0, r1 = s_edges[c], s_edges[c + 1]
        st = _dot(k_ref[0, h, r0:r1, :], q_ref[0, h, :, cols])
        st_ref[i % 2, r0:r1, :] = st
        return jnp.maximum(m_run, jnp.max(st, axis=0, keepdims=True))

    def values_chunk(i, c):
        h, _ = units[i]
        r0, r1 = v_edges[c], v_edges[c + 1]
        return _dot(vt_ref[0, h, :, r0:r1], p_ref[i % 2, r0:r1, :])

    stats = {0: scores(0)}
    if n_units > 1:
        stats[1] = scores(1)
    probs(0, stats[0][1])
    for i in range(n_units):
        h, cc = units[i]
        cols = slice(cc * qs, (cc + 1) * qs)
        m_prev_i, m_cur_i = stats.pop(i)
        nxt = i + 2 < n_units
        if nxt:
            h2, c2 = units[i + 2]
            cols2 = slice(c2 * qs, (c2 + 1) * qs)
            m_prev2 = m_ref[h2, :, cols2]
            m_run = m_prev2
        pv = None
        for c in range(n_chunks):
            part = values_chunk(i, c)
            pv = part if pv is None else pv + part
            if nxt:
                m_run = scores_chunk(i + 2, c, m_run)
        alpha = jnp.exp2(m_prev_i - m_cur_i)
        acc_ref[h, :rows, cols] = alpha * acc_ref[h, :rows, cols] + pv
        if nxt:
            m_ref[h2, :, cols2] = m_run
            stats[i + 2] = (m_prev2, m_run)
        if i + 1 < n_units:
            probs(i + 1, stats[i + 1][1])

    @pl.when(j == pl.num_programs(2) - 1)
    def _():
        outs = []
        for h in range(nh):
            a = acc_ref[h]
            o = (a / a[dv:dv + 1]).T
            outs.append(o[:, :dv])
        o_ref[0] = jnp.concatenate(outs, axis=1).astype(o_ref.dtype)


def _flash_t(q, k, vt, dv):
    b, nh, dk, lq = q.shape
    lk, rows = k.shape[2], vt.shape[2]
    tq = _pick_tile(lq, FLASH_Q_TILE, 128)
    tk = _pick_tile(lk, FLASH_K_TILE, 128)
    return pl.pallas_call(
        functools.partial(_flash_t_kernel, nh=nh, dv=dv),
        grid=(b, lq // tq, lk // tk),
        in_specs=[pl.BlockSpec((1, nh, dk, tq), lambda bb, i, j: (bb, 0, 0, i)),
                  pl.BlockSpec((1, nh, tk, dk), lambda bb, i, j: (bb, 0, j, 0)),
                  pl.BlockSpec((1, nh, rows, tk), lambda bb, i, j: (bb, 0, 0, j))],
        out_specs=pl.BlockSpec((1, tq, nh * dv), lambda bb, i, j: (bb, i, 0)),
        out_shape=jax.ShapeDtypeStruct((b, lq, nh * dv), BF16),
        scratch_shapes=[pltpu.VMEM((nh, 1, tq), F32), pltpu.VMEM((nh, 128, tq), F32),
                        pltpu.VMEM((2, tk, min(tq, FLASH_Q_SUB)), F32),
                        pltpu.VMEM((2, tk, min(tq, FLASH_Q_SUB)), PV_DTYPE)],
        compiler_params=_params(("parallel", "parallel", "arbitrary"), 56),
        name="softmax_attention_t",
    )(q, k, vt)


def _na_kernel(q_ref, k0, k1, k2, k3, v0, v1, v2, v3, kc_ref, vc_ref, b_ref, o_ref):
    q = q_ref[0]
    kcat = jnp.concatenate([k0[0], k1[0], k2[0], k3[0]], axis=0)
    vtcat = jnp.concatenate([v0[0], v1[0], v2[0], v3[0]], axis=1)
    kc, vtc = kc_ref[0], vc_ref[0]
    dh = NA_HEAD_DIM

    def with_ones(vt):
        pad = lax.broadcasted_iota(jnp.int32, (NA_VT_PAD, vt.shape[1]), 0)
        return jnp.concatenate([vt, jnp.where(pad == 0, 1.0, 0.0).astype(vt.dtype)], axis=0)

    def scores(h):
        sl = slice(h * dh, (h + 1) * dh)
        qh = q[:, sl]
        st = _dot_nt(kcat[:, sl], qh) + b_ref[0, h]
        stc = _dot_nt(kc[:, sl], qh)
        m = jnp.maximum(jnp.max(st, axis=0, keepdims=True), jnp.max(stc, axis=0, keepdims=True))
        return st, stc, m

    def values(h, st, stc, m):
        sl = slice(h * dh, (h + 1) * dh)
        p = jnp.exp(st - m).astype(BF16)
        pc = jnp.exp(stc - m).astype(BF16)
        acc = _dot(with_ones(vtcat[sl]), p) + _dot(with_ones(vtc[sl]), pc)
        o = jnp.concatenate([acc / acc[dh:dh + 1],
                             jnp.zeros((128 - dh - NA_VT_PAD, acc.shape[1]), F32)], axis=0)
        return o.T[:, :dh]

    outs = []
    pending = scores(0)
    for h in range(NA_HEADS):
        nxt = scores(h + 1) if h + 1 < NA_HEADS else None
        outs.append(values(h, *pending))
        pending = nxt
    o_ref[0] = jnp.concatenate(outs, axis=1).astype(o_ref.dtype)


def _na_bias_tables(rpb):
    j = np.arange(NA_QROWS)[:, None]
    m = np.arange(NA_KROWS)[None, :]
    valid = np.zeros((3, NA_QROWS, NA_KROWS), bool)
    st0 = np.maximum(j - NA_KH // 2, 0)
    valid[0] = (m - 4 >= st0) & (m - 4 < st0 + NA_KH)
    valid[1] = (m - j >= 0) & (m - j < NA_KH)
    st2 = 4 + np.minimum(j - NA_KH // 2, 0)
    valid[2] = (m < 12) & (m >= st2) & (m < st2 + NA_KH)
    dr_idx = np.clip(m - 4 - j + NA_KH - 1, 0, 2 * NA_KH - 2)
    qc = np.arange(GRID_W)[:, None]
    kc = np.arange(GRID_W)[None, :]
    dc_idx = np.clip(kc - qc, -(NA_KW - 1), NA_KW - 1) + NA_KW - 1
    cs = np.clip(qc - NA_KW // 2, 0, GRID_W - NA_KW)
    col_ok = (kc >= cs) & (kc < cs + NA_KW)
    t = rpb.astype(F32)[:, dr_idx]
    t = t[..., dc_idx]
    t = jnp.transpose(t, (0, 2, 4, 1, 3))
    ok = (valid.transpose(0, 2, 1)[:, None, :, None, :, None]
          & col_ok.T[None, None, None, :, None, :])
    out = jnp.where(jnp.asarray(ok), t[None], MASK_VALUE)
    return out.reshape(3, NA_HEADS, NA_KROWS * GRID_W, NA_QROWS * GRID_W)


def _na(q, k, vt, kc, vtc, bias):
    b, n, w = q.shape
    c = kc.shape[1]
    tq = NA_QROWS * GRID_W
    tkb = 4 * GRID_W
    nb = n // tq
    nkb = n // tkb
    blocks = [lambda i: jnp.maximum(2 * i - 1, 0), lambda i: 2 * i, lambda i: 2 * i + 1,
              lambda i: jnp.minimum(2 * i + 2, nkb - 1)]
    kspecs = [pl.BlockSpec((1, tkb, w), lambda bb, i, f=f: (bb, f(i), 0)) for f in blocks]
    vspecs = [pl.BlockSpec((1, w, tkb), lambda bb, i, f=f: (bb, 0, f(i))) for f in blocks]
    return pl.pallas_call(
        _na_kernel,
        grid=(b, nb),
        in_specs=[pl.BlockSpec((1, tq, w), lambda bb, i: (bb, i, 0))] + kspecs + vspecs
                 + [pl.BlockSpec((1, c, w), lambda bb, i: (bb, 0, 0)),
                    pl.BlockSpec((1, w, c), lambda bb, i: (bb, 0, 0)),
                    pl.BlockSpec((1, NA_HEADS, NA_KROWS * GRID_W, tq),
                                 lambda bb, i: (jnp.where(i == 0, 0, jnp.where(i == nb - 1, 2, 1)), 0, 0, 0))],
        out_specs=pl.BlockSpec((1, tq, w), lambda bb, i: (bb, i, 0)),
        out_shape=jax.ShapeDtypeStruct((b, n, w), BF16),
        compiler_params=_params(("parallel", "arbitrary"), 56),
        name="neighbourhood_attention",
    )(q, k, k, k, k, vt, vt, vt, vt, kc, vtc, bias)


def _sconv_kernel(u_ref, pv_ref, nx_ref, w_ref, v_ref, x1_ref, x2_ref, *, nt):
    i = pl.program_id(1)
    u = u_ref[0]
    tl = u.shape[0]
    prev = jnp.where(i > 0, pv_ref[0, 7:8, :], 0.0)
    nxt = jnp.where(i < nt - 1, nx_ref[0, 0:1, :], 0.0)
    rows = lax.broadcasted_iota(jnp.int32, u.shape, 0)
    um = jnp.where(rows == 0, prev, pltpu.roll(u, 1, 0))
    up = jnp.where(rows == tl - 1, nxt, pltpu.roll(u, tl - 1, 0))
    w = w_ref[...]
    y = um * w[0:1] + u * w[1:2] + up * w[2:3]
    v_ref[...] = y[:, :HY_WIDTH]
    x1_ref[...] = y[:, HY_WIDTH:2 * HY_WIDTH]
    x2_ref[...] = y[:, 2 * HY_WIDTH:]


def _sconv(u, w, seq_major):
    b, n, _ = u.shape
    tl = _pick_tile(n, 512, 8)
    nt = n // tl
    r8 = tl // 8
    if seq_major:
        osd = jax.ShapeDtypeStruct((n, b * HY_WIDTH), F32)
        ospec = pl.BlockSpec((tl, HY_WIDTH), lambda bb, i: (i, bb))
    else:
        osd = jax.ShapeDtypeStruct((b * n, HY_WIDTH), F32)
        ospec = pl.BlockSpec((tl, HY_WIDTH), lambda bb, i: (bb * nt + i, 0))
    return pl.pallas_call(
        functools.partial(_sconv_kernel, nt=nt),
        grid=(b, nt),
        in_specs=[pl.BlockSpec((1, tl, HY_IN), lambda bb, i: (bb, i, 0)),
                  pl.BlockSpec((1, 8, HY_IN), lambda bb, i: (bb, jnp.maximum(i * r8 - 1, 0), 0)),
                  pl.BlockSpec((1, 8, HY_IN), lambda bb, i: (bb, jnp.minimum((i + 1) * r8, n // 8 - 1), 0)),
                  pl.BlockSpec((3, HY_IN), lambda bb, i: (0, 0))],
        out_specs=[ospec, ospec, ospec],
        out_shape=[osd, osd, osd],
        compiler_params=_params(("parallel", "parallel")),
        name="hyena_short_conv",
    )(u, u, u, w)


def _filt_kernel(z_ref, w1_ref, b1_ref, f1_ref, w2_ref, b2_ref, f2_ref, w3_ref, win_ref, g_ref, ss_ref):
    d = pl.program_id(0)
    i = pl.program_id(1)
    h = jnp.sin(f1_ref[...] * (_dot(w1_ref[...], z_ref[0].astype(BF16)) + b1_ref[...]))
    h = jnp.sin(f2_ref[...] * (_dot(w2_ref[...], h.astype(BF16)) + b2_ref[...]))
    h = _dot(w3_ref[0], h.astype(BF16)).T
    h = h * jnp.concatenate([win_ref[0]] * HY_ORDER, axis=1)
    rows = lax.broadcasted_iota(jnp.int32, h.shape, 0)
    h = jnp.where((rows == 0) & (i == 0) & (d == 1), 0.0, h)
    g_ref[0] = h
    ss = jnp.sum(h * h, axis=0, keepdims=True)
    first = (i == 0) & (d == 0)

    @pl.when(first)
    def _():
        ss_ref[...] = ss

    @pl.when(jnp.logical_not(first))
    def _():
        ss_ref[...] += ss


def _padded_filter(n, fw):
    r = jnp.arange(n, dtype=jnp.int32)
    idx = jnp.stack([r, jnp.mod(n - r, n)], axis=0).astype(F32)[:, :, None]
    t = idx / (n - 1)
    bands = jnp.linspace(1e-4, HY_BANDS - 1, HY_BANDS, dtype=F32)
    ang = (2.0 * math.pi / n) * idx * bands
    z = jnp.concatenate([t, jnp.cos(ang), -jnp.sin(ang), jnp.zeros((2, n, HY_EMB_PAD - HY_EMB), F32)], axis=-1)
    z = jnp.swapaxes(z, 1, 2)
    deltas = jnp.linspace(math.log(HY_DECAY_TARGET) / HY_DECAY_LONG_PCT,
                          math.log(HY_DECAY_TARGET) / HY_DECAY_SHORT_PCT, HY_WIDTH, dtype=F32)
    window = jnp.exp(-t * jnp.abs(deltas))
    tl = _pick_tile(n, 512, 128)
    half = HY_ORDER * HY_WIDTH
    c2 = lambda d, i: (0, 0)
    g, ss = pl.pallas_call(
        _filt_kernel,
        grid=(2, n // tl),
        in_specs=[pl.BlockSpec((1, HY_EMB_PAD, tl), lambda d, i: (d, 0, i)),
                  pl.BlockSpec((HY_HIDDEN, HY_EMB_PAD), c2), pl.BlockSpec((HY_HIDDEN, 1), c2),
                  pl.BlockSpec((HY_HIDDEN, 1), c2),
                  pl.BlockSpec((HY_HIDDEN, HY_HIDDEN), c2), pl.BlockSpec((HY_HIDDEN, 1), c2),
                  pl.BlockSpec((HY_HIDDEN, 1), c2),
                  pl.BlockSpec((1, half, HY_HIDDEN), lambda d, i: (d, 0, 0)),
                  pl.BlockSpec((1, tl, HY_WIDTH), lambda d, i: (d, i, 0))],
        out_specs=[pl.BlockSpec((1, tl, half), lambda d, i: (d, i, 0)), pl.BlockSpec((1, half), c2)],
        out_shape=[jax.ShapeDtypeStruct((2, n, half), F32), jax.ShapeDtypeStruct((1, half), F32)],
        compiler_params=_params(("arbitrary", "arbitrary")),
        name="hyena_filter_mlp",
    )(z, fw["w1"], fw["b1"], fw["f1"], fw["w2"], fw["b2"], fw["f2"], fw["w3"], window)
    return g.reshape(2 * n, half), lax.rsqrt(ss + NORM_EPS)


def _colmm_kernel(*refs, epi):
    w_ref, x_ref = refs[0], refs[1]
    o_ref = refs[-1]
    acc = _dot(w_ref[...], x_ref[...].astype(BF16))
    if epi == "gate":
        xg_ref, z_ref, b_ref = refs[2:5]
        acc = xg_ref[...] * (acc + z_ref[...] * b_ref[...])
    elif epi == "scale":
        acc = acc * refs[2][...]
    elif epi == "cmul":
        g = refs[2][...]
        mh = acc.shape[0] // 2
        ar, ai, gr, gi = acc[:mh], acc[mh:], g[:mh], g[mh:]
        acc = jnp.concatenate([ar * gr - ai * gi, ar * gi + ai * gr], axis=0)
    o_ref[...] = acc.astype(o_ref.dtype)


def _colmm(w, x, out_dtype, epi="none", extra=(), tn_cap=4096, g_col=None):
    m, k = w.shape
    nc = x.shape[1]
    tn = _pick_tile(nc, tn_cap, 128)
    col = lambda j: (0, j)
    specs = [pl.BlockSpec((m, k), lambda j: (0, 0)), pl.BlockSpec((k, tn), col)]
    if epi == "gate":
        specs += [pl.BlockSpec((m, tn), col), pl.BlockSpec((m, tn), col), pl.BlockSpec((1, tn), col)]
    elif epi == "scale":
        specs += [pl.BlockSpec((1, tn), col)]
    elif epi == "cmul":
        specs += [pl.BlockSpec((m, tn), lambda j: (0, g_col))]
    return pl.pallas_call(
        functools.partial(_colmm_kernel, epi=epi),
        grid=(nc // tn,),
        in_specs=specs,
        out_specs=pl.BlockSpec((m, tn), col),
        out_shape=jax.ShapeDtypeStruct((m, nc), out_dtype),
        compiler_params=_params(("parallel",), 48),
        name="dft_stage_" + epi,
    )(w, x, *extra)


def _bmm_kernel(m_ref, x_ref, o_ref, *, gs):
    for s in range(gs):
        x = x_ref[:, s]
        x = x.reshape(x.shape[0] * x.shape[1], x.shape[2])
        acc = _dot(m_ref[s], x)
        c = acc.shape[1]
        o_ref[:, s * c:(s + 1) * c] = acc.astype(o_ref.dtype)


def _bmm(mats, x, out_dtype):
    ng, p, qq = mats.shape
    c = x.shape[3]
    gs = _pick_tile(ng, BMM_GROUPS_PER_STEP, 1)
    return pl.pallas_call(
        functools.partial(_bmm_kernel, gs=gs),
        grid=(ng // gs,),
        in_specs=[pl.BlockSpec((gs, p, qq), lambda g: (g, 0, 0)),
                  pl.BlockSpec((2, gs, qq // 2, c), lambda g: (0, g, 0, 0))],
        out_specs=pl.BlockSpec((p, gs * c), lambda g: (0, g)),
        out_shape=jax.ShapeDtypeStruct((p, ng * c), out_dtype),
        compiler_params=_params(("parallel",), 48),
        name="dft_batched_real",
    )(mats, x)


def _tbmm_kernel(*refs, mode, gs):
    f_ref, twr_ref, twi_ref, x_ref = refs[:4]
    o_ref = refs[-1]
    for s in range(gs):
        xr, xi = x_ref[0, s].astype(F32), x_ref[1, s].astype(F32)
        reps = xr.shape[1] // twr_ref.shape[2]
        tr = jnp.concatenate([twr_ref[s]] * reps, axis=1)
        ti = jnp.concatenate([twi_ref[s]] * reps, axis=1)
        x = jnp.concatenate([xr * tr - xi * ti, xr * ti + xi * tr], axis=0).astype(BF16)
        acc = _dot(f_ref[...], x)
        ph = acc.shape[0] // 2
        ar, ai = acc[:ph], acc[ph:]
        if mode == "scale":
            sc = refs[4][...]
            o_ref[0, s] = ar * sc
            o_ref[1, s] = ai * sc
        else:
            g_ref, fi_ref = refs[4], refs[5]
            gr, gi = g_ref[0, s], g_ref[1, s]
            y = jnp.concatenate([ar * gr - ai * gi, ar * gi + ai * gr], axis=0).astype(BF16)
            back = _dot(fi_ref[...], y)
            br, bi = back[:ph], back[ph:]
            o_ref[0, s] = (br * tr + bi * ti).astype(o_ref.dtype)
            o_ref[1, s] = (bi * tr - br * ti).astype(o_ref.dtype)


def _tbmm(tabs, x, mode, extra, out_dtype, g_col=0):
    _, ng, n2, c = x.shape
    gs = _pick_tile(ng, BMM_GROUPS_PER_STEP, 1)
    p = 2 * n2
    blk = lambda g: (0, g, 0, 0)
    specs = [pl.BlockSpec((p, p), lambda g: (0, 0)),
             pl.BlockSpec((gs, n2, 128), lambda g: (g, 0, 0)),
             pl.BlockSpec((gs, n2, 128), lambda g: (g, 0, 0)),
             pl.BlockSpec((2, gs, n2, c), blk)]
    args = [tabs["hy_fwd"], tabs["hy_twr"], tabs["hy_twi"], x]
    if mode == "scale":
        specs.append(pl.BlockSpec((1, c), lambda g: (0, 0)))
        args.append(extra)
    else:
        specs += [pl.BlockSpec((2, gs, n2, c), lambda g: (0, g, 0, g_col)),
                  pl.BlockSpec((p, p), lambda g: (0, 0))]
        args += [extra, tabs["hy_inv"]]
    return pl.pallas_call(
        functools.partial(_tbmm_kernel, mode=mode, gs=gs),
        grid=(ng // gs,),
        in_specs=specs,
        out_specs=pl.BlockSpec((2, gs, n2, c), blk),
        out_shape=jax.ShapeDtypeStruct(x.shape, out_dtype),
        compiler_params=_params(("parallel",), 48),
        name="dft_inner_" + mode,
    )(*args)


def _angle(num, den):
    return (2.0 * math.pi / den) * jnp.mod(num, den).astype(F32)


def _dft_tables(n_lat):
    nn = 2 * n_lat
    n2 = DFT_N2
    n1 = nn // n2
    i1 = jnp.arange(n1, dtype=jnp.int32)
    a1 = _angle(i1[:, None] * i1[None, :], n1)
    c1, s1 = jnp.cos(a1), jnp.sin(a1)
    t = {}
    t["hy_f1"] = jnp.concatenate([c1, -s1], axis=0).astype(BF16)
    ch, sh = c1[:, :n1 // 2], s1[:, :n1 // 2]
    t["hy_f1c"] = jnp.concatenate([jnp.concatenate([ch, sh], axis=1),
                                   jnp.concatenate([-sh, ch], axis=1)], axis=0).astype(BF16)
    t["hy_i2c"] = jnp.concatenate([jnp.concatenate([ch.T, -sh.T], axis=1),
                                   jnp.concatenate([sh.T, ch.T], axis=1)], axis=0).astype(BF16)
    i2 = jnp.arange(n2, dtype=jnp.int32)
    a2 = _angle(i2[:, None] * i2[None, :], n2)
    c2, s2 = jnp.cos(a2), jnp.sin(a2)
    fwd = jnp.concatenate([jnp.concatenate([c2, s2], axis=1), jnp.concatenate([-s2, c2], axis=1)], axis=0)
    t["hy_fwd"] = fwd.astype(BF16)
    t["hy_inv"] = fwd.T.astype(BF16)
    at = _angle(i1[:, None] * i2[None, :], nn)
    lanes = (n1, n2, 128)
    t["hy_twr"] = jnp.broadcast_to(jnp.cos(at)[:, :, None], lanes)
    t["hy_twi"] = jnp.broadcast_to(-jnp.sin(at)[:, :, None], lanes)
    m2 = FN_N2
    m1 = n_lat // m2
    j1 = jnp.arange(m1, dtype=jnp.int32)
    b1 = _angle(j1[:, None] * j1[None, :], m1)
    cb, sb = jnp.cos(b1), jnp.sin(b1)
    t["fn_s1"] = jnp.concatenate([jnp.concatenate([cb, sb], axis=1),
                                  jnp.concatenate([-sb, cb], axis=1)], axis=0).astype(BF16)
    j2 = jnp.arange(m2, dtype=jnp.int32)
    kf = j1[:, None, None] + m1 * j2[None, :, None]
    b2 = _angle(kf * j2[None, None, :], n_lat)
    norm = 1.0 / math.sqrt(n_lat * FN_GROUP_DIM)
    t["fn_s3"] = (norm * jnp.concatenate([jnp.cos(b2), jnp.sin(b2)], axis=2)).astype(BF16)
    return t


def _small_dft_tables(n):
    nn = 2 * n
    k = jnp.arange(nn, dtype=jnp.int32)
    a = _angle(k[:, None] * k[None, :], nn)
    c, s = jnp.cos(a), jnp.sin(a)
    t = {}
    t["f_full"] = jnp.concatenate([c, -s], axis=0).astype(BF16)
    t["f_half"] = jnp.concatenate([c[:, :n], -s[:, :n]], axis=0).astype(BF16)
    t["i_half"] = jnp.concatenate([c[:n], -s[:n]], axis=1).astype(BF16)
    j = jnp.arange(n, dtype=jnp.int32)
    b = _angle(j[:, None] * j[None, :], n)
    norm = 1.0 / math.sqrt(n * FN_GROUP_DIM)
    t["fn"] = (norm * jnp.concatenate([jnp.cos(b), jnp.sin(b)], axis=1)).astype(BF16)
    return t


def _hyena_long(hy, short_w, fw, bias, tabs):
    b, n, _ = hy.shape
    assert b == 2
    cw = HY_WIDTH
    n2 = DFT_N2
    n1 = 2 * n // n2
    v, x1, x2 = _sconv(hy, short_w, False)
    g, gscale = _padded_filter(n, fw)
    half = HY_ORDER * HY_WIDTH
    ga = _colmm(tabs["hy_f1"], g.reshape(n1, n2 * half), BF16)
    gspec = _tbmm(tabs, ga.reshape(2, n1, n2, half), "scale", gscale * (1.0 / (2 * n)), F32)
    z = v.reshape(n1, n2 * cw)
    for o, xg in enumerate((x1, x2)):
        a = _colmm(tabs["hy_f1c"], z, BF16)
        bm = _tbmm(tabs, a.reshape(2, n1, n2, cw), "conv", gspec, BF16, g_col=o)
        brow = jnp.tile(bias[o], n2)[None, :]
        z = _colmm(tabs["hy_i2c"], bm.reshape(2 * n1, n2 * cw), F32, epi="gate",
                   extra=(xg.reshape(n1, n2 * cw), z, brow))
    return z.reshape(b, n, cw)


def _hyena_short(hy, short_w, fw, bias, tabs):
    b, n, _ = hy.shape
    cw = b * HY_WIDTH
    v, x1, x2 = _sconv(hy, short_w, True)
    g, gscale = _padded_filter(n, fw)
    gspec = _colmm(tabs["f_full"], g, F32, epi="scale", extra=(gscale * (1.0 / (2 * n)),), tn_cap=256)
    z = v
    for o, xg in enumerate((x1, x2)):
        y = _colmm(tabs["f_half"], z, BF16, epi="cmul", extra=(gspec,), tn_cap=HY_WIDTH, g_col=o)
        brow = jnp.tile(bias[o], b)[None, :]
        z = _colmm(tabs["i_half"], y, F32, epi="gate", extra=(xg, z, brow), tn_cap=256)
    return z


def _fn0_kernel(x_ref, w_ref, o_ref):
    acc = _dot(x_ref[0].astype(BF16), w_ref[...])
    o_ref[0] = acc[:, :BRANCH_WIDTH].astype(o_ref.dtype)
    o_ref[1] = acc[:, BRANCH_WIDTH:].astype(o_ref.dtype)


def _fnet_channel_dft(u):
    b, n, w = u.shape
    j = np.arange(FN_GROUP_DIM)
    ang = 2.0 * np.pi * ((j[:, None] * j[None, :]) % FN_GROUP_DIM) / FN_GROUP_DIM
    eye = np.eye(FN_GROUPS)
    wc = np.concatenate([np.kron(eye, np.cos(ang)), -np.kron(eye, np.sin(ang))], axis=1)
    wc = jnp.asarray(wc, F32).astype(BF16)
    tm = _pick_tile(n, 1024, 8)
    return pl.pallas_call(
        _fn0_kernel,
        grid=(b, n // tm),
        in_specs=[pl.BlockSpec((1, tm, w), lambda bb, i: (bb, i, 0)),
                  pl.BlockSpec((w, 2 * w), lambda bb, i: (0, 0))],
        out_specs=pl.BlockSpec((2, tm, w), lambda bb, i: (0, i, bb)),
        out_shape=jax.ShapeDtypeStruct((2, n, b * w), BF16),
        compiler_params=_params(("parallel", "parallel")),
        name="fnet_channel_dft",
    )(u, wc)


def _fnet_long(u, tabs):
    b, n, w = u.shape
    cw = b * w
    m2 = FN_N2
    m1 = n // m2
    wri = _fnet_channel_dft(u)
    a = _colmm(tabs["fn_s1"], wri.reshape(2 * m1, m2 * cw), BF16)
    y = _bmm(tabs["fn_s3"], a.reshape(2, m1, m2, cw), BF16)
    return y.reshape(n, cw)


def _fnet_short(u, tabs):
    b, n, w = u.shape
    wri = _fnet_channel_dft(u)
    return _colmm(tabs["fn"], wri.reshape(2 * n, b * w), BF16, tn_cap=256)


def _merge_kernel(x_ref, g_ref, sc_ref, sh_ref, gt_ref, oa_ref, ob_ref, oc_ref, od_ref,
                  wg_ref, wb_ref, wo_ref, o_ref):
    x = x_ref[0]
    hb = _normmod(x, g_ref[...], sc_ref[0], sh_ref[0]).astype(BF16)
    branches = (oa_ref[0], ob_ref[0], oc_ref[...], od_ref[0])
    acc = jnp.zeros(x.shape, F32)
    for nbr, o in enumerate(branches):
        gate = _dot(hb, wg_ref[:, nbr * D_MODEL:(nbr + 1) * D_MODEL])
        proj = _dot(o.astype(BF16), wb_ref[nbr])
        acc = acc + jax.nn.sigmoid(gate) * proj
    m = _dot(acc.astype(BF16), wo_ref[...])
    o_ref[0] = x + gt_ref[0] * m


def _merge(x, g, sc, sh, gt, oa, ob, oc, od, w):
    b, n, _ = x.shape
    tm = _pick_tile(n, 512, 8)
    tok = lambda bb, i: (bb, i, 0)
    row = lambda bb, i: (bb, 0, 0)
    seq = lambda bb, i: (i, bb)
    c2 = lambda bb, i: (0, 0)
    return pl.pallas_call(
        _merge_kernel,
        grid=(b, n // tm),
        in_specs=[pl.BlockSpec((1, tm, D_MODEL), tok),
                  pl.BlockSpec((1, D_MODEL), c2),
                  pl.BlockSpec((1, 1, D_MODEL), row), pl.BlockSpec((1, 1, D_MODEL), row),
                  pl.BlockSpec((1, 1, D_MODEL), row),
                  pl.BlockSpec((1, tm, BRANCH_WIDTH), tok),
                  pl.BlockSpec((1, tm, BRANCH_WIDTH), tok),
                  pl.BlockSpec((tm, BRANCH_WIDTH), seq),
                  pl.BlockSpec((1, tm, BRANCH_WIDTH), tok),
                  pl.BlockSpec((D_MODEL, N_BRANCH * D_MODEL), c2),
                  pl.BlockSpec((N_BRANCH, BRANCH_WIDTH, D_MODEL), lambda bb, i: (0, 0, 0)),
                  pl.BlockSpec((D_MODEL, D_MODEL), c2)],
        out_specs=pl.BlockSpec((1, tm, D_MODEL), tok),
        out_shape=jax.ShapeDtypeStruct(x.shape, F32),
        compiler_params=_params(("parallel", "parallel"), 56),
        name="branch_merge",
    )(x, g, sc, sh, gt, oa, ob, oc, od, w["wgate"], w["wbranch"], w["wout"])


def _moe_kernel(x_ref, xn_ref, g_ref, sc_ref, sh_ref, gt_ref, wr_ref, br_ref, wg_ref, wu_ref, wd_ref, fg_ref,
                o_ref, h_s, wt_s, acc_s, *, final):
    i = pl.program_id(1)
    e = pl.program_id(2)
    slot = i % 2
    quarter = xn_ref.shape[1]

    def route(xv):
        hb = _normmod(xv, g_ref[...], sc_ref[0], sh_ref[0]).astype(BF16)
        r = _dot(hb, wr_ref[...]) + br_ref[...]
        lane = lax.broadcasted_iota(jnp.int32, r.shape, 1)
        neg = jnp.float32(-jnp.inf)
        isg = lane < MOE_GROUPS
        gmax = jnp.max(jnp.where(isg, r, neg), axis=1, keepdims=True)
        gsum = jnp.sum(jnp.where(isg, jnp.exp(r - gmax), 0.0), axis=1, keepdims=True)
        g_p = 1.0 / gsum
        gidx = jnp.min(jnp.where(isg & (r == gmax), lane, ROUTER_LANES), axis=1, keepdims=True)
        ise = ((lane >= MOE_GROUPS) & (lane < MOE_GROUPS + MOE_EXPERTS)
               & (jnp.right_shift(lane - MOE_GROUPS, 2) == gidx))
        el = jnp.where(ise, r, neg)
        e1 = jnp.max(el, axis=1, keepdims=True)
        i1 = jnp.min(jnp.where(ise & (r == e1), lane, ROUTER_LANES), axis=1, keepdims=True)
        el2 = jnp.where(lane == i1, neg, el)
        e2 = jnp.max(el2, axis=1, keepdims=True)
        i2 = jnp.min(jnp.where(el2 == e2, lane, ROUTER_LANES), axis=1, keepdims=True)
        d = jnp.exp(e2 - e1)
        w1 = 1.0 / (1.0 + d)
        w2 = d / (1.0 + d)
        return hb, jnp.where(lane == i1, g_p * w1, jnp.where(lane == i2, g_p * w2, 0.0))

    @pl.when((e == 0) & (i == 0))
    def _():
        hb0, wt0 = route(x_ref[0])
        h_s[0] = hb0
        wt_s[0] = wt0

    @pl.when(e == 0)
    def _():
        acc_s[...] = jnp.zeros(acc_s.shape, F32)

    rows_n = pl.ds(pl.multiple_of(e * quarter, quarter), quarter)
    hbn, wtn = route(xn_ref[0])
    h_s[1 - slot, rows_n, :] = hbn
    wt_s[1 - slot, rows_n, :] = wtn

    hb = h_s[slot]
    wt = wt_s[slot]
    lane = lax.broadcasted_iota(jnp.int32, wt.shape, 1)
    acts = []
    for k in range(MOE_PER_GROUP):
        a = _dot(hb, wg_ref[k])
        u = _dot(hb, wu_ref[k])
        col = MOE_GROUPS + e * MOE_PER_GROUP + k
        wcol = jnp.sum(jnp.where(lane == col, wt, 0.0), axis=1, keepdims=True)
        acts.append(((a * jax.nn.sigmoid(a)) * u * wcol).astype(BF16))
    act = jnp.concatenate(acts, axis=1)
    wd = wd_ref[...].reshape(MOE_PER_GROUP * MOE_HIDDEN, D_MODEL)
    acc_s[...] += _dot(act, wd)

    @pl.when(e == MOE_GROUPS - 1)
    def _():
        y = x_ref[0] + gt_ref[0] * acc_s[...]
        if final:
            y = _rms(y) * fg_ref[...]
        o_ref[0] = y


def _moe(x, g, sc, sh, gt, w, fg, final):
    b, n, _ = x.shape
    tm = _pick_tile(n, 1024, 8)
    nt = n // tm
    quarter = tm // MOE_GROUPS
    tok = lambda bb, i, e: (bb, i, 0)
    nxt = lambda bb, i, e: (bb, jnp.minimum(i + 1, nt - 1) * MOE_GROUPS + e, 0)
    row = lambda bb, i, e: (bb, 0, 0)
    c2 = lambda bb, i, e: (0, 0)
    ex = lambda bb, i, e: (e, 0, 0)
    return pl.pallas_call(
        functools.partial(_moe_kernel, final=final),
        grid=(b, nt, MOE_GROUPS),
        in_specs=[pl.BlockSpec((1, tm, D_MODEL), tok),
                  pl.BlockSpec((1, quarter, D_MODEL), nxt),
                  pl.BlockSpec((1, D_MODEL), c2),
                  pl.BlockSpec((1, 1, D_MODEL), row), pl.BlockSpec((1, 1, D_MODEL), row),
                  pl.BlockSpec((1, 1, D_MODEL), row),
                  pl.BlockSpec((D_MODEL, ROUTER_LANES), c2), pl.BlockSpec((1, ROUTER_LANES), c2),
                  pl.BlockSpec((MOE_PER_GROUP, D_MODEL, MOE_HIDDEN), ex),
                  pl.BlockSpec((MOE_PER_GROUP, D_MODEL, MOE_HIDDEN), ex),
                  pl.BlockSpec((MOE_PER_GROUP, MOE_HIDDEN, D_MODEL), ex),
                  pl.BlockSpec((1, D_MODEL), c2)],
        out_specs=pl.BlockSpec((1, tm, D_MODEL), tok),
        out_shape=jax.ShapeDtypeStruct(x.shape, F32),
        scratch_shapes=[pltpu.VMEM((2, tm, D_MODEL), BF16), pltpu.VMEM((2, tm, ROUTER_LANES), F32),
                        pltpu.VMEM((tm, D_MODEL), F32)],
        compiler_params=_params(("parallel", "arbitrary", "arbitrary"), 56),
        name="hier_moe",
    )(x, x, g, sc, sh, gt, w["wr"], w["br"], w["wg"], w["wu"], w["wd"], fg)


def _layer_weights(l, w_in, mla_q_norm, mla_w_uq, mla_kv_norm, mla_w_ukv, w_branch, w_out,
                   moe_w_group, moe_b_group, moe_w_expert, moe_b_expert, moe_w_gate, moe_w_up, moe_w_down):
    wi = w_in[l]
    b0 = MLA_Q_RANK
    b1 = b0 + MLA_KV_RANK
    b2 = b1 + MLA_ROPE
    b3 = b2 + HY_IN
    b4 = b3 + BRANCH_WIDTH
    b5 = b4 + 3 * BRANCH_WIDTH
    zpad = lambda r, c: jnp.zeros((r, c), F32)
    w_kr = wi[:, b1:b2]
    krp = jnp.concatenate([zpad(D_MODEL, MLA_NOPE), w_kr[:, 0::2], w_kr[:, 1::2],
                           zpad(D_MODEL, MLA_HEAD_PAD - MLA_NOPE - MLA_ROPE)], axis=1)
    w_na = wi[:, b4:b5]
    w_na = jnp.concatenate([w_na[:, :BRANCH_WIDTH] * NA_SCALE, w_na[:, BRANCH_WIDTH:]], axis=1)
    wa = jnp.concatenate([wi[:, :b1], krp, wi[:, b2:b4], w_na], axis=1).astype(BF16)
    uq = mla_w_uq[l].reshape(MLA_Q_RANK, MLA_HEADS, MLA_NOPE + MLA_ROPE) * (MLA_SCALE * LOG2E)
    uq = jnp.concatenate([uq[..., :MLA_NOPE], uq[..., MLA_NOPE::2], uq[..., MLA_NOPE + 1::2],
                          jnp.zeros((MLA_Q_RANK, MLA_HEADS, MLA_HEAD_PAD - MLA_NOPE - MLA_ROPE), F32)], axis=-1)
    ukv = mla_w_ukv[l].reshape(MLA_KV_RANK, MLA_HEADS, MLA_NOPE + MLA_V)
    uk = jnp.concatenate([ukv[..., :MLA_NOPE],
                          jnp.zeros((MLA_KV_RANK, MLA_HEADS, MLA_HEAD_PAD - MLA_NOPE), F32)], axis=-1)
    uv = ukv[..., MLA_NOPE:]
    wr = jnp.concatenate([moe_w_group[l], moe_w_expert[l],
                          zpad(D_MODEL, ROUTER_LANES - MOE_GROUPS - MOE_EXPERTS)], axis=1)
    br = jnp.concatenate([moe_b_group[l], moe_b_expert[l],
                          jnp.zeros((ROUTER_LANES - MOE_GROUPS - MOE_EXPERTS,), F32)])[None, :]
    return dict(
        wa=wa, qn=mla_q_norm[l][None, :], kvn=mla_kv_norm[l][None, :],
        wuq=uq.reshape(MLA_Q_RANK, -1).astype(BF16), wuk=uk.reshape(MLA_KV_RANK, -1).astype(BF16),
        wuv=uv.reshape(MLA_KV_RANK, -1).astype(BF16),
        wgate=wi[:, b5:].astype(BF16), wbranch=w_branch[l].astype(BF16), wout=w_out[l].astype(BF16),
        wr=wr.astype(BF16), br=br, wg=moe_w_gate[l].astype(BF16), wu=moe_w_up[l].astype(BF16),
        wd=moe_w_down[l].astype(BF16))


def _rope_tables(n):
    t = jnp.arange(n, dtype=jnp.int32)
    row = (t // GRID_W).astype(F32)
    col = (t % GRID_W).astype(F32)
    n_freq = MLA_ROPE // 4
    inv = ROPE_BASE ** (-jnp.arange(n_freq, dtype=F32) / n_freq)
    ang = jnp.concatenate([row[:, None] * inv, col[:, None] * inv], axis=-1)
    c, s = jnp.cos(ang), jnp.sin(ang)
    one = jnp.ones((n, MLA_NOPE), F32)
    zero = jnp.zeros((n, MLA_NOPE), F32)
    hr = MLA_ROPE // 2
    tail1 = jnp.ones((n, MLA_HEAD_PAD - MLA_NOPE - MLA_ROPE), F32)
    tail0 = jnp.zeros((n, MLA_HEAD_PAD - MLA_NOPE - MLA_ROPE), F32)
    cos = jnp.concatenate([one, c, c, tail1], axis=1)
    s1 = jnp.concatenate([zero, -s, jnp.zeros((n, hr), F32), tail0], axis=1)
    s2 = jnp.concatenate([zero, jnp.zeros((n, hr), F32), s, tail0], axis=1)
    return cos, s1, s2


def _identity_rope(n):
    return (jnp.ones((n, MLA_HEAD_PAD), F32), jnp.zeros((n, MLA_HEAD_PAD), F32),
            jnp.zeros((n, MLA_HEAD_PAD), F32))


def kernel(x, c, ctx, c_ctx, ada_w, ada_b, norm1_g, norm2_g, w_in, mla_q_norm, mla_w_uq, mla_kv_norm, mla_w_ukv, hy_short_w, hy_w1, hy_b1, hy_freq1, hy_w2, hy_b2, hy_freq2, hy_w3, hy_bias, na_rpb, w_branch, w_out, moe_w_group, moe_b_group, moe_w_expert, moe_b_expert, moe_w_gate, moe_w_up, moe_w_down, final_norm_g):
    b, n_lat, _ = x.shape
    n_ctx = ctx.shape[1]
    assert b <= 4 and n_lat % (NA_QROWS * GRID_W) == 0 and n_lat // GRID_W >= 2 * NA_QROWS

    cvec = jnp.concatenate([c, c_ctx[None, :], jnp.zeros((8 - b - 1, D_MODEL), F32)], axis=0)
    mod = _mod_vectors(cvec, ada_w, ada_b)

    rope_lat = _rope_tables(n_lat)
    rope_ctx = _identity_rope(n_ctx)
    tabs_lat = _dft_tables(n_lat)
    tabs_ctx = _small_dft_tables(n_ctx)
    fg = final_norm_g[None, :]

    for l in range(DEPTH):
        last = l == DEPTH - 1
        w = _layer_weights(l, w_in, mla_q_norm, mla_w_uq, mla_kv_norm, mla_w_ukv, w_branch, w_out,
                           moe_w_group, moe_b_group, moe_w_expert, moe_b_expert,
                           moe_w_gate, moe_w_up, moe_w_down)
        ml = mod[l, :b].reshape(b, 1, 6, D_MODEL)
        mc = jnp.broadcast_to(mod[l, b].reshape(1, 1, 6, D_MODEL), (b, 1, 6, D_MODEL))
        sh1, sc1, gt1, sh2, sc2, gt2 = [ml[:, :, i] for i in range(6)]
        sh1c, sc1c, gt1c, sh2c, sc2c, gt2c = [mc[:, :, i] for i in range(6)]
        g1 = norm1_g[l][None, :]
        g2 = norm2_g[l][None, :]
        fw = dict(
            w1=jnp.concatenate([hy_w1[l], jnp.zeros((HY_EMB_PAD - HY_EMB, HY_HIDDEN), F32)], axis=0).T.astype(BF16),
            b1=hy_b1[l][:, None], f1=hy_freq1[l][:, None], w2=hy_w2[l].T.astype(BF16), b2=hy_b2[l][:, None],
            f2=hy_freq2[l][:, None],
            w3=jnp.transpose(hy_w3[l].reshape(HY_HIDDEN, 2, HY_ORDER * HY_WIDTH), (1, 2, 0)).astype(BF16))

        n_keys = n_lat + n_ctx
        kv_bufs = (jnp.zeros((b, MLA_HEADS, n_keys, MLA_HEAD_PAD), BF16),
                   jnp.zeros((b, MLA_HEADS, MLA_VT_ROWS, n_keys), PV_DTYPE))
        q, kall, vtall, hy, fn, naq, nak, nav = _inproj(x, g1, sc1, sh1, w, rope_lat, kv_bufs, 0, False)
        qc, kc, vc, kall, vtall, hyc, fnc, naqc, nakc, navc = _inproj(
            ctx, g1, sc1c, sh1c, w, rope_ctx, (kall, vtall), n_lat, True)

        oa = _flash_t(q, kall, vtall, MLA_V)
        od = _na(naq, nak, nav, nakc, navc, _na_bias_tables(na_rpb[l]))
        ob = _hyena_long(hy, hy_short_w[l], fw, hy_bias[l], tabs_lat)
        oc = _fnet_long(fn, tabs_lat)
        x_new = _merge(x, g1, sc1, sh1, gt1, oa, ob, oc, od, w)

        if not last:
            oa_c = _flash(qc, kc, vc, True)
            heads = lambda t: jnp.transpose(t.reshape(b, n_ctx, NA_HEADS, NA_HEAD_DIM), (0, 2, 1, 3))
            od_c = _flash(heads(naqc), heads(nakc), heads(jnp.swapaxes(navc, 1, 2)), False)
            ob_c = _hyena_short(hyc, hy_short_w[l], fw, hy_bias[l], tabs_ctx)
            ob_c = jnp.transpose(ob_c.reshape(n_ctx, b, HY_WIDTH), (1, 0, 2))
            oc_c = _fnet_short(fnc, tabs_ctx)
            ctx = _merge(ctx, g1, sc1c, sh1c, gt1c, oa_c, ob_c, oc_c, od_c, w)
            ctx = _moe(ctx, g2, sc2c, sh2c, gt2c, w, fg, False)
        x = _moe(x_new, g2, sc2, sh2, gt2, w, fg, last)
    return x
```
